```python
import jax, jax.numpy as jnp
from jax import lax
import numpy as np

D_MODEL = 1024
BATCH = 16
SEQ = 256
DEPTH = 2
DEC_BATCH = 4
DEC_SEQ = 2048
PAST_LEN = 256

GRID_W = 64
N_EVEN = (DEPTH + 1) // 2
N_ODD = DEPTH // 2
HEAD_DIM = 64
ATTN_WIDTH = D_MODEL // 2
N_Q_HEADS = ATTN_WIDTH // HEAD_DIM
N_KV_HEADS = N_Q_HEADS // 4
Q_PER_KV = N_Q_HEADS // N_KV_HEADS
KV_WIDTH = N_KV_HEADS * HEAD_DIM
CONV_WIDTH = D_MODEL - ATTN_WIDTH
WINDOW = 128
BLOCK = 128
ROPE_BASE = 10000.0
ROPE_FREQS = HEAD_DIM // 4
ATTN_SCALE = HEAD_DIM ** -0.5
NEG = -1e30
POOL_WIDTH = D_MODEL
POOL_SIZES = (2, 4, 8, 16)
N_POOL_GROUPS = len(POOL_SIZES)
POOL_GROUP = POOL_WIDTH // N_POOL_GROUPS
EPS = 1e-6
EVEN_SIZES = (CONV_WIDTH, CONV_WIDTH, CONV_WIDTH, CONV_WIDTH, ATTN_WIDTH, KV_WIDTH, KV_WIDTH, ATTN_WIDTH)
EVEN_IN = sum(EVEN_SIZES)
EVEN_SPLITS = tuple(int(s) for s in np.cumsum(EVEN_SIZES)[:-1])
ODD_IN = 2 * POOL_WIDTH

kernel_name = "hybrid_diffusion_prefix_conv_swa_pool_step"


def rmsnorm(x, g):
    xf = x.astype(jnp.float32)
    y = xf * lax.rsqrt(jnp.mean(xf * xf, axis=-1, keepdims=True) + EPS)
    return (y * g.astype(jnp.float32)).astype(x.dtype)


def adaln(cond, w, b):
    m = jax.nn.silu(cond) @ w + b
    shift, scale, gate = jnp.split(m, 3, axis=-1)
    return shift[:, None], scale[:, None], gate[:, None]


def modulate(x, g, shift, scale):
    return rmsnorm(x, g) * (1 + scale) + shift


def short_conv(u, w, b):
    up = jnp.pad(u, ((0, 0), (1, 1), (0, 0)))
    return up[:, :-2] * w[0] + up[:, 1:-1] * w[1] + up[:, 2:] * w[2] + b


def axial_rope_tables(n_rows):
    row = jnp.repeat(jnp.arange(n_rows), GRID_W).astype(jnp.float32)
    col = jnp.tile(jnp.arange(GRID_W), n_rows).astype(jnp.float32)
    inv = ROPE_BASE ** (-jnp.arange(ROPE_FREQS, dtype=jnp.float32) / ROPE_FREQS)
    ang = jnp.stack([row[:, None] * inv, col[:, None] * inv], axis=1)
    return jnp.cos(ang), jnp.sin(ang)


def apply_axial_rope(x, cos, sin):
    B, L, H, _ = x.shape
    xr = x.astype(jnp.float32).reshape(B, L, H, 2, 2, ROPE_FREQS)
    x1, x2 = xr[..., 0, :], xr[..., 1, :]
    c, s = cos[None, :, None], sin[None, :, None]
    out = jnp.stack([x1 * c - x2 * s, x2 * c + x1 * s], axis=-2)
    return out.reshape(x.shape).astype(x.dtype)


def sink_softmax(s, sink):
    m = jnp.maximum(jnp.max(s, axis=-1, keepdims=True), sink)
    e = jnp.exp(s - m)
    return e / (jnp.sum(e, axis=-1, keepdims=True) + jnp.exp(sink - m))


def context_attention(q, k, v, sink):
    B, S = q.shape[:2]
    nb = S // BLOCK
    qb = q.reshape(B, nb, BLOCK, N_KV_HEADS, Q_PER_KV, HEAD_DIM).swapaxes(0, 1).astype(jnp.float32)
    kf, vf = k.astype(jnp.float32), v.astype(jnp.float32)
    sk = sink.astype(jnp.float32)[None, :, :, None, None]

    def attend(qblk):
        s = jnp.einsum('bqkgd,bskd->bkgqs', qblk, kf) * ATTN_SCALE
        p = sink_softmax(s, sk)
        return jnp.einsum('bkgqs,bskd->bqkgd', p, vf)

    o = lax.map(attend, qb)
    return o.swapaxes(0, 1).reshape(B, S, ATTN_WIDTH).astype(q.dtype)


def latent_attention(q, k, v, ctx_k, ctx_v, sink):
    B, L = q.shape[:2]
    nb = L // BLOCK
    pad = ((0, 0), (BLOCK, BLOCK), (0, 0), (0, 0))
    kblk = jnp.pad(k, pad).reshape(B, nb + 2, BLOCK, N_KV_HEADS, HEAD_DIM)
    vblk = jnp.pad(v, pad).reshape(B, nb + 2, BLOCK, N_KV_HEADS, HEAD_DIM)
    kb = jnp.concatenate([kblk[:, :-2], kblk[:, 1:-1], kblk[:, 2:]], axis=2).astype(jnp.float32)
    vb = jnp.concatenate([vblk[:, :-2], vblk[:, 1:-1], vblk[:, 2:]], axis=2).astype(jnp.float32)
    qb = q.reshape(B, nb, BLOCK, N_KV_HEADS, Q_PER_KV, HEAD_DIM).astype(jnp.float32)
    qpos = jnp.arange(L).reshape(nb, BLOCK)
    kpos = (jnp.arange(nb)[:, None] - 1) * BLOCK + jnp.arange(3 * BLOCK)[None]
    valid = (jnp.abs(qpos[:, :, None] - kpos[:, None, :]) <= WINDOW) & ((kpos >= 0) & (kpos < L))[:, None, :]
    s_loc = jnp.einsum('bnqkgd,bnskd->bnkgqs', qb, kb) * ATTN_SCALE
    s_loc = jnp.where(valid[None, :, None, None], s_loc, NEG)
    s_ctx = jnp.einsum('bnqkgd,bpkd->bnkgqp', qb, ctx_k.astype(jnp.float32)) * ATTN_SCALE
    p = sink_softmax(jnp.concatenate([s_loc, s_ctx], axis=-1), sink.astype(jnp.float32)[None, None, :, :, None, None])
    o = (jnp.einsum('bnkgqs,bnskd->bnqkgd', p[..., :3 * BLOCK], vb)
         + jnp.einsum('bnkgqp,bpkd->bnqkgd', p[..., 3 * BLOCK:], ctx_v.astype(jnp.float32)))
    return o.reshape(B, L, ATTN_WIDTH).astype(q.dtype)


def even_branches(h, w_in, conv_w, conv_b, q_norm_g, k_norm_g):
    B, L, _ = h.shape
    z = h @ w_in
    bg, cg, xs, ga, q, k, v, gb = jnp.split(z, EVEN_SPLITS, axis=-1)
    ya = bg * short_conv(cg * xs, conv_w, conv_b) * jax.nn.silu(ga)
    q = rmsnorm(q.reshape(B, L, N_Q_HEADS, HEAD_DIM), q_norm_g)
    k = rmsnorm(k.reshape(B, L, N_KV_HEADS, HEAD_DIM), k_norm_g)
    v = v.reshape(B, L, N_KV_HEADS, HEAD_DIM)
    return ya, q, k, v, jax.nn.silu(gb)


def multiscale_pool(u):
    L = u.shape[1]
    uf = u.astype(jnp.float32)
    cs = jnp.pad(jnp.cumsum(uf, axis=1), ((0, 0), (1, 0), (0, 0)))
    t = jnp.arange(L)
    outs = []
    for gi, w in enumerate(POOL_SIZES):
        lo = jnp.clip(t - w // 2, 0, L)
        hi = jnp.clip(t + w // 2, 0, L)
        seg = cs[:, :, gi * POOL_GROUP:(gi + 1) * POOL_GROUP]
        outs.append((seg[:, hi] - seg[:, lo]) / (hi - lo).astype(jnp.float32)[None, :, None])
    return (jnp.concatenate(outs, axis=-1) - uf).astype(u.dtype)


def pool_mixer(h, w_in, pool_w, pool_scale, w_out):
    B, L, _ = h.shape
    u, g = jnp.split(h @ w_in, 2, axis=-1)
    y = multiscale_pool(u).reshape(B, L, N_POOL_GROUPS, POOL_GROUP)
    y = jnp.einsum('blgc,gcd->blgd', y, pool_w).reshape(B, L, POOL_WIDTH) * pool_scale
    return (y * jax.nn.silu(g)) @ w_out


def setup_inputs(seed: int = 0) -> dict:
    key = jax.random.key(seed)
    ks = jax.random.split(key, 24)
    nrm = jax.random.normal
    D = D_MODEL
    return {
        "x_prompt": nrm(ks[0], (BATCH, SEQ, D), jnp.float32),
        "x_sample": nrm(ks[1], (DEC_BATCH, DEC_SEQ, D), jnp.float32),
        "cache_k": nrm(ks[2], (DEC_BATCH, N_EVEN, PAST_LEN, N_KV_HEADS, HEAD_DIM), jnp.float32),
        "cache_v": nrm(ks[3], (DEC_BATCH, N_EVEN, PAST_LEN, N_KV_HEADS, HEAD_DIM), jnp.float32),
        "c": nrm(ks[4], (DEC_BATCH, D), jnp.float32),
        "c_ctx": nrm(ks[5], (D,), jnp.float32),
        "ada_w_e": nrm(ks[6], (N_EVEN, D, 3 * D), jnp.float32) * (0.5 * D ** -0.5),
        "ada_b_e": nrm(ks[7], (N_EVEN, 3 * D), jnp.float32) * 0.01,
        "norm_g_e": 1.0 + 0.01 * nrm(ks[8], (N_EVEN, D), jnp.float32),
        "w_in_e": nrm(ks[9], (N_EVEN, D, EVEN_IN), jnp.float32) * D ** -0.5,
        "conv_w": nrm(ks[10], (N_EVEN, 3, CONV_WIDTH), jnp.float32) * 0.5,
        "conv_b": nrm(ks[11], (N_EVEN, CONV_WIDTH), jnp.float32) * 0.01,
        "q_norm_g": 1.0 + 0.01 * nrm(ks[12], (N_EVEN, HEAD_DIM), jnp.float32),
        "k_norm_g": 1.0 + 0.01 * nrm(ks[13], (N_EVEN, HEAD_DIM), jnp.float32),
        "sink": nrm(ks[14], (N_EVEN, N_Q_HEADS), jnp.float32) * 0.5,
        "w_out_e": nrm(ks[15], (N_EVEN, D, D), jnp.float32) * D ** -0.5,
        "ada_w_o": nrm(ks[16], (N_ODD, D, 3 * D), jnp.float32) * (0.5 * D ** -0.5),
        "ada_b_o": nrm(ks[17], (N_ODD, 3 * D), jnp.float32) * 0.01,
        "norm_g_o": 1.0 + 0.01 * nrm(ks[18], (N_ODD, D), jnp.float32),
        "w_in_o": nrm(ks[19], (N_ODD, D, ODD_IN), jnp.float32) * D ** -0.5,
        "pool_w": nrm(ks[20], (N_ODD, N_POOL_GROUPS, POOL_GROUP, POOL_GROUP), jnp.float32) * POOL_GROUP ** -0.5,
        "pool_scale": 1.0 + 0.1 * nrm(ks[21], (N_ODD, POOL_WIDTH), jnp.float32),
        "w_out_o": nrm(ks[22], (N_ODD, POOL_WIDTH, D), jnp.float32) * POOL_WIDTH ** -0.5,
    }


def reference(x_prompt, x_sample, cache_k, cache_v, c, c_ctx,
              ada_w_e, ada_b_e, norm_g_e, w_in_e, conv_w, conv_b, q_norm_g, k_norm_g, sink, w_out_e,
              ada_w_o, ada_b_o, norm_g_o, w_in_o, pool_w, pool_scale, w_out_o):
    n_rows = x_sample.shape[1] // GRID_W
    cos, sin = axial_rope_tables(n_rows)
    yp, ys = x_prompt, x_sample
    new_k, new_v = [], []
    for layer in range(DEPTH):
        i = layer // 2
        if layer % 2 == 0:
            sp, scp, gp = adaln(c_ctx[None], ada_w_e[i], ada_b_e[i])
            ss, scs, gs = adaln(c, ada_w_e[i], ada_b_e[i])
            snk = sink[i].reshape(N_KV_HEADS, Q_PER_KV)
            hp = modulate(yp, norm_g_e[i], sp, scp)
            ya, q, k, v, gb = even_branches(hp, w_in_e[i], conv_w[i], conv_b[i], q_norm_g[i], k_norm_g[i])
            ob = context_attention(q, k, v, snk)
            yp = yp + gp * (jnp.concatenate([ya, ob * gb], axis=-1) @ w_out_e[i])
            new_k.append(k)
            new_v.append(v)
            hs = modulate(ys, norm_g_e[i], ss, scs)
            ya, q, k, v, gb = even_branches(hs, w_in_e[i], conv_w[i], conv_b[i], q_norm_g[i], k_norm_g[i])
            q = apply_axial_rope(q, cos, sin)
            k = apply_axial_rope(k, cos, sin)
            ob = latent_attention(q, k, v, cache_k[:, i], cache_v[:, i], snk)
            ys = ys + gs * (jnp.concatenate([ya, ob * gb], axis=-1) @ w_out_e[i])
        else:
            sp, scp, gp = adaln(c_ctx[None], ada_w_o[i], ada_b_o[i])
            ss, scs, gs = adaln(c, ada_w_o[i], ada_b_o[i])
            hp = modulate(yp, norm_g_o[i], sp, scp)
            yp = yp + gp * pool_mixer(hp, w_in_o[i], pool_w[i], pool_scale[i], w_out_o[i])
            hs = modulate(ys, norm_g_o[i], ss, scs)
            ys = ys + gs * pool_mixer(hs, w_in_o[i], pool_w[i], pool_scale[i], w_out_o[i])
    new_cache_k = jnp.stack(new_k, axis=1)
    new_cache_v = jnp.stack(new_v, axis=1)
    return (yp, ys, new_cache_k, new_cache_v)
```

```python
import functools

import jax
import jax.numpy as jnp
import numpy as np
from jax import lax
from jax.experimental import pallas as pl
from jax.experimental.pallas import tpu as pltpu

F32 = jnp.float32
BF16 = jnp.bfloat16

D_MODEL = 1024
GRID_W = 64
HEAD_DIM = 64
HEAD_SHIFT = 6
ATTN_WIDTH = D_MODEL // 2
N_Q_HEADS = ATTN_WIDTH // HEAD_DIM
N_KV_HEADS = N_Q_HEADS // 4
Q_PER_KV = N_Q_HEADS // N_KV_HEADS
KV_WIDTH = N_KV_HEADS * HEAD_DIM
CONV_WIDTH = D_MODEL - ATTN_WIDTH
WINDOW = 128
BLOCK = 128
ROPE_BASE = 10000.0
ROPE_FREQS = HEAD_DIM // 4
ATTN_SCALE = HEAD_DIM ** -0.5
NEG = -1e30
POOL_SIZES = (2, 4, 8, 16)
POOL_GROUP = D_MODEL // len(POOL_SIZES)
EPS = 1e-6
EVEN_SIZES = (CONV_WIDTH, CONV_WIDTH, CONV_WIDTH, CONV_WIDTH, ATTN_WIDTH, KV_WIDTH, KV_WIDTH, ATTN_WIDTH)
EVEN_IN = sum(EVEN_SIZES)
EVEN_OFFS = tuple(int(s) for s in np.cumsum((0,) + EVEN_SIZES))

TM = 256
LANES = 128
GROUP_W = Q_PER_KV * HEAD_DIM
HALO = 8
EXT = TM + 2 * HALO
ADA_TN = 768
VMEM_LIMIT = 56 * 1024 * 1024


def _silu(x):
    return x / (1.0 + jnp.exp(-x))


def _rows_down(x, k):
    return pltpu.roll(x, k, axis=0)


def _rows_up(x, k):
    return pltpu.roll(x, x.shape[0] - k, axis=0)


def _rms_modulate(x, g, shift, scale):
    ms = jnp.mean(x * x, axis=-1, keepdims=True)
    return (x * lax.rsqrt(ms + EPS) * g) * (1.0 + scale) + shift


def _adaln_kernel(cond_ref, we_ref, be_ref, wo_ref, bo_ref, oe_ref, oo_ref):
    s = _silu(cond_ref[...]).astype(BF16)
    oe_ref[...] = jnp.dot(s, we_ref[...].astype(BF16), preferred_element_type=F32) + be_ref[...]
    oo_ref[...] = jnp.dot(s, wo_ref[...].astype(BF16), preferred_element_type=F32) + bo_ref[...]


def _adaln(cond8, w_e, b_e, w_o, b_o):
    n = 3 * D_MODEL
    wspec = pl.BlockSpec((D_MODEL, ADA_TN), lambda j: (0, j))
    vspec = pl.BlockSpec((1, ADA_TN), lambda j: (0, j))
    ospec = pl.BlockSpec((8, ADA_TN), lambda j: (0, j))
    return pl.pallas_call(
        _adaln_kernel,
        out_shape=(jax.ShapeDtypeStruct((8, n), F32), jax.ShapeDtypeStruct((8, n), F32)),
        grid=(n // ADA_TN,),
        in_specs=[pl.BlockSpec((8, D_MODEL), lambda j: (0, 0)), wspec, vspec, wspec, vspec],
        out_specs=(ospec, ospec),
        compiler_params=pltpu.CompilerParams(dimension_semantics=("arbitrary",), vmem_limit_bytes=VMEM_LIMIT),
        name="adaln",
    )(cond8, w_e, b_e.reshape(1, n), w_o, b_o.reshape(1, n))


def _head_rms(x, g, bd):
    sq = x * x
    hi = sq.astype(BF16)
    lo = (sq - hi.astype(F32)).astype(BF16)
    ms = (jnp.dot(hi, bd, preferred_element_type=F32) + jnp.dot(lo, bd, preferred_element_type=F32)) * (1.0 / HEAD_DIM)
    return x * lax.rsqrt(ms + EPS) * g


def _rope(x, cos, sin_signed, first_half):
    partner = jnp.where(first_half, pltpu.roll(x, LANES - ROPE_FREQS, axis=1), pltpu.roll(x, ROPE_FREQS, axis=1))
    return x * cos + partner * sin_signed


def _tile_kv(x):
    lane = lax.broadcasted_iota(jnp.int32, (1, LANES), 1)
    swapped = pltpu.roll(x, HEAD_DIM, axis=1)
    a = jnp.where(lane < HEAD_DIM, x, swapped)
    b = jnp.where(lane < HEAD_DIM, swapped, x)
    return jnp.concatenate([a, a], axis=1), jnp.concatenate([b, b], axis=1)


def _even_kernel(*refs, seq_len, windowed):
    if windowed:
        (x_ref, mod_ref, g_ref, win_ref, cw_ref, cb_ref, qg_ref, kg_ref, sink_ref, wout_ref,
         cos_ref, sin_ref, ck_ref, cv_ref,
         y_ref,
         u_s, gate_s, q_s, k_s, v_s, gb_s, x_s, a_s, ck_s, cv_s) = refs
        ko_ref = vo_ref = None
        kpad = BLOCK
    else:
        (x_ref, mod_ref, g_ref, win_ref, cw_ref, cb_ref, qg_ref, kg_ref, sink_ref, wout_ref,
         y_ref, ko_ref, vo_ref,
         u_s, gate_s, q_s, k_s, v_s, gb_s, x_s, a_s) = refs
        kpad = 0
    nc = seq_len // TM
    c = pl.program_id(1)
    shift = mod_ref[0, :, 0:D_MODEL]
    scale = mod_ref[0, :, D_MODEL:2 * D_MODEL]
    gate = mod_ref[0, :, 2 * D_MODEL:3 * D_MODEL]

    @pl.when((pl.program_id(0) == 0) & (c == 0))
    def _zero_pads():
        zu = jnp.zeros((HALO, CONV_WIDTH), F32)
        u_s[0:HALO, :] = zu
        u_s[HALO + seq_len:2 * HALO + seq_len, :] = zu
        if windowed:
            zk = jnp.zeros((BLOCK, 2 * GROUP_W), BF16)
            k_s[0:BLOCK, :] = zk
            v_s[0:BLOCK, :] = zk
            k_s[BLOCK + seq_len:2 * BLOCK + seq_len, :] = zk
            v_s[BLOCK + seq_len:2 * BLOCK + seq_len, :] = zk

    if windowed:
        @pl.when(c == 0)
        def _load_ctx():
            ka, kb = _tile_kv(ck_ref[0])
            va, vb = _tile_kv(cv_ref[0])
            ck_s[:, 0:GROUP_W] = ka.astype(BF16)
            ck_s[:, GROUP_W:2 * GROUP_W] = kb.astype(BF16)
            cv_s[:, 0:GROUP_W] = va.astype(BF16)
            cv_s[:, GROUP_W:2 * GROUP_W] = vb.astype(BF16)

    @pl.when(c < nc)
    def _project():
        r0 = pl.multiple_of(c * TM, TM)
        x = x_ref[0]
        x_s[c % 2] = x
        h = _rms_modulate(x, g_ref[...], shift, scale).astype(BF16)

        def proj(i):
            return jnp.dot(h, win_ref[:, EVEN_OFFS[i]:EVEN_OFFS[i + 1]], preferred_element_type=F32)

        gate_s[pl.ds(r0, TM), :] = proj(0) * _silu(proj(3))
        u_s[pl.ds(HALO + r0, TM), :] = proj(1) * proj(2)
        gb_s[pl.ds(r0, TM), :] = _silu(proj(7))

        lane = lax.broadcasted_iota(jnp.int32, (1, LANES), 1)
        first_half = (lane & (2 * ROPE_FREQS - 1)) < ROPE_FREQS
        ri = lax.broadcasted_iota(jnp.int32, (LANES, LANES), 0) >> HEAD_SHIFT
        ci = lax.broadcasted_iota(jnp.int32, (LANES, LANES), 1) >> HEAD_SHIFT
        bd = (ri == ci).astype(BF16)
        if windowed:
            cos = cos_ref[...]
            sin = sin_ref[...]

        q = proj(4)
        for j in range(ATTN_WIDTH // LANES):
            qj = _head_rms(q[:, j * LANES:(j + 1) * LANES], qg_ref[...], bd)
            if windowed:
                qj = _rope(qj, cos, sin, first_half)
            q_s[pl.ds(r0, TM), j * LANES:(j + 1) * LANES] = (qj * ATTN_SCALE).astype(BF16)

        k = _head_rms(proj(5), kg_ref[...], bd)
        v = proj(6)
        if windowed:
            k = _rope(k, cos, sin, first_half)
        else:
            ko_ref[0] = k
            vo_ref[0] = v
        ka, kb = _tile_kv(k)
        va, vb = _tile_kv(v)
        k_s[pl.ds(kpad + r0, TM), 0:GROUP_W] = ka.astype(BF16)
        k_s[pl.ds(kpad + r0, TM), GROUP_W:2 * GROUP_W] = kb.astype(BF16)
        v_s[pl.ds(kpad + r0, TM), 0:GROUP_W] = va.astype(BF16)
        v_s[pl.ds(kpad + r0, TM), GROUP_W:2 * GROUP_W] = vb.astype(BF16)

    @pl.when(c >= 1)
    def _mix():
        m = c - 1
        r0 = pl.multiple_of(m * TM, TM)
        ext = u_s[pl.ds(r0, EXT), :]
        conv = (_rows_down(ext, 1)[HALO:HALO + TM] * cw_ref[0:1, :]
                + ext[HALO:HALO + TM] * cw_ref[1:2, :]
                + _rows_up(ext, 1)[HALO:HALO + TM] * cw_ref[2:3, :]
                + cb_ref[...])
        a_s[:, 0:CONV_WIDTH] = (gate_s[pl.ds(r0, TM), :] * conv).astype(BF16)

        lane_head = lax.broadcasted_iota(jnp.int32, (1, GROUP_W), 1) >> HEAD_SHIFT
        rows = Q_PER_KV * BLOCK
        for qb in range(TM // BLOCK):
            q0 = pl.multiple_of(r0 + qb * BLOCK, BLOCK)
            if windowed:
                row = lax.broadcasted_iota(jnp.int32, (rows, 3 * BLOCK), 0) & (BLOCK - 1)
                col = lax.broadcasted_iota(jnp.int32, (rows, 3 * BLOCK), 1)
                kpos = col + (q0 - BLOCK)
                valid = (jnp.abs(col - BLOCK - row) <= WINDOW) & (kpos >= 0) & (kpos < seq_len)
            for kv in range(N_KV_HEADS):
                cols = slice(kv * GROUP_W, (kv + 1) * GROUP_W)
                qblk = q_s[pl.ds(q0, BLOCK), cols]
                q4 = jnp.concatenate(
                    [jnp.where(lane_head == g, qblk, jnp.zeros_like(qblk)) for g in range(Q_PER_KV)], axis=0)
                if windowed:
                    keys = [(k_s[pl.ds(q0, 3 * BLOCK), cols], v_s[pl.ds(q0, 3 * BLOCK), cols], valid),
                            (ck_s[:, cols], cv_s[:, cols], None)]
                else:
                    keys = [(k_s[:, cols], v_s[:, cols], None)]
                snk = jnp.concatenate(
                    [jnp.full((BLOCK, 1), sink_ref[kv * Q_PER_KV + g], F32) for g in range(Q_PER_KV)], axis=0)
                scores = []
                mx = snk
                for kk, _, msk in keys:
                    s = lax.dot_general(q4, kk, (((1,), (1,)), ((), ())), preferred_element_type=F32)
                    if msk is not None:
                        s = jnp.where(msk, s, NEG)
                    scores.append(s)
                    mx = jnp.maximum(mx, jnp.max(s, axis=1, keepdims=True))
                den = jnp.exp(snk - mx)
                o = jnp.zeros((BLOCK, GROUP_W), F32)
                for s, (_, vv, _) in zip(scores, keys):
                    e = jnp.exp(s - mx)
                    den = den + jnp.sum(e, axis=1, keepdims=True)
                    e = e.astype(BF16)
                    for g in range(Q_PER_KV):
                        vg = jnp.where(lane_head == g, vv, jnp.zeros_like(vv))
                        o = o + jnp.dot(e[g * BLOCK:(g + 1) * BLOCK], vg, preferred_element_type=F32)
                inv = 1.0 / den
                norm = jnp.zeros((BLOCK, GROUP_W), F32)
                for g in range(Q_PER_KV):
                    norm = jnp.where(lane_head == g, inv[g * BLOCK:(g + 1) * BLOCK], norm)
                gb = gb_s[pl.ds(q0, BLOCK), cols]
                a_s[qb * BLOCK:(qb + 1) * BLOCK, CONV_WIDTH + kv * GROUP_W:CONV_WIDTH + (kv + 1) * GROUP_W] = (
                    o * norm * gb).astype(BF16)

        mixed = jnp.dot(a_s[...], wout_ref[...], preferred_element_type=F32)
        y_ref[0] = x_s[m % 2] + gate * mixed


def _const_spec(shape):
    return pl.BlockSpec(shape, lambda b, c: (0,) * len(shape))


def _even_layer(x, mod, norm_g, w_in, conv_w, conv_b, q_g, k_g, sink, w_out, rope=None, ctx=None):
    n_seq, seq_len, _ = x.shape
    windowed = rope is not None
    nc = seq_len // TM
    last = nc - 1
    chunk_in = lambda b, c: (b, jnp.minimum(c, last), 0)
    chunk_out = lambda b, c: (b, jnp.maximum(c - 1, 0), 0)
    in_specs = [
        pl.BlockSpec((1, TM, D_MODEL), chunk_in),
        pl.BlockSpec((1, 1, 3 * D_MODEL), lambda b, c: (b if windowed else 0, 0, 0)),
        _const_spec((1, D_MODEL)),
        _const_spec((D_MODEL, EVEN_IN)),
        _const_spec((3, CONV_WIDTH)),
        _const_spec((1, CONV_WIDTH)),
        _const_spec((1, LANES)),
        _const_spec((1, LANES)),
        pl.BlockSpec(memory_space=pltpu.SMEM),
        _const_spec((D_MODEL, D_MODEL)),
    ]
    args = [x, mod, norm_g.reshape(1, D_MODEL), w_in, conv_w, conv_b.reshape(1, CONV_WIDTH),
            jnp.tile(q_g, LANES // HEAD_DIM).reshape(1, LANES), jnp.tile(k_g, LANES // HEAD_DIM).reshape(1, LANES),
            sink, w_out]
    kv_rows = seq_len + (2 * BLOCK if windowed else 0)
    scratch = [
        pltpu.VMEM((seq_len + 2 * HALO, CONV_WIDTH), F32),
        pltpu.VMEM((seq_len, CONV_WIDTH), F32),
        pltpu.VMEM((seq_len, ATTN_WIDTH), BF16),
        pltpu.VMEM((kv_rows, 2 * GROUP_W), BF16),
        pltpu.VMEM((kv_rows, 2 * GROUP_W), BF16),
        pltpu.VMEM((seq_len, ATTN_WIDTH), F32),
        pltpu.VMEM((2, TM, D_MODEL), F32),
        pltpu.VMEM((TM, D_MODEL), BF16),
    ]
    y_shape = jax.ShapeDtypeStruct(x.shape, F32)
    y_spec = pl.BlockSpec((1, TM, D_MODEL), chunk_out)
    if windowed:
        cos, sin = rope
        ck, cv = ctx
        in_specs += [pl.BlockSpec((TM, LANES), lambda b, c: (jnp.minimum(c, last), 0)),
                     pl.BlockSpec((TM, LANES), lambda b, c: (jnp.minimum(c, last), 0)),
                     pl.BlockSpec((1,) + ck.shape[1:], lambda b, c: (b, 0, 0)),
                     pl.BlockSpec((1,) + cv.shape[1:], lambda b, c: (b, 0, 0))]
        args += [cos, sin, ck, cv]
        scratch += [pltpu.VMEM((ck.shape[1], 2 * GROUP_W), BF16), pltpu.VMEM((cv.shape[1], 2 * GROUP_W), BF16)]
        out_shape, out_specs = y_shape, y_spec
    else:
        kv_shape = jax.ShapeDtypeStruct((n_seq, seq_len, KV_WIDTH), F32)
        kv_spec = pl.BlockSpec((1, TM, KV_WIDTH), chunk_in)
        out_shape, out_specs = (y_shape, kv_shape, kv_shape), (y_spec, kv_spec, kv_spec)
    return pl.pallas_call(
        functools.partial(_even_kernel, seq_len=seq_len, windowed=windowed),
        out_shape=out_shape,
        grid=(n_seq, nc + 1),
        in_specs=in_specs,
        out_specs=out_specs,
        scratch_shapes=scratch,
        compiler_params=pltpu.CompilerParams(dimension_semantics=("arbitrary", "arbitrary"),
                                             vmem_limit_bytes=VMEM_LIMIT),
        name="even_latent" if windowed else "even_context",
    )(*args)


def _odd_kernel(x_ref, mod_ref, g_ref, win_ref, pw_ref, ps_ref, wout_ref, y_ref, u_s, sg_s, x_s, *, seq_len):
    nc = seq_len // TM
    c = pl.program_id(1)
    shift = mod_ref[0, :, 0:D_MODEL]
    scale = mod_ref[0, :, D_MODEL:2 * D_MODEL]
    gate = mod_ref[0, :, 2 * D_MODEL:3 * D_MODEL]

    @pl.when((pl.program_id(0) == 0) & (c == 0))
    def _zero_pads():
        zu = jnp.zeros((HALO, D_MODEL), F32)
        u_s[0:HALO, :] = zu
        u_s[HALO + seq_len:2 * HALO + seq_len, :] = zu

    @pl.when(c < nc)
    def _project():
        r0 = pl.multiple_of(c * TM, TM)
        x = x_ref[0]
        x_s[c % 2] = x
        h = _rms_modulate(x, g_ref[...], shift, scale).astype(BF16)
        u_s[pl.ds(HALO + r0, TM), :] = jnp.dot(h, win_ref[:, 0:D_MODEL], preferred_element_type=F32)
        sg_s[pl.ds(r0, TM), :] = _silu(jnp.dot(h, win_ref[:, D_MODEL:2 * D_MODEL], preferred_element_type=F32))

    @pl.when(c >= 1)
    def _mix():
        m = c - 1
        r0 = pl.multiple_of(m * TM, TM)
        t = r0 + lax.broadcasted_iota(jnp.int32, (TM, 1), 0)
        outs = []
        for gi, w in enumerate(POOL_SIZES):
            cols = slice(gi * POOL_GROUP, (gi + 1) * POOL_GROUP)
            ext = u_s[pl.ds(r0, EXT), cols]
            acc = ext + _rows_down(ext, 1)
            half = 1
            while 2 * half < w:
                acc = _rows_down(acc, half) + _rows_up(acc, half)
                half *= 2
            cnt = (jnp.minimum(t + w // 2, seq_len) - jnp.maximum(t - w // 2, 0)).astype(F32)
            pooled = acc[HALO:HALO + TM] / cnt - ext[HALO:HALO + TM]
            outs.append(jnp.dot(pooled.astype(BF16), pw_ref[gi], preferred_element_type=F32))
        y = jnp.concatenate(outs, axis=1) * ps_ref[...] * sg_s[pl.ds(r0, TM), :]
        mixed = jnp.dot(y.astype(BF16), wout_ref[...], preferred_element_type=F32)
        y_ref[0] = x_s[m % 2] + gate * mixed


def _odd_layer(x, mod, per_seq_mod, norm_g, w_in, pool_w, pool_scale, w_out):
    n_seq, seq_len, _ = x.shape
    nc = seq_len // TM
    last = nc - 1
    in_specs = [
        pl.BlockSpec((1, TM, D_MODEL), lambda b, c: (b, jnp.minimum(c, last), 0)),
        pl.BlockSpec((1, 1, 3 * D_MODEL), lambda b, c: (b if per_seq_mod else 0, 0, 0)),
        _const_spec((1, D_MODEL)),
        _const_spec((D_MODEL, 2 * D_MODEL)),
        _const_spec(pool_w.shape),
        _const_spec((1, D_MODEL)),
        _const_spec((D_MODEL, D_MODEL)),
    ]
    return pl.pallas_call(
        functools.partial(_odd_kernel, seq_len=seq_len),
        out_shape=jax.ShapeDtypeStruct(x.shape, F32),
        grid=(n_seq, nc + 1),
        in_specs=in_specs,
        out_specs=pl.BlockSpec((1, TM, D_MODEL), lambda b, c: (b, jnp.maximum(c - 1, 0), 0)),
        scratch_shapes=[
            pltpu.VMEM((seq_len + 2 * HALO, D_MODEL), F32),
            pltpu.VMEM((seq_len, D_MODEL), F32),
            pltpu.VMEM((2, TM, D_MODEL), F32),
        ],
        compiler_params=pltpu.CompilerParams(dimension_semantics=("arbitrary", "arbitrary"),
                                             vmem_limit_bytes=VMEM_LIMIT),
        name="odd_latent" if per_seq_mod else "odd_context",
    )(x, mod, norm_g.reshape(1, D_MODEL), w_in, pool_w, pool_scale.reshape(1, D_MODEL), w_out)


def _rope_tables(seq_len):
    n_rows = seq_len // GRID_W
    row = jnp.repeat(jnp.arange(n_rows), GRID_W).astype(F32)
    col = jnp.tile(jnp.arange(GRID_W), n_rows).astype(F32)
    inv = ROPE_BASE ** (-jnp.arange(ROPE_FREQS, dtype=F32) / ROPE_FREQS)
    ang = jnp.stack([row[:, None] * inv, col[:, None] * inv], axis=1)
    cos = jnp.broadcast_to(jnp.cos(ang)[:, :, None, :], (seq_len, 2, 2, ROPE_FREQS))
    sin = jnp.sin(ang)[:, :, None, :] * jnp.array([-1.0, 1.0], F32)[None, None, :, None]
    cos = cos.reshape(seq_len, HEAD_DIM)
    sin = sin.reshape(seq_len, HEAD_DIM)
    return jnp.tile(cos, (1, LANES // HEAD_DIM)), jnp.tile(sin, (1, LANES // HEAD_DIM))


def kernel(x_prompt, x_sample, cache_k, cache_v, c, c_ctx, ada_w_e, ada_b_e, norm_g_e, w_in_e, conv_w, conv_b,
           q_norm_g, k_norm_g, sink, w_out_e, ada_w_o, ada_b_o, norm_g_o, w_in_o, pool_w, pool_scale, w_out_o):
    n_dec = x_sample.shape[0]
    depth = ada_w_e.shape[0] + ada_w_o.shape[0]
    assert ada_w_e.shape[0] == 1 and ada_w_o.shape[0] == 1 and n_dec + 1 <= 8
    cond8 = jnp.concatenate([c_ctx[None], c, jnp.zeros((8 - 1 - n_dec, D_MODEL), F32)], axis=0)
    mod_e, mod_o = _adaln(cond8, ada_w_e[0], ada_b_e[0], ada_w_o[0], ada_b_o[0])
    rope = _rope_tables(x_sample.shape[1])

    yp, ys = x_prompt, x_sample
    new_k, new_v = [], []
    for layer in range(depth):
        i = layer // 2
        if layer % 2 == 0:
            mod = mod_e.reshape(8, 1, 3 * D_MODEL)
            w_in = w_in_e[i].astype(BF16)
            w_out = w_out_e[i].astype(BF16)
            common = (norm_g_e[i], w_in, conv_w[i], conv_b[i], q_norm_g[i], k_norm_g[i], sink[i], w_out)
            yp, k, v = _even_layer(yp, mod[0:1], *common)
            new_k.append(k.reshape(k.shape[0], k.shape[1], N_KV_HEADS, HEAD_DIM))
            new_v.append(v.reshape(v.shape[0], v.shape[1], N_KV_HEADS, HEAD_DIM))
            ck = cache_k[:, i].reshape(n_dec, cache_k.shape[2], KV_WIDTH)
            cv = cache_v[:, i].reshape(n_dec, cache_v.shape[2], KV_WIDTH)
            ys = _even_layer(ys, mod[1:1 + n_dec], *common, rope=rope, ctx=(ck, cv))
        else:
            mod = mod_o.reshape(8, 1, 3 * D_MODEL)
            common = (norm_g_o[i], w_in_o[i].astype(BF16), pool_w[i].astype(BF16), pool_scale[i],
                      w_out_o[i].astype(BF16))
            yp = _odd_layer(yp, mod[0:1], False, *common)
            ys = _odd_layer(ys, mod[1:1 + n_dec], True, *common)
    return yp, ys, jnp.stack(new_k, axis=1), jnp.stack(new_v, axis=1)
```

```python
import functools

import jax
import jax.numpy as jnp
import numpy as np
from jax import lax
from jax.experimental import pallas as pl
from jax.experimental.pallas import tpu as pltpu

F32 = jnp.float32
BF16 = jnp.bfloat16

D_MODEL = 1024
GRID_W = 64
HEAD_DIM = 64
HEAD_SHIFT = 6
ATTN_WIDTH = D_MODEL // 2
N_Q_HEADS = ATTN_WIDTH // HEAD_DIM
N_KV_HEADS = N_Q_HEADS // 4
Q_PER_KV = N_Q_HEADS // N_KV_HEADS
KV_WIDTH = N_KV_HEADS * HEAD_DIM
CONV_WIDTH = D_MODEL - ATTN_WIDTH
WINDOW = 128
BLOCK = 128
ROPE_BASE = 10000.0
ROPE_FREQS = HEAD_DIM // 4
ATTN_SCALE = HEAD_DIM ** -0.5
NEG = -1e30
POOL_SIZES = (2, 4, 8, 16)
POOL_GROUP = D_MODEL // len(POOL_SIZES)
EPS = 1e-6
EVEN_SIZES = (CONV_WIDTH, CONV_WIDTH, CONV_WIDTH, CONV_WIDTH, ATTN_WIDTH, KV_WIDTH, KV_WIDTH, ATTN_WIDTH)
EVEN_IN = sum(EVEN_SIZES)
EVEN_OFFS = tuple(int(s) for s in np.cumsum((0,) + EVEN_SIZES))

TM = 256
LANES = 128
GROUP_W = Q_PER_KV * HEAD_DIM
HALO = 8
EXT = TM + 2 * HALO
ADA_TN = 768
VMEM_LIMIT = 56 * 1024 * 1024


def _silu(x):
    return x / (1.0 + jnp.exp(-x))


def _rows_down(x, k):
    return pltpu.roll(x, k, axis=0)


def _rows_up(x, k):
    return pltpu.roll(x, x.shape[0] - k, axis=0)


def _rms_modulate(x, g, shift, scale):
    ms = jnp.mean(x * x, axis=-1, keepdims=True)
    return (x * lax.rsqrt(ms + EPS) * g) * (1.0 + scale) + shift


def _adaln_kernel(cond_ref, we_ref, be_ref, wo_ref, bo_ref, oe_ref, oo_ref):
    s = _silu(cond_ref[...]).astype(BF16)
    oe_ref[...] = jnp.dot(s, we_ref[...].astype(BF16), preferred_element_type=F32) + be_ref[...]
    oo_ref[...] = jnp.dot(s, wo_ref[...].astype(BF16), preferred_element_type=F32) + bo_ref[...]


def _adaln(cond8, w_e, b_e, w_o, b_o):
    n = 3 * D_MODEL
    wspec = pl.BlockSpec((D_MODEL, ADA_TN), lambda j: (0, j))
    vspec = pl.BlockSpec((1, ADA_TN), lambda j: (0, j))
    ospec = pl.BlockSpec((8, ADA_TN), lambda j: (0, j))
    return pl.pallas_call(
        _adaln_kernel,
        out_shape=(jax.ShapeDtypeStruct((8, n), F32), jax.ShapeDtypeStruct((8, n), F32)),
        grid=(n // ADA_TN,),
        in_specs=[pl.BlockSpec((8, D_MODEL), lambda j: (0, 0)), wspec, vspec, wspec, vspec],
        out_specs=(ospec, ospec),
        compiler_params=pltpu.CompilerParams(dimension_semantics=("arbitrary",), vmem_limit_bytes=VMEM_LIMIT),
        name="adaln",
    )(cond8, w_e, b_e.reshape(1, n), w_o, b_o.reshape(1, n))


def _head_rms(x, g, bd):
    sq = x * x
    hi = sq.astype(BF16)
    lo = (sq - hi.astype(F32)).astype(BF16)
    ms = (jnp.dot(hi, bd, preferred_element_type=F32) + jnp.dot(lo, bd, preferred_element_type=F32)) * (1.0 / HEAD_DIM)
    return x * lax.rsqrt(ms + EPS) * g


def _rope(x, cos, sin_signed, first_half):
    partner = jnp.where(first_half, pltpu.roll(x, LANES - ROPE_FREQS, axis=1), pltpu.roll(x, ROPE_FREQS, axis=1))
    return x * cos + partner * sin_signed


def _dup_kv(x):
    lane = lax.broadcasted_iota(jnp.int32, (1, LANES), 1)
    swapped = pltpu.roll(x, HEAD_DIM, axis=1)
    return jnp.where(lane < HEAD_DIM, x, swapped), jnp.where(lane < HEAD_DIM, swapped, x)


def _store_keys(k_s, rows, k):
    ka, kb = _dup_kv(k)
    k_s[rows, :] = jnp.concatenate([ka, ka, kb, kb], axis=1).astype(BF16)


def _store_values_t(vt_s, cols, v):
    va, vb = _dup_kv(v)
    vt_s[0:LANES, cols] = va.T.astype(BF16)
    vt_s[LANES:2 * LANES, cols] = vb.T.astype(BF16)


def _even_kernel(*refs, seq_len, windowed):
    if windowed:
        (x_ref, mod_ref, g_ref, win_ref, cw_ref, cb_ref, qg_ref, kg_ref, sink_ref, wout_ref,
         cos_ref, sin_ref, ck_ref, cv_ref,
         y_ref,
         u_s, gate_s, q_s, k_s, v_s, gb_s, x_s, a_s, ck_s, cv_s) = refs
        ko_ref = vo_ref = None
        kpad = BLOCK
    else:
        (x_ref, mod_ref, g_ref, win_ref, cw_ref, cb_ref, qg_ref, kg_ref, sink_ref, wout_ref,
         y_ref, ko_ref, vo_ref,
         u_s, gate_s, q_s, k_s, v_s, gb_s, x_s, a_s) = refs
        kpad = 0
    nc = seq_len // TM
    c = pl.program_id(1)
    shift = mod_ref[0, :, 0:D_MODEL]
    scale = mod_ref[0, :, D_MODEL:2 * D_MODEL]
    gate = mod_ref[0, :, 2 * D_MODEL:3 * D_MODEL]

    @pl.when((pl.program_id(0) == 0) & (c == 0))
    def _zero_pads():
        zu = jnp.zeros((HALO, CONV_WIDTH), F32)
        u_s[0:HALO, :] = zu
        u_s[HALO + seq_len:2 * HALO + seq_len, :] = zu
        if windowed:
            zk = jnp.zeros((BLOCK, 2 * GROUP_W), BF16)
            k_s[0:BLOCK, :] = zk
            k_s[BLOCK + seq_len:2 * BLOCK + seq_len, :] = zk
            zv = jnp.zeros((2 * LANES, BLOCK), BF16)
            v_s[:, 0:BLOCK] = zv
            v_s[:, BLOCK + seq_len:2 * BLOCK + seq_len] = zv

    if windowed:
        @pl.when(c == 0)
        def _load_ctx():
            _store_keys(ck_s, slice(None), ck_ref[0])
            _store_values_t(cv_s, slice(None), cv_ref[0])

    @pl.when(c < nc)
    def _project():
        r0 = pl.multiple_of(c * TM, TM)
        x = x_ref[0]
        x_s[c % 2] = x
        h = _rms_modulate(x, g_ref[...], shift, scale).astype(BF16)

        def proj(i):
            return jnp.dot(h, win_ref[:, EVEN_OFFS[i]:EVEN_OFFS[i + 1]], preferred_element_type=F32)

        gate_s[pl.ds(r0, TM), :] = proj(0) * _silu(proj(3))
        u_s[pl.ds(HALO + r0, TM), :] = proj(1) * proj(2)
        gb_s[pl.ds(r0, TM), :] = _silu(proj(7))

        lane = lax.broadcasted_iota(jnp.int32, (1, LANES), 1)
        first_half = (lane & (2 * ROPE_FREQS - 1)) < ROPE_FREQS
        ri = lax.broadcasted_iota(jnp.int32, (LANES, LANES), 0) >> HEAD_SHIFT
        ci = lax.broadcasted_iota(jnp.int32, (LANES, LANES), 1) >> HEAD_SHIFT
        bd = (ri == ci).astype(BF16)
        if windowed:
            cos = cos_ref[...]
            sin = sin_ref[...]

        q = proj(4)
        for j in range(ATTN_WIDTH // LANES):
            qj = _head_rms(q[:, j * LANES:(j + 1) * LANES], qg_ref[...], bd)
            if windowed:
                qj = _rope(qj, cos, sin, first_half)
            q_s[pl.ds(r0, TM), j * LANES:(j + 1) * LANES] = (qj * ATTN_SCALE).astype(BF16)

        k = _head_rms(proj(5), kg_ref[...], bd)
        v = proj(6)
        if windowed:
            k = _rope(k, cos, sin, first_half)
        else:
            ko_ref[0] = k
            vo_ref[0] = v
        _store_keys(k_s, pl.ds(kpad + r0, TM), k)
        _store_values_t(v_s, pl.ds(kpad + r0, TM), v)

    @pl.when(c >= 1)
    def _mix():
        m = c - 1
        r0 = pl.multiple_of(m * TM, TM)
        ext = u_s[pl.ds(r0, EXT), :]
        conv = (_rows_down(ext, 1)[HALO:HALO + TM] * cw_ref[0:1, :]
                + ext[HALO:HALO + TM] * cw_ref[1:2, :]
                + _rows_up(ext, 1)[HALO:HALO + TM] * cw_ref[2:3, :]
                + cb_ref[...])
        a_s[:, 0:CONV_WIDTH] = (gate_s[pl.ds(r0, TM), :] * conv).astype(BF16)

        lane_head = lax.broadcasted_iota(jnp.int32, (1, GROUP_W), 1) >> HEAD_SHIFT
        cols4 = Q_PER_KV * BLOCK
        key_i = lax.broadcasted_iota(jnp.int32, (BLOCK, cols4), 0)
        qry_i = lax.broadcasted_iota(jnp.int32, (BLOCK, cols4), 1) & (BLOCK - 1)
        upper = lax.broadcasted_iota(jnp.int32, (LANES, LANES), 0) < HEAD_DIM
        for qb in range(TM // BLOCK):
            q0 = pl.multiple_of(r0 + qb * BLOCK, BLOCK)
            if windowed:
                prev_ok = (key_i - qry_i) >= jnp.where(q0 > 0, 0, BLOCK)
                next_ok = (qry_i - key_i) >= jnp.where(q0 + BLOCK < seq_len, 0, BLOCK)
            for kv in range(N_KV_HEADS):
                cols = slice(kv * GROUP_W, (kv + 1) * GROUP_W)
                vrows = slice(kv * LANES, (kv + 1) * LANES)
                qblk = q_s[pl.ds(q0, BLOCK), cols]
                q4 = jnp.concatenate(
                    [jnp.where(lane_head == g, qblk, jnp.zeros_like(qblk)) for g in range(Q_PER_KV)], axis=0)

                def scores_t(kk):
                    return lax.dot_general(kk, q4, (((1,), (1,)), ((), ())), preferred_element_type=F32)

                if windowed:
                    s_loc = scores_t(k_s[pl.ds(q0, 3 * BLOCK), cols])
                    groups = [([jnp.where(prev_ok, s_loc[0:BLOCK], NEG), s_loc[BLOCK:2 * BLOCK],
                                jnp.where(next_ok, s_loc[2 * BLOCK:3 * BLOCK], NEG)],
                               v_s[vrows, pl.ds(q0, 3 * BLOCK)]),
                              ([scores_t(ck_s[:, cols])], cv_s[vrows, :])]
                else:
                    groups = [([scores_t(k_s[:, cols])], v_s[vrows, :])]
                snk = jnp.concatenate(
                    [jnp.full((1, BLOCK), sink_ref[kv * Q_PER_KV + g], F32) for g in range(Q_PER_KV)], axis=1)
                mx = snk
                for ss, _ in groups:
                    for s in ss:
                        mx = jnp.maximum(mx, jnp.max(s, axis=0, keepdims=True))
                den = jnp.exp(snk - mx)
                o_t = jnp.zeros((LANES, cols4), F32)
                for ss, vt in groups:
                    es = []
                    for s in ss:
                        e = jnp.exp(s - mx)
                        den = den + jnp.sum(e, axis=0, keepdims=True)
                        es.append(e.astype(BF16))
                    e_all = es[0] if len(es) == 1 else jnp.concatenate(es, axis=0)
                    o_t = o_t + jnp.dot(vt, e_all, preferred_element_type=F32)
                o_t = o_t * (1.0 / den)
                o01 = jnp.where(upper, o_t[:, 0:BLOCK], o_t[:, BLOCK:2 * BLOCK]).T
                o23 = jnp.where(upper, o_t[:, 2 * BLOCK:3 * BLOCK], o_t[:, 3 * BLOCK:4 * BLOCK]).T
                gb = gb_s[pl.ds(q0, BLOCK), cols]
                a_s[qb * BLOCK:(qb + 1) * BLOCK, CONV_WIDTH + kv * GROUP_W:CONV_WIDTH + (kv + 1) * GROUP_W] = (
                    jnp.concatenate([o01, o23], axis=1) * gb).astype(BF16)

        mixed = jnp.dot(a_s[...], wout_ref[...], preferred_element_type=F32)
        y_ref[0] = x_s[m % 2] + gate * mixed


def _const_spec(shape):
    return pl.BlockSpec(shape, lambda b, c: (0,) * len(shape))


def _even_layer(x, mod, norm_g, w_in, conv_w, conv_b, q_g, k_g, sink, w_out, rope=None, ctx=None):
    n_seq, seq_len, _ = x.shape
    windowed = rope is not None
    nc = seq_len // TM
    last = nc - 1
    chunk_in = lambda b, c: (b, jnp.minimum(c, last), 0)
    chunk_out = lambda b, c: (b, jnp.maximum(c - 1, 0), 0)
    in_specs = [
        pl.BlockSpec((1, TM, D_MODEL), chunk_in),
        pl.BlockSpec((1, 1, 3 * D_MODEL), lambda b, c: (b if windowed else 0, 0, 0)),
        _const_spec((1, D_MODEL)),
        _const_spec((D_MODEL, EVEN_IN)),
        _const_spec((3, CONV_WIDTH)),
        _const_spec((1, CONV_WIDTH)),
        _const_spec((1, LANES)),
        _const_spec((1, LANES)),
        pl.BlockSpec(memory_space=pltpu.SMEM),
        _const_spec((D_MODEL, D_MODEL)),
    ]
    args = [x, mod, norm_g.reshape(1, D_MODEL), w_in, conv_w, conv_b.reshape(1, CONV_WIDTH),
            jnp.tile(q_g, LANES // HEAD_DIM).reshape(1, LANES), jnp.tile(k_g, LANES // HEAD_DIM).reshape(1, LANES),
            sink, w_out]
    kv_rows = seq_len + (2 * BLOCK if windowed else 0)
    scratch = [
        pltpu.VMEM((seq_len + 2 * HALO, CONV_WIDTH), F32),
        pltpu.VMEM((seq_len, CONV_WIDTH), F32),
        pltpu.VMEM((seq_len, ATTN_WIDTH), BF16),
        pltpu.VMEM((kv_rows, 2 * GROUP_W), BF16),
        pltpu.VMEM((2 * LANES, kv_rows), BF16),
        pltpu.VMEM((seq_len, ATTN_WIDTH), F32),
        pltpu.VMEM((2, TM, D_MODEL), F32),
        pltpu.VMEM((TM, D_MODEL), BF16),
    ]
    y_shape = jax.ShapeDtypeStruct(x.shape, F32)
    y_spec = pl.BlockSpec((1, TM, D_MODEL), chunk_out)
    if windowed:
        cos, sin = rope
        ck, cv = ctx
        in_specs += [pl.BlockSpec((TM, LANES), lambda b, c: (jnp.minimum(c, last), 0)),
                     pl.BlockSpec((TM, LANES), lambda b, c: (jnp.minimum(c, last), 0)),
                     pl.BlockSpec((1,) + ck.shape[1:], lambda b, c: (b, 0, 0)),
                     pl.BlockSpec((1,) + cv.shape[1:], lambda b, c: (b, 0, 0))]
        args += [cos, sin, ck, cv]
        scratch += [pltpu.VMEM((ck.shape[1], 2 * GROUP_W), BF16), pltpu.VMEM((2 * LANES, cv.shape[1]), BF16)]
        out_shape, out_specs = y_shape, y_spec
    else:
        kv_shape = jax.ShapeDtypeStruct((n_seq, seq_len, KV_WIDTH), F32)
        kv_spec = pl.BlockSpec((1, TM, KV_WIDTH), chunk_in)
        out_shape, out_specs = (y_shape, kv_shape, kv_shape), (y_spec, kv_spec, kv_spec)
    return pl.pallas_call(
        functools.partial(_even_kernel, seq_len=seq_len, windowed=windowed),
        out_shape=out_shape,
        grid=(n_seq, nc + 1),
        in_specs=in_specs,
        out_specs=out_specs,
        scratch_shapes=scratch,
        compiler_params=pltpu.CompilerParams(dimension_semantics=("arbitrary", "arbitrary"),
                                             vmem_limit_bytes=VMEM_LIMIT),
        name="even_latent" if windowed else "even_context",
    )(*args)


def _odd_kernel(x_ref, mod_ref, g_ref, win_ref, pw_ref, ps_ref, wout_ref, y_ref, u_s, sg_s, x_s, *, seq_len):
    nc = seq_len // TM
    c = pl.program_id(1)
    shift = mod_ref[0, :, 0:D_MODEL]
    scale = mod_ref[0, :, D_MODEL:2 * D_MODEL]
    gate = mod_ref[0, :, 2 * D_MODEL:3 * D_MODEL]

    @pl.when((pl.program_id(0) == 0) & (c == 0))
    def _zero_pads():
        zu = jnp.zeros((HALO, D_MODEL), F32)
        u_s[0:HALO, :] = zu
        u_s[HALO + seq_len:2 * HALO + seq_len, :] = zu

    @pl.when(c < nc)
    def _project():
        r0 = pl.multiple_of(c * TM, TM)
        x = x_ref[0]
        x_s[c % 2] = x
        h = _rms_modulate(x, g_ref[...], shift, scale).astype(BF16)
        u_s[pl.ds(HALO + r0, TM), :] = jnp.dot(h, win_ref[:, 0:D_MODEL], preferred_element_type=F32)
        sg_s[pl.ds(r0, TM), :] = _silu(jnp.dot(h, win_ref[:, D_MODEL:2 * D_MODEL], preferred_element_type=F32))

    @pl.when(c >= 1)
    def _mix():
        m = c - 1
        r0 = pl.multiple_of(m * TM, TM)
        t = r0 + lax.broadcasted_iota(jnp.int32, (TM, 1), 0)
        outs = []
        for gi, w in enumerate(POOL_SIZES):
            cols = slice(gi * POOL_GROUP, (gi + 1) * POOL_GROUP)
            ext = u_s[pl.ds(r0, EXT), cols]
            acc = ext + _rows_down(ext, 1)
            half = 1
            while 2 * half < w:
                acc = _rows_down(acc, half) + _rows_up(acc, half)
                half *= 2
            cnt = (jnp.minimum(t + w // 2, seq_len) - jnp.maximum(t - w // 2, 0)).astype(F32)
            pooled = acc[HALO:HALO + TM] / cnt - ext[HALO:HALO + TM]
            outs.append(jnp.dot(pooled.astype(BF16), pw_ref[gi], preferred_element_type=F32))
        y = jnp.concatenate(outs, axis=1) * ps_ref[...] * sg_s[pl.ds(r0, TM), :]
        mixed = jnp.dot(y.astype(BF16), wout_ref[...], preferred_element_type=F32)
        y_ref[0] = x_s[m % 2] + gate * mixed


def _odd_layer(x, mod, per_seq_mod, norm_g, w_in, pool_w, pool_scale, w_out):
    n_seq, seq_len, _ = x.shape
    nc = seq_len // TM
    last = nc - 1
    in_specs = [
        pl.BlockSpec((1, TM, D_MODEL), lambda b, c: (b, jnp.minimum(c, last), 0)),
        pl.BlockSpec((1, 1, 3 * D_MODEL), lambda b, c: (b if per_seq_mod else 0, 0, 0)),
        _const_spec((1, D_MODEL)),
        _const_spec((D_MODEL, 2 * D_MODEL)),
        _const_spec(pool_w.shape),
        _const_spec((1, D_MODEL)),
        _const_spec((D_MODEL, D_MODEL)),
    ]
    return pl.pallas_call(
        functools.partial(_odd_kernel, seq_len=seq_len),
        out_shape=jax.ShapeDtypeStruct(x.shape, F32),
        grid=(n_seq, nc + 1),
        in_specs=in_specs,
        out_specs=pl.BlockSpec((1, TM, D_MODEL), lambda b, c: (b, jnp.maximum(c - 1, 0), 0)),
        scratch_shapes=[
            pltpu.VMEM((seq_len + 2 * HALO, D_MODEL), F32),
            pltpu.VMEM((seq_len, D_MODEL), F32),
            pltpu.VMEM((2, TM, D_MODEL), F32),
        ],
        compiler_params=pltpu.CompilerParams(dimension_semantics=("arbitrary", "arbitrary"),
                                             vmem_limit_bytes=VMEM_LIMIT),
        name="odd_latent" if per_seq_mod else "odd_context",
    )(x, mod, norm_g.reshape(1, D_MODEL), w_in, pool_w, pool_scale.reshape(1, D_MODEL), w_out)


def _rope_tables(seq_len):
    n_rows = seq_len // GRID_W
    row = jnp.repeat(jnp.arange(n_rows), GRID_W).astype(F32)
    col = jnp.tile(jnp.arange(GRID_W), n_rows).astype(F32)
    inv = ROPE_BASE ** (-jnp.arange(ROPE_FREQS, dtype=F32) / ROPE_FREQS)
    ang = jnp.stack([row[:, None] * inv, col[:, None] * inv], axis=1)
    cos = jnp.broadcast_to(jnp.cos(ang)[:, :, None, :], (seq_len, 2, 2, ROPE_FREQS))
    sin = jnp.sin(ang)[:, :, None, :] * jnp.array([-1.0, 1.0], F32)[None, None, :, None]
    cos = cos.reshape(seq_len, HEAD_DIM)
    sin = sin.reshape(seq_len, HEAD_DIM)
    return jnp.tile(cos, (1, LANES // HEAD_DIM)), jnp.tile(sin, (1, LANES // HEAD_DIM))


def kernel(x_prompt, x_sample, cache_k, cache_v, c, c_ctx, ada_w_e, ada_b_e, norm_g_e, w_in_e, conv_w, conv_b,
           q_norm_g, k_norm_g, sink, w_out_e, ada_w_o, ada_b_o, norm_g_o, w_in_o, pool_w, pool_scale, w_out_o):
    n_dec = x_sample.shape[0]
    depth = ada_w_e.shape[0] + ada_w_o.shape[0]
    assert ada_w_e.shape[0] == 1 and ada_w_o.shape[0] == 1 and n_dec + 1 <= 8
    cond8 = jnp.concatenate([c_ctx[None], c, jnp.zeros((8 - 1 - n_dec, D_MODEL), F32)], axis=0)
    mod_e, mod_o = _adaln(cond8, ada_w_e[0], ada_b_e[0], ada_w_o[0], ada_b_o[0])
    rope = _rope_tables(x_sample.shape[1])

    yp, ys = x_prompt, x_sample
    new_k, new_v = [], []
    for layer in range(depth):
        i = layer // 2
        if layer % 2 == 0:
            mod = mod_e.reshape(8, 1, 3 * D_MODEL)
            w_in = w_in_e[i].astype(BF16)
            w_out = w_out_e[i].astype(BF16)
            common = (norm_g_e[i], w_in, conv_w[i], conv_b[i], q_norm_g[i], k_norm_g[i], sink[i], w_out)
            yp, k, v = _even_layer(yp, mod[0:1], *common)
            new_k.append(k.reshape(k.shape[0], k.shape[1], N_KV_HEADS, HEAD_DIM))
            new_v.append(v.reshape(v.shape[0], v.shape[1], N_KV_HEADS, HEAD_DIM))
            ck = cache_k[:, i].reshape(n_dec, cache_k.shape[2], KV_WIDTH)
            cv = cache_v[:, i].reshape(n_dec, cache_v.shape[2], KV_WIDTH)
            ys = _even_layer(ys, mod[1:1 + n_dec], *common, rope=rope, ctx=(ck, cv))
        else:
            mod = mod_o.reshape(8, 1, 3 * D_MODEL)
            common = (norm_g_o[i], w_in_o[i].astype(BF16), pool_w[i].astype(BF16), pool_scale[i],
                      w_out_o[i].astype(BF16))
            yp = _odd_layer(yp, mod[0:1], False, *common)
            ys = _odd_layer(ys, mod[1:1 + n_dec], True, *common)
    return yp, ys, jnp.stack(new_k, axis=1), jnp.stack(new_v, axis=1)
```

```python
import functools

import jax
import jax.numpy as jnp
import numpy as np
from jax import lax
from jax.experimental import pallas as pl
from jax.experimental.pallas import tpu as pltpu

F32 = jnp.float32
BF16 = jnp.bfloat16

D_MODEL = 1024
GRID_W = 64
HEAD_DIM = 64
HEAD_SHIFT = 6
ATTN_WIDTH = D_MODEL // 2
N_Q_HEADS = ATTN_WIDTH // HEAD_DIM
N_KV_HEADS = N_Q_HEADS // 4
Q_PER_KV = N_Q_HEADS // N_KV_HEADS
KV_WIDTH = N_KV_HEADS * HEAD_DIM
CONV_WIDTH = D_MODEL - ATTN_WIDTH
WINDOW = 128
BLOCK = 128
ROPE_BASE = 10000.0
ROPE_FREQS = HEAD_DIM // 4
ATTN_SCALE = HEAD_DIM ** -0.5
NEG = -1e30
POOL_SIZES = (2, 4, 8, 16)
POOL_GROUP = D_MODEL // len(POOL_SIZES)
EPS = 1e-6
EVEN_SIZES = (CONV_WIDTH, CONV_WIDTH, CONV_WIDTH, CONV_WIDTH, ATTN_WIDTH, KV_WIDTH, KV_WIDTH, ATTN_WIDTH)
EVEN_IN = sum(EVEN_SIZES)
EVEN_OFFS = tuple(int(s) for s in np.cumsum((0,) + EVEN_SIZES))

TM = 256
NT = 256
NORM_ROWS = 64
LANES = 128
GROUP_W = Q_PER_KV * HEAD_DIM
HALO = 8
ADA_TN = 768
VMEM_LIMIT = 56 * 1024 * 1024
PIPELINE_PARAMS = pltpu.CompilerParams(dimension_semantics=("arbitrary",), vmem_limit_bytes=VMEM_LIMIT)


def _silu(x):
    return x / (1.0 + jnp.exp(-x))


def _rows_down(x, k):
    return pltpu.roll(x, k, axis=0)


def _rows_up(x, k):
    return pltpu.roll(x, x.shape[0] - k, axis=0)


def _rms_modulate(x, g, shift, scale):
    ms = jnp.mean(x * x, axis=-1, keepdims=True)
    return (x * lax.rsqrt(ms + EPS) * g) * (1.0 + scale) + shift


def _normalize_items(h_s, x_ref, g_ref, mod_ref):
    shift = mod_ref[0, :, 0:D_MODEL]
    scale = mod_ref[0, :, D_MODEL:2 * D_MODEL]
    for r in range(0, TM, NORM_ROWS):
        yield 0, 170
        h_s[r:r + NORM_ROWS] = _rms_modulate(x_ref[0, r:r + NORM_ROWS], g_ref[...], shift, scale).astype(BF16)


def _out_proj_items(get_a, wout_ref, x_ref, mod_ref, y_ref):
    for n in range(D_MODEL // NT):
        yield _mxu_cost(TM, D_MODEL, NT), 70
        cols = slice(n * NT, (n + 1) * NT)
        mixed = jnp.dot(get_a(), wout_ref[:, cols], preferred_element_type=F32)
        y_ref[0, :, cols] = x_ref[0, :, cols] + mod_ref[0, :, 2 * D_MODEL + n * NT:2 * D_MODEL + (n + 1) * NT] * mixed


def _mxu_cost(m, k, n):
    tiles = -(-k // NT) * -(-n // NT)
    return tiles * max(m // 2, 64) // 2


def _weave(**stages):
    pending = {}
    for key, stage in stages.items():
        cost = next(stage, None)
        if cost is not None:
            pending[key] = cost
    emitted = dict.fromkeys(stages, 0)
    while pending:
        key = min(pending, key=lambda k: emitted[k])
        emitted[key] += sum(pending[key])
        cost = next(stages[key], None)
        if cost is None:
            del pending[key]
        else:
            pending[key] = cost


def _with_halo(u_s, slot, cols, prev_slot, next_slot, has_prev, has_next):
    u = u_s[slot, :, cols]
    zero = jnp.zeros((HALO, u.shape[1]), F32)
    if prev_slot is None:
        top = bot = zero
    else:
        top = jnp.where(has_prev, u_s[prev_slot, TM - HALO:TM, cols], zero)
        bot = jnp.where(has_next, u_s[next_slot, 0:HALO, cols], zero)
    return jnp.concatenate([top, u, bot], axis=0)


def _pipeline_steps(s, n_chunks, lag, ring, normalized, staged, normalize, project, finish):
    fin_slot = lax.rem(s + (2 * ring - 1 - lag), jnp.int32(ring))

    @pl.when((s >= 1) & (s <= n_chunks))
    def _publish_normalized():
        h_w, h_s = normalized
        h_s[...] = h_w[...]

    @pl.when((s >= 2) & (s <= n_chunks + 1))
    def _publish_projected():
        slot = lax.rem(s + (ring - 2), jnp.int32(ring))
        for stage_ref, ring_ref in staged:
            ring_ref[slot] = stage_ref[...]

    @pl.when(s == 0)
    def _first():
        _weave(normalize=normalize())

    @pl.when((s >= 1) & (s <= lag))
    def _fill():
        _weave(project=project(), normalize=normalize())

    @pl.when((s > lag) & (s <= n_chunks))
    def _steady():
        _weave(project=project(), finish=finish(fin_slot), normalize=normalize())

    @pl.when(s > n_chunks)
    def _drain():
        _weave(finish=finish(fin_slot))


def _adaln_kernel(cond_ref, we_ref, be_ref, wo_ref, bo_ref, oe_ref, oo_ref):
    s = _silu(cond_ref[...]).astype(BF16)
    oe_ref[...] = jnp.dot(s, we_ref[...].astype(BF16), preferred_element_type=F32) + be_ref[...]
    oo_ref[...] = jnp.dot(s, wo_ref[...].astype(BF16), preferred_element_type=F32) + bo_ref[...]


def _adaln(cond8, w_e, b_e, w_o, b_o):
    n = 3 * D_MODEL
    wspec = pl.BlockSpec((D_MODEL, ADA_TN), lambda j: (0, j))
    vspec = pl.BlockSpec((1, ADA_TN), lambda j: (0, j))
    ospec = pl.BlockSpec((8, ADA_TN), lambda j: (0, j))
    return pl.pallas_call(
        _adaln_kernel,
        out_shape=(jax.ShapeDtypeStruct((8, n), F32), jax.ShapeDtypeStruct((8, n), F32)),
        grid=(n // ADA_TN,),
        in_specs=[pl.BlockSpec((8, D_MODEL), lambda j: (0, 0)), wspec, vspec, wspec, vspec],
        out_specs=(ospec, ospec),
        compiler_params=pltpu.CompilerParams(dimension_semantics=("arbitrary",), vmem_limit_bytes=VMEM_LIMIT),
        name="adaln",
    )(cond8, w_e, b_e.reshape(1, n), w_o, b_o.reshape(1, n))


def _head_rms(x, g, bd):
    sq = x * x
    hi = sq.astype(BF16)
    lo = (sq - hi.astype(F32)).astype(BF16)
    ms = (jnp.dot(hi, bd, preferred_element_type=F32) + jnp.dot(lo, bd, preferred_element_type=F32)) * (1.0 / HEAD_DIM)
    return x * lax.rsqrt(ms + EPS) * g


def _rope(x, cos, sin_signed, first_half):
    partner = jnp.where(first_half, pltpu.roll(x, LANES - ROPE_FREQS, axis=1), pltpu.roll(x, ROPE_FREQS, axis=1))
    return x * cos + partner * sin_signed


def _dup_kv(x):
    lane = lax.broadcasted_iota(jnp.int32, (1, LANES), 1)
    swapped = pltpu.roll(x, HEAD_DIM, axis=1)
    return jnp.where(lane < HEAD_DIM, x, swapped), jnp.where(lane < HEAD_DIM, swapped, x)


def _tiled_keys(k):
    ka, kb = _dup_kv(k)
    return jnp.concatenate([ka, ka, kb, kb], axis=1).astype(BF16)


def _values_t(v):
    va, vb = _dup_kv(v)
    return jnp.concatenate([va.T, vb.T], axis=0).astype(BF16)


def _even_kernel(*refs, nc, n_chunks, lag, ring, windowed):
    if windowed:
        (xp_ref, xa_ref, modp_ref, moda_ref, g_ref, win_ref, cw_ref, cb_ref, qg_ref, kg_ref, sink_ref, wout_ref,
         cos_ref, sin_ref, ck_ref, cv_ref,
         y_ref,
         h_s, h_w, u_s, gate_s, q_s, k_s, vt_s, gb_s, u_w, gate_w, q_w, k_w, vt_w, gb_w, a_s, ck_s, cvt_s) = refs
        ko_ref = vo_ref = None
    else:
        (xp_ref, xa_ref, modp_ref, moda_ref, g_ref, win_ref, cw_ref, cb_ref, qg_ref, kg_ref, sink_ref, wout_ref,
         y_ref, ko_ref, vo_ref,
         h_s, h_w, u_s, gate_s, q_s, k_s, vt_s, gb_s, u_w, gate_w, q_w, k_w, vt_w, gb_w, a_s) = refs
    s = pl.program_id(0)
    i = s - 1 - lag
    ci = lax.rem(i, jnp.int32(nc))

    @pl.when(s == 0)
    def _init():
        k_s[...] = jnp.zeros(k_s.shape, BF16)
        vt_s[...] = jnp.zeros(vt_s.shape, BF16)

    if windowed:
        @pl.when((i >= 0) & (ci == 0))
        def _load_ctx():
            ck_s[...] = _tiled_keys(ck_ref[0])
            cvt_s[...] = _values_t(cv_ref[0])

    def normalize():
        yield from _normalize_items(h_w, xp_ref, g_ref, modp_ref)

    def project():
        o_bg, o_cg, o_xs, o_ga, o_q, o_k, _, o_gb = EVEN_OFFS[:8]

        def proj(start):
            return jnp.dot(h_s[...], win_ref[:, start:start + NT], preferred_element_type=F32)

        lane = lax.broadcasted_iota(jnp.int32, (1, LANES), 1)
        first_half = (lane & (2 * ROPE_FREQS - 1)) < ROPE_FREQS
        ri = lax.broadcasted_iota(jnp.int32, (LANES, LANES), 0) >> HEAD_SHIFT
        cj = lax.broadcasted_iota(jnp.int32, (LANES, LANES), 1) >> HEAD_SHIFT
        bd = (ri == cj).astype(BF16)

        def qk_norm(x, g):
            x = _head_rms(x, g, bd)
            return _rope(x, cos_ref[...], sin_ref[...], first_half) if windowed else x

        tile = _mxu_cost(TM, D_MODEL, NT)
        yield tile, 350
        kv = proj(o_k)
        k = qk_norm(kv[:, 0:KV_WIDTH], kg_ref[...])
        v = kv[:, KV_WIDTH:2 * KV_WIDTH]
        if not windowed:
            ko_ref[0] = k
            vo_ref[0] = v
        k_w[...] = _tiled_keys(k)
        vt_w[...] = _values_t(v)
        for n in range(ATTN_WIDTH // NT):
            yield tile, 300
            q = proj(o_q + n * NT)
            for j in range(NT // LANES):
                qj = qk_norm(q[:, j * LANES:(j + 1) * LANES], qg_ref[...])
                q_w[:, n * NT + j * LANES:n * NT + (j + 1) * LANES] = (qj * ATTN_SCALE).astype(BF16)
        for n in range(CONV_WIDTH // NT):
            cols = slice(n * NT, (n + 1) * NT)
            yield 2 * tile, 180
            gate_w[:, cols] = proj(o_bg + n * NT) * _silu(proj(o_ga + n * NT))
            yield 2 * tile, 40
            u_w[:, cols] = proj(o_cg + n * NT) * proj(o_xs + n * NT)
        for n in range(ATTN_WIDTH // NT):
            yield tile, 110
            gb_w[:, n * NT:(n + 1) * NT] = _silu(proj(o_gb + n * NT))

    def finish(slot):
        if windowed:
            prev_slot = lax.rem(slot + (ring - 1), jnp.int32(ring))
            next_slot = lax.rem(slot + 1, jnp.int32(ring))
            has_prev = ci > 0
            has_next = ci < nc - 1
        else:
            prev_slot = next_slot = has_prev = has_next = None

        yield 0, 400
        ext = _with_halo(u_s, slot, slice(None), prev_slot, next_slot, has_prev, has_next)
        conv = (_rows_down(ext, 1)[HALO:HALO + TM] * cw_ref[0:1, :]
                + ext[HALO:HALO + TM] * cw_ref[1:2, :]
                + _rows_up(ext, 1)[HALO:HALO + TM] * cw_ref[2:3, :]
                + cb_ref[...])
        a_s[:, 0:CONV_WIDTH] = (gate_s[slot] * conv).astype(BF16)

        lane_head = lax.broadcasted_iota(jnp.int32, (1, GROUP_W), 1) >> HEAD_SHIFT
        cols4 = Q_PER_KV * BLOCK
        n_keys = 3 * BLOCK + ck_s.shape[0] if windowed else TM
        key_i =lax.broadcasted_iota(jnp.int32, (BLOCK, cols4), 0)
        qry_i = lax.broadcasted_iota(jnp.int32, (BLOCK, cols4), 1) & (BLOCK - 1)
        upper = lax.broadcasted_iota(jnp.int32, (LANES, LANES), 0) < HEAD_DIM
        halves = (slice(0, BLOCK), slice(BLOCK, TM))
        for qb in range(TM // BLOCK):
            own = halves[qb]
            if windowed:
                if qb == 0:
                    blocks = ((prev_slot, halves[1]), (slot, halves[0]), (slot, halves[1]))
                    prev_in, next_in = has_prev, True
                else:
                    blocks = ((slot, halves[0]), (slot, halves[1]), (next_slot, halves[0]))
                    prev_in, next_in = True, has_next
                prev_ok = (key_i - qry_i) >= jnp.where(prev_in, 0, BLOCK)
                next_ok = (qry_i - key_i) >= jnp.where(next_in, 0, BLOCK)
            for kv in range(N_KV_HEADS):
                cols = slice(kv * GROUP_W, (kv + 1) * GROUP_W)
                vrows = slice(kv * LANES, (kv + 1) * LANES)
                yield _mxu_cost(n_keys, GROUP_W, cols4), 60
                qblk = q_s[slot, own, cols]
                q4 = jnp.concatenate(
                    [jnp.where(lane_head == g, qblk, jnp.zeros_like(qblk)) for g in range(Q_PER_KV)], axis=0)

                def scores_t(kk):
                    return lax.dot_general(kk, q4, (((1,), (1,)), ((), ())), preferred_element_type=F32)

                if windowed:
                    s_loc = scores_t(jnp.concatenate([k_s[sl, r, cols] for sl, r in blocks], axis=0))
                    vt_loc = jnp.concatenate([vt_s[sl, vrows, r] for sl, r in blocks], axis=1)
                    s_ctx = scores_t(ck_s[:, cols])
                    yield 0, n_keys * cols4 * 5 // (8 * LANES * 4)
                    groups = [([jnp.where(prev_ok, s_loc[0:BLOCK], NEG), s_loc[BLOCK:2 * BLOCK],
                                jnp.where(next_ok, s_loc[2 * BLOCK:3 * BLOCK], NEG)], vt_loc),
                              ([s_ctx], cvt_s[vrows, :])]
                else:
                    s_own = scores_t(k_s[slot, :, cols])
                    yield 0, n_keys * cols4 * 5 // (8 * LANES * 4)
                    groups = [([s_own], vt_s[slot, vrows, :])]
                snk = jnp.concatenate(
                    [jnp.full((1, BLOCK), sink_ref[kv * Q_PER_KV + g], F32) for g in range(Q_PER_KV)], axis=1)
                mx = snk
                for ss, _ in groups:
                    for sc in ss:
                        mx = jnp.maximum(mx, jnp.max(sc, axis=0, keepdims=True))
                den = jnp.exp(snk - mx)
                probs = []
                for ss, vt in groups:
                    es = []
                    for sc in ss:
                        e = jnp.exp(sc - mx)
                        den = den + jnp.sum(e, axis=0, keepdims=True)
                        es.append(e.astype(BF16))
                    probs.append((es[0] if len(es) == 1 else jnp.concatenate(es, axis=0), vt))
                yield _mxu_cost(LANES, n_keys, cols4), 120
                o_t = jnp.zeros((LANES, cols4), F32)
                for e_all, vt in probs:
                    o_t = o_t + jnp.dot(vt, e_all, preferred_element_type=F32)
                o_t = o_t * (1.0 / den)
                o01 = jnp.where(upper, o_t[:, 0:BLOCK], o_t[:, BLOCK:2 * BLOCK]).T
                o23 = jnp.where(upper, o_t[:, 2 * BLOCK:3 * BLOCK], o_t[:, 3 * BLOCK:4 * BLOCK]).T
                a_s[own, CONV_WIDTH + kv * GROUP_W:CONV_WIDTH + (kv + 1) * GROUP_W] = (
                    jnp.concatenate([o01, o23], axis=1) * gb_s[slot, own, cols]).astype(BF16)

        yield from _out_proj_items(lambda: a_s[...], wout_ref, xa_ref, moda_ref, y_ref)

    staged = ((u_w, u_s), (gate_w, gate_s), (q_w, q_s), (k_w, k_s), (vt_w, vt_s), (gb_w, gb_s))
    _pipeline_steps(s, n_chunks, lag, ring, (h_w, h_s), staged, normalize, project, finish)


def _chunk_specs(n_chunks, nc, lag, per_seq_mod):
    last = n_chunks - 1
    norm_chunk = lambda s: jnp.minimum(s, last)
    proj_chunk = lambda s: jnp.clip(s - 1, 0, last)
    fin_chunk = lambda s: jnp.clip(s - 1 - lag, 0, last)
    seq_of = (lambda j: j // nc) if per_seq_mod else (lambda j: 0)
    return dict(
        x_norm=pl.BlockSpec((1, TM, D_MODEL), lambda s: (norm_chunk(s), 0, 0)),
        x_fin=pl.BlockSpec((1, TM, D_MODEL), lambda s: (fin_chunk(s), 0, 0)),
        mod_norm=pl.BlockSpec((1, 1, 3 * D_MODEL), lambda s: (seq_of(norm_chunk(s)), 0, 0)),
        mod_fin=pl.BlockSpec((1, 1, 3 * D_MODEL), lambda s: (seq_of(fin_chunk(s)), 0, 0)),
        proj_chunk=proj_chunk, fin_chunk=fin_chunk)


def _const_spec(shape):
    return pl.BlockSpec(shape, lambda c: (0,) * len(shape))


def _even_layer(x, mod, norm_g, w_in, conv_w, conv_b, q_g, k_g, sink, w_out, rope=None, ctx=None):
    n_seq, seq_len, _ = x.shape
    windowed = rope is not None
    nc = seq_len // TM
    n_chunks = n_seq * nc
    lag, ring = (2, 4) if windowed else (1, 2)
    assert windowed or nc == 1
    sp = _chunk_specs(n_chunks, nc, lag, windowed)
    xc = x.reshape(n_chunks, TM, D_MODEL)
    in_specs = [
        sp["x_norm"], sp["x_fin"], sp["mod_norm"], sp["mod_fin"],
        _const_spec((1, D_MODEL)),
        _const_spec((D_MODEL, EVEN_IN)),
        _const_spec((3, CONV_WIDTH)),
        _const_spec((1, CONV_WIDTH)),
        _const_spec((1, LANES)),
        _const_spec((1, LANES)),
        pl.BlockSpec(memory_space=pltpu.SMEM),
        _const_spec((D_MODEL, D_MODEL)),
    ]
    args = [xc, xc, mod, mod, norm_g.reshape(1, D_MODEL), w_in, conv_w, conv_b.reshape(1, CONV_WIDTH),
            jnp.tile(q_g, LANES // HEAD_DIM).reshape(1, LANES), jnp.tile(k_g, LANES // HEAD_DIM).reshape(1, LANES),
            sink, w_out]
    per_chunk = [
        ((TM, CONV_WIDTH), F32),
        ((TM, CONV_WIDTH), F32),
        ((TM, ATTN_WIDTH), BF16),
        ((TM, 2 * GROUP_W), BF16),
        ((2 * LANES, TM), BF16),
        ((TM, ATTN_WIDTH), F32),
    ]
    scratch = (
        [pltpu.VMEM((TM, D_MODEL), BF16)] * 2
        + [pltpu.VMEM((ring,) + shape, dt) for shape, dt in per_chunk]
        + [pltpu.VMEM(shape, dt) for shape, dt in per_chunk]
        + [pltpu.VMEM((TM, D_MODEL), BF16)])
    y_shape = jax.ShapeDtypeStruct(xc.shape, F32)
    y_spec = pl.BlockSpec((1, TM, D_MODEL), lambda c: (sp["fin_chunk"](c), 0, 0))
    if windowed:
        cos, sin = rope
        ck, cv = ctx
        rope_spec = pl.BlockSpec((TM, LANES), lambda c: (sp["proj_chunk"](c) % nc, 0))
        ctx_spec = pl.BlockSpec((1,) + ck.shape[1:], lambda c: (sp["fin_chunk"](c) // nc, 0, 0))
        in_specs += [rope_spec, rope_spec, ctx_spec, ctx_spec]
        args += [cos, sin, ck, cv]
        scratch += [pltpu.VMEM((ck.shape[1], 2 * GROUP_W), BF16), pltpu.VMEM((2 * LANES, cv.shape[1]), BF16)]
        out_shape, out_specs = y_shape, y_spec
    else:
        kv_shape = jax.ShapeDtypeStruct((n_chunks, TM, KV_WIDTH), F32)
        kv_spec = pl.BlockSpec((1, TM, KV_WIDTH), lambda c: (sp["proj_chunk"](c), 0, 0))
        out_shape, out_specs = (y_shape, kv_shape, kv_shape), (y_spec, kv_spec, kv_spec)
    out = pl.pallas_call(
        functools.partial(_even_kernel, nc=nc, n_chunks=n_chunks, lag=lag, ring=ring, windowed=windowed),
        out_shape=out_shape,
        grid=(n_chunks + lag + 1,),
        in_specs=in_specs,
        out_specs=out_specs,
        scratch_shapes=scratch,
        compiler_params=PIPELINE_PARAMS,
        name="even_latent" if windowed else "even_context",
    )(*args)
    if windowed:
        return out.reshape(x.shape)
    y, k, v = out
    return y.reshape(x.shape), k.reshape(n_seq, seq_len, KV_WIDTH), v.reshape(n_seq, seq_len, KV_WIDTH)


def _odd_kernel(xp_ref, xa_ref, modp_ref, moda_ref, g_ref, win_ref, pw_ref, ps_ref, wout_ref, y_ref,
                h_s, h_w, u_s, sg_s, u_w, sg_w, *, nc, n_chunks, lag, ring):
    s = pl.program_id(0)
    i = s - 1 - lag
    ci = lax.rem(i, jnp.int32(nc))

    def normalize():
        yield from _normalize_items(h_w, xp_ref, g_ref, modp_ref)

    def project():
        tile = _mxu_cost(TM, D_MODEL, NT)
        for n in range(D_MODEL // NT):
            yield tile, 10
            cols = slice(n * NT, (n + 1) * NT)
            u_w[:, cols] = jnp.dot(h_s[...], win_ref[:, cols], preferred_element_type=F32)
        for n in range(D_MODEL // NT):
            yield tile, 110
            cols = slice(n * NT, (n + 1) * NT)
            sg_w[:, cols] = _silu(jnp.dot(h_s[...], win_ref[:, D_MODEL + n * NT:D_MODEL + (n + 1) * NT],
                                          preferred_element_type=F32))

    def finish(slot):
        if nc > 1:
            prev_slot = lax.rem(slot + (ring - 1), jnp.int32(ring))
            next_slot = lax.rem(slot + 1, jnp.int32(ring))
            has_prev = ci > 0
            has_next = ci < nc - 1
        else:
            prev_slot = next_slot = has_prev = has_next = None
        t = ci * TM + lax.broadcasted_iota(jnp.int32, (TM, 1), 0)
        outs = []
        for gi, w in enumerate(POOL_SIZES):
            yield _mxu_cost(TM, POOL_GROUP, POOL_GROUP), 140 + 120 * gi
            cols = slice(gi * POOL_GROUP, (gi + 1) * POOL_GROUP)
            ext = _with_halo(u_s, slot, cols, prev_slot, next_slot, has_prev, has_next)
            acc = ext + _rows_down(ext, 1)
            half = 1
            while 2 * half < w:
                acc = _rows_down(acc, half) + _rows_up(acc, half)
                half *= 2
            cnt = (jnp.minimum(t + w // 2, nc * TM) - jnp.maximum(t - w // 2, 0)).astype(F32)
            pooled = acc[HALO:HALO + TM] / cnt - ext[HALO:HALO + TM]
            outs.append(jnp.dot(pooled.astype(BF16), pw_ref[gi], preferred_element_type=F32))
        yield 0, 130
        y = (jnp.concatenate(outs, axis=1) * ps_ref[...] * sg_s[slot]).astype(BF16)
        yield from _out_proj_items(lambda: y, wout_ref, xa_ref, moda_ref, y_ref)

    _pipeline_steps(s, n_chunks, lag, ring, (h_w, h_s), ((u_w, u_s), (sg_w, sg_s)), normalize, project, finish)


def _odd_layer(x, mod, per_seq_mod, norm_g, w_in, pool_w, pool_scale, w_out):
    n_seq, seq_len, _ = x.shape
    nc = seq_len // TM
    n_chunks = n_seq * nc
    lag, ring = (2, 4) if nc > 1 else (1, 2)
    sp = _chunk_specs(n_chunks, nc, lag, per_seq_mod)
    xc = x.reshape(n_chunks, TM, D_MODEL)
    in_specs = [
        sp["x_norm"], sp["x_fin"], sp["mod_norm"], sp["mod_fin"],
        _const_spec((1, D_MODEL)),
        _const_spec((D_MODEL, 2 * D_MODEL)),
        _const_spec(pool_w.shape),
        _const_spec((1, D_MODEL)),
        _const_spec((D_MODEL, D_MODEL)),
    ]
    out = pl.pallas_call(
        functools.partial(_odd_kernel, nc=nc, n_chunks=n_chunks, lag=lag, ring=ring),
        out_shape=jax.ShapeDtypeStruct(xc.shape, F32),
        grid=(n_chunks + lag + 1,),
        in_specs=in_specs,
        out_specs=pl.BlockSpec((1, TM, D_MODEL), lambda c: (sp["fin_chunk"](c), 0, 0)),
        scratch_shapes=[
            pltpu.VMEM((TM, D_MODEL), BF16),
            pltpu.VMEM((TM, D_MODEL), BF16),
            pltpu.VMEM((ring, TM, D_MODEL), F32),
            pltpu.VMEM((ring, TM, D_MODEL), F32),
            pltpu.VMEM((TM, D_MODEL), F32),
            pltpu.VMEM((TM, D_MODEL), F32),
        ],
        compiler_params=PIPELINE_PARAMS,
        name="odd_latent" if per_seq_mod else "odd_context",
    )(xc, xc, mod, mod, norm_g.reshape(1, D_MODEL), w_in, pool_w, pool_scale.reshape(1, D_MODEL), w_out)
    return out.reshape(x.shape)


def _rope_tables(seq_len):
    n_rows = seq_len // GRID_W
    row = jnp.repeat(jnp.arange(n_rows), GRID_W).astype(F32)
    col = jnp.tile(jnp.arange(GRID_W), n_rows).astype(F32)
    inv = ROPE_BASE ** (-jnp.arange(ROPE_FREQS, dtype=F32) / ROPE_FREQS)
    ang = jnp.stack([row[:, None] * inv, col[:, None] * inv], axis=1)
    cos = jnp.broadcast_to(jnp.cos(ang)[:, :, None, :], (seq_len, 2, 2, ROPE_FREQS))
    sin = jnp.sin(ang)[:, :, None, :] * jnp.array([-1.0, 1.0], F32)[None, None, :, None]
    cos = cos.reshape(seq_len, HEAD_DIM)
    sin = sin.reshape(seq_len, HEAD_DIM)
    return jnp.tile(cos, (1, LANES // HEAD_DIM)), jnp.tile(sin, (1, LANES // HEAD_DIM))


def kernel(x_prompt, x_sample, cache_k, cache_v, c, c_ctx, ada_w_e, ada_b_e, norm_g_e, w_in_e, conv_w, conv_b,
           q_norm_g, k_norm_g, sink, w_out_e, ada_w_o, ada_b_o, norm_g_o, w_in_o, pool_w, pool_scale, w_out_o):
    n_dec = x_sample.shape[0]
    depth = ada_w_e.shape[0] + ada_w_o.shape[0]
    assert ada_w_e.shape[0] == 1 and ada_w_o.shape[0] == 1 and n_dec + 1 <= 8
    cond8 = jnp.concatenate([c_ctx[None], c, jnp.zeros((8 - 1 - n_dec, D_MODEL), F32)], axis=0)
    mod_e, mod_o = _adaln(cond8, ada_w_e[0], ada_b_e[0], ada_w_o[0], ada_b_o[0])
    rope = _rope_tables(x_sample.shape[1])

    yp, ys = x_prompt, x_sample
    new_k, new_v = [], []
    for layer in range(depth):
        i = layer // 2
        if layer % 2 == 0:
            mod = mod_e.reshape(8, 1, 3 * D_MODEL)
            w_in = w_in_e[i].astype(BF16)
            w_out = w_out_e[i].astype(BF16)
            common = (norm_g_e[i], w_in, conv_w[i], conv_b[i], q_norm_g[i], k_norm_g[i], sink[i], w_out)
            yp, k, v = _even_layer(yp, mod[0:1], *common)
            new_k.append(k.reshape(k.shape[0], k.shape[1], N_KV_HEADS, HEAD_DIM))
            new_v.append(v.reshape(v.shape[0], v.shape[1], N_KV_HEADS, HEAD_DIM))
            ck = cache_k[:, i].reshape(n_dec, cache_k.shape[2], KV_WIDTH)
            cv = cache_v[:, i].reshape(n_dec, cache_v.shape[2], KV_WIDTH)
            ys = _even_layer(ys, mod[1:1 + n_dec], *common, rope=rope, ctx=(ck, cv))
        else:
            mod = mod_o.reshape(8, 1, 3 * D_MODEL)
            common = (norm_g_o[i], w_in_o[i].astype(BF16), pool_w[i].astype(BF16), pool_scale[i],
                      w_out_o[i].astype(BF16))
            yp = _odd_layer(yp, mod[0:1], False, *common)
            ys = _odd_layer(ys, mod[1:1 + n_dec], True, *common)
    return yp, ys, jnp.stack(new_k, axis=1), jnp.stack(new_v, axis=1)
```

```python
import functools

import jax
import jax.numpy as jnp
import numpy as np
from jax import lax
from jax.experimental import pallas as pl
from jax.experimental.pallas import tpu as pltpu

F32 = jnp.float32
BF16 = jnp.bfloat16

D_MODEL = 1024
GRID_W = 64
HEAD_DIM = 64
HEAD_SHIFT = 6
ATTN_WIDTH = D_MODEL // 2
N_Q_HEADS = ATTN_WIDTH // HEAD_DIM
N_KV_HEADS = N_Q_HEADS // 4
Q_PER_KV = N_Q_HEADS // N_KV_HEADS
KV_WIDTH = N_KV_HEADS * HEAD_DIM
CONV_WIDTH = D_MODEL - ATTN_WIDTH
WINDOW = 128
BLOCK = 128
ROPE_BASE = 10000.0
ROPE_FREQS = HEAD_DIM // 4
ATTN_SCALE = HEAD_DIM ** -0.5
NEG = -1e30
POOL_SIZES = (2, 4, 8, 16)
POOL_GROUP = D_MODEL // len(POOL_SIZES)
EPS = 1e-6
EVEN_SIZES = (CONV_WIDTH, CONV_WIDTH, CONV_WIDTH, CONV_WIDTH, ATTN_WIDTH, KV_WIDTH, KV_WIDTH, ATTN_WIDTH)
EVEN_IN = sum(EVEN_SIZES)
EVEN_OFFS = tuple(int(s) for s in np.cumsum((0,) + EVEN_SIZES))

TM = 256
NT = 256
NORM_ROWS = 64
EVEN_ORDER = "PF" + "FPFPF" * 4 + "FNFNFNFN"
ODD_ORDER = "PFPPFPPFPPFP" + "FNFNFNFN"
LANES = 128
GROUP_W = Q_PER_KV * HEAD_DIM
HALO = 8
ADA_TN = 768
VMEM_LIMIT = 56 * 1024 * 1024
PIPELINE_PARAMS = pltpu.CompilerParams(dimension_semantics=("arbitrary",), vmem_limit_bytes=VMEM_LIMIT)


def _silu(x):
    return x / (1.0 + jnp.exp(-x))


def _rows_down(x, k):
    return pltpu.roll(x, k, axis=0)


def _rows_up(x, k):
    return pltpu.roll(x, x.shape[0] - k, axis=0)


def _rms_modulate(x, g, shift, scale):
    ms = jnp.mean(x * x, axis=-1, keepdims=True)
    return (x * lax.rsqrt(ms + EPS) * g) * (1.0 + scale) + shift


def _normalize_items(h_s, x_ref, g_ref, mod_ref):
    shift = mod_ref[0, :, 0:D_MODEL]
    scale = mod_ref[0, :, D_MODEL:2 * D_MODEL]
    for r in range(0, TM, NORM_ROWS):
        yield 0, 170
        h_s[r:r + NORM_ROWS] = _rms_modulate(x_ref[0, r:r + NORM_ROWS], g_ref[...], shift, scale).astype(BF16)


def _out_proj_items(get_a, wout_ref, x_ref, mod_ref, y_ref):
    for n in range(D_MODEL // NT):
        yield _mxu_cost(TM, D_MODEL, NT), 70
        cols = slice(n * NT, (n + 1) * NT)
        mixed = jnp.dot(get_a(), wout_ref[:, cols], preferred_element_type=F32)
        y_ref[0, :, cols] = x_ref[0, :, cols] + mod_ref[0, :, 2 * D_MODEL + n * NT:2 * D_MODEL + (n + 1) * NT] * mixed


def _mxu_cost(m, k, n):
    tiles = -(-k // NT) * -(-n // NT)
    return tiles * max(m // 2, 64) // 2


def _weave(order, **stages):
    by_letter = {name[0].upper(): stage for name, stage in stages.items()}
    for stage in by_letter.values():
        next(stage, None)
    for letter in order:
        if letter in by_letter:
            next(by_letter[letter], None)
    for stage in by_letter.values():
        for _ in stage:
            pass


def _with_halo(u_s, slot, cols, prev_slot, next_slot, has_prev, has_next):
    u = u_s[slot, :, cols]
    zero = jnp.zeros((HALO, u.shape[1]), F32)
    if prev_slot is None:
        top = bot = zero
    else:
        top = jnp.where(has_prev, u_s[prev_slot, TM - HALO:TM, cols], zero)
        bot = jnp.where(has_next, u_s[next_slot, 0:HALO, cols], zero)
    return jnp.concatenate([top, u, bot], axis=0)


def _pipeline_steps(s, n_chunks, lag, ring, normalized, staged, order, normalize, project, finish):
    fin_slot = lax.rem(s + (2 * ring - 1 - lag), jnp.int32(ring))

    @pl.when((s >= 1) & (s <= n_chunks))
    def _publish_normalized():
        h_w, h_s = normalized
        h_s[...] = h_w[...]

    @pl.when((s >= 2) & (s <= n_chunks + 1))
    def _publish_projected():
        slot = lax.rem(s + (ring - 2), jnp.int32(ring))
        for stage_ref, ring_ref in staged:
            ring_ref[slot] = stage_ref[...]

    @pl.when(s == 0)
    def _first():
        _weave(order, normalize=normalize())

    @pl.when((s >= 1) & (s <= lag))
    def _fill():
        _weave(order, project=project(), normalize=normalize())

    @pl.when((s > lag) & (s <= n_chunks))
    def _steady():
        _weave(order, finish=finish(fin_slot), project=project(), normalize=normalize())

    @pl.when(s > n_chunks)
    def _drain():
        _weave(order, finish=finish(fin_slot))


def _adaln_kernel(cond_ref, we_ref, be_ref, wo_ref, bo_ref, oe_ref, oo_ref):
    s = _silu(cond_ref[...]).astype(BF16)
    oe_ref[...] = jnp.dot(s, we_ref[...].astype(BF16), preferred_element_type=F32) + be_ref[...]
    oo_ref[...] = jnp.dot(s, wo_ref[...].astype(BF16), preferred_element_type=F32) + bo_ref[...]


def _adaln(cond8, w_e, b_e, w_o, b_o):
    n = 3 * D_MODEL
    wspec = pl.BlockSpec((D_MODEL, ADA_TN), lambda j: (0, j))
    vspec = pl.BlockSpec((1, ADA_TN), lambda j: (0, j))
    ospec = pl.BlockSpec((8, ADA_TN), lambda j: (0, j))
    return pl.pallas_call(
        _adaln_kernel,
        out_shape=(jax.ShapeDtypeStruct((8, n), F32), jax.ShapeDtypeStruct((8, n), F32)),
        grid=(n // ADA_TN,),
        in_specs=[pl.BlockSpec((8, D_MODEL), lambda j: (0, 0)), wspec, vspec, wspec, vspec],
        out_specs=(ospec, ospec),
        compiler_params=pltpu.CompilerParams(dimension_semantics=("arbitrary",), vmem_limit_bytes=VMEM_LIMIT),
        name="adaln",
    )(cond8, w_e, b_e.reshape(1, n), w_o, b_o.reshape(1, n))


def _head_inv_rms(x, bd):
    sq = x * x
    hi = sq.astype(BF16)
    lo = (sq - hi.astype(F32)).astype(BF16)
    ms = (jnp.dot(hi, bd, preferred_element_type=F32) + jnp.dot(lo, bd, preferred_element_type=F32)) * (1.0 / HEAD_DIM)
    return lax.rsqrt(ms + EPS)


def _rope(x, cos, sin_signed, first_half):
    partner = jnp.where(first_half, pltpu.roll(x, LANES - ROPE_FREQS, axis=1), pltpu.roll(x, ROPE_FREQS, axis=1))
    return x * cos + partner * sin_signed


def _dup_kv(x):
    lane = lax.broadcasted_iota(jnp.int32, (1, LANES), 1)
    swapped = pltpu.roll(x, HEAD_DIM, axis=1)
    return jnp.where(lane < HEAD_DIM, x, swapped), jnp.where(lane < HEAD_DIM, swapped, x)


def _tiled_keys(k):
    ka, kb = _dup_kv(k)
    return jnp.concatenate([ka, ka, kb, kb], axis=1).astype(BF16)


def _values_t(v):
    va, vb = _dup_kv(v)
    return jnp.concatenate([va.T, vb.T], axis=0).astype(BF16)


def _even_kernel(*refs, nc, n_chunks, lag, ring, windowed):
    if windowed:
        (xp_ref, xa_ref, modp_ref, moda_ref, g_ref, win_ref, cw_ref, cb_ref, qg_ref, kg_ref, sink_ref, wout_ref,
         cos_ref, sin_ref, ck_ref, cv_ref,
         y_ref,
         h_s, h_w, u_s, gate_s, q_s, k_s, vt_s, gb_s, u_w, gate_w, q_w, k_w, vt_w, gb_w, a_s, ck_s, cvt_s) = refs
        ko_ref = vo_ref = None
    else:
        (xp_ref, xa_ref, modp_ref, moda_ref, g_ref, win_ref, cw_ref, cb_ref, qg_ref, kg_ref, sink_ref, wout_ref,
         y_ref, ko_ref, vo_ref,
         h_s, h_w, u_s, gate_s, q_s, k_s, vt_s, gb_s, u_w, gate_w, q_w, k_w, vt_w, gb_w, a_s) = refs
    s = pl.program_id(0)
    i = s - 1 - lag
    ci = lax.rem(i, jnp.int32(nc))

    @pl.when(s == 0)
    def _init():
        k_s[...] = jnp.zeros(k_s.shape, BF16)
        vt_s[...] = jnp.zeros(vt_s.shape, BF16)

    if windowed:
        @pl.when((i >= 0) & (ci == 0))
        def _load_ctx():
            ck_s[...] = _tiled_keys(ck_ref[0])
            cvt_s[...] = _values_t(cv_ref[0])

    def normalize():
        yield from _normalize_items(h_w, xp_ref, g_ref, modp_ref)

    def project():
        o_bg, o_cg, o_xs, o_ga, o_q, o_k, _, o_gb = EVEN_OFFS[:8]

        def proj(start):
            return jnp.dot(h_s[...], win_ref[:, start:start + NT], preferred_element_type=F32)

        lane = lax.broadcasted_iota(jnp.int32, (1, LANES), 1)
        first_half = (lane & (2 * ROPE_FREQS - 1)) < ROPE_FREQS
        ri = lax.broadcasted_iota(jnp.int32, (NT, NT), 0) >> HEAD_SHIFT
        cj = lax.broadcasted_iota(jnp.int32, (NT, NT), 1) >> HEAD_SHIFT
        bd = (ri == cj).astype(BF16)

        def qk_norm(x, g):
            w = x.shape[1]
            inv = _head_inv_rms(x, bd[0:w, 0:w])
            pieces = []
            for j in range(w // LANES):
                lanes = slice(j * LANES, (j + 1) * LANES)
                xj = x[:, lanes] * inv[:, lanes] * g
                pieces.append(_rope(xj, cos_ref[...], sin_ref[...], first_half) if windowed else xj)
            return pieces

        tile = _mxu_cost(TM, D_MODEL, NT)

        def keys_values():
            kv = proj(o_k)
            k, = qk_norm(kv[:, 0:KV_WIDTH], kg_ref[...])
            v = kv[:, KV_WIDTH:2 * KV_WIDTH]
            if not windowed:
                ko_ref[0] = k
                vo_ref[0] = v
            k_w[...] = _tiled_keys(k)
            vt_w[...] = _values_t(v)

        def queries(n):
            for j, qj in enumerate(qk_norm(proj(o_q + n * NT), qg_ref[...])):
                q_w[:, n * NT + j * LANES:n * NT + (j + 1) * LANES] = (qj * ATTN_SCALE).astype(BF16)

        def conv_gate(n):
            gate_w[:, n * NT:(n + 1) * NT] = proj(o_bg + n * NT) * _silu(proj(o_ga + n * NT))

        def conv_input(n):
            u_w[:, n * NT:(n + 1) * NT] = proj(o_cg + n * NT) * proj(o_xs + n * NT)

        def attn_gate(n):
            gb_w[:, n * NT:(n + 1) * NT] = _silu(proj(o_gb + n * NT))

        items = [((2 * tile, 180), conv_gate, 0), ((tile, 350), keys_values, None), ((2 * tile, 40), conv_input, 0),
                 ((tile, 300), queries, 0), ((2 * tile, 180), conv_gate, 1), ((tile, 300), queries, 1),
                 ((2 * tile, 40), conv_input, 1), ((tile, 110), attn_gate, 0), ((tile, 110), attn_gate, 1)]
        for cost, emit, n in items:
            yield cost
            emit() if n is None else emit(n)

    def finish(slot):
        if windowed:
            prev_slot = lax.rem(slot + (ring - 1), jnp.int32(ring))
            next_slot = lax.rem(slot + 1, jnp.int32(ring))
            has_prev = ci > 0
            has_next = ci < nc - 1
        else:
            prev_slot = next_slot = has_prev = has_next = None

        yield 0, 400
        ext = _with_halo(u_s, slot, slice(None), prev_slot, next_slot, has_prev, has_next)
        conv = (_rows_down(ext, 1)[HALO:HALO + TM] * cw_ref[0:1, :]
                + ext[HALO:HALO + TM] * cw_ref[1:2, :]
                + _rows_up(ext, 1)[HALO:HALO + TM] * cw_ref[2:3, :]
                + cb_ref[...])
        a_s[:, 0:CONV_WIDTH] = (gate_s[slot] * conv).astype(BF16)

        lane_head = lax.broadcasted_iota(jnp.int32, (1, GROUP_W), 1) >> HEAD_SHIFT
        cols4 = Q_PER_KV * BLOCK
        n_keys = 3 * BLOCK + ck_s.shape[0] if windowed else TM
        key_i =lax.broadcasted_iota(jnp.int32, (BLOCK, cols4), 0)
        qry_i = lax.broadcasted_iota(jnp.int32, (BLOCK, cols4), 1) & (BLOCK - 1)
        upper = lax.broadcasted_iota(jnp.int32, (LANES, LANES), 0) < HEAD_DIM
        halves = (slice(0, BLOCK), slice(BLOCK, TM))
        for qb in range(TM // BLOCK):
            own = halves[qb]
            if windowed:
                if qb == 0:
                    blocks = ((prev_slot, halves[1]), (slot, halves[0]), (slot, halves[1]))
                    prev_in, next_in = has_prev, True
                else:
                    blocks = ((slot, halves[0]), (slot, halves[1]), (next_slot, halves[0]))
                    prev_in, next_in = True, has_next
                prev_ok = (key_i - qry_i) >= jnp.where(prev_in, 0, BLOCK)
                next_ok = (qry_i - key_i) >= jnp.where(next_in, 0, BLOCK)
            for kv in range(N_KV_HEADS):
                cols = slice(kv * GROUP_W, (kv + 1) * GROUP_W)
                vrows = slice(kv * LANES, (kv + 1) * LANES)
                yield _mxu_cost(n_keys, GROUP_W, cols4), 60
                qblk = q_s[slot, own, cols]
                q4 = jnp.concatenate(
                    [jnp.where(lane_head == g, qblk, jnp.zeros_like(qblk)) for g in range(Q_PER_KV)], axis=0)

                def scores_t(kk):
                    return lax.dot_general(kk, q4, (((1,), (1,)), ((), ())), preferred_element_type=F32)

                if windowed:
                    s_loc = scores_t(jnp.concatenate([k_s[sl, r, cols] for sl, r in blocks], axis=0))
                    vt_loc = jnp.concatenate([vt_s[sl, vrows, r] for sl, r in blocks], axis=1)
                    s_ctx = scores_t(ck_s[:, cols])
                    yield 0, n_keys * cols4 * 5 // (8 * LANES * 4)
                    groups = [([jnp.where(prev_ok, s_loc[0:BLOCK], NEG), s_loc[BLOCK:2 * BLOCK],
                                jnp.where(next_ok, s_loc[2 * BLOCK:3 * BLOCK], NEG)], vt_loc),
                              ([s_ctx], cvt_s[vrows, :])]
                else:
                    s_own = scores_t(k_s[slot, :, cols])
                    yield 0, n_keys * cols4 * 5 // (8 * LANES * 4)
                    groups = [([s_own], vt_s[slot, vrows, :])]
                snk = jnp.concatenate(
                    [jnp.full((1, BLOCK), sink_ref[kv * Q_PER_KV + g], F32) for g in range(Q_PER_KV)], axis=1)
                mx = snk
                for ss, _ in groups:
                    for sc in ss:
                        mx = jnp.maximum(mx, jnp.max(sc, axis=0, keepdims=True))
                den = jnp.exp(snk - mx)
                probs = []
                for ss, vt in groups:
                    es = []
                    for sc in ss:
                        e = jnp.exp(sc - mx)
                        den = den + jnp.sum(e, axis=0, keepdims=True)
                        es.append(e.astype(BF16))
                    probs.append((es[0] if len(es) == 1 else jnp.concatenate(es, axis=0), vt))
                yield _mxu_cost(LANES, n_keys, cols4), 120
                o_t = jnp.zeros((LANES, cols4), F32)
                for e_all, vt in probs:
                    o_t = o_t + jnp.dot(vt, e_all, preferred_element_type=F32)
                o_t = o_t * (1.0 / den)
                o01 = jnp.where(upper, o_t[:, 0:BLOCK], o_t[:, BLOCK:2 * BLOCK]).T
                o23 = jnp.where(upper, o_t[:, 2 * BLOCK:3 * BLOCK], o_t[:, 3 * BLOCK:4 * BLOCK]).T
                a_s[own, CONV_WIDTH + kv * GROUP_W:CONV_WIDTH + (kv + 1) * GROUP_W] = (
                    jnp.concatenate([o01, o23], axis=1) * gb_s[slot, own, cols]).astype(BF16)

        yield from _out_proj_items(lambda: a_s[...], wout_ref, xa_ref, moda_ref, y_ref)

    staged = ((u_w, u_s), (gate_w, gate_s), (q_w, q_s), (k_w, k_s), (vt_w, vt_s), (gb_w, gb_s))
    _pipeline_steps(s, n_chunks, lag, ring, (h_w, h_s), staged, EVEN_ORDER, normalize, project, finish)


def _chunk_specs(n_chunks, nc, lag, per_seq_mod):
    last = n_chunks - 1
    norm_chunk = lambda s: jnp.minimum(s, last)
    proj_chunk = lambda s: jnp.clip(s - 1, 0, last)
    fin_chunk = lambda s: jnp.clip(s - 1 - lag, 0, last)
    seq_of = (lambda j: j // nc) if per_seq_mod else (lambda j: 0)
    return dict(
        x_norm=pl.BlockSpec((1, TM, D_MODEL), lambda s: (norm_chunk(s), 0, 0)),
        x_fin=pl.BlockSpec((1, TM, D_MODEL), lambda s: (fin_chunk(s), 0, 0)),
        mod_norm=pl.BlockSpec((1, 1, 3 * D_MODEL), lambda s: (seq_of(norm_chunk(s)), 0, 0)),
        mod_fin=pl.BlockSpec((1, 1, 3 * D_MODEL), lambda s: (seq_of(fin_chunk(s)), 0, 0)),
        proj_chunk=proj_chunk, fin_chunk=fin_chunk)


def _const_spec(shape):
    return pl.BlockSpec(shape, lambda c: (0,) * len(shape))


def _even_layer(x, mod, norm_g, w_in, conv_w, conv_b, q_g, k_g, sink, w_out, rope=None, ctx=None):
    n_seq, seq_len, _ = x.shape
    windowed = rope is not None
    nc = seq_len // TM
    n_chunks = n_seq * nc
    lag, ring = (2, 4) if windowed else (1, 2)
    assert windowed or nc == 1
    sp = _chunk_specs(n_chunks, nc, lag, windowed)
    xc = x.reshape(n_chunks, TM, D_MODEL)
    in_specs = [
        sp["x_norm"], sp["x_fin"], sp["mod_norm"], sp["mod_fin"],
        _const_spec((1, D_MODEL)),
        _const_spec((D_MODEL, EVEN_IN)),
        _const_spec((3, CONV_WIDTH)),
        _const_spec((1, CONV_WIDTH)),
        _const_spec((1, LANES)),
        _const_spec((1, LANES)),
        pl.BlockSpec(memory_space=pltpu.SMEM),
        _const_spec((D_MODEL, D_MODEL)),
    ]
    args = [xc, xc, mod, mod, norm_g.reshape(1, D_MODEL), w_in, conv_w, conv_b.reshape(1, CONV_WIDTH),
            jnp.tile(q_g, LANES // HEAD_DIM).reshape(1, LANES), jnp.tile(k_g, LANES // HEAD_DIM).reshape(1, LANES),
            sink, w_out]
    per_chunk = [
        ((TM, CONV_WIDTH), F32),
        ((TM, CONV_WIDTH), F32),
        ((TM, ATTN_WIDTH), BF16),
        ((TM, 2 * GROUP_W), BF16),
        ((2 * LANES, TM), BF16),
        ((TM, ATTN_WIDTH), F32),
    ]
    scratch = (
        [pltpu.VMEM((TM, D_MODEL), BF16)] * 2
        + [pltpu.VMEM((ring,) + shape, dt) for shape, dt in per_chunk]
        + [pltpu.VMEM(shape, dt) for shape, dt in per_chunk]
        + [pltpu.VMEM((TM, D_MODEL), BF16)])
    y_shape = jax.ShapeDtypeStruct(xc.shape, F32)
    y_spec = pl.BlockSpec((1, TM, D_MODEL), lambda c: (sp["fin_chunk"](c), 0, 0))
    if windowed:
        cos, sin = rope
        ck, cv = ctx
        rope_spec = pl.BlockSpec((TM, LANES), lambda c: (sp["proj_chunk"](c) % nc, 0))
        ctx_spec = pl.BlockSpec((1,) + ck.shape[1:], lambda c: (sp["fin_chunk"](c) // nc, 0, 0))
        in_specs += [rope_spec, rope_spec, ctx_spec, ctx_spec]
        args += [cos, sin, ck, cv]
        scratch += [pltpu.VMEM((ck.shape[1], 2 * GROUP_W), BF16), pltpu.VMEM((2 * LANES, cv.shape[1]), BF16)]
        out_shape, out_specs = y_shape, y_spec
    else:
        kv_shape = jax.ShapeDtypeStruct((n_chunks, TM, KV_WIDTH), F32)
        kv_spec = pl.BlockSpec((1, TM, KV_WIDTH), lambda c: (sp["proj_chunk"](c), 0, 0))
        out_shape, out_specs = (y_shape, kv_shape, kv_shape), (y_spec, kv_spec, kv_spec)
    out = pl.pallas_call(
        functools.partial(_even_kernel, nc=nc, n_chunks=n_chunks, lag=lag, ring=ring, windowed=windowed),
        out_shape=out_shape,
        grid=(n_chunks + lag + 1,),
        in_specs=in_specs,
        out_specs=out_specs,
        scratch_shapes=scratch,
        compiler_params=PIPELINE_PARAMS,
        name="even_latent" if windowed else "even_context",
    )(*args)
    if windowed:
        return out.reshape(x.shape)
    y, k, v = out
    return y.reshape(x.shape), k.reshape(n_seq, seq_len, KV_WIDTH), v.reshape(n_seq, seq_len, KV_WIDTH)


def _odd_kernel(xp_ref, xa_ref, modp_ref, moda_ref, g_ref, win_ref, pw_ref, ps_ref, wout_ref, y_ref,
                h_s, h_w, u_s, sg_s, u_w, sg_w, *, nc, n_chunks, lag, ring):
    s = pl.program_id(0)
    i = s - 1 - lag
    ci = lax.rem(i, jnp.int32(nc))

    def normalize():
        yield from _normalize_items(h_w, xp_ref, g_ref, modp_ref)

    def project():
        tile = _mxu_cost(TM, D_MODEL, NT)
        for n in range(D_MODEL // NT):
            yield tile, 10
            cols = slice(n * NT, (n + 1) * NT)
            u_w[:, cols] = jnp.dot(h_s[...], win_ref[:, cols], preferred_element_type=F32)
        for n in range(D_MODEL // NT):
            yield tile, 110
            cols = slice(n * NT, (n + 1) * NT)
            sg_w[:, cols] = _silu(jnp.dot(h_s[...], win_ref[:, D_MODEL + n * NT:D_MODEL + (n + 1) * NT],
                                          preferred_element_type=F32))

    def finish(slot):
        if nc > 1:
            prev_slot = lax.rem(slot + (ring - 1), jnp.int32(ring))
            next_slot = lax.rem(slot + 1, jnp.int32(ring))
            has_prev = ci > 0
            has_next = ci < nc - 1
        else:
            prev_slot = next_slot = has_prev = has_next = None
        t = ci * TM + lax.broadcasted_iota(jnp.int32, (TM, 1), 0)
        outs = []
        for gi, w in enumerate(POOL_SIZES):
            yield _mxu_cost(TM, POOL_GROUP, POOL_GROUP), 140 + 120 * gi
            cols = slice(gi * POOL_GROUP, (gi + 1) * POOL_GROUP)
            ext = _with_halo(u_s, slot, cols, prev_slot, next_slot, has_prev, has_next)
            acc = ext + _rows_down(ext, 1)
            half = 1
            while 2 * half < w:
                acc = _rows_down(acc, half) + _rows_up(acc, half)
                half *= 2
            cnt = (jnp.minimum(t + w // 2, nc * TM) - jnp.maximum(t - w // 2, 0)).astype(F32)
            pooled = acc[HALO:HALO + TM] / cnt - ext[HALO:HALO + TM]
            outs.append(jnp.dot(pooled.astype(BF16), pw_ref[gi], preferred_element_type=F32))
        yield 0, 130
        y = (jnp.concatenate(outs, axis=1) * ps_ref[...] * sg_s[slot]).astype(BF16)
        yield from _out_proj_items(lambda: y, wout_ref, xa_ref, moda_ref, y_ref)

    _pipeline_steps(s, n_chunks, lag, ring, (h_w, h_s), ((u_w, u_s), (sg_w, sg_s)), ODD_ORDER,
                    normalize, project, finish)


def _odd_layer(x, mod, per_seq_mod, norm_g, w_in, pool_w, pool_scale, w_out):
    n_seq, seq_len, _ = x.shape
    nc = seq_len // TM
    n_chunks = n_seq * nc
    lag, ring = (2, 4) if nc > 1 else (1, 2)
    sp = _chunk_specs(n_chunks, nc, lag, per_seq_mod)
    xc = x.reshape(n_chunks, TM, D_MODEL)
    in_specs = [
        sp["x_norm"], sp["x_fin"], sp["mod_norm"], sp["mod_fin"],
        _const_spec((1, D_MODEL)),
        _const_spec((D_MODEL, 2 * D_MODEL)),
        _const_spec(pool_w.shape),
        _const_spec((1, D_MODEL)),
        _const_spec((D_MODEL, D_MODEL)),
    ]
    out = pl.pallas_call(
        functools.partial(_odd_kernel, nc=nc, n_chunks=n_chunks, lag=lag, ring=ring),
        out_shape=jax.ShapeDtypeStruct(xc.shape, F32),
        grid=(n_chunks + lag + 1,),
        in_specs=in_specs,
        out_specs=pl.BlockSpec((1, TM, D_MODEL), lambda c: (sp["fin_chunk"](c), 0, 0)),
        scratch_shapes=[
            pltpu.VMEM((TM, D_MODEL), BF16),
            pltpu.VMEM((TM, D_MODEL), BF16),
            pltpu.VMEM((ring, TM, D_MODEL), F32),
            pltpu.VMEM((ring, TM, D_MODEL), F32),
            pltpu.VMEM((TM, D_MODEL), F32),
            pltpu.VMEM((TM, D_MODEL), F32),
        ],
        compiler_params=PIPELINE_PARAMS,
        name="odd_latent" if per_seq_mod else "odd_context",
    )(xc, xc, mod, mod, norm_g.reshape(1, D_MODEL), w_in, pool_w, pool_scale.reshape(1, D_MODEL), w_out)
    return out.reshape(x.shape)


def _rope_tables(seq_len):
    n_rows = seq_len // GRID_W
    row = jnp.repeat(jnp.arange(n_rows), GRID_W).astype(F32)
    col = jnp.tile(jnp.arange(GRID_W), n_rows).astype(F32)
    inv = ROPE_BASE ** (-jnp.arange(ROPE_FREQS, dtype=F32) / ROPE_FREQS)
    ang = jnp.stack([row[:, None] * inv, col[:, None] * inv], axis=1)
    cos = jnp.broadcast_to(jnp.cos(ang)[:, :, None, :], (seq_len, 2, 2, ROPE_FREQS))
    sin = jnp.sin(ang)[:, :, None, :] * jnp.array([-1.0, 1.0], F32)[None, None, :, None]
    cos = cos.reshape(seq_len, HEAD_DIM)
    sin = sin.reshape(seq_len, HEAD_DIM)
    return jnp.tile(cos, (1, LANES // HEAD_DIM)), jnp.tile(sin, (1, LANES // HEAD_DIM))


def kernel(x_prompt, x_sample, cache_k, cache_v, c, c_ctx, ada_w_e, ada_b_e, norm_g_e, w_in_e, conv_w, conv_b,
           q_norm_g, k_norm_g, sink, w_out_e, ada_w_o, ada_b_o, norm_g_o, w_in_o, pool_w, pool_scale, w_out_o):
    n_dec = x_sample.shape[0]
    depth = ada_w_e.shape[0] + ada_w_o.shape[0]
    assert ada_w_e.shape[0] == 1 and ada_w_o.shape[0] == 1 and n_dec + 1 <= 8
    cond8 = jnp.concatenate([c_ctx[None], c, jnp.zeros((8 - 1 - n_dec, D_MODEL), F32)], axis=0)
    mod_e, mod_o = _adaln(cond8, ada_w_e[0], ada_b_e[0], ada_w_o[0], ada_b_o[0])
    rope = _rope_tables(x_sample.shape[1])

    yp, ys = x_prompt, x_sample
    new_k, new_v = [], []
    for layer in range(depth):
        i = layer // 2
        if layer % 2 == 0:
            mod = mod_e.reshape(8, 1, 3 * D_MODEL)
            w_in = w_in_e[i].astype(BF16)
            w_out = w_out_e[i].astype(BF16)
            common = (norm_g_e[i], w_in, conv_w[i], conv_b[i], q_norm_g[i], k_norm_g[i], sink[i], w_out)
            yp, k, v = _even_layer(yp, mod[0:1], *common)
            new_k.append(k.reshape(k.shape[0], k.shape[1], N_KV_HEADS, HEAD_DIM))
            new_v.append(v.reshape(v.shape[0], v.shape[1], N_KV_HEADS, HEAD_DIM))
            ck = cache_k[:, i].reshape(n_dec, cache_k.shape[2], KV_WIDTH)
            cv = cache_v[:, i].reshape(n_dec, cache_v.shape[2], KV_WIDTH)
            ys = _even_layer(ys, mod[1:1 + n_dec], *common, rope=rope, ctx=(ck, cv))
        else:
            mod = mod_o.reshape(8, 1, 3 * D_MODEL)
            common = (norm_g_o[i], w_in_o[i].astype(BF16), pool_w[i].astype(BF16), pool_scale[i],
                      w_out_o[i].astype(BF16))
            yp = _odd_layer(yp, mod[0:1], False, *common)
            ys = _odd_layer(ys, mod[1:1 + n_dec], True, *common)
    return yp, ys, jnp.stack(new_k, axis=1), jnp.stack(new_v, axis=1)
```

```python
import functools

import jax
import jax.numpy as jnp
import numpy as np
from jax import lax
from jax.experimental import pallas as pl
from jax.experimental.pallas import tpu as pltpu

F32 = jnp.float32
BF16 = jnp.bfloat16

D_MODEL = 1024
GRID_W = 64
HEAD_DIM = 64
HEAD_SHIFT = 6
ATTN_WIDTH = D_MODEL // 2
N_Q_HEADS = ATTN_WIDTH // HEAD_DIM
N_KV_HEADS = N_Q_HEADS // 4
Q_PER_KV = N_Q_HEADS // N_KV_HEADS
KV_WIDTH = N_KV_HEADS * HEAD_DIM
CONV_WIDTH = D_MODEL - ATTN_WIDTH
WINDOW = 128
BLOCK = 128
ROPE_BASE = 10000.0
ROPE_FREQS = HEAD_DIM // 4
ATTN_SCALE = HEAD_DIM ** -0.5
NEG = -1e30
POOL_SIZES = (2, 4, 8, 16)
POOL_GROUP = D_MODEL // len(POOL_SIZES)
EPS = 1e-6
EVEN_SIZES = (CONV_WIDTH, CONV_WIDTH, CONV_WIDTH, CONV_WIDTH, ATTN_WIDTH, KV_WIDTH, KV_WIDTH, ATTN_WIDTH)
EVEN_IN = sum(EVEN_SIZES)
EVEN_OFFS = tuple(int(s) for s in np.cumsum((0,) + EVEN_SIZES))

TM = 256
NT = 256
NORM_ROWS = 64
EVEN_ORDER = "PF" + "FNPFPF" * 4 + "FFFF"
ODD_ORDER = "PNFP" * 4 + "FFFFF"
LANES = 128
GROUP_W = Q_PER_KV * HEAD_DIM
HALO = 8
ADA_TN = 768
VMEM_LIMIT = 56 * 1024 * 1024
PIPELINE_PARAMS = pltpu.CompilerParams(dimension_semantics=("arbitrary",), vmem_limit_bytes=VMEM_LIMIT)


def _silu(x):
    return x / (1.0 + jnp.exp(-x))


def _rows_down(x, k):
    return pltpu.roll(x, k, axis=0)


def _rows_up(x, k):
    return pltpu.roll(x, x.shape[0] - k, axis=0)


def _rms_modulate(x, g, shift, scale):
    ms = jnp.mean(x * x, axis=-1, keepdims=True)
    return (x * lax.rsqrt(ms + EPS) * g) * (1.0 + scale) + shift


def _opaque_zero(x):
    bits = pltpu.bitcast(x, jnp.uint32)
    return pltpu.bitcast((bits >> 16) >> 16, F32)


def _normalize_items(h_s, x_ref, g_ref, mod_ref, tokens):
    shift = mod_ref[0, :, 0:D_MODEL]
    scale = mod_ref[0, :, D_MODEL:2 * D_MODEL]
    for r in range(0, TM, NORM_ROWS):
        yield 0, 170
        h = _rms_modulate(x_ref[0, r:r + NORM_ROWS], g_ref[...], shift, scale)
        h_s[r:r + NORM_ROWS] = h.astype(BF16)
        tokens.append(_opaque_zero(h[0:8, 0:LANES])[0:1])


def _out_proj_items(get_a, wout_ref, x_ref, mod_ref, y_ref):
    for n in range(D_MODEL // NT):
        yield _mxu_cost(TM, D_MODEL, NT), 70
        cols = slice(n * NT, (n + 1) * NT)
        mixed = jnp.dot(get_a(), wout_ref[:, cols], preferred_element_type=F32)
        y_ref[0, :, cols] = x_ref[0, :, cols] + mod_ref[0, :, 2 * D_MODEL + n * NT:2 * D_MODEL + (n + 1) * NT] * mixed


def _mxu_cost(m, k, n):
    tiles = -(-k // NT) * -(-n // NT)
    return tiles * max(m // 2, 64) // 2


def _weave(order, **stages):
    by_letter = {name[0].upper(): stage for name, stage in stages.items()}
    for stage in by_letter.values():
        next(stage, None)
    for letter in order:
        if letter in by_letter:
            next(by_letter[letter], None)
    for stage in by_letter.values():
        for _ in stage:
            pass


def _with_halo(u_s, slot, cols, prev_slot, next_slot, has_prev, has_next):
    u = u_s[slot, :, cols]
    zero = jnp.zeros((HALO, u.shape[1]), F32)
    if prev_slot is None:
        top = bot = zero
    else:
        top = jnp.where(has_prev, u_s[prev_slot, TM - HALO:TM, cols], zero)
        bot = jnp.where(has_next, u_s[next_slot, 0:HALO, cols], zero)
    return jnp.concatenate([top, u, bot], axis=0)


def _pipeline_steps(s, n_chunks, lag, ring, normalized, staged, order, normalize, project, finish):
    fin_slot = lax.rem(s + (2 * ring - 1 - lag), jnp.int32(ring))

    @pl.when((s >= 1) & (s <= n_chunks))
    def _publish_normalized():
        h_w, h_s = normalized
        h_s[...] = h_w[...]

    @pl.when((s >= 2) & (s <= n_chunks + 1))
    def _publish_projected():
        slot = lax.rem(s + (ring - 2), jnp.int32(ring))
        for stage_ref, ring_ref in staged:
            ring_ref[slot] = stage_ref[...]

    @pl.when(s == 0)
    def _first():
        _weave(order, normalize=normalize([]))

    @pl.when((s >= 1) & (s <= lag))
    def _fill():
        _weave(order, project=project(), normalize=normalize([]))

    @pl.when((s > lag) & (s <= n_chunks))
    def _steady():
        tokens = []
        _weave(order, finish=finish(fin_slot, tokens), project=project(), normalize=normalize(tokens))

    @pl.when(s > n_chunks)
    def _drain():
        _weave(order, finish=finish(fin_slot, []))


def _adaln_kernel(cond_ref, we_ref, be_ref, wo_ref, bo_ref, oe_ref, oo_ref):
    s = _silu(cond_ref[...]).astype(BF16)
    oe_ref[...] = jnp.dot(s, we_ref[...].astype(BF16), preferred_element_type=F32) + be_ref[...]
    oo_ref[...] = jnp.dot(s, wo_ref[...].astype(BF16), preferred_element_type=F32) + bo_ref[...]


def _adaln(cond8, w_e, b_e, w_o, b_o):
    n = 3 * D_MODEL
    wspec = pl.BlockSpec((D_MODEL, ADA_TN), lambda j: (0, j))
    vspec = pl.BlockSpec((1, ADA_TN), lambda j: (0, j))
    ospec = pl.BlockSpec((8, ADA_TN), lambda j: (0, j))
    return pl.pallas_call(
        _adaln_kernel,
        out_shape=(jax.ShapeDtypeStruct((8, n), F32), jax.ShapeDtypeStruct((8, n), F32)),
        grid=(n // ADA_TN,),
        in_specs=[pl.BlockSpec((8, D_MODEL), lambda j: (0, 0)), wspec, vspec, wspec, vspec],
        out_specs=(ospec, ospec),
        compiler_params=pltpu.CompilerParams(dimension_semantics=("arbitrary",), vmem_limit_bytes=VMEM_LIMIT),
        name="adaln",
    )(cond8, w_e, b_e.reshape(1, n), w_o, b_o.reshape(1, n))


def _head_inv_rms(x, bd):
    sq = x * x
    hi = sq.astype(BF16)
    lo = (sq - hi.astype(F32)).astype(BF16)
    ms = (jnp.dot(hi, bd, preferred_element_type=F32) + jnp.dot(lo, bd, preferred_element_type=F32)) * (1.0 / HEAD_DIM)
    return lax.rsqrt(ms + EPS)


def _rope(x, cos, sin_signed, first_half):
    partner = jnp.where(first_half, pltpu.roll(x, LANES - ROPE_FREQS, axis=1), pltpu.roll(x, ROPE_FREQS, axis=1))
    return x * cos + partner * sin_signed


def _dup_kv(x):
    lane = lax.broadcasted_iota(jnp.int32, (1, LANES), 1)
    swapped = pltpu.roll(x, HEAD_DIM, axis=1)
    return jnp.where(lane < HEAD_DIM, x, swapped), jnp.where(lane < HEAD_DIM, swapped, x)


def _tiled_keys(k):
    ka, kb = _dup_kv(k)
    return jnp.concatenate([ka, ka, kb, kb], axis=1).astype(BF16)


def _values_t(v):
    va, vb = _dup_kv(v)
    return jnp.concatenate([va.T, vb.T], axis=0).astype(BF16)


def _even_kernel(*refs, nc, n_chunks, lag, ring, windowed):
    if windowed:
        (xp_ref, xa_ref, modp_ref, moda_ref, g_ref, win_ref, cw_ref, cb_ref, qg_ref, kg_ref, sink_ref, wout_ref,
         cos_ref, sin_ref, ck_ref, cv_ref,
         y_ref,
         h_s, h_w, u_s, gate_s, q_s, k_s, vt_s, gb_s, u_w, gate_w, q_w, k_w, vt_w, gb_w, a_s, ck_s, cvt_s) = refs
        ko_ref = vo_ref = None
    else:
        (xp_ref, xa_ref, modp_ref, moda_ref, g_ref, win_ref, cw_ref, cb_ref, qg_ref, kg_ref, sink_ref, wout_ref,
         y_ref, ko_ref, vo_ref,
         h_s, h_w, u_s, gate_s, q_s, k_s, vt_s, gb_s, u_w, gate_w, q_w, k_w, vt_w, gb_w, a_s) = refs
    s = pl.program_id(0)
    i = s - 1 - lag
    ci = lax.rem(i, jnp.int32(nc))

    @pl.when(s == 0)
    def _init():
        k_s[...] = jnp.zeros(k_s.shape, BF16)
        vt_s[...] = jnp.zeros(vt_s.shape, BF16)

    if windowed:
        @pl.when((i >= 0) & (ci == 0))
        def _load_ctx():
            ck_s[...] = _tiled_keys(ck_ref[0])
            cvt_s[...] = _values_t(cv_ref[0])

    def normalize(tokens):
        yield from _normalize_items(h_w, xp_ref, g_ref, modp_ref, tokens)

    def project():
        o_bg, o_cg, o_xs, o_ga, o_q, o_k, _, o_gb = EVEN_OFFS[:8]

        def proj(start):
            return jnp.dot(h_s[...], win_ref[:, start:start + NT], preferred_element_type=F32)

        lane = lax.broadcasted_iota(jnp.int32, (1, LANES), 1)
        first_half = (lane & (2 * ROPE_FREQS - 1)) < ROPE_FREQS
        ri = lax.broadcasted_iota(jnp.int32, (NT, NT), 0) >> HEAD_SHIFT
        cj = lax.broadcasted_iota(jnp.int32, (NT, NT), 1) >> HEAD_SHIFT
        bd = (ri == cj).astype(BF16)

        def qk_norm(x, g):
            w = x.shape[1]
            inv = _head_inv_rms(x, bd[0:w, 0:w])
            pieces = []
            for j in range(w // LANES):
                lanes = slice(j * LANES, (j + 1) * LANES)
                xj = x[:, lanes] * inv[:, lanes] * g
                pieces.append(_rope(xj, cos_ref[...], sin_ref[...], first_half) if windowed else xj)
            return pieces

        tile = _mxu_cost(TM, D_MODEL, NT)

        def keys_values():
            kv = proj(o_k)
            k, = qk_norm(kv[:, 0:KV_WIDTH], kg_ref[...])
            v = kv[:, KV_WIDTH:2 * KV_WIDTH]
            if not windowed:
                ko_ref[0] = k
                vo_ref[0] = v
            k_w[...] = _tiled_keys(k)
            vt_w[...] = _values_t(v)

        def queries(n):
            for j, qj in enumerate(qk_norm(proj(o_q + n * NT), qg_ref[...])):
                q_w[:, n * NT + j * LANES:n * NT + (j + 1) * LANES] = (qj * ATTN_SCALE).astype(BF16)

        def conv_gate(n):
            gate_w[:, n * NT:(n + 1) * NT] = proj(o_bg + n * NT) * _silu(proj(o_ga + n * NT))

        def conv_input(n):
            u_w[:, n * NT:(n + 1) * NT] = proj(o_cg + n * NT) * proj(o_xs + n * NT)

        def attn_gate(n):
            gb_w[:, n * NT:(n + 1) * NT] = _silu(proj(o_gb + n * NT))

        items = [((2 * tile, 180), conv_gate, 0), ((tile, 350), keys_values, None), ((2 * tile, 40), conv_input, 0),
                 ((tile, 300), queries, 0), ((2 * tile, 180), conv_gate, 1), ((tile, 300), queries, 1),
                 ((2 * tile, 40), conv_input, 1), ((tile, 110), attn_gate, 0), ((tile, 110), attn_gate, 1)]
        for cost, emit, n in items:
            yield cost
            emit() if n is None else emit(n)

    def finish(slot, tokens):
        if windowed:
            prev_slot = lax.rem(slot + (ring - 1), jnp.int32(ring))
            next_slot = lax.rem(slot + 1, jnp.int32(ring))
            has_prev = ci > 0
            has_next = ci < nc - 1
        else:
            prev_slot = next_slot = has_prev = has_next = None

        yield 0, 400
        ext = _with_halo(u_s, slot, slice(None), prev_slot, next_slot, has_prev, has_next)
        conv = (_rows_down(ext, 1)[HALO:HALO + TM] * cw_ref[0:1, :]
                + ext[HALO:HALO + TM] * cw_ref[1:2, :]
                + _rows_up(ext, 1)[HALO:HALO + TM] * cw_ref[2:3, :]
                + cb_ref[...])
        a_s[:, 0:CONV_WIDTH] = (gate_s[slot] * conv).astype(BF16)

        lane_head = lax.broadcasted_iota(jnp.int32, (1, GROUP_W), 1) >> HEAD_SHIFT
        cols4 = Q_PER_KV * BLOCK
        n_keys = 3 * BLOCK + ck_s.shape[0] if windowed else TM
        key_i =lax.broadcasted_iota(jnp.int32, (BLOCK, cols4), 0)
        qry_i = lax.broadcasted_iota(jnp.int32, (BLOCK, cols4), 1) & (BLOCK - 1)
        upper = lax.broadcasted_iota(jnp.int32, (LANES, LANES), 0) < HEAD_DIM
        halves = (slice(0, BLOCK), slice(BLOCK, TM))
        for qb in range(TM // BLOCK):
            own = halves[qb]
            if windowed:
                if qb == 0:
                    blocks = ((prev_slot, halves[1]), (slot, halves[0]), (slot, halves[1]))
                    prev_in, next_in = has_prev, True
                else:
                    blocks = ((slot, halves[0]), (slot, halves[1]), (next_slot, halves[0]))
                    prev_in, next_in = True, has_next
                prev_ok = (key_i - qry_i) >= jnp.where(prev_in, 0, BLOCK)
                next_ok = (qry_i - key_i) >= jnp.where(next_in, 0, BLOCK)
            for kv in range(N_KV_HEADS):
                cols = slice(kv * GROUP_W, (kv + 1) * GROUP_W)
                vrows = slice(kv * LANES, (kv + 1) * LANES)
                yield _mxu_cost(n_keys, GROUP_W, cols4), 60
                qblk = q_s[slot, own, cols]
                q4 = jnp.concatenate(
                    [jnp.where(lane_head == g, qblk, jnp.zeros_like(qblk)) for g in range(Q_PER_KV)], axis=0)

                def scores_t(kk):
                    return lax.dot_general(kk, q4, (((1,), (1,)), ((), ())), preferred_element_type=F32)

                if windowed:
                    s_loc = scores_t(jnp.concatenate([k_s[sl, r, cols] for sl, r in blocks], axis=0))
                    vt_loc = jnp.concatenate([vt_s[sl, vrows, r] for sl, r in blocks], axis=1)
                    s_ctx = scores_t(ck_s[:, cols])
                    yield 0, n_keys * cols4 * 5 // (8 * LANES * 4)
                    groups = [([jnp.where(prev_ok, s_loc[0:BLOCK], NEG), s_loc[BLOCK:2 * BLOCK],
                                jnp.where(next_ok, s_loc[2 * BLOCK:3 * BLOCK], NEG)], vt_loc),
                              ([s_ctx], cvt_s[vrows, :])]
                else:
                    s_own = scores_t(k_s[slot, :, cols])
                    yield 0, n_keys * cols4 * 5 // (8 * LANES * 4)
                    groups = [([s_own], vt_s[slot, vrows, :])]
                snk = jnp.concatenate(
                    [jnp.full((1, BLOCK), sink_ref[kv * Q_PER_KV + g], F32) for g in range(Q_PER_KV)], axis=1)
                if tokens:
                    snk = snk + jnp.concatenate([tokens.pop(0)] * Q_PER_KV, axis=1)
                mx = snk
                for ss, _ in groups:
                    for sc in ss:
                        mx = jnp.maximum(mx, jnp.max(sc, axis=0, keepdims=True))
                den = jnp.exp(snk - mx)
                probs = []
                for ss, vt in groups:
                    es = []
                    for sc in ss:
                        e = jnp.exp(sc - mx)
                        den = den + jnp.sum(e, axis=0, keepdims=True)
                        es.append(e.astype(BF16))
                    probs.append((es[0] if len(es) == 1 else jnp.concatenate(es, axis=0), vt))
                yield _mxu_cost(LANES, n_keys, cols4), 120
                o_t = jnp.zeros((LANES, cols4), F32)
                for e_all, vt in probs:
                    o_t = o_t + jnp.dot(vt, e_all, preferred_element_type=F32)
                o_t = o_t * (1.0 / den)
                o01 = jnp.where(upper, o_t[:, 0:BLOCK], o_t[:, BLOCK:2 * BLOCK]).T
                o23 = jnp.where(upper, o_t[:, 2 * BLOCK:3 * BLOCK], o_t[:, 3 * BLOCK:4 * BLOCK]).T
                a_s[own, CONV_WIDTH + kv * GROUP_W:CONV_WIDTH + (kv + 1) * GROUP_W] = (
                    jnp.concatenate([o01, o23], axis=1) * gb_s[slot, own, cols]).astype(BF16)

        yield from _out_proj_items(lambda: a_s[...], wout_ref, xa_ref, moda_ref, y_ref)

    staged = ((u_w, u_s), (gate_w, gate_s), (q_w, q_s), (k_w, k_s), (vt_w, vt_s), (gb_w, gb_s))
    _pipeline_steps(s, n_chunks, lag, ring, (h_w, h_s), staged, EVEN_ORDER, normalize, project, finish)


def _chunk_specs(n_chunks, nc, lag, per_seq_mod):
    last = n_chunks - 1
    norm_chunk = lambda s: jnp.minimum(s, last)
    proj_chunk = lambda s: jnp.clip(s - 1, 0, last)
    fin_chunk = lambda s: jnp.clip(s - 1 - lag, 0, last)
    seq_of = (lambda j: j // nc) if per_seq_mod else (lambda j: 0)
    return dict(
        x_norm=pl.BlockSpec((1, TM, D_MODEL), lambda s: (norm_chunk(s), 0, 0)),
        x_fin=pl.BlockSpec((1, TM, D_MODEL), lambda s: (fin_chunk(s), 0, 0)),
        mod_norm=pl.BlockSpec((1, 1, 3 * D_MODEL), lambda s: (seq_of(norm_chunk(s)), 0, 0)),
        mod_fin=pl.BlockSpec((1, 1, 3 * D_MODEL), lambda s: (seq_of(fin_chunk(s)), 0, 0)),
        proj_chunk=proj_chunk, fin_chunk=fin_chunk)


def _const_spec(shape):
    return pl.BlockSpec(shape, lambda c: (0,) * len(shape))


def _even_layer(x, mod, norm_g, w_in, conv_w, conv_b, q_g, k_g, sink, w_out, rope=None, ctx=None):
    n_seq, seq_len, _ = x.shape
    windowed = rope is not None
    nc = seq_len // TM
    n_chunks = n_seq * nc
    lag, ring = (2, 4) if windowed else (1, 2)
    assert windowed or nc == 1
    sp = _chunk_specs(n_chunks, nc, lag, windowed)
    xc = x.reshape(n_chunks, TM, D_MODEL)
    in_specs = [
        sp["x_norm"], sp["x_fin"], sp["mod_norm"], sp["mod_fin"],
        _const_spec((1, D_MODEL)),
        _const_spec((D_MODEL, EVEN_IN)),
        _const_spec((3, CONV_WIDTH)),
        _const_spec((1, CONV_WIDTH)),
        _const_spec((1, LANES)),
        _const_spec((1, LANES)),
        pl.BlockSpec(memory_space=pltpu.SMEM),
        _const_spec((D_MODEL, D_MODEL)),
    ]
    args = [xc, xc, mod, mod, norm_g.reshape(1, D_MODEL), w_in, conv_w, conv_b.reshape(1, CONV_WIDTH),
            jnp.tile(q_g, LANES // HEAD_DIM).reshape(1, LANES), jnp.tile(k_g, LANES // HEAD_DIM).reshape(1, LANES),
            sink, w_out]
    per_chunk = [
        ((TM, CONV_WIDTH), F32),
        ((TM, CONV_WIDTH), F32),
        ((TM, ATTN_WIDTH), BF16),
        ((TM, 2 * GROUP_W), BF16),
        ((2 * LANES, TM), BF16),
        ((TM, ATTN_WIDTH), F32),
    ]
    scratch = (
        [pltpu.VMEM((TM, D_MODEL), BF16)] * 2
        + [pltpu.VMEM((ring,) + shape, dt) for shape, dt in per_chunk]
        + [pltpu.VMEM(shape, dt) for shape, dt in per_chunk]
        + [pltpu.VMEM((TM, D_MODEL), BF16)])
    y_shape = jax.ShapeDtypeStruct(xc.shape, F32)
    y_spec = pl.BlockSpec((1, TM, D_MODEL), lambda c: (sp["fin_chunk"](c), 0, 0))
    if windowed:
        cos, sin = rope
        ck, cv = ctx
        rope_spec = pl.BlockSpec((TM, LANES), lambda c: (sp["proj_chunk"](c) % nc, 0))
        ctx_spec = pl.BlockSpec((1,) + ck.shape[1:], lambda c: (sp["fin_chunk"](c) // nc, 0, 0))
        in_specs += [rope_spec, rope_spec, ctx_spec, ctx_spec]
        args += [cos, sin, ck, cv]
        scratch += [pltpu.VMEM((ck.shape[1], 2 * GROUP_W), BF16), pltpu.VMEM((2 * LANES, cv.shape[1]), BF16)]
        out_shape, out_specs = y_shape, y_spec
    else:
        kv_shape = jax.ShapeDtypeStruct((n_chunks, TM, KV_WIDTH), F32)
        kv_spec = pl.BlockSpec((1, TM, KV_WIDTH), lambda c: (sp["proj_chunk"](c), 0, 0))
        out_shape, out_specs = (y_shape, kv_shape, kv_shape), (y_spec, kv_spec, kv_spec)
    out = pl.pallas_call(
        functools.partial(_even_kernel, nc=nc, n_chunks=n_chunks, lag=lag, ring=ring, windowed=windowed),
        out_shape=out_shape,
        grid=(n_chunks + lag + 1,),
        in_specs=in_specs,
        out_specs=out_specs,
        scratch_shapes=scratch,
        compiler_params=PIPELINE_PARAMS,
        name="even_latent" if windowed else "even_context",
    )(*args)
    if windowed:
        return out.reshape(x.shape)
    y, k, v = out
    return y.reshape(x.shape), k.reshape(n_seq, seq_len, KV_WIDTH), v.reshape(n_seq, seq_len, KV_WIDTH)


def _odd_kernel(xp_ref, xa_ref, modp_ref, moda_ref, g_ref, win_ref, pw_ref, ps_ref, wout_ref, y_ref,
                h_s, h_w, u_s, sg_s, u_w, sg_w, *, nc, n_chunks, lag, ring):
    s = pl.program_id(0)
    i = s - 1 - lag
    ci = lax.rem(i, jnp.int32(nc))

    def normalize(tokens):
        yield from _normalize_items(h_w, xp_ref, g_ref, modp_ref, tokens)

    def project():
        tile = _mxu_cost(TM, D_MODEL, NT)
        for n in range(D_MODEL // NT):
            yield tile, 10
            cols = slice(n * NT, (n + 1) * NT)
            u_w[:, cols] = jnp.dot(h_s[...], win_ref[:, cols], preferred_element_type=F32)
        for n in range(D_MODEL // NT):
            yield tile, 110
            cols = slice(n * NT, (n + 1) * NT)
            sg_w[:, cols] = _silu(jnp.dot(h_s[...], win_ref[:, D_MODEL + n * NT:D_MODEL + (n + 1) * NT],
                                          preferred_element_type=F32))

    def finish(slot, tokens):
        if nc > 1:
            prev_slot = lax.rem(slot + (ring - 1), jnp.int32(ring))
            next_slot = lax.rem(slot + 1, jnp.int32(ring))
            has_prev = ci > 0
            has_next = ci < nc - 1
        else:
            prev_slot = next_slot = has_prev = has_next = None
        t = ci * TM + lax.broadcasted_iota(jnp.int32, (TM, 1), 0)
        outs = []
        for gi, w in enumerate(POOL_SIZES):
            yield _mxu_cost(TM, POOL_GROUP, POOL_GROUP), 140 + 120 * gi
            cols = slice(gi * POOL_GROUP, (gi + 1) * POOL_GROUP)
            ext = _with_halo(u_s, slot, cols, prev_slot, next_slot, has_prev, has_next)
            acc = ext + _rows_down(ext, 1)
            half = 1
            while 2 * half < w:
                acc = _rows_down(acc, half) + _rows_up(acc, half)
                half *= 2
            cnt = (jnp.minimum(t + w // 2, nc * TM) - jnp.maximum(t - w // 2, 0)).astype(F32)
            if tokens:
                cnt = cnt + tokens.pop(0)[:, 0:1]
            pooled = acc[HALO:HALO + TM] / cnt - ext[HALO:HALO + TM]
            outs.append(jnp.dot(pooled.astype(BF16), pw_ref[gi], preferred_element_type=F32))
        yield 0, 130
        y = (jnp.concatenate(outs, axis=1) * ps_ref[...] * sg_s[slot]).astype(BF16)
        yield from _out_proj_items(lambda: y, wout_ref, xa_ref, moda_ref, y_ref)

    _pipeline_steps(s, n_chunks, lag, ring, (h_w, h_s), ((u_w, u_s), (sg_w, sg_s)), ODD_ORDER,
                    normalize, project, finish)


def _odd_layer(x, mod, per_seq_mod, norm_g, w_in, pool_w, pool_scale, w_out):
    n_seq, seq_len, _ = x.shape
    nc = seq_len // TM
    n_chunks = n_seq * nc
    lag, ring = (2, 4) if nc > 1 else (1, 2)
    sp = _chunk_specs(n_chunks, nc, lag, per_seq_mod)
    xc = x.reshape(n_chunks, TM, D_MODEL)
    in_specs = [
        sp["x_norm"], sp["x_fin"], sp["mod_norm"], sp["mod_fin"],
        _const_spec((1, D_MODEL)),
        _const_spec((D_MODEL, 2 * D_MODEL)),
        _const_spec(pool_w.shape),
        _const_spec((1, D_MODEL)),
        _const_spec((D_MODEL, D_MODEL)),
    ]
    out = pl.pallas_call(
        functools.partial(_odd_kernel, nc=nc, n_chunks=n_chunks, lag=lag, ring=ring),
        out_shape=jax.ShapeDtypeStruct(xc.shape, F32),
        grid=(n_chunks + lag + 1,),
        in_specs=in_specs,
        out_specs=pl.BlockSpec((1, TM, D_MODEL), lambda c: (sp["fin_chunk"](c), 0, 0)),
        scratch_shapes=[
            pltpu.VMEM((TM, D_MODEL), BF16),
            pltpu.VMEM((TM, D_MODEL), BF16),
            pltpu.VMEM((ring, TM, D_MODEL), F32),
            pltpu.VMEM((ring, TM, D_MODEL), F32),
            pltpu.VMEM((TM, D_MODEL), F32),
            pltpu.VMEM((TM, D_MODEL), F32),
        ],
        compiler_params=PIPELINE_PARAMS,
        name="odd_latent" if per_seq_mod else "odd_context",
    )(xc, xc, mod, mod, norm_g.reshape(1, D_MODEL), w_in, pool_w, pool_scale.reshape(1, D_MODEL), w_out)
    return out.reshape(x.shape)


def _rope_tables(seq_len):
    n_rows = seq_len // GRID_W
    row = jnp.repeat(jnp.arange(n_rows), GRID_W).astype(F32)
    col = jnp.tile(jnp.arange(GRID_W), n_rows).astype(F32)
    inv = ROPE_BASE ** (-jnp.arange(ROPE_FREQS, dtype=F32) / ROPE_FREQS)
    ang = jnp.stack([row[:, None] * inv, col[:, None] * inv], axis=1)
    cos = jnp.broadcast_to(jnp.cos(ang)[:, :, None, :], (seq_len, 2, 2, ROPE_FREQS))
    sin = jnp.sin(ang)[:, :, None, :] * jnp.array([-1.0, 1.0], F32)[None, None, :, None]
    cos = cos.reshape(seq_len, HEAD_DIM)
    sin = sin.reshape(seq_len, HEAD_DIM)
    return jnp.tile(cos, (1, LANES // HEAD_DIM)), jnp.tile(sin, (1, LANES // HEAD_DIM))


def kernel(x_prompt, x_sample, cache_k, cache_v, c, c_ctx, ada_w_e, ada_b_e, norm_g_e, w_in_e, conv_w, conv_b,
           q_norm_g, k_norm_g, sink, w_out_e, ada_w_o, ada_b_o, norm_g_o, w_in_o, pool_w, pool_scale, w_out_o):
    n_dec = x_sample.shape[0]
    depth = ada_w_e.shape[0] + ada_w_o.shape[0]
    assert ada_w_e.shape[0] == 1 and ada_w_o.shape[0] == 1 and n_dec + 1 <= 8
    cond8 = jnp.concatenate([c_ctx[None], c, jnp.zeros((8 - 1 - n_dec, D_MODEL), F32)], axis=0)
    mod_e, mod_o = _adaln(cond8, ada_w_e[0], ada_b_e[0], ada_w_o[0], ada_b_o[0])
    rope = _rope_tables(x_sample.shape[1])

    yp, ys = x_prompt, x_sample
    new_k, new_v = [], []
    for layer in range(depth):
        i = layer // 2
        if layer % 2 == 0:
            mod = mod_e.reshape(8, 1, 3 * D_MODEL)
            w_in = w_in_e[i].astype(BF16)
            w_out = w_out_e[i].astype(BF16)
            common = (norm_g_e[i], w_in, conv_w[i], conv_b[i], q_norm_g[i], k_norm_g[i], sink[i], w_out)
            yp, k, v = _even_layer(yp, mod[0:1], *common)
            new_k.append(k.reshape(k.shape[0], k.shape[1], N_KV_HEADS, HEAD_DIM))
            new_v.append(v.reshape(v.shape[0], v.shape[1], N_KV_HEADS, HEAD_DIM))
            ck = cache_k[:, i].reshape(n_dec, cache_k.shape[2], KV_WIDTH)
            cv = cache_v[:, i].reshape(n_dec, cache_v.shape[2], KV_WIDTH)
            ys = _even_layer(ys, mod[1:1 + n_dec], *common, rope=rope, ctx=(ck, cv))
        else:
            mod = mod_o.reshape(8, 1, 3 * D_MODEL)
            common = (norm_g_o[i], w_in_o[i].astype(BF16), pool_w[i].astype(BF16), pool_scale[i],
                      w_out_o[i].astype(BF16))
            yp = _odd_layer(yp, mod[0:1], False, *common)
            ys = _odd_layer(ys, mod[1:1 + n_dec], True, *common)
    return yp, ys, jnp.stack(new_k, axis=1), jnp.stack(new_v, axis=1)
```

```python
import functools

import jax
import jax.numpy as jnp
import numpy as np
from jax import lax
from jax.experimental import pallas as pl
from jax.experimental.pallas import tpu as pltpu

F32 = jnp.float32
BF16 = jnp.bfloat16

D_MODEL = 1024
GRID_W = 64
HEAD_DIM = 64
HEAD_SHIFT = 6
ATTN_WIDTH = D_MODEL // 2
N_Q_HEADS = ATTN_WIDTH // HEAD_DIM
N_KV_HEADS = N_Q_HEADS // 4
Q_PER_KV = N_Q_HEADS // N_KV_HEADS
KV_WIDTH = N_KV_HEADS * HEAD_DIM
CONV_WIDTH = D_MODEL - ATTN_WIDTH
WINDOW = 128
BLOCK = 128
ROPE_BASE = 10000.0
ROPE_FREQS = HEAD_DIM // 4
ATTN_SCALE = HEAD_DIM ** -0.5
NEG = -1e30
POOL_SIZES = (2, 4, 8, 16)
POOL_GROUP = D_MODEL // len(POOL_SIZES)
EPS = 1e-6
EVEN_SIZES = (CONV_WIDTH, CONV_WIDTH, CONV_WIDTH, CONV_WIDTH, ATTN_WIDTH, KV_WIDTH, KV_WIDTH, ATTN_WIDTH)
EVEN_IN = sum(EVEN_SIZES)
EVEN_OFFS = tuple(int(s) for s in np.cumsum((0,) + EVEN_SIZES))

TM = 256
NT = 256
NORM_ROWS = 64
EVEN_ORDER = "PF" + "FNPFPF" * 4 + "FFFF"
ODD_ORDER = "PNFP" * 4 + "FFFFF"
LANES = 128
GROUP_W = Q_PER_KV * HEAD_DIM
HALO = 8
ADA_TN = 768
VMEM_LIMIT = 56 * 1024 * 1024
PIPELINE_PARAMS = pltpu.CompilerParams(dimension_semantics=("arbitrary",), vmem_limit_bytes=VMEM_LIMIT)


def _silu(x):
    return x / (1.0 + jnp.exp(-x))


def _rows_down(x, k):
    return pltpu.roll(x, k, axis=0)


def _rows_up(x, k):
    return pltpu.roll(x, x.shape[0] - k, axis=0)


def _rms_modulate(x, g, shift, scale):
    ms = jnp.mean(x * x, axis=-1, keepdims=True)
    return (x * lax.rsqrt(ms + EPS) * g) * (1.0 + scale) + shift


def _opaque_zero(x):
    bits = pltpu.bitcast(x, jnp.uint32)
    return pltpu.bitcast((bits >> 16) >> 16, F32)


def _normalize_items(h_s, x_ref, g_ref, mod_ref, tokens):
    shift = mod_ref[0, :, 0:D_MODEL]
    scale = mod_ref[0, :, D_MODEL:2 * D_MODEL]
    for r in range(0, TM, NORM_ROWS):
        yield 0, 170
        h = _rms_modulate(x_ref[0, r:r + NORM_ROWS], g_ref[...], shift, scale)
        h_s[r:r + NORM_ROWS] = h.astype(BF16)
        tokens.append(_opaque_zero(h[0:8, 0:LANES])[0:1])


def _out_proj_items(get_a, wout_ref, x_ref, mod_ref, y_ref):
    for n in range(D_MODEL // NT):
        yield _mxu_cost(TM, D_MODEL, NT), 70
        cols = slice(n * NT, (n + 1) * NT)
        mixed = jnp.dot(get_a(), wout_ref[:, cols], preferred_element_type=F32)
        y_ref[0, :, cols] = x_ref[0, :, cols] + mod_ref[0, :, 2 * D_MODEL + n * NT:2 * D_MODEL + (n + 1) * NT] * mixed


def _mxu_cost(m, k, n):
    tiles = -(-k // NT) * -(-n // NT)
    return tiles * max(m // 2, 64) // 2


def _weave(order, **stages):
    by_letter = {name[0].upper(): stage for name, stage in stages.items()}
    for stage in by_letter.values():
        next(stage, None)
    for letter in order:
        if letter in by_letter:
            next(by_letter[letter], None)
    for stage in by_letter.values():
        for _ in stage:
            pass


def _with_halo(u_s, slot, cols, prev_slot, next_slot, has_prev, has_next):
    u = u_s[slot, :, cols]
    zero = jnp.zeros((HALO, u.shape[1]), F32)
    if prev_slot is None:
        top = bot = zero
    else:
        top = jnp.where(has_prev, u_s[prev_slot, TM - HALO:TM, cols], zero)
        bot = jnp.where(has_next, u_s[next_slot, 0:HALO, cols], zero)
    return jnp.concatenate([top, u, bot], axis=0)


def _pipeline_steps(s, n_chunks, lag, ring, normalized, staged, order, normalize, project, finish):
    fin_slot = lax.rem(s + (2 * ring - 1 - lag), jnp.int32(ring))

    @pl.when((s >= 1) & (s <= n_chunks))
    def _publish_normalized():
        h_w, h_s = normalized
        h_s[...] = h_w[...]

    @pl.when((s >= 2) & (s <= n_chunks + 1))
    def _publish_projected():
        slot = lax.rem(s + (ring - 2), jnp.int32(ring))
        for stage_ref, ring_ref in staged:
            ring_ref[slot] = stage_ref[...]

    @pl.when(s == 0)
    def _first():
        _weave(order, normalize=normalize([]))

    @pl.when((s >= 1) & (s <= lag))
    def _fill():
        _weave(order, project=project(), normalize=normalize([]))

    @pl.when((s > lag) & (s <= n_chunks))
    def _steady():
        tokens = []
        _weave(order, finish=finish(fin_slot, tokens), project=project(), normalize=normalize(tokens))

    @pl.when(s > n_chunks)
    def _drain():
        _weave(order, finish=finish(fin_slot, []))


def _adaln_kernel(cond_ref, we_ref, be_ref, wo_ref, bo_ref, oe_ref, oo_ref):
    s = _silu(cond_ref[...]).astype(BF16)
    oe_ref[...] = jnp.dot(s, we_ref[...].astype(BF16), preferred_element_type=F32) + be_ref[...]
    oo_ref[...] = jnp.dot(s, wo_ref[...].astype(BF16), preferred_element_type=F32) + bo_ref[...]


def _adaln(cond8, w_e, b_e, w_o, b_o):
    n = 3 * D_MODEL
    wspec = pl.BlockSpec((D_MODEL, ADA_TN), lambda j: (0, j))
    vspec = pl.BlockSpec((1, ADA_TN), lambda j: (0, j))
    ospec = pl.BlockSpec((8, ADA_TN), lambda j: (0, j))
    return pl.pallas_call(
        _adaln_kernel,
        out_shape=(jax.ShapeDtypeStruct((8, n), F32), jax.ShapeDtypeStruct((8, n), F32)),
        grid=(n // ADA_TN,),
        in_specs=[pl.BlockSpec((8, D_MODEL), lambda j: (0, 0)), wspec, vspec, wspec, vspec],
        out_specs=(ospec, ospec),
        compiler_params=pltpu.CompilerParams(dimension_semantics=("arbitrary",), vmem_limit_bytes=VMEM_LIMIT),
        name="adaln",
    )(cond8, w_e, b_e.reshape(1, n), w_o, b_o.reshape(1, n))


def _head_inv_rms(x, bd):
    sq = x * x
    hi = sq.astype(BF16)
    lo = (sq - hi.astype(F32)).astype(BF16)
    ms = (jnp.dot(hi, bd, preferred_element_type=F32) + jnp.dot(lo, bd, preferred_element_type=F32)) * (1.0 / HEAD_DIM)
    return lax.rsqrt(ms + EPS)


def _rope(x, cos, sin_signed, first_half):
    partner = jnp.where(first_half, pltpu.roll(x, LANES - ROPE_FREQS, axis=1), pltpu.roll(x, ROPE_FREQS, axis=1))
    return x * cos + partner * sin_signed


def _dup_kv(x):
    lane = lax.broadcasted_iota(jnp.int32, (1, LANES), 1)
    swapped = pltpu.roll(x, HEAD_DIM, axis=1)
    return jnp.where(lane < HEAD_DIM, x, swapped), jnp.where(lane < HEAD_DIM, swapped, x)


def _tiled_keys(k):
    ka, kb = _dup_kv(k)
    return jnp.concatenate([ka, ka, kb, kb], axis=1).astype(BF16)


def _values_t(v):
    va, vb = _dup_kv(v)
    return jnp.concatenate([va.T, vb.T], axis=0).astype(BF16)


def _even_kernel(*refs, nc, n_chunks, lag, ring, windowed):
    if windowed:
        (xp_ref, xa_ref, modp_ref, moda_ref, g_ref, win_ref, cw_ref, cb_ref, qg_ref, kg_ref, sink_ref, wout_ref,
         cos_ref, sin_ref, ck_ref, cv_ref,
         y_ref,
         h_s, h_w, u_s, gate_s, q_s, k_s, vt_s, gb_s, u_w, gate_w, q_w, k_w, vt_w, gb_w, a_s, ck_s, cvt_s) = refs
        ko_ref = vo_ref = None
    else:
        (xp_ref, xa_ref, modp_ref, moda_ref, g_ref, win_ref, cw_ref, cb_ref, qg_ref, kg_ref, sink_ref, wout_ref,
         y_ref, ko_ref, vo_ref,
         h_s, h_w, u_s, gate_s, q_s, k_s, vt_s, gb_s, u_w, gate_w, q_w, k_w, vt_w, gb_w, a_s) = refs
    s = pl.program_id(0)
    i = s - 1 - lag
    ci = lax.rem(i, jnp.int32(nc))

    @pl.when(s == 0)
    def _init():
        k_s[...] = jnp.zeros(k_s.shape, BF16)
        vt_s[...] = jnp.zeros(vt_s.shape, BF16)

    if windowed:
        @pl.when((i >= 0) & (ci == 0))
        def _load_ctx():
            ck_s[...] = _tiled_keys(ck_ref[0])
            cvt_s[...] = _values_t(cv_ref[0])

    def normalize(tokens):
        yield from _normalize_items(h_w, xp_ref, g_ref, modp_ref, tokens)

    def project():
        o_bg, o_cg, o_xs, o_ga, o_q, o_k, _, o_gb = EVEN_OFFS[:8]

        def proj(start):
            return jnp.dot(h_s[...], win_ref[:, start:start + NT], preferred_element_type=F32)

        lane = lax.broadcasted_iota(jnp.int32, (1, LANES), 1)
        first_half = (lane & (2 * ROPE_FREQS - 1)) < ROPE_FREQS
        ri = lax.broadcasted_iota(jnp.int32, (NT, NT), 0) >> HEAD_SHIFT
        cj = lax.broadcasted_iota(jnp.int32, (NT, NT), 1) >> HEAD_SHIFT
        bd = (ri == cj).astype(BF16)

        def qk_norm(x, g):
            w = x.shape[1]
            inv = _head_inv_rms(x, bd[0:w, 0:w])
            pieces = []
            for j in range(w // LANES):
                lanes = slice(j * LANES, (j + 1) * LANES)
                xj = x[:, lanes] * inv[:, lanes] * g
                pieces.append(_rope(xj, cos_ref[...], sin_ref[...], first_half) if windowed else xj)
            return pieces

        tile = _mxu_cost(TM, D_MODEL, NT)

        def keys_values():
            kv = proj(o_k)
            k, = qk_norm(kv[:, 0:KV_WIDTH], kg_ref[...])
            v = kv[:, KV_WIDTH:2 * KV_WIDTH]
            if not windowed:
                ko_ref[0] = k
                vo_ref[0] = v
            k_w[...] = _tiled_keys(k)
            vt_w[...] = _values_t(v)

        def queries(n):
            for j, qj in enumerate(qk_norm(proj(o_q + n * NT), qg_ref[...])):
                q_w[:, n * NT + j * LANES:n * NT + (j + 1) * LANES] = (qj * ATTN_SCALE).astype(BF16)

        def conv_gate(n):
            gate_w[:, n * NT:(n + 1) * NT] = (proj(o_bg + n * NT) * _silu(proj(o_ga + n * NT))).astype(BF16)

        def conv_input(n):
            u_w[:, n * NT:(n + 1) * NT] = proj(o_cg + n * NT) * proj(o_xs + n * NT)

        def attn_gate(n):
            gb_w[:, n * NT:(n + 1) * NT] = _silu(proj(o_gb + n * NT)).astype(BF16)

        items = [((2 * tile, 180), conv_gate, 0), ((tile, 350), keys_values, None), ((2 * tile, 40), conv_input, 0),
                 ((tile, 300), queries, 0), ((2 * tile, 180), conv_gate, 1), ((tile, 300), queries, 1),
                 ((2 * tile, 40), conv_input, 1), ((tile, 110), attn_gate, 0), ((tile, 110), attn_gate, 1)]
        for cost, emit, n in items:
            yield cost
            emit() if n is None else emit(n)

    def finish(slot, tokens):
        if windowed:
            prev_slot = lax.rem(slot + (ring - 1), jnp.int32(ring))
            next_slot = lax.rem(slot + 1, jnp.int32(ring))
            has_prev = ci > 0
            has_next = ci < nc - 1
        else:
            prev_slot = next_slot = has_prev = has_next = None

        yield 0, 400
        ext = _with_halo(u_s, slot, slice(None), prev_slot, next_slot, has_prev, has_next)
        conv = (_rows_down(ext, 1)[HALO:HALO + TM] * cw_ref[0:1, :]
                + ext[HALO:HALO + TM] * cw_ref[1:2, :]
                + _rows_up(ext, 1)[HALO:HALO + TM] * cw_ref[2:3, :]
                + cb_ref[...])
        a_s[:, 0:CONV_WIDTH] = (gate_s[slot].astype(F32) * conv).astype(BF16)

        lane_head = lax.broadcasted_iota(jnp.int32, (1, GROUP_W), 1) >> HEAD_SHIFT
        cols4 = Q_PER_KV * BLOCK
        n_keys = 3 * BLOCK + ck_s.shape[0] if windowed else TM
        key_i =lax.broadcasted_iota(jnp.int32, (BLOCK, cols4), 0)
        qry_i = lax.broadcasted_iota(jnp.int32, (BLOCK, cols4), 1) & (BLOCK - 1)
        upper = lax.broadcasted_iota(jnp.int32, (LANES, LANES), 0) < HEAD_DIM
        halves = (slice(0, BLOCK), slice(BLOCK, TM))
        for qb in range(TM // BLOCK):
            own = halves[qb]
            if windowed:
                if qb == 0:
                    blocks = ((prev_slot, halves[1]), (slot, halves[0]), (slot, halves[1]))
                    prev_in, next_in = has_prev, True
                else:
                    blocks = ((slot, halves[0]), (slot, halves[1]), (next_slot, halves[0]))
                    prev_in, next_in = True, has_next
                prev_ok = (key_i - qry_i) >= jnp.where(prev_in, 0, BLOCK)
                next_ok = (qry_i - key_i) >= jnp.where(next_in, 0, BLOCK)
            for kv in range(N_KV_HEADS):
                cols = slice(kv * GROUP_W, (kv + 1) * GROUP_W)
                vrows = slice(kv * LANES, (kv + 1) * LANES)
                yield _mxu_cost(n_keys, GROUP_W, cols4), 60
                qblk = q_s[slot, own, cols]
                q4 = jnp.concatenate(
                    [jnp.where(lane_head == g, qblk, jnp.zeros_like(qblk)) for g in range(Q_PER_KV)], axis=0)

                def scores_t(kk):
                    return lax.dot_general(kk, q4, (((1,), (1,)), ((), ())), preferred_element_type=F32)

                if windowed:
                    s_loc = scores_t(jnp.concatenate([k_s[sl, r, cols] for sl, r in blocks], axis=0))
                    vt_loc = jnp.concatenate([vt_s[sl, vrows, r] for sl, r in blocks], axis=1)
                    s_ctx = scores_t(ck_s[:, cols])
                    yield 0, n_keys * cols4 * 5 // (8 * LANES * 4)
                    groups = [([jnp.where(prev_ok, s_loc[0:BLOCK], NEG), s_loc[BLOCK:2 * BLOCK],
                                jnp.where(next_ok, s_loc[2 * BLOCK:3 * BLOCK], NEG)], vt_loc),
                              ([s_ctx], cvt_s[vrows, :])]
                else:
                    s_own = scores_t(k_s[slot, :, cols])
                    yield 0, n_keys * cols4 * 5 // (8 * LANES * 4)
                    groups = [([s_own], vt_s[slot, vrows, :])]
                snk = jnp.concatenate(
                    [jnp.full((1, BLOCK), sink_ref[kv * Q_PER_KV + g], F32) for g in range(Q_PER_KV)], axis=1)
                if tokens:
                    snk = snk + jnp.concatenate([tokens.pop(0)] * Q_PER_KV, axis=1)
                mx = snk
                for ss, _ in groups:
                    for sc in ss:
                        mx = jnp.maximum(mx, jnp.max(sc, axis=0, keepdims=True))
                den = jnp.exp(snk - mx)
                probs = []
                for ss, vt in groups:
                    es = []
                    for sc in ss:
                        e = jnp.exp(sc - mx)
                        den = den + jnp.sum(e, axis=0, keepdims=True)
                        es.append(e.astype(BF16))
                    probs.append((es[0] if len(es) == 1 else jnp.concatenate(es, axis=0), vt))
                yield _mxu_cost(LANES, n_keys, cols4), 120
                o_t = jnp.zeros((LANES, cols4), F32)
                for e_all, vt in probs:
                    o_t = o_t + jnp.dot(vt, e_all, preferred_element_type=F32)
                o_t = o_t * (1.0 / den)
                o01 = jnp.where(upper, o_t[:, 0:BLOCK], o_t[:, BLOCK:2 * BLOCK]).T
                o23 = jnp.where(upper, o_t[:, 2 * BLOCK:3 * BLOCK], o_t[:, 3 * BLOCK:4 * BLOCK]).T
                a_s[own, CONV_WIDTH + kv * GROUP_W:CONV_WIDTH + (kv + 1) * GROUP_W] = (
                    jnp.concatenate([o01, o23], axis=1) * gb_s[slot, own, cols].astype(F32)).astype(BF16)

        yield from _out_proj_items(lambda: a_s[...], wout_ref, xa_ref, moda_ref, y_ref)

    staged = ((u_w, u_s), (gate_w, gate_s), (q_w, q_s), (k_w, k_s), (vt_w, vt_s), (gb_w, gb_s))
    _pipeline_steps(s, n_chunks, lag, ring, (h_w, h_s), staged, EVEN_ORDER, normalize, project, finish)


def _chunk_specs(n_chunks, nc, lag, per_seq_mod):
    last = n_chunks - 1
    norm_chunk = lambda s: jnp.minimum(s, last)
    proj_chunk = lambda s: jnp.clip(s - 1, 0, last)
    fin_chunk = lambda s: jnp.clip(s - 1 - lag, 0, last)
    seq_of = (lambda j: j // nc) if per_seq_mod else (lambda j: 0)
    return dict(
        x_norm=pl.BlockSpec((1, TM, D_MODEL), lambda s: (norm_chunk(s), 0, 0)),
        x_fin=pl.BlockSpec((1, TM, D_MODEL), lambda s: (fin_chunk(s), 0, 0)),
        mod_norm=pl.BlockSpec((1, 1, 3 * D_MODEL), lambda s: (seq_of(norm_chunk(s)), 0, 0)),
        mod_fin=pl.BlockSpec((1, 1, 3 * D_MODEL), lambda s: (seq_of(fin_chunk(s)), 0, 0)),
        proj_chunk=proj_chunk, fin_chunk=fin_chunk)


def _const_spec(shape):
    return pl.BlockSpec(shape, lambda c: (0,) * len(shape))


def _even_layer(x, mod, norm_g, w_in, conv_w, conv_b, q_g, k_g, sink, w_out, rope=None, ctx=None):
    n_seq, seq_len, _ = x.shape
    windowed = rope is not None
    nc = seq_len // TM
    n_chunks = n_seq * nc
    lag, ring = (2, 4) if windowed else (1, 2)
    assert windowed or nc == 1
    sp = _chunk_specs(n_chunks, nc, lag, windowed)
    xc = x.reshape(n_chunks, TM, D_MODEL)
    in_specs = [
        sp["x_norm"], sp["x_fin"], sp["mod_norm"], sp["mod_fin"],
        _const_spec((1, D_MODEL)),
        _const_spec((D_MODEL, EVEN_IN)),
        _const_spec((3, CONV_WIDTH)),
        _const_spec((1, CONV_WIDTH)),
        _const_spec((1, LANES)),
        _const_spec((1, LANES)),
        pl.BlockSpec(memory_space=pltpu.SMEM),
        _const_spec((D_MODEL, D_MODEL)),
    ]
    args = [xc, xc, mod, mod, norm_g.reshape(1, D_MODEL), w_in, conv_w, conv_b.reshape(1, CONV_WIDTH),
            jnp.tile(q_g, LANES // HEAD_DIM).reshape(1, LANES), jnp.tile(k_g, LANES // HEAD_DIM).reshape(1, LANES),
            sink, w_out]
    per_chunk = [
        ((TM, CONV_WIDTH), F32),
        ((TM, CONV_WIDTH), BF16),
        ((TM, ATTN_WIDTH), BF16),
        ((TM, 2 * GROUP_W), BF16),
        ((2 * LANES, TM), BF16),
        ((TM, ATTN_WIDTH), BF16),
    ]
    scratch = (
        [pltpu.VMEM((TM, D_MODEL), BF16)] * 2
        + [pltpu.VMEM((ring,) + shape, dt) for shape, dt in per_chunk]
        + [pltpu.VMEM(shape, dt) for shape, dt in per_chunk]
        + [pltpu.VMEM((TM, D_MODEL), BF16)])
    y_shape = jax.ShapeDtypeStruct(xc.shape, F32)
    y_spec = pl.BlockSpec((1, TM, D_MODEL), lambda c: (sp["fin_chunk"](c), 0, 0))
    if windowed:
        cos, sin = rope
        ck, cv = ctx
        rope_spec = pl.BlockSpec((TM, LANES), lambda c: (sp["proj_chunk"](c) % nc, 0))
        ctx_spec = pl.BlockSpec((1,) + ck.shape[1:], lambda c: (sp["fin_chunk"](c) // nc, 0, 0))
        in_specs += [rope_spec, rope_spec, ctx_spec, ctx_spec]
        args += [cos, sin, ck, cv]
        scratch += [pltpu.VMEM((ck.shape[1], 2 * GROUP_W), BF16), pltpu.VMEM((2 * LANES, cv.shape[1]), BF16)]
        out_shape, out_specs = y_shape, y_spec
    else:
        kv_shape = jax.ShapeDtypeStruct((n_chunks, TM, KV_WIDTH), F32)
        kv_spec = pl.BlockSpec((1, TM, KV_WIDTH), lambda c: (sp["proj_chunk"](c), 0, 0))
        out_shape, out_specs = (y_shape, kv_shape, kv_shape), (y_spec, kv_spec, kv_spec)
    out = pl.pallas_call(
        functools.partial(_even_kernel, nc=nc, n_chunks=n_chunks, lag=lag, ring=ring, windowed=windowed),
        out_shape=out_shape,
        grid=(n_chunks + lag + 1,),
        in_specs=in_specs,
        out_specs=out_specs,
        scratch_shapes=scratch,
        compiler_params=PIPELINE_PARAMS,
        name="even_latent" if windowed else "even_context",
    )(*args)
    if windowed:
        return out.reshape(x.shape)
    y, k, v = out
    return y.reshape(x.shape), k.reshape(n_seq, seq_len, KV_WIDTH), v.reshape(n_seq, seq_len, KV_WIDTH)


def _odd_kernel(xp_ref, xa_ref, modp_ref, moda_ref, g_ref, win_ref, pw_ref, ps_ref, wout_ref, y_ref,
                h_s, h_w, u_s, sg_s, u_w, sg_w, *, nc, n_chunks, lag, ring):
    s = pl.program_id(0)
    i = s - 1 - lag
    ci = lax.rem(i, jnp.int32(nc))

    def normalize(tokens):
        yield from _normalize_items(h_w, xp_ref, g_ref, modp_ref, tokens)

    def project():
        tile = _mxu_cost(TM, D_MODEL, NT)
        for n in range(D_MODEL // NT):
            yield tile, 10
            cols = slice(n * NT, (n + 1) * NT)
            u_w[:, cols] = jnp.dot(h_s[...], win_ref[:, cols], preferred_element_type=F32)
        for n in range(D_MODEL // NT):
            yield tile, 110
            cols = slice(n * NT, (n + 1) * NT)
            sg_w[:, cols] = _silu(jnp.dot(h_s[...], win_ref[:, D_MODEL + n * NT:D_MODEL + (n + 1) * NT],
                                          preferred_element_type=F32)).astype(BF16)

    def finish(slot, tokens):
        if nc > 1:
            prev_slot = lax.rem(slot + (ring - 1), jnp.int32(ring))
            next_slot = lax.rem(slot + 1, jnp.int32(ring))
            has_prev = ci > 0
            has_next = ci < nc - 1
        else:
            prev_slot = next_slot = has_prev = has_next = None
        t = ci * TM + lax.broadcasted_iota(jnp.int32, (TM, 1), 0)
        outs = []
        for gi, w in enumerate(POOL_SIZES):
            yield _mxu_cost(TM, POOL_GROUP, POOL_GROUP), 140 + 120 * gi
            cols = slice(gi * POOL_GROUP, (gi + 1) * POOL_GROUP)
            ext = _with_halo(u_s, slot, cols, prev_slot, next_slot, has_prev, has_next)
            acc = ext + _rows_down(ext, 1)
            half = 1
            while 2 * half < w:
                acc = _rows_down(acc, half) + _rows_up(acc, half)
                half *= 2
            cnt = (jnp.minimum(t + w // 2, nc * TM) - jnp.maximum(t - w // 2, 0)).astype(F32)
            if tokens:
                cnt = cnt + tokens.pop(0)[:, 0:1]
            pooled = acc[HALO:HALO + TM] / cnt - ext[HALO:HALO + TM]
            outs.append(jnp.dot(pooled.astype(BF16), pw_ref[gi], preferred_element_type=F32))
        yield 0, 130
        y = (jnp.concatenate(outs, axis=1) * ps_ref[...] * sg_s[slot].astype(F32)).astype(BF16)
        yield from _out_proj_items(lambda: y, wout_ref, xa_ref, moda_ref, y_ref)

    _pipeline_steps(s, n_chunks, lag, ring, (h_w, h_s), ((u_w, u_s), (sg_w, sg_s)), ODD_ORDER,
                    normalize, project, finish)


def _odd_layer(x, mod, per_seq_mod, norm_g, w_in, pool_w, pool_scale, w_out):
    n_seq, seq_len, _ = x.shape
    nc = seq_len // TM
    n_chunks = n_seq * nc
    lag, ring = (2, 4) if nc > 1 else (1, 2)
    sp = _chunk_specs(n_chunks, nc, lag, per_seq_mod)
    xc = x.reshape(n_chunks, TM, D_MODEL)
    in_specs = [
        sp["x_norm"], sp["x_fin"], sp["mod_norm"], sp["mod_fin"],
        _const_spec((1, D_MODEL)),
        _const_spec((D_MODEL, 2 * D_MODEL)),
        _const_spec(pool_w.shape),
        _const_spec((1, D_MODEL)),
        _const_spec((D_MODEL, D_MODEL)),
    ]
    out = pl.pallas_call(
        functools.partial(_odd_kernel, nc=nc, n_chunks=n_chunks, lag=lag, ring=ring),
        out_shape=jax.ShapeDtypeStruct(xc.shape, F32),
        grid=(n_chunks + lag + 1,),
        in_specs=in_specs,
        out_specs=pl.BlockSpec((1, TM, D_MODEL), lambda c: (sp["fin_chunk"](c), 0, 0)),
        scratch_shapes=[
            pltpu.VMEM((TM, D_MODEL), BF16),
            pltpu.VMEM((TM, D_MODEL), BF16),
            pltpu.VMEM((ring, TM, D_MODEL), F32),
            pltpu.VMEM((ring, TM, D_MODEL), BF16),
            pltpu.VMEM((TM, D_MODEL), F32),
            pltpu.VMEM((TM, D_MODEL), BF16),
        ],
        compiler_params=PIPELINE_PARAMS,
        name="odd_latent" if per_seq_mod else "odd_context",
    )(xc, xc, mod, mod, norm_g.reshape(1, D_MODEL), w_in, pool_w, pool_scale.reshape(1, D_MODEL), w_out)
    return out.reshape(x.shape)


def _rope_tables(seq_len):
    n_rows = seq_len // GRID_W
    row = jnp.repeat(jnp.arange(n_rows), GRID_W).astype(F32)
    col = jnp.tile(jnp.arange(GRID_W), n_rows).astype(F32)
    inv = ROPE_BASE ** (-jnp.arange(ROPE_FREQS, dtype=F32) / ROPE_FREQS)
    ang = jnp.stack([row[:, None] * inv, col[:, None] * inv], axis=1)
    cos = jnp.broadcast_to(jnp.cos(ang)[:, :, None, :], (seq_len, 2, 2, ROPE_FREQS))
    sin = jnp.sin(ang)[:, :, None, :] * jnp.array([-1.0, 1.0], F32)[None, None, :, None]
    cos = cos.reshape(seq_len, HEAD_DIM)
    sin = sin.reshape(seq_len, HEAD_DIM)
    return jnp.tile(cos, (1, LANES // HEAD_DIM)), jnp.tile(sin, (1, LANES // HEAD_DIM))


def kernel(x_prompt, x_sample, cache_k, cache_v, c, c_ctx, ada_w_e, ada_b_e, norm_g_e, w_in_e, conv_w, conv_b,
           q_norm_g, k_norm_g, sink, w_out_e, ada_w_o, ada_b_o, norm_g_o, w_in_o, pool_w, pool_scale, w_out_o):
    n_dec = x_sample.shape[0]
    depth = ada_w_e.shape[0] + ada_w_o.shape[0]
    assert ada_w_e.shape[0] == 1 and ada_w_o.shape[0] == 1 and n_dec + 1 <= 8
    cond8 = jnp.concatenate([c_ctx[None], c, jnp.zeros((8 - 1 - n_dec, D_MODEL), F32)], axis=0)
    mod_e, mod_o = _adaln(cond8, ada_w_e[0], ada_b_e[0], ada_w_o[0], ada_b_o[0])
    rope = _rope_tables(x_sample.shape[1])

    yp, ys = x_prompt, x_sample
    new_k, new_v = [], []
    for layer in range(depth):
        i = layer // 2
        if layer % 2 == 0:
            mod = mod_e.reshape(8, 1, 3 * D_MODEL)
            w_in = w_in_e[i].astype(BF16)
            w_out = w_out_e[i].astype(BF16)
            common = (norm_g_e[i], w_in, conv_w[i], conv_b[i], q_norm_g[i], k_norm_g[i], sink[i], w_out)
            yp, k, v = _even_layer(yp, mod[0:1], *common)
            new_k.append(k.reshape(k.shape[0], k.shape[1], N_KV_HEADS, HEAD_DIM))
            new_v.append(v.reshape(v.shape[0], v.shape[1], N_KV_HEADS, HEAD_DIM))
            ck = cache_k[:, i].reshape(n_dec, cache_k.shape[2], KV_WIDTH)
            cv = cache_v[:, i].reshape(n_dec, cache_v.shape[2], KV_WIDTH)
            ys = _even_layer(ys, mod[1:1 + n_dec], *common, rope=rope, ctx=(ck, cv))
        else:
            mod = mod_o.reshape(8, 1, 3 * D_MODEL)
            common = (norm_g_o[i], w_in_o[i].astype(BF16), pool_w[i].astype(BF16), pool_scale[i],
                      w_out_o[i].astype(BF16))
            yp = _odd_layer(yp, mod[0:1], False, *common)
            ys = _odd_layer(ys, mod[1:1 + n_dec], True, *common)
    return yp, ys, jnp.stack(new_k, axis=1), jnp.stack(new_v, axis=1)
```

```python
import functools

import jax
import jax.numpy as jnp
import numpy as np
from jax import lax
from jax.experimental import pallas as pl
from jax.experimental.pallas import tpu as pltpu

F32 = jnp.float32
BF16 = jnp.bfloat16

D_MODEL = 1024
GRID_W = 64
HEAD_DIM = 64
HEAD_SHIFT = 6
ATTN_WIDTH = D_MODEL // 2
N_Q_HEADS = ATTN_WIDTH // HEAD_DIM
N_KV_HEADS = N_Q_HEADS // 4
Q_PER_KV = N_Q_HEADS // N_KV_HEADS
KV_WIDTH = N_KV_HEADS * HEAD_DIM
CONV_WIDTH = D_MODEL - ATTN_WIDTH
WINDOW = 128
BLOCK = 128
ROPE_BASE = 10000.0
ROPE_FREQS = HEAD_DIM // 4
ATTN_SCALE = HEAD_DIM ** -0.5
NEG = -1e30
POOL_SIZES = (2, 4, 8, 16)
POOL_GROUP = D_MODEL // len(POOL_SIZES)
EPS = 1e-6
EVEN_SIZES = (CONV_WIDTH, CONV_WIDTH, CONV_WIDTH, CONV_WIDTH, ATTN_WIDTH, KV_WIDTH, KV_WIDTH, ATTN_WIDTH)
EVEN_IN = sum(EVEN_SIZES)
EVEN_OFFS = tuple(int(s) for s in np.cumsum((0,) + EVEN_SIZES))

TM = 256
NT = 256
NORM_ROWS = 64
EVEN_ORDER = "PF" + "FNPFPF" * 4 + "FFFF"
ODD_ORDER = "PNFP" * 4 + "FFFFF"
LANES = 128
GROUP_W = Q_PER_KV * HEAD_DIM
HALO = 8
ADA_TN = 384
VMEM_LIMIT = 56 * 1024 * 1024
PIPELINE_PARAMS = pltpu.CompilerParams(dimension_semantics=("arbitrary",), vmem_limit_bytes=VMEM_LIMIT)


def _silu(x):
    return x / (1.0 + jnp.exp(-x))


def _rows_down(x, k):
    return pltpu.roll(x, k, axis=0)


def _rows_up(x, k):
    return pltpu.roll(x, x.shape[0] - k, axis=0)


def _rms_modulate(x, g, shift, scale):
    ms = jnp.mean(x * x, axis=-1, keepdims=True)
    return (x * lax.rsqrt(ms + EPS) * g) * (1.0 + scale) + shift


def _opaque_zero(x):
    bits = pltpu.bitcast(x, jnp.uint32)
    return pltpu.bitcast((bits >> 16) >> 16, F32)


def _normalize_items(h_s, x_ref, g_ref, mod_ref, tokens):
    shift = mod_ref[0, :, 0:D_MODEL]
    scale = mod_ref[0, :, D_MODEL:2 * D_MODEL]
    for r in range(0, TM, NORM_ROWS):
        yield 0, 170
        h = _rms_modulate(x_ref[0, r:r + NORM_ROWS], g_ref[...], shift, scale)
        h_s[r:r + NORM_ROWS] = h.astype(BF16)
        tokens.append(_opaque_zero(h[0:8, 0:LANES])[0:1])


def _out_proj_items(get_a, wout_ref, x_ref, mod_ref, y_ref):
    for n in range(D_MODEL // NT):
        yield _mxu_cost(TM, D_MODEL, NT), 70
        cols = slice(n * NT, (n + 1) * NT)
        mixed = jnp.dot(get_a(), wout_ref[:, cols], preferred_element_type=F32)
        y_ref[0, :, cols] = x_ref[0, :, cols] + mod_ref[0, :, 2 * D_MODEL + n * NT:2 * D_MODEL + (n + 1) * NT] * mixed


def _mxu_cost(m, k, n):
    tiles = -(-k // NT) * -(-n // NT)
    return tiles * max(m // 2, 64) // 2


def _weave(order, **stages):
    by_letter = {name[0].upper(): stage for name, stage in stages.items()}
    for stage in by_letter.values():
        next(stage, None)
    for letter in order:
        if letter in by_letter:
            next(by_letter[letter], None)
    for stage in by_letter.values():
        for _ in stage:
            pass


def _with_halo(u_s, slot, cols, prev_slot, next_slot, has_prev, has_next):
    u = u_s[slot, :, cols]
    zero = jnp.zeros((HALO, u.shape[1]), F32)
    if prev_slot is None:
        top = bot = zero
    else:
        top = jnp.where(has_prev, u_s[prev_slot, TM - HALO:TM, cols], zero)
        bot = jnp.where(has_next, u_s[next_slot, 0:HALO, cols], zero)
    return jnp.concatenate([top, u, bot], axis=0)


def _pipeline_steps(s, n_chunks, lag, ring, normalized, staged, order, normalize, project, finish):
    fin_slot = lax.rem(s + (2 * ring - 1 - lag), jnp.int32(ring))

    @pl.when((s >= 1) & (s <= n_chunks))
    def _publish_normalized():
        h_w, h_s = normalized
        h_s[...] = h_w[...]

    @pl.when((s >= 2) & (s <= n_chunks + 1))
    def _publish_projected():
        slot = lax.rem(s + (ring - 2), jnp.int32(ring))
        for stage_ref, ring_ref in staged:
            ring_ref[slot] = stage_ref[...]

    @pl.when(s == 0)
    def _first():
        _weave(order, normalize=normalize([]))

    @pl.when((s >= 1) & (s <= lag))
    def _fill():
        _weave(order, project=project(), normalize=normalize([]))

    @pl.when((s > lag) & (s <= n_chunks))
    def _steady():
        tokens = []
        _weave(order, finish=finish(fin_slot, tokens), project=project(), normalize=normalize(tokens))

    @pl.when(s > n_chunks)
    def _drain():
        _weave(order, finish=finish(fin_slot, []))


def _adaln_kernel(cond_ref, we_ref, be_ref, wo_ref, bo_ref, oe_ref, oo_ref):
    s = _silu(cond_ref[...]).astype(BF16)
    oe_ref[...] = jnp.dot(s, we_ref[...].astype(BF16), preferred_element_type=F32) + be_ref[...]
    oo_ref[...] = jnp.dot(s, wo_ref[...].astype(BF16), preferred_element_type=F32) + bo_ref[...]


def _adaln(cond8, w_e, b_e, w_o, b_o):
    n = 3 * D_MODEL
    wspec = pl.BlockSpec((D_MODEL, ADA_TN), lambda j: (0, j))
    vspec = pl.BlockSpec((1, ADA_TN), lambda j: (0, j))
    ospec = pl.BlockSpec((8, ADA_TN), lambda j: (0, j))
    return pl.pallas_call(
        _adaln_kernel,
        out_shape=(jax.ShapeDtypeStruct((8, n), F32), jax.ShapeDtypeStruct((8, n), F32)),
        grid=(n // ADA_TN,),
        in_specs=[pl.BlockSpec((8, D_MODEL), lambda j: (0, 0)), wspec, vspec, wspec, vspec],
        out_specs=(ospec, ospec),
        compiler_params=pltpu.CompilerParams(dimension_semantics=("arbitrary",), vmem_limit_bytes=VMEM_LIMIT),
        name="adaln",
    )(cond8, w_e, b_e.reshape(1, n), w_o, b_o.reshape(1, n))


def _head_inv_rms(x, bd):
    sq = x * x
    hi = sq.astype(BF16)
    lo = (sq - hi.astype(F32)).astype(BF16)
    ms = (jnp.dot(hi, bd, preferred_element_type=F32) + jnp.dot(lo, bd, preferred_element_type=F32)) * (1.0 / HEAD_DIM)
    return lax.rsqrt(ms + EPS)


def _rope(x, cos, sin_signed, first_half):
    partner = jnp.where(first_half, pltpu.roll(x, LANES - ROPE_FREQS, axis=1), pltpu.roll(x, ROPE_FREQS, axis=1))
    return x * cos + partner * sin_signed


def _dup_kv(x):
    lane = lax.broadcasted_iota(jnp.int32, (1, LANES), 1)
    swapped = pltpu.roll(x, HEAD_DIM, axis=1)
    return jnp.where(lane < HEAD_DIM, x, swapped), jnp.where(lane < HEAD_DIM, swapped, x)


def _tiled_keys(k):
    ka, kb = _dup_kv(k)
    return jnp.concatenate([ka, ka, kb, kb], axis=1).astype(BF16)


def _values_t(v):
    va, vb = _dup_kv(v)
    return jnp.concatenate([va.T, vb.T], axis=0).astype(BF16)


def _even_kernel(*refs, nc, n_chunks, lag, ring, windowed):
    if windowed:
        (xp_ref, xa_ref, modp_ref, moda_ref, g_ref, win_ref, cw_ref, cb_ref, qg_ref, kg_ref, sink_ref, wout_ref,
         cos_ref, sin_ref, ck_ref, cv_ref,
         y_ref,
         h_s, h_w, u_s, gate_s, q_s, k_s, vt_s, gb_s, u_w, gate_w, q_w, k_w, vt_w, gb_w, a_s, ck_s, cvt_s) = refs
        ko_ref = vo_ref = None
    else:
        (xp_ref, xa_ref, modp_ref, moda_ref, g_ref, win_ref, cw_ref, cb_ref, qg_ref, kg_ref, sink_ref, wout_ref,
         y_ref, ko_ref, vo_ref,
         h_s, h_w, u_s, gate_s, q_s, k_s, vt_s, gb_s, u_w, gate_w, q_w, k_w, vt_w, gb_w, a_s) = refs
    s = pl.program_id(0)
    i = s - 1 - lag
    ci = lax.rem(i, jnp.int32(nc))

    @pl.when(s == 0)
    def _init():
        k_s[...] = jnp.zeros(k_s.shape, BF16)
        vt_s[...] = jnp.zeros(vt_s.shape, BF16)

    if windowed:
        @pl.when((i >= 0) & (ci == 0))
        def _load_ctx():
            ck_s[...] = _tiled_keys(ck_ref[0])
            cvt_s[...] = _values_t(cv_ref[0])

    def normalize(tokens):
        yield from _normalize_items(h_w, xp_ref, g_ref, modp_ref, tokens)

    def project():
        o_bg, o_cg, o_xs, o_ga, o_q, o_k, _, o_gb = EVEN_OFFS[:8]

        def proj(start):
            return jnp.dot(h_s[...], win_ref[:, start:start + NT], preferred_element_type=F32)

        lane = lax.broadcasted_iota(jnp.int32, (1, LANES), 1)
        first_half = (lane & (2 * ROPE_FREQS - 1)) < ROPE_FREQS
        ri = lax.broadcasted_iota(jnp.int32, (NT, NT), 0) >> HEAD_SHIFT
        cj = lax.broadcasted_iota(jnp.int32, (NT, NT), 1) >> HEAD_SHIFT
        bd = (ri == cj).astype(BF16)

        def qk_norm(x, g):
            w = x.shape[1]
            inv = _head_inv_rms(x, bd[0:w, 0:w])
            pieces = []
            for j in range(w // LANES):
                lanes = slice(j * LANES, (j + 1) * LANES)
                xj = x[:, lanes] * inv[:, lanes] * g
                pieces.append(_rope(xj, cos_ref[...], sin_ref[...], first_half) if windowed else xj)
            return pieces

        tile = _mxu_cost(TM, D_MODEL, NT)

        def keys_values():
            kv = proj(o_k)
            k, = qk_norm(kv[:, 0:KV_WIDTH], kg_ref[...])
            v = kv[:, KV_WIDTH:2 * KV_WIDTH]
            if not windowed:
                ko_ref[0] = k
                vo_ref[0] = v
            k_w[...] = _tiled_keys(k)
            vt_w[...] = _values_t(v)

        def queries(n):
            for j, qj in enumerate(qk_norm(proj(o_q + n * NT), qg_ref[...])):
                q_w[:, n * NT + j * LANES:n * NT + (j + 1) * LANES] = (qj * ATTN_SCALE).astype(BF16)

        def conv_gate(n):
            gate_w[:, n * NT:(n + 1) * NT] = proj(o_bg + n * NT) * _silu(proj(o_ga + n * NT))

        def conv_input(n):
            u_w[:, n * NT:(n + 1) * NT] = proj(o_cg + n * NT) * proj(o_xs + n * NT)

        def attn_gate(n):
            gb_w[:, n * NT:(n + 1) * NT] = _silu(proj(o_gb + n * NT))

        items = [((2 * tile, 180), conv_gate, 0), ((tile, 350), keys_values, None), ((2 * tile, 40), conv_input, 0),
                 ((tile, 300), queries, 0), ((2 * tile, 180), conv_gate, 1), ((tile, 300), queries, 1),
                 ((2 * tile, 40), conv_input, 1), ((tile, 110), attn_gate, 0), ((tile, 110), attn_gate, 1)]
        for cost, emit, n in items:
            yield cost
            emit() if n is None else emit(n)

    def finish(slot, tokens):
        if windowed:
            prev_slot = lax.rem(slot + (ring - 1), jnp.int32(ring))
            next_slot = lax.rem(slot + 1, jnp.int32(ring))
            has_prev = ci > 0
            has_next = ci < nc - 1
        else:
            prev_slot = next_slot = has_prev = has_next = None

        yield 0, 400
        ext = _with_halo(u_s, slot, slice(None), prev_slot, next_slot, has_prev, has_next)
        conv = (_rows_down(ext, 1)[HALO:HALO + TM] * cw_ref[0:1, :]
                + ext[HALO:HALO + TM] * cw_ref[1:2, :]
                + _rows_up(ext, 1)[HALO:HALO + TM] * cw_ref[2:3, :]
                + cb_ref[...])
        a_s[:, 0:CONV_WIDTH] = (gate_s[slot] * conv).astype(BF16)

        lane_head = lax.broadcasted_iota(jnp.int32, (1, GROUP_W), 1) >> HEAD_SHIFT
        cols4 = Q_PER_KV * BLOCK
        n_keys = 3 * BLOCK + ck_s.shape[0] if windowed else TM
        key_i =lax.broadcasted_iota(jnp.int32, (BLOCK, cols4), 0)
        qry_i = lax.broadcasted_iota(jnp.int32, (BLOCK, cols4), 1) & (BLOCK - 1)
        upper = lax.broadcasted_iota(jnp.int32, (LANES, LANES), 0) < HEAD_DIM
        halves = (slice(0, BLOCK), slice(BLOCK, TM))
        for qb in range(TM // BLOCK):
            own = halves[qb]
            if windowed:
                if qb == 0:
                    blocks = ((prev_slot, halves[1]), (slot, halves[0]), (slot, halves[1]))
                    prev_in, next_in = has_prev, True
                else:
                    blocks = ((slot, halves[0]), (slot, halves[1]), (next_slot, halves[0]))
                    prev_in, next_in = True, has_next
                prev_ok = (key_i - qry_i) >= jnp.where(prev_in, 0, BLOCK)
                next_ok = (qry_i - key_i) >= jnp.where(next_in, 0, BLOCK)
            for kv in range(N_KV_HEADS):
                cols = slice(kv * GROUP_W, (kv + 1) * GROUP_W)
                vrows = slice(kv * LANES, (kv + 1) * LANES)
                yield _mxu_cost(n_keys, GROUP_W, cols4), 60
                qblk = q_s[slot, own, cols]
                q4 = jnp.concatenate(
                    [jnp.where(lane_head == g, qblk, jnp.zeros_like(qblk)) for g in range(Q_PER_KV)], axis=0)

                def scores_t(kk):
                    return lax.dot_general(kk, q4, (((1,), (1,)), ((), ())), preferred_element_type=F32)

                if windowed:
                    s_loc = scores_t(jnp.concatenate([k_s[sl, r, cols] for sl, r in blocks], axis=0))
                    vt_loc = jnp.concatenate([vt_s[sl, vrows, r] for sl, r in blocks], axis=1)
                    s_ctx = scores_t(ck_s[:, cols])
                    yield 0, n_keys * cols4 * 5 // (8 * LANES * 4)
                    groups = [([jnp.where(prev_ok, s_loc[0:BLOCK], NEG), s_loc[BLOCK:2 * BLOCK],
                                jnp.where(next_ok, s_loc[2 * BLOCK:3 * BLOCK], NEG)], vt_loc),
                              ([s_ctx], cvt_s[vrows, :])]
                else:
                    s_own = scores_t(k_s[slot, :, cols])
                    yield 0, n_keys * cols4 * 5 // (8 * LANES * 4)
                    groups = [([s_own], vt_s[slot, vrows, :])]
                snk = jnp.concatenate(
                    [jnp.full((1, BLOCK), sink_ref[kv * Q_PER_KV + g], F32) for g in range(Q_PER_KV)], axis=1)
                if tokens:
                    snk = snk + jnp.concatenate([tokens.pop(0)] * Q_PER_KV, axis=1)
                mx = snk
                for ss, _ in groups:
                    for sc in ss:
                        mx = jnp.maximum(mx, jnp.max(sc, axis=0, keepdims=True))
                den = jnp.exp(snk - mx)
                probs = []
                for ss, vt in groups:
                    es = []
                    for sc in ss:
                        e = jnp.exp(sc - mx)
                        den = den + jnp.sum(e, axis=0, keepdims=True)
                        es.append(e.astype(BF16))
                    probs.append((es[0] if len(es) == 1 else jnp.concatenate(es, axis=0), vt))
                yield _mxu_cost(LANES, n_keys, cols4), 120
                o_t = jnp.zeros((LANES, cols4), F32)
                for e_all, vt in probs:
                    o_t = o_t + jnp.dot(vt, e_all, preferred_element_type=F32)
                o_t = o_t * (1.0 / den)
                o01 = jnp.where(upper, o_t[:, 0:BLOCK], o_t[:, BLOCK:2 * BLOCK]).T
                o23 = jnp.where(upper, o_t[:, 2 * BLOCK:3 * BLOCK], o_t[:, 3 * BLOCK:4 * BLOCK]).T
                a_s[own, CONV_WIDTH + kv * GROUP_W:CONV_WIDTH + (kv + 1) * GROUP_W] = (
                    jnp.concatenate([o01, o23], axis=1) * gb_s[slot, own, cols]).astype(BF16)

        yield from _out_proj_items(lambda: a_s[...], wout_ref, xa_ref, moda_ref, y_ref)

    staged = ((u_w, u_s), (gate_w, gate_s), (q_w, q_s), (k_w, k_s), (vt_w, vt_s), (gb_w, gb_s))
    _pipeline_steps(s, n_chunks, lag, ring, (h_w, h_s), staged, EVEN_ORDER, normalize, project, finish)


def _chunk_specs(n_chunks, nc, lag, per_seq_mod):
    last = n_chunks - 1
    norm_chunk = lambda s: jnp.minimum(s, last)
    proj_chunk = lambda s: jnp.clip(s - 1, 0, last)
    fin_chunk = lambda s: jnp.clip(s - 1 - lag, 0, last)
    seq_of = (lambda j: j // nc) if per_seq_mod else (lambda j: 0)
    return dict(
        x_norm=pl.BlockSpec((1, TM, D_MODEL), lambda s: (norm_chunk(s), 0, 0)),
        x_fin=pl.BlockSpec((1, TM, D_MODEL), lambda s: (fin_chunk(s), 0, 0)),
        mod_norm=pl.BlockSpec((1, 1, 3 * D_MODEL), lambda s: (seq_of(norm_chunk(s)), 0, 0)),
        mod_fin=pl.BlockSpec((1, 1, 3 * D_MODEL), lambda s: (seq_of(fin_chunk(s)), 0, 0)),
        proj_chunk=proj_chunk, fin_chunk=fin_chunk)


def _const_spec(shape):
    return pl.BlockSpec(shape, lambda c: (0,) * len(shape))


def _even_layer(x, mod, norm_g, w_in, conv_w, conv_b, q_g, k_g, sink, w_out, rope=None, ctx=None):
    n_seq, seq_len, _ = x.shape
    windowed = rope is not None
    nc = seq_len // TM
    n_chunks = n_seq * nc
    lag, ring = (2, 4) if windowed else (1, 2)
    assert windowed or nc == 1
    sp = _chunk_specs(n_chunks, nc, lag, windowed)
    xc = x.reshape(n_chunks, TM, D_MODEL)
    in_specs = [
        sp["x_norm"], sp["x_fin"], sp["mod_norm"], sp["mod_fin"],
        _const_spec((1, D_MODEL)),
        _const_spec((D_MODEL, EVEN_IN)),
        _const_spec((3, CONV_WIDTH)),
        _const_spec((1, CONV_WIDTH)),
        _const_spec((1, LANES)),
        _const_spec((1, LANES)),
        pl.BlockSpec(memory_space=pltpu.SMEM),
        _const_spec((D_MODEL, D_MODEL)),
    ]
    args = [xc, xc, mod, mod, norm_g.reshape(1, D_MODEL), w_in, conv_w, conv_b.reshape(1, CONV_WIDTH),
            jnp.tile(q_g, LANES // HEAD_DIM).reshape(1, LANES), jnp.tile(k_g, LANES // HEAD_DIM).reshape(1, LANES),
            sink, w_out]
    per_chunk = [
        ((TM, CONV_WIDTH), F32),
        ((TM, CONV_WIDTH), F32),
        ((TM, ATTN_WIDTH), BF16),
        ((TM, 2 * GROUP_W), BF16),
        ((2 * LANES, TM), BF16),
        ((TM, ATTN_WIDTH), F32),
    ]
    scratch = (
        [pltpu.VMEM((TM, D_MODEL), BF16)] * 2
        + [pltpu.VMEM((ring,) + shape, dt) for shape, dt in per_chunk]
        + [pltpu.VMEM(shape, dt) for shape, dt in per_chunk]
        + [pltpu.VMEM((TM, D_MODEL), BF16)])
    y_shape = jax.ShapeDtypeStruct(xc.shape, F32)
    y_spec = pl.BlockSpec((1, TM, D_MODEL), lambda c: (sp["fin_chunk"](c), 0, 0))
    if windowed:
        cos, sin = rope
        ck, cv = ctx
        rope_spec = pl.BlockSpec((TM, LANES), lambda c: (sp["proj_chunk"](c) % nc, 0))
        ctx_spec = pl.BlockSpec((1,) + ck.shape[1:], lambda c: (sp["fin_chunk"](c) // nc, 0, 0))
        in_specs += [rope_spec, rope_spec, ctx_spec, ctx_spec]
        args += [cos, sin, ck, cv]
        scratch += [pltpu.VMEM((ck.shape[1], 2 * GROUP_W), BF16), pltpu.VMEM((2 * LANES, cv.shape[1]), BF16)]
        out_shape, out_specs = y_shape, y_spec
    else:
        kv_shape = jax.ShapeDtypeStruct((n_chunks, TM, KV_WIDTH), F32)
        kv_spec = pl.BlockSpec((1, TM, KV_WIDTH), lambda c: (sp["proj_chunk"](c), 0, 0))
        out_shape, out_specs = (y_shape, kv_shape, kv_shape), (y_spec, kv_spec, kv_spec)
    out = pl.pallas_call(
        functools.partial(_even_kernel, nc=nc, n_chunks=n_chunks, lag=lag, ring=ring, windowed=windowed),
        out_shape=out_shape,
        grid=(n_chunks + lag + 1,),
        in_specs=in_specs,
        out_specs=out_specs,
        scratch_shapes=scratch,
        compiler_params=PIPELINE_PARAMS,
        name="even_latent" if windowed else "even_context",
    )(*args)
    if windowed:
        return out.reshape(x.shape)
    y, k, v = out
    return y.reshape(x.shape), k.reshape(n_seq, seq_len, KV_WIDTH), v.reshape(n_seq, seq_len, KV_WIDTH)


def _odd_kernel(xp_ref, xa_ref, modp_ref, moda_ref, g_ref, win_ref, pw_ref, ps_ref, wout_ref, y_ref,
                h_s, h_w, u_s, sg_s, u_w, sg_w, *, nc, n_chunks, lag, ring):
    s = pl.program_id(0)
    i = s - 1 - lag
    ci = lax.rem(i, jnp.int32(nc))

    def normalize(tokens):
        yield from _normalize_items(h_w, xp_ref, g_ref, modp_ref, tokens)

    def project():
        tile = _mxu_cost(TM, D_MODEL, NT)
        for n in range(D_MODEL // NT):
            yield tile, 10
            cols = slice(n * NT, (n + 1) * NT)
            u_w[:, cols] = jnp.dot(h_s[...], win_ref[:, cols], preferred_element_type=F32)
        for n in range(D_MODEL // NT):
            yield tile, 110
            cols = slice(n * NT, (n + 1) * NT)
            sg_w[:, cols] = _silu(jnp.dot(h_s[...], win_ref[:, D_MODEL + n * NT:D_MODEL + (n + 1) * NT],
                                          preferred_element_type=F32))

    def finish(slot, tokens):
        if nc > 1:
            prev_slot = lax.rem(slot + (ring - 1), jnp.int32(ring))
            next_slot = lax.rem(slot + 1, jnp.int32(ring))
            has_prev = ci > 0
            has_next = ci < nc - 1
        else:
            prev_slot = next_slot = has_prev = has_next = None
        t = ci * TM + lax.broadcasted_iota(jnp.int32, (TM, 1), 0)
        outs = []
        for gi, w in enumerate(POOL_SIZES):
            yield _mxu_cost(TM, POOL_GROUP, POOL_GROUP), 140 + 120 * gi
            cols = slice(gi * POOL_GROUP, (gi + 1) * POOL_GROUP)
            ext = _with_halo(u_s, slot, cols, prev_slot, next_slot, has_prev, has_next)
            acc = ext
            span = 1
            while span < w:
                acc = acc + _rows_down(acc, span)
                span *= 2
            if w > 2:
                acc = _rows_up(acc, w // 2 - 1)
            cnt = (jnp.minimum(t + w // 2, nc * TM) - jnp.maximum(t - w // 2, 0)).astype(F32)
            if tokens:
                cnt = cnt + tokens.pop(0)[:, 0:1]
            pooled = acc[HALO:HALO + TM] / cnt - ext[HALO:HALO + TM]
            outs.append(jnp.dot(pooled.astype(BF16), pw_ref[gi], preferred_element_type=F32))
        yield 0, 130
        y = (jnp.concatenate(outs, axis=1) * ps_ref[...] * sg_s[slot]).astype(BF16)
        yield from _out_proj_items(lambda: y, wout_ref, xa_ref, moda_ref, y_ref)

    _pipeline_steps(s, n_chunks, lag, ring, (h_w, h_s), ((u_w, u_s), (sg_w, sg_s)), ODD_ORDER,
                    normalize, project, finish)


def _odd_layer(x, mod, per_seq_mod, norm_g, w_in, pool_w, pool_scale, w_out):
    n_seq, seq_len, _ = x.shape
    nc = seq_len // TM
    n_chunks = n_seq * nc
    lag, ring = (2, 4) if nc > 1 else (1, 2)
    sp = _chunk_specs(n_chunks, nc, lag, per_seq_mod)
    xc = x.reshape(n_chunks, TM, D_MODEL)
    in_specs = [
        sp["x_norm"], sp["x_fin"], sp["mod_norm"], sp["mod_fin"],
        _const_spec((1, D_MODEL)),
        _const_spec((D_MODEL, 2 * D_MODEL)),
        _const_spec(pool_w.shape),
        _const_spec((1, D_MODEL)),
        _const_spec((D_MODEL, D_MODEL)),
    ]
    out = pl.pallas_call(
        functools.partial(_odd_kernel, nc=nc, n_chunks=n_chunks, lag=lag, ring=ring),
        out_shape=jax.ShapeDtypeStruct(xc.shape, F32),
        grid=(n_chunks + lag + 1,),
        in_specs=in_specs,
        out_specs=pl.BlockSpec((1, TM, D_MODEL), lambda c: (sp["fin_chunk"](c), 0, 0)),
        scratch_shapes=[
            pltpu.VMEM((TM, D_MODEL), BF16),
            pltpu.VMEM((TM, D_MODEL), BF16),
            pltpu.VMEM((ring, TM, D_MODEL), F32),
            pltpu.VMEM((ring, TM, D_MODEL), F32),
            pltpu.VMEM((TM, D_MODEL), F32),
            pltpu.VMEM((TM, D_MODEL), F32),
        ],
        compiler_params=PIPELINE_PARAMS,
        name="odd_latent" if per_seq_mod else "odd_context",
    )(xc, xc, mod, mod, norm_g.reshape(1, D_MODEL), w_in, pool_w, pool_scale.reshape(1, D_MODEL), w_out)
    return out.reshape(x.shape)


def _rope_tables(seq_len):
    n_rows = seq_len // GRID_W
    row = np.repeat(np.arange(n_rows), GRID_W).astype(np.float32)
    col = np.tile(np.arange(GRID_W), n_rows).astype(np.float32)
    inv = (np.float32(ROPE_BASE) ** (-np.arange(ROPE_FREQS, dtype=np.float32) / np.float32(ROPE_FREQS))).astype(np.float32)
    ang = np.stack([row[:, None] * inv, col[:, None] * inv], axis=1)
    cos = np.broadcast_to(np.cos(ang)[:, :, None, :], (seq_len, 2, 2, ROPE_FREQS))
    sin = np.sin(ang)[:, :, None, :] * np.array([-1.0, 1.0], np.float32)[None, None, :, None]
    cos = cos.reshape(seq_len, HEAD_DIM).astype(np.float32)
    sin = sin.reshape(seq_len, HEAD_DIM).astype(np.float32)
    return jnp.asarray(np.tile(cos, (1, LANES // HEAD_DIM))), jnp.asarray(np.tile(sin, (1, LANES // HEAD_DIM)))


def kernel(x_prompt, x_sample, cache_k, cache_v, c, c_ctx, ada_w_e, ada_b_e, norm_g_e, w_in_e, conv_w, conv_b,
           q_norm_g, k_norm_g, sink, w_out_e, ada_w_o, ada_b_o, norm_g_o, w_in_o, pool_w, pool_scale, w_out_o):
    n_dec = x_sample.shape[0]
    depth = ada_w_e.shape[0] + ada_w_o.shape[0]
    assert ada_w_e.shape[0] == 1 and ada_w_o.shape[0] == 1 and n_dec + 1 <= 8
    cond8 = jnp.concatenate([c_ctx[None], c, jnp.zeros((8 - 1 - n_dec, D_MODEL), F32)], axis=0)
    mod_e, mod_o = _adaln(cond8, ada_w_e[0], ada_b_e[0], ada_w_o[0], ada_b_o[0])
    rope = _rope_tables(x_sample.shape[1])

    yp, ys = x_prompt, x_sample
    new_k, new_v = [], []
    for layer in range(depth):
        i = layer // 2
        if layer % 2 == 0:
            mod = mod_e.reshape(8, 1, 3 * D_MODEL)
            w_in = w_in_e[i].astype(BF16)
            w_out = w_out_e[i].astype(BF16)
            common = (norm_g_e[i], w_in, conv_w[i], conv_b[i], q_norm_g[i], k_norm_g[i], sink[i], w_out)
            yp, k, v = _even_layer(yp, mod[0:1], *common)
            new_k.append(k.reshape(k.shape[0], k.shape[1], N_KV_HEADS, HEAD_DIM))
            new_v.append(v.reshape(v.shape[0], v.shape[1], N_KV_HEADS, HEAD_DIM))
            ck = cache_k[:, i].reshape(n_dec, cache_k.shape[2], KV_WIDTH)
            cv = cache_v[:, i].reshape(n_dec, cache_v.shape[2], KV_WIDTH)
            ys = _even_layer(ys, mod[1:1 + n_dec], *common, rope=rope, ctx=(ck, cv))
        else:
            mod = mod_o.reshape(8, 1, 3 * D_MODEL)
            common = (norm_g_o[i], w_in_o[i].astype(BF16), pool_w[i].astype(BF16), pool_scale[i],
                      w_out_o[i].astype(BF16))
            yp = _odd_layer(yp, mod[0:1], False, *common)
            ys = _odd_layer(ys, mod[1:1 + n_dec], True, *common)
    return yp, ys, jnp.stack(new_k, axis=1), jnp.stack(new_v, axis=1)
```

```python
import functools

import jax
import jax.numpy as jnp
import numpy as np
from jax import lax
from jax.experimental import pallas as pl
from jax.experimental.pallas import tpu as pltpu

F32 = jnp.float32
BF16 = jnp.bfloat16

D_MODEL = 1024
GRID_W = 64
HEAD_DIM = 64
HEAD_SHIFT = 6
ATTN_WIDTH = D_MODEL // 2
N_Q_HEADS = ATTN_WIDTH // HEAD_DIM
N_KV_HEADS = N_Q_HEADS // 4
Q_PER_KV = N_Q_HEADS // N_KV_HEADS
KV_WIDTH = N_KV_HEADS * HEAD_DIM
CONV_WIDTH = D_MODEL - ATTN_WIDTH
WINDOW = 128
BLOCK = 128
ROPE_BASE = 10000.0
ROPE_FREQS = HEAD_DIM // 4
ATTN_SCALE = HEAD_DIM ** -0.5
LOG2E = float(np.log2(np.e))
NEG = -1e30
POOL_SIZES = (2, 4, 8, 16)
POOL_GROUP = D_MODEL // len(POOL_SIZES)
EPS = 1e-6
EVEN_SIZES = (CONV_WIDTH, CONV_WIDTH, CONV_WIDTH, CONV_WIDTH, ATTN_WIDTH, KV_WIDTH, KV_WIDTH, ATTN_WIDTH)
EVEN_IN = sum(EVEN_SIZES)
EVEN_OFFS = tuple(int(s) for s in np.cumsum((0,) + EVEN_SIZES))

TM = 256
NT = 256
NORM_ROWS = 64
EVEN_ORDER = "PF" + "FNPFPF" * 4 + "FFFF"
ODD_ORDER = "PNFP" * 4 + "FFFFF"
LANES = 128
GROUP_W = Q_PER_KV * HEAD_DIM
HALO = 8
ADA_TN = 768
VMEM_LIMIT = 56 * 1024 * 1024
PIPELINE_PARAMS = pltpu.CompilerParams(dimension_semantics=("arbitrary",), vmem_limit_bytes=VMEM_LIMIT)


def _silu(x):
    return x / (1.0 + jnp.exp(-x))


def _rows_down(x, k):
    return pltpu.roll(x, k, axis=0)


def _rows_up(x, k):
    return pltpu.roll(x, x.shape[0] - k, axis=0)


def _rms_modulate(x, g, shift, scale):
    ms = jnp.mean(x * x, axis=-1, keepdims=True)
    return (x * lax.rsqrt(ms + EPS) * g) * (1.0 + scale) + shift


def _opaque_zero(x):
    bits = pltpu.bitcast(x, jnp.uint32)
    return pltpu.bitcast((bits >> 16) >> 16, F32)


def _normalize_items(h_s, x_ref, g_ref, mod_ref, tokens):
    shift = mod_ref[0, :, 0:D_MODEL]
    scale = mod_ref[0, :, D_MODEL:2 * D_MODEL]
    for r in range(0, TM, NORM_ROWS):
        yield 0, 170
        h = _rms_modulate(x_ref[0, r:r + NORM_ROWS], g_ref[...], shift, scale)
        h_s[r:r + NORM_ROWS] = h.astype(BF16)
        tokens.append(_opaque_zero(h[0:8, 0:LANES])[0:1])


def _out_proj_items(get_a, wout_ref, x_ref, mod_ref, y_ref):
    for n in range(D_MODEL // NT):
        yield _mxu_cost(TM, D_MODEL, NT), 70
        cols = slice(n * NT, (n + 1) * NT)
        mixed = jnp.dot(get_a(), wout_ref[:, cols], preferred_element_type=F32)
        y_ref[0, :, cols] = x_ref[0, :, cols] + mod_ref[0, :, 2 * D_MODEL + n * NT:2 * D_MODEL + (n + 1) * NT] * mixed


def _mxu_cost(m, k, n):
    tiles = -(-k // NT) * -(-n // NT)
    return tiles * max(m // 2, 64) // 2


def _weave(order, **stages):
    by_letter = {name[0].upper(): stage for name, stage in stages.items()}
    for stage in by_letter.values():
        next(stage, None)
    for letter in order:
        if letter in by_letter:
            next(by_letter[letter], None)
    for stage in by_letter.values():
        for _ in stage:
            pass


def _with_halo(u_s, slot, cols, prev_slot, next_slot, has_prev, has_next):
    u = u_s[slot, :, cols]
    zero = jnp.zeros((HALO, u.shape[1]), F32)
    if prev_slot is None:
        top = bot = zero
    else:
        top = jnp.where(has_prev, u_s[prev_slot, TM - HALO:TM, cols], zero)
        bot = jnp.where(has_next, u_s[next_slot, 0:HALO, cols], zero)
    return jnp.concatenate([top, u, bot], axis=0)


def _pipeline_steps(s, n_chunks, lag, ring, normalized, staged, order, normalize, project, finish):
    fin_slot = lax.rem(s + (2 * ring - 1 - lag), jnp.int32(ring))

    @pl.when((s >= 1) & (s <= n_chunks))
    def _publish_normalized():
        h_w, h_s = normalized
        h_s[...] = h_w[...]

    @pl.when((s >= 2) & (s <= n_chunks + 1))
    def _publish_projected():
        slot = lax.rem(s + (ring - 2), jnp.int32(ring))
        for stage_ref, ring_ref in staged:
            ring_ref[slot] = stage_ref[...]

    @pl.when(s == 0)
    def _first():
        _weave(order, normalize=normalize([]))

    @pl.when((s >= 1) & (s <= lag))
    def _fill():
        _weave(order, project=project([]), normalize=normalize([]))

    @pl.when((s > lag) & (s <= n_chunks))
    def _steady():
        tokens = []
        _weave(order, finish=finish(fin_slot, tokens), project=project(tokens), normalize=normalize(tokens))

    @pl.when(s > n_chunks)
    def _drain():
        _weave(order, finish=finish(fin_slot, []))


def _adaln_kernel(cond_ref, we_ref, be_ref, wo_ref, bo_ref, oe_ref, oo_ref):
    s = _silu(cond_ref[...]).astype(BF16)
    oe_ref[...] = jnp.dot(s, we_ref[...].astype(BF16), preferred_element_type=F32) + be_ref[...]
    oo_ref[...] = jnp.dot(s, wo_ref[...].astype(BF16), preferred_element_type=F32) + bo_ref[...]


def _adaln(cond8, w_e, b_e, w_o, b_o):
    n = 3 * D_MODEL
    wspec = pl.BlockSpec((D_MODEL, ADA_TN), lambda j: (0, j))
    vspec = pl.BlockSpec((1, ADA_TN), lambda j: (0, j))
    ospec = pl.BlockSpec((8, ADA_TN), lambda j: (0, j))
    return pl.pallas_call(
        _adaln_kernel,
        out_shape=(jax.ShapeDtypeStruct((8, n), F32), jax.ShapeDtypeStruct((8, n), F32)),
        grid=(n // ADA_TN,),
        in_specs=[pl.BlockSpec((8, D_MODEL), lambda j: (0, 0)), wspec, vspec, wspec, vspec],
        out_specs=(ospec, ospec),
        compiler_params=pltpu.CompilerParams(dimension_semantics=("arbitrary",), vmem_limit_bytes=VMEM_LIMIT),
        name="adaln",
    )(cond8, w_e, b_e.reshape(1, n), w_o, b_o.reshape(1, n))


def _head_inv_rms(x, bd):
    sq = x * x
    hi = sq.astype(BF16)
    lo = (sq - hi.astype(F32)).astype(BF16)
    ms = (jnp.dot(hi, bd, preferred_element_type=F32) + jnp.dot(lo, bd, preferred_element_type=F32)) * (1.0 / HEAD_DIM)
    return lax.rsqrt(ms + EPS)


def _rope(x, cos, sin_signed, first_half):
    partner = jnp.where(first_half, pltpu.roll(x, LANES - ROPE_FREQS, axis=1), pltpu.roll(x, ROPE_FREQS, axis=1))
    return x * cos + partner * sin_signed


def _dup_kv(x):
    lane = lax.broadcasted_iota(jnp.int32, (1, LANES), 1)
    swapped = pltpu.roll(x, HEAD_DIM, axis=1)
    return jnp.where(lane < HEAD_DIM, x, swapped), jnp.where(lane < HEAD_DIM, swapped, x)


def _tiled_keys(k):
    ka, kb = _dup_kv(k)
    return jnp.concatenate([ka, ka, kb, kb], axis=1).astype(BF16)


def _values_t(v):
    va, vb = _dup_kv(v)
    return jnp.concatenate([va.T, vb.T], axis=0).astype(BF16)


def _even_kernel(*refs, nc, n_chunks, lag, ring, windowed):
    if windowed:
        (xp_ref, xa_ref, modp_ref, moda_ref, g_ref, win_ref, cw_ref, cb_ref, qg_ref, kg_ref, sink_ref, wout_ref,
         cos_ref, sin_ref, ck_ref, cv_ref,
         y_ref,
         h_s, h_w, u_s, gate_s, q_s, k_s, vt_s, gb_s, u_w, gate_w, q_w, k_w, vt_w, gb_w, a_s, ck_s, cvt_s) = refs
        ko_ref = vo_ref = None
    else:
        (xp_ref, xa_ref, modp_ref, moda_ref, g_ref, win_ref, cw_ref, cb_ref, qg_ref, kg_ref, sink_ref, wout_ref,
         y_ref, ko_ref, vo_ref,
         h_s, h_w, u_s, gate_s, q_s, k_s, vt_s, gb_s, u_w, gate_w, q_w, k_w, vt_w, gb_w, a_s) = refs
    s = pl.program_id(0)
    i = s - 1 - lag
    ci = lax.rem(i, jnp.int32(nc))

    @pl.when(s == 0)
    def _init():
        k_s[...] = jnp.zeros(k_s.shape, BF16)
        vt_s[...] = jnp.zeros(vt_s.shape, BF16)

    if windowed:
        @pl.when((i >= 0) & (ci == 0))
        def _load_ctx():
            ck_s[...] = _tiled_keys(ck_ref[0])
            cvt_s[...] = _values_t(cv_ref[0])

    def normalize(tokens):
        yield from _normalize_items(h_w, xp_ref, g_ref, modp_ref, tokens)

    def project(tokens):
        o_bg, o_cg, o_xs, o_ga, o_q, o_k, _, o_gb = EVEN_OFFS[:8]

        def proj(start):
            return jnp.dot(h_s[...], win_ref[:, start:start + NT], preferred_element_type=F32)

        lane = lax.broadcasted_iota(jnp.int32, (1, LANES), 1)
        first_half = (lane & (2 * ROPE_FREQS - 1)) < ROPE_FREQS
        ri = lax.broadcasted_iota(jnp.int32, (NT, NT), 0) >> HEAD_SHIFT
        cj = lax.broadcasted_iota(jnp.int32, (NT, NT), 1) >> HEAD_SHIFT
        bd = (ri == cj).astype(BF16)

        def qk_norm(x, g):
            w = x.shape[1]
            inv = _head_inv_rms(x, bd[0:w, 0:w])
            pieces = []
            for j in range(w // LANES):
                lanes = slice(j * LANES, (j + 1) * LANES)
                xj = x[:, lanes] * inv[:, lanes] * g
                pieces.append(_rope(xj, cos_ref[...], sin_ref[...], first_half) if windowed else xj)
            return pieces

        tile = _mxu_cost(TM, D_MODEL, NT)

        def keys_values():
            kv = proj(o_k)
            k, = qk_norm(kv[:, 0:KV_WIDTH], kg_ref[...])
            v = kv[:, KV_WIDTH:2 * KV_WIDTH]
            if not windowed:
                ko_ref[0] = k
                vo_ref[0] = v
            k_w[...] = _tiled_keys(k)
            vt_w[...] = _values_t(v)
            return v

        def queries(n):
            for j, qj in enumerate(qk_norm(proj(o_q + n * NT), qg_ref[...])):
                q_w[:, n * NT + j * LANES:n * NT + (j + 1) * LANES] = (qj * (ATTN_SCALE * LOG2E)).astype(BF16)
            return qj

        def conv_gate(n):
            gate = proj(o_bg + n * NT) * _silu(proj(o_ga + n * NT))
            gate_w[:, n * NT:(n + 1) * NT] = gate
            return gate

        def conv_input(n):
            u = proj(o_cg + n * NT) * proj(o_xs + n * NT)
            u_w[:, n * NT:(n + 1) * NT] = u
            return u

        def attn_gate(n):
            gb = _silu(proj(o_gb + n * NT))
            gb_w[:, n * NT:(n + 1) * NT] = gb
            return gb

        items = [((2 * tile, 180), conv_gate, 0), ((tile, 350), keys_values, None), ((2 * tile, 40), conv_input, 0),
                 ((tile, 300), queries, 0), ((2 * tile, 180), conv_gate, 1), ((tile, 300), queries, 1),
                 ((2 * tile, 40), conv_input, 1), ((tile, 110), attn_gate, 0), ((tile, 110), attn_gate, 1)]
        for cost, emit, n in items:
            yield cost
            emit() if n is None else emit(n)

    def finish(slot, tokens):
        if windowed:
            prev_slot = lax.rem(slot + (ring - 1), jnp.int32(ring))
            next_slot = lax.rem(slot + 1, jnp.int32(ring))
            has_prev = ci > 0
            has_next = ci < nc - 1
        else:
            prev_slot = next_slot = has_prev = has_next = None

        yield 0, 400
        ext = _with_halo(u_s, slot, slice(None), prev_slot, next_slot, has_prev, has_next)
        conv = (_rows_down(ext, 1)[HALO:HALO + TM] * cw_ref[0:1, :]
                + ext[HALO:HALO + TM] * cw_ref[1:2, :]
                + _rows_up(ext, 1)[HALO:HALO + TM] * cw_ref[2:3, :]
                + cb_ref[...])
        a_s[:, 0:CONV_WIDTH] = (gate_s[slot] * conv).astype(BF16)

        lane_head = lax.broadcasted_iota(jnp.int32, (1, GROUP_W), 1) >> HEAD_SHIFT
        cols4 = Q_PER_KV * BLOCK
        n_keys = 3 * BLOCK + ck_s.shape[0] if windowed else TM
        key_i =lax.broadcasted_iota(jnp.int32, (BLOCK, cols4), 0)
        qry_i = lax.broadcasted_iota(jnp.int32, (BLOCK, cols4), 1) & (BLOCK - 1)
        upper = lax.broadcasted_iota(jnp.int32, (LANES, LANES), 0) < HEAD_DIM
        halves = (slice(0, BLOCK), slice(BLOCK, TM))
        for qb in range(TM // BLOCK):
            own = halves[qb]
            if windowed:
                if qb == 0:
                    blocks = ((prev_slot, halves[1]), (slot, halves[0]), (slot, halves[1]))
                    prev_in, next_in = has_prev, True
                else:
                    blocks = ((slot, halves[0]), (slot, halves[1]), (next_slot, halves[0]))
                    prev_in, next_in = True, has_next
                prev_ok = (key_i - qry_i) >= jnp.where(prev_in, 0, BLOCK)
                next_ok = (qry_i - key_i) >= jnp.where(next_in, 0, BLOCK)
            for kv in range(N_KV_HEADS):
                cols = slice(kv * GROUP_W, (kv + 1) * GROUP_W)
                vrows = slice(kv * LANES, (kv + 1) * LANES)
                yield _mxu_cost(n_keys, GROUP_W, cols4), 60
                qblk = q_s[slot, own, cols]
                q4 = jnp.concatenate(
                    [jnp.where(lane_head == g, qblk, jnp.zeros_like(qblk)) for g in range(Q_PER_KV)], axis=0)

                def scores_t(kk):
                    return lax.dot_general(kk, q4, (((1,), (1,)), ((), ())), preferred_element_type=F32)

                if windowed:
                    s_loc = scores_t(jnp.concatenate([k_s[sl, r, cols] for sl, r in blocks], axis=0))
                    vt_loc = jnp.concatenate([vt_s[sl, vrows, r] for sl, r in blocks], axis=1)
                    s_ctx = scores_t(ck_s[:, cols])
                    yield 0, n_keys * cols4 * 5 // (8 * LANES * 4)
                    groups = [([jnp.where(prev_ok, s_loc[0:BLOCK], NEG), s_loc[BLOCK:2 * BLOCK],
                                jnp.where(next_ok, s_loc[2 * BLOCK:3 * BLOCK], NEG)], vt_loc),
                              ([s_ctx], cvt_s[vrows, :])]
                else:
                    s_own = scores_t(k_s[slot, :, cols])
                    yield 0, n_keys * cols4 * 5 // (8 * LANES * 4)
                    groups = [([s_own], vt_s[slot, vrows, :])]
                snk = jnp.concatenate(
                    [jnp.full((1, BLOCK), sink_ref[kv * Q_PER_KV + g] * LOG2E, F32) for g in range(Q_PER_KV)], axis=1)
                if tokens:
                    token = tokens.pop(0)
                    while tokens:
                        token = token + tokens.pop(0)
                    snk = snk + jnp.concatenate([token] * Q_PER_KV, axis=1)
                mx = snk
                for ss, _ in groups:
                    for sc in ss:
                        mx = jnp.maximum(mx, jnp.max(sc, axis=0, keepdims=True))
                den = jnp.exp2(snk - mx)
                probs = []
                for ss, vt in groups:
                    es = []
                    for sc in ss:
                        e = jnp.exp2(sc - mx)
                        den = den + jnp.sum(e, axis=0, keepdims=True)
                        es.append(e.astype(BF16))
                    probs.append((es[0] if len(es) == 1 else jnp.concatenate(es, axis=0), vt))
                yield _mxu_cost(LANES, n_keys, cols4), 120
                o_t = jnp.zeros((LANES, cols4), F32)
                for e_all, vt in probs:
                    o_t = o_t + jnp.dot(vt, e_all, preferred_element_type=F32)
                o_t = o_t * (1.0 / den)
                o01 = jnp.where(upper, o_t[:, 0:BLOCK], o_t[:, BLOCK:2 * BLOCK]).T
                o23 = jnp.where(upper, o_t[:, 2 * BLOCK:3 * BLOCK], o_t[:, 3 * BLOCK:4 * BLOCK]).T
                a_s[own, CONV_WIDTH + kv * GROUP_W:CONV_WIDTH + (kv + 1) * GROUP_W] = (
                    jnp.concatenate([o01, o23], axis=1) * gb_s[slot, own, cols]).astype(BF16)

        yield from _out_proj_items(lambda: a_s[...], wout_ref, xa_ref, moda_ref, y_ref)

    staged = ((u_w, u_s), (gate_w, gate_s), (q_w, q_s), (k_w, k_s), (vt_w, vt_s), (gb_w, gb_s))
    _pipeline_steps(s, n_chunks, lag, ring, (h_w, h_s), staged, EVEN_ORDER, normalize, project, finish)


def _chunk_specs(n_chunks, nc, lag, per_seq_mod):
    last = n_chunks - 1
    norm_chunk = lambda s: jnp.minimum(s, last)
    proj_chunk = lambda s: jnp.clip(s - 1, 0, last)
    fin_chunk = lambda s: jnp.clip(s - 1 - lag, 0, last)
    seq_of = (lambda j: j // nc) if per_seq_mod else (lambda j: 0)
    return dict(
        x_norm=pl.BlockSpec((1, TM, D_MODEL), lambda s: (norm_chunk(s), 0, 0)),
        x_fin=pl.BlockSpec((1, TM, D_MODEL), lambda s: (fin_chunk(s), 0, 0)),
        mod_norm=pl.BlockSpec((1, 1, 3 * D_MODEL), lambda s: (seq_of(norm_chunk(s)), 0, 0)),
        mod_fin=pl.BlockSpec((1, 1, 3 * D_MODEL), lambda s: (seq_of(fin_chunk(s)), 0, 0)),
        proj_chunk=proj_chunk, fin_chunk=fin_chunk)


def _const_spec(shape):
    return pl.BlockSpec(shape, lambda c: (0,) * len(shape))


def _even_layer(x, mod, norm_g, w_in, conv_w, conv_b, q_g, k_g, sink, w_out, rope=None, ctx=None):
    n_seq, seq_len, _ = x.shape
    windowed = rope is not None
    nc = seq_len // TM
    n_chunks = n_seq * nc
    lag, ring = (2, 4) if windowed else (1, 2)
    assert windowed or nc == 1
    sp = _chunk_specs(n_chunks, nc, lag, windowed)
    xc = x.reshape(n_chunks, TM, D_MODEL)
    in_specs = [
        sp["x_norm"], sp["x_fin"], sp["mod_norm"], sp["mod_fin"],
        _const_spec((1, D_MODEL)),
        _const_spec((D_MODEL, EVEN_IN)),
        _const_spec((3, CONV_WIDTH)),
        _const_spec((1, CONV_WIDTH)),
        _const_spec((1, LANES)),
        _const_spec((1, LANES)),
        pl.BlockSpec(memory_space=pltpu.SMEM),
        _const_spec((D_MODEL, D_MODEL)),
    ]
    args = [xc, xc, mod, mod, norm_g.reshape(1, D_MODEL), w_in, conv_w, conv_b.reshape(1, CONV_WIDTH),
            jnp.tile(q_g, LANES // HEAD_DIM).reshape(1, LANES), jnp.tile(k_g, LANES // HEAD_DIM).reshape(1, LANES),
            sink, w_out]
    per_chunk = [
        ((TM, CONV_WIDTH), F32),
        ((TM, CONV_WIDTH), F32),
        ((TM, ATTN_WIDTH), BF16),
        ((TM, 2 * GROUP_W), BF16),
        ((2 * LANES, TM), BF16),
        ((TM, ATTN_WIDTH), F32),
    ]
    scratch = (
        [pltpu.VMEM((TM, D_MODEL), BF16)] * 2
        + [pltpu.VMEM((ring,) + shape, dt) for shape, dt in per_chunk]
        + [pltpu.VMEM(shape, dt) for shape, dt in per_chunk]
        + [pltpu.VMEM((TM, D_MODEL), BF16)])
    y_shape = jax.ShapeDtypeStruct(xc.shape, F32)
    y_spec = pl.BlockSpec((1, TM, D_MODEL), lambda c: (sp["fin_chunk"](c), 0, 0))
    if windowed:
        cos, sin = rope
        ck, cv = ctx
        rope_spec = pl.BlockSpec((TM, LANES), lambda c: (sp["proj_chunk"](c) % nc, 0))
        ctx_spec = pl.BlockSpec((1,) + ck.shape[1:], lambda c: (sp["fin_chunk"](c) // nc, 0, 0))
        in_specs += [rope_spec, rope_spec, ctx_spec, ctx_spec]
        args += [cos, sin, ck, cv]
        scratch += [pltpu.VMEM((ck.shape[1], 2 * GROUP_W), BF16), pltpu.VMEM((2 * LANES, cv.shape[1]), BF16)]
        out_shape, out_specs = y_shape, y_spec
    else:
        kv_shape = jax.ShapeDtypeStruct((n_chunks, TM, KV_WIDTH), F32)
        kv_spec = pl.BlockSpec((1, TM, KV_WIDTH), lambda c: (sp["proj_chunk"](c), 0, 0))
        out_shape, out_specs = (y_shape, kv_shape, kv_shape), (y_spec, kv_spec, kv_spec)
    out = pl.pallas_call(
        functools.partial(_even_kernel, nc=nc, n_chunks=n_chunks, lag=lag, ring=ring, windowed=windowed),
        out_shape=out_shape,
        grid=(n_chunks + lag + 1,),
        in_specs=in_specs,
        out_specs=out_specs,
        scratch_shapes=scratch,
        compiler_params=PIPELINE_PARAMS,
        name="even_latent" if windowed else "even_context",
    )(*args)
    if windowed:
        return out.reshape(x.shape)
    y, k, v = out
    return y.reshape(x.shape), k.reshape(n_seq, seq_len, KV_WIDTH), v.reshape(n_seq, seq_len, KV_WIDTH)


def _odd_kernel(xp_ref, xa_ref, modp_ref, moda_ref, g_ref, win_ref, pw_ref, ps_ref, wout_ref, y_ref,
                h_s, h_w, u_s, sg_s, u_w, sg_w, *, nc, n_chunks, lag, ring):
    s = pl.program_id(0)
    i = s - 1 - lag
    ci = lax.rem(i, jnp.int32(nc))

    def normalize(tokens):
        yield from _normalize_items(h_w, xp_ref, g_ref, modp_ref, tokens)

    def project(tokens):
        tile = _mxu_cost(TM, D_MODEL, NT)
        for n in range(D_MODEL // NT):
            yield tile, 10
            cols = slice(n * NT, (n + 1) * NT)
            u = jnp.dot(h_s[...], win_ref[:, cols], preferred_element_type=F32)
            u_w[:, cols] = u
        for n in range(D_MODEL // NT):
            yield tile, 110
            cols = slice(n * NT, (n + 1) * NT)
            sg = _silu(jnp.dot(h_s[...], win_ref[:, D_MODEL + n * NT:D_MODEL + (n + 1) * NT],
                               preferred_element_type=F32))
            sg_w[:, cols] = sg

    def finish(slot, tokens):
        if nc > 1:
            prev_slot = lax.rem(slot + (ring - 1), jnp.int32(ring))
            next_slot = lax.rem(slot + 1, jnp.int32(ring))
            has_prev = ci > 0
            has_next = ci < nc - 1
        else:
            prev_slot = next_slot = has_prev = has_next = None
        t = ci * TM + lax.broadcasted_iota(jnp.int32, (TM, 1), 0)
        outs = []
        for gi, w in enumerate(POOL_SIZES):
            yield _mxu_cost(TM, POOL_GROUP, POOL_GROUP), 140 + 120 * gi
            cols = slice(gi * POOL_GROUP, (gi + 1) * POOL_GROUP)
            ext = _with_halo(u_s, slot, cols, prev_slot, next_slot, has_prev, has_next)
            acc = ext
            span = 1
            while span < w:
                acc = acc + _rows_down(acc, span)
                span *= 2
            if w > 2:
                acc = _rows_up(acc, w // 2 - 1)
            cnt = (jnp.minimum(t + w // 2, nc * TM) - jnp.maximum(t - w // 2, 0)).astype(F32)
            while tokens:
                cnt = cnt + tokens.pop(0)[:, 0:1]
            pooled = acc[HALO:HALO + TM] / cnt - ext[HALO:HALO + TM]
            outs.append(jnp.dot(pooled.astype(BF16), pw_ref[gi], preferred_element_type=F32))
        yield 0, 130
        y = (jnp.concatenate(outs, axis=1) * ps_ref[...] * sg_s[slot]).astype(BF16)
        yield from _out_proj_items(lambda: y, wout_ref, xa_ref, moda_ref, y_ref)

    _pipeline_steps(s, n_chunks, lag, ring, (h_w, h_s), ((u_w, u_s), (sg_w, sg_s)), ODD_ORDER,
                    normalize, project, finish)


def _odd_layer(x, mod, per_seq_mod, norm_g, w_in, pool_w, pool_scale, w_out):
    n_seq, seq_len, _ = x.shape
    nc = seq_len // TM
    n_chunks = n_seq * nc
    lag, ring = (2, 4) if nc > 1 else (1, 2)
    sp = _chunk_specs(n_chunks, nc, lag, per_seq_mod)
    xc = x.reshape(n_chunks, TM, D_MODEL)
    in_specs = [
        sp["x_norm"], sp["x_fin"], sp["mod_norm"], sp["mod_fin"],
        _const_spec((1, D_MODEL)),
        _const_spec((D_MODEL, 2 * D_MODEL)),
        _const_spec(pool_w.shape),
        _const_spec((1, D_MODEL)),
        _const_spec((D_MODEL, D_MODEL)),
    ]
    out = pl.pallas_call(
        functools.partial(_odd_kernel, nc=nc, n_chunks=n_chunks, lag=lag, ring=ring),
        out_shape=jax.ShapeDtypeStruct(xc.shape, F32),
        grid=(n_chunks + lag + 1,),
        in_specs=in_specs,
        out_specs=pl.BlockSpec((1, TM, D_MODEL), lambda c: (sp["fin_chunk"](c), 0, 0)),
        scratch_shapes=[
            pltpu.VMEM((TM, D_MODEL), BF16),
            pltpu.VMEM((TM, D_MODEL), BF16),
            pltpu.VMEM((ring, TM, D_MODEL), F32),
            pltpu.VMEM((ring, TM, D_MODEL), F32),
            pltpu.VMEM((TM, D_MODEL), F32),
            pltpu.VMEM((TM, D_MODEL), F32),
        ],
        compiler_params=PIPELINE_PARAMS,
        name="odd_latent" if per_seq_mod else "odd_context",
    )(xc, xc, mod, mod, norm_g.reshape(1, D_MODEL), w_in, pool_w, pool_scale.reshape(1, D_MODEL), w_out)
    return out.reshape(x.shape)


def _rope_tables(seq_len):
    n_rows = seq_len // GRID_W
    row = np.repeat(np.arange(n_rows), GRID_W).astype(np.float32)
    col = np.tile(np.arange(GRID_W), n_rows).astype(np.float32)
    inv = (np.float32(ROPE_BASE) ** (-np.arange(ROPE_FREQS, dtype=np.float32) / np.float32(ROPE_FREQS))).astype(np.float32)
    ang = np.stack([row[:, None] * inv, col[:, None] * inv], axis=1)
    cos = np.broadcast_to(np.cos(ang)[:, :, None, :], (seq_len, 2, 2, ROPE_FREQS))
    sin = np.sin(ang)[:, :, None, :] * np.array([-1.0, 1.0], np.float32)[None, None, :, None]
    cos = cos.reshape(seq_len, HEAD_DIM).astype(np.float32)
    sin = sin.reshape(seq_len, HEAD_DIM).astype(np.float32)
    return jnp.asarray(np.tile(cos, (1, LANES // HEAD_DIM))), jnp.asarray(np.tile(sin, (1, LANES // HEAD_DIM)))


def kernel(x_prompt, x_sample, cache_k, cache_v, c, c_ctx, ada_w_e, ada_b_e, norm_g_e, w_in_e, conv_w, conv_b,
           q_norm_g, k_norm_g, sink, w_out_e, ada_w_o, ada_b_o, norm_g_o, w_in_o, pool_w, pool_scale, w_out_o):
    n_dec = x_sample.shape[0]
    depth = ada_w_e.shape[0] + ada_w_o.shape[0]
    assert ada_w_e.shape[0] == 1 and ada_w_o.shape[0] == 1 and n_dec + 1 <= 8
    cond8 = jnp.concatenate([c_ctx[None], c, jnp.zeros((8 - 1 - n_dec, D_MODEL), F32)], axis=0)
    mod_e, mod_o = _adaln(cond8, ada_w_e[0], ada_b_e[0], ada_w_o[0], ada_b_o[0])
    rope = _rope_tables(x_sample.shape[1])

    yp, ys = x_prompt, x_sample
    new_k, new_v = [], []
    for layer in range(depth):
        i = layer // 2
        if layer % 2 == 0:
            mod = mod_e.reshape(8, 1, 3 * D_MODEL)
            w_in = w_in_e[i].astype(BF16)
            w_out = w_out_e[i].astype(BF16)
            common = (norm_g_e[i], w_in, conv_w[i], conv_b[i], q_norm_g[i], k_norm_g[i], sink[i], w_out)
            yp, k, v = _even_layer(yp, mod[0:1], *common)
            new_k.append(k.reshape(k.shape[0], k.shape[1], N_KV_HEADS, HEAD_DIM))
            new_v.append(v.reshape(v.shape[0], v.shape[1], N_KV_HEADS, HEAD_DIM))
            ck = cache_k[:, i].reshape(n_dec, cache_k.shape[2], KV_WIDTH)
            cv = cache_v[:, i].reshape(n_dec, cache_v.shape[2], KV_WIDTH)
            ys = _even_layer(ys, mod[1:1 + n_dec], *common, rope=rope, ctx=(ck, cv))
        else:
            mod = mod_o.reshape(8, 1, 3 * D_MODEL)
            common = (norm_g_o[i], w_in_o[i].astype(BF16), pool_w[i].astype(BF16), pool_scale[i],
                      w_out_o[i].astype(BF16))
            yp = _odd_layer(yp, mod[0:1], False, *common)
            ys = _odd_layer(ys, mod[1:1 + n_dec], True, *common)
    return yp, ys, jnp.stack(new_k, axis=1), jnp.stack(new_v, axis=1)
```

```python
import functools

import jax
import jax.numpy as jnp
import numpy as np
from jax import lax
from jax.experimental import pallas as pl
from jax.experimental.pallas import tpu as pltpu

F32 = jnp.float32
BF16 = jnp.bfloat16

D_MODEL = 1024
GRID_W = 64
HEAD_DIM = 64
HEAD_SHIFT = 6
ATTN_WIDTH = D_MODEL // 2
N_Q_HEADS = ATTN_WIDTH // HEAD_DIM
N_KV_HEADS = N_Q_HEADS // 4
Q_PER_KV = N_Q_HEADS // N_KV_HEADS
KV_WIDTH = N_KV_HEADS * HEAD_DIM
CONV_WIDTH = D_MODEL - ATTN_WIDTH
WINDOW = 128
BLOCK = 128
ROPE_BASE = 10000.0
ROPE_FREQS = HEAD_DIM // 4
ATTN_SCALE = HEAD_DIM ** -0.5
LOG2E = float(np.log2(np.e))
NEG = -1e30
POOL_SIZES = (2, 4, 8, 16)
POOL_GROUP = D_MODEL // len(POOL_SIZES)
EPS = 1e-6
EVEN_SIZES = (CONV_WIDTH, CONV_WIDTH, CONV_WIDTH, CONV_WIDTH, ATTN_WIDTH, KV_WIDTH, KV_WIDTH, ATTN_WIDTH)
EVEN_IN = sum(EVEN_SIZES)
EVEN_OFFS = tuple(int(s) for s in np.cumsum((0,) + EVEN_SIZES))

TM = 256
NT = 256
NORM_ROWS = 64
EVEN_ORDER = "PF" + "FNPFPF" * 4 + "FFFF"
ODD_ORDER = "PNFP" * 4 + "FFFFF"
LANES = 128
GROUP_W = Q_PER_KV * HEAD_DIM
ONES_ROWS = 16
HALO = 8
ADA_TN = 768
VMEM_LIMIT = 56 * 1024 * 1024
PIPELINE_PARAMS = pltpu.CompilerParams(dimension_semantics=("arbitrary",), vmem_limit_bytes=VMEM_LIMIT)


def _silu(x):
    return x / (1.0 + jnp.exp(-x))


def _rows_down(x, k):
    return pltpu.roll(x, k, axis=0)


def _rows_up(x, k):
    return pltpu.roll(x, x.shape[0] - k, axis=0)


def _rms_modulate(x, g, shift, scale):
    ms = jnp.mean(x * x, axis=-1, keepdims=True)
    return (x * lax.rsqrt(ms + EPS) * g) * (1.0 + scale) + shift


def _opaque_zero(x):
    bits = pltpu.bitcast(x, jnp.uint32)
    return pltpu.bitcast((bits >> 16) >> 16, F32)


def _normalize_items(h_s, x_ref, g_ref, mod_ref, tokens):
    shift = mod_ref[0, :, 0:D_MODEL]
    scale = mod_ref[0, :, D_MODEL:2 * D_MODEL]
    for r in range(0, TM, NORM_ROWS):
        yield 0, 170
        h = _rms_modulate(x_ref[0, r:r + NORM_ROWS], g_ref[...], shift, scale)
        h_s[r:r + NORM_ROWS] = h.astype(BF16)
        tokens.append(_opaque_zero(h[0:8, 0:LANES])[0:1])


def _out_proj_items(get_a, wout_ref, x_ref, mod_ref, y_ref):
    for n in range(D_MODEL // NT):
        yield _mxu_cost(TM, D_MODEL, NT), 70
        cols = slice(n * NT, (n + 1) * NT)
        mixed = jnp.dot(get_a(), wout_ref[:, cols], preferred_element_type=F32)
        y_ref[0, :, cols] = x_ref[0, :, cols] + mod_ref[0, :, 2 * D_MODEL + n * NT:2 * D_MODEL + (n + 1) * NT] * mixed


def _mxu_cost(m, k, n):
    tiles = -(-k // NT) * -(-n // NT)
    return tiles * max(m // 2, 64) // 2


def _weave(order, **stages):
    by_letter = {name[0].upper(): stage for name, stage in stages.items()}
    for stage in by_letter.values():
        next(stage, None)
    for letter in order:
        if letter in by_letter:
            next(by_letter[letter], None)
    for stage in by_letter.values():
        for _ in stage:
            pass


def _with_halo(u_s, slot, cols, prev_slot, next_slot, has_prev, has_next):
    u = u_s[slot, :, cols]
    zero = jnp.zeros((HALO, u.shape[1]), F32)
    if prev_slot is None:
        top = bot = zero
    else:
        top = jnp.where(has_prev, u_s[prev_slot, TM - HALO:TM, cols], zero)
        bot = jnp.where(has_next, u_s[next_slot, 0:HALO, cols], zero)
    return jnp.concatenate([top, u, bot], axis=0)


def _pipeline_steps(s, n_chunks, lag, ring, normalized, staged, order, normalize, project, finish):
    fin_slot = lax.rem(s + (2 * ring - 1 - lag), jnp.int32(ring))

    @pl.when((s >= 1) & (s <= n_chunks))
    def _publish_normalized():
        h_w, h_s = normalized
        h_s[...] = h_w[...]

    @pl.when((s >= 2) & (s <= n_chunks + 1))
    def _publish_projected():
        slot = lax.rem(s + (ring - 2), jnp.int32(ring))
        for stage_ref, ring_ref in staged:
            ring_ref[slot] = stage_ref[...]

    @pl.when(s == 0)
    def _first():
        _weave(order, normalize=normalize([]))

    @pl.when((s >= 1) & (s <= lag))
    def _fill():
        _weave(order, project=project([]), normalize=normalize([]))

    @pl.when((s > lag) & (s <= n_chunks))
    def _steady():
        tokens = []
        _weave(order, finish=finish(fin_slot, tokens), project=project(tokens), normalize=normalize(tokens))

    @pl.when(s > n_chunks)
    def _drain():
        _weave(order, finish=finish(fin_slot, []))


def _adaln_kernel(cond_ref, we_ref, be_ref, wo_ref, bo_ref, oe_ref, oo_ref):
    s = _silu(cond_ref[...]).astype(BF16)
    oe_ref[...] = jnp.dot(s, we_ref[...].astype(BF16), preferred_element_type=F32) + be_ref[...]
    oo_ref[...] = jnp.dot(s, wo_ref[...].astype(BF16), preferred_element_type=F32) + bo_ref[...]


def _adaln(cond8, w_e, b_e, w_o, b_o):
    n = 3 * D_MODEL
    wspec = pl.BlockSpec((D_MODEL, ADA_TN), lambda j: (0, j))
    vspec = pl.BlockSpec((1, ADA_TN), lambda j: (0, j))
    ospec = pl.BlockSpec((8, ADA_TN), lambda j: (0, j))
    return pl.pallas_call(
        _adaln_kernel,
        out_shape=(jax.ShapeDtypeStruct((8, n), F32), jax.ShapeDtypeStruct((8, n), F32)),
        grid=(n // ADA_TN,),
        in_specs=[pl.BlockSpec((8, D_MODEL), lambda j: (0, 0)), wspec, vspec, wspec, vspec],
        out_specs=(ospec, ospec),
        compiler_params=pltpu.CompilerParams(dimension_semantics=("arbitrary",), vmem_limit_bytes=VMEM_LIMIT),
        name="adaln",
    )(cond8, w_e, b_e.reshape(1, n), w_o, b_o.reshape(1, n))


def _head_inv_rms(x, bd):
    sq = x * x
    hi = sq.astype(BF16)
    lo = (sq - hi.astype(F32)).astype(BF16)
    ms = (jnp.dot(hi, bd, preferred_element_type=F32) + jnp.dot(lo, bd, preferred_element_type=F32)) * (1.0 / HEAD_DIM)
    return lax.rsqrt(ms + EPS)


def _rope(x, cos, sin_signed, first_half):
    partner = jnp.where(first_half, pltpu.roll(x, LANES - ROPE_FREQS, axis=1), pltpu.roll(x, ROPE_FREQS, axis=1))
    return x * cos + partner * sin_signed


def _dup_kv(x):
    lane = lax.broadcasted_iota(jnp.int32, (1, LANES), 1)
    swapped = pltpu.roll(x, HEAD_DIM, axis=1)
    return jnp.where(lane < HEAD_DIM, x, swapped), jnp.where(lane < HEAD_DIM, swapped, x)


def _tiled_keys(k):
    ka, kb = _dup_kv(k)
    return jnp.concatenate([ka, ka, kb, kb], axis=1).astype(BF16)


def _values_t(v):
    va, vb = _dup_kv(v)
    return jnp.concatenate([va.T, vb.T], axis=0).astype(BF16)


def _even_kernel(*refs, nc, n_chunks, lag, ring, windowed):
    if windowed:
        (xp_ref, xa_ref, modp_ref, moda_ref, g_ref, win_ref, cw_ref, cb_ref, qg_ref, kg_ref, sink_ref, wout_ref,
         cos_ref, sin_ref, ck_ref, cv_ref,
         y_ref,
         h_s, h_w, u_s, gate_s, q_s, k_s, vt_s, gb_s, u_w, gate_w, q_w, k_w, vt_w, gb_w, a_s, ck_s, cvt_s) = refs
        ko_ref = vo_ref = None
    else:
        (xp_ref, xa_ref, modp_ref, moda_ref, g_ref, win_ref, cw_ref, cb_ref, qg_ref, kg_ref, sink_ref, wout_ref,
         y_ref, ko_ref, vo_ref,
         h_s, h_w, u_s, gate_s, q_s, k_s, vt_s, gb_s, u_w, gate_w, q_w, k_w, vt_w, gb_w, a_s) = refs
    s = pl.program_id(0)
    i = s - 1 - lag
    ci = lax.rem(i, jnp.int32(nc))

    @pl.when(s == 0)
    def _init():
        k_s[...] = jnp.zeros(k_s.shape, BF16)
        vt_s[...] = jnp.zeros(vt_s.shape, BF16)

    if windowed:
        @pl.when((i >= 0) & (ci == 0))
        def _load_ctx():
            ck_s[...] = _tiled_keys(ck_ref[0])
            cvt_s[...] = _values_t(cv_ref[0])

    def normalize(tokens):
        yield from _normalize_items(h_w, xp_ref, g_ref, modp_ref, tokens)

    def project(tokens):
        o_bg, o_cg, o_xs, o_ga, o_q, o_k, _, o_gb = EVEN_OFFS[:8]

        def proj(start):
            return jnp.dot(h_s[...], win_ref[:, start:start + NT], preferred_element_type=F32)

        lane = lax.broadcasted_iota(jnp.int32, (1, LANES), 1)
        first_half = (lane & (2 * ROPE_FREQS - 1)) < ROPE_FREQS
        ri = lax.broadcasted_iota(jnp.int32, (NT, NT), 0) >> HEAD_SHIFT
        cj = lax.broadcasted_iota(jnp.int32, (NT, NT), 1) >> HEAD_SHIFT
        bd = (ri == cj).astype(BF16)

        def qk_norm(x, g):
            w = x.shape[1]
            inv = _head_inv_rms(x, bd[0:w, 0:w])
            pieces = []
            for j in range(w // LANES):
                lanes = slice(j * LANES, (j + 1) * LANES)
                xj = x[:, lanes] * inv[:, lanes] * g
                pieces.append(_rope(xj, cos_ref[...], sin_ref[...], first_half) if windowed else xj)
            return pieces

        tile = _mxu_cost(TM, D_MODEL, NT)

        def keys_values():
            kv = proj(o_k)
            k, = qk_norm(kv[:, 0:KV_WIDTH], kg_ref[...])
            v = kv[:, KV_WIDTH:2 * KV_WIDTH]
            if not windowed:
                ko_ref[0] = k
                vo_ref[0] = v
            k_w[...] = _tiled_keys(k)
            vt_w[...] = _values_t(v)
            return v

        def queries(n):
            for j, qj in enumerate(qk_norm(proj(o_q + n * NT), qg_ref[...])):
                q_w[:, n * NT + j * LANES:n * NT + (j + 1) * LANES] = (qj * (ATTN_SCALE * LOG2E)).astype(BF16)
            return qj

        def conv_gate(n):
            gate = proj(o_bg + n * NT) * _silu(proj(o_ga + n * NT))
            gate_w[:, n * NT:(n + 1) * NT] = gate
            return gate

        def conv_input(n):
            u = proj(o_cg + n * NT) * proj(o_xs + n * NT)
            u_w[:, n * NT:(n + 1) * NT] = u
            return u

        def attn_gate(n):
            gb = _silu(proj(o_gb + n * NT))
            gb_w[:, n * NT:(n + 1) * NT] = gb
            return gb

        items = [((2 * tile, 180), conv_gate, 0), ((tile, 350), keys_values, None), ((2 * tile, 40), conv_input, 0),
                 ((tile, 300), queries, 0), ((2 * tile, 180), conv_gate, 1), ((tile, 300), queries, 1),
                 ((2 * tile, 40), conv_input, 1), ((tile, 110), attn_gate, 0), ((tile, 110), attn_gate, 1)]
        for cost, emit, n in items:
            yield cost
            emit() if n is None else emit(n)

    def finish(slot, tokens):
        if windowed:
            prev_slot = lax.rem(slot + (ring - 1), jnp.int32(ring))
            next_slot = lax.rem(slot + 1, jnp.int32(ring))
            has_prev = ci > 0
            has_next = ci < nc - 1
        else:
            prev_slot = next_slot = has_prev = has_next = None

        yield 0, 400
        ext = _with_halo(u_s, slot, slice(None), prev_slot, next_slot, has_prev, has_next)
        conv = (_rows_down(ext, 1)[HALO:HALO + TM] * cw_ref[0:1, :]
                + ext[HALO:HALO + TM] * cw_ref[1:2, :]
                + _rows_up(ext, 1)[HALO:HALO + TM] * cw_ref[2:3, :]
                + cb_ref[...])
        a_s[:, 0:CONV_WIDTH] = (gate_s[slot] * conv).astype(BF16)

        lane_head = lax.broadcasted_iota(jnp.int32, (1, GROUP_W), 1) >> HEAD_SHIFT
        cols4 = Q_PER_KV * BLOCK
        n_keys = 3 * BLOCK + ck_s.shape[0] if windowed else TM
        key_i =lax.broadcasted_iota(jnp.int32, (BLOCK, cols4), 0)
        qry_i = lax.broadcasted_iota(jnp.int32, (BLOCK, cols4), 1) & (BLOCK - 1)
        upper = lax.broadcasted_iota(jnp.int32, (LANES, LANES), 0) < HEAD_DIM
        halves = (slice(0, BLOCK), slice(BLOCK, TM))
        for qb in range(TM // BLOCK):
            own = halves[qb]
            if windowed:
                if qb == 0:
                    blocks = ((prev_slot, halves[1]), (slot, halves[0]), (slot, halves[1]))
                    prev_in, next_in = has_prev, True
                else:
                    blocks = ((slot, halves[0]), (slot, halves[1]), (next_slot, halves[0]))
                    prev_in, next_in = True, has_next
                prev_ok = (key_i - qry_i) >= jnp.where(prev_in, 0, BLOCK)
                next_ok = (qry_i - key_i) >= jnp.where(next_in, 0, BLOCK)
            for kv in range(N_KV_HEADS):
                cols = slice(kv * GROUP_W, (kv + 1) * GROUP_W)
                vrows = slice(kv * LANES, (kv + 1) * LANES)
                yield _mxu_cost(n_keys, GROUP_W, cols4), 60
                qblk = q_s[slot, own, cols]
                q4 = jnp.concatenate(
                    [jnp.where(lane_head == g, qblk, jnp.zeros_like(qblk)) for g in range(Q_PER_KV)], axis=0)

                def scores_t(kk):
                    return lax.dot_general(kk, q4, (((1,), (1,)), ((), ())), preferred_element_type=F32)

                if windowed:
                    s_loc = scores_t(jnp.concatenate([k_s[sl, r, cols] for sl, r in blocks], axis=0))
                    vt_loc = jnp.concatenate([vt_s[sl, vrows, r] for sl, r in blocks], axis=1)
                    s_ctx = scores_t(ck_s[:, cols])
                    yield 0, n_keys * cols4 * 5 // (8 * LANES * 4)
                    groups = [([jnp.where(prev_ok, s_loc[0:BLOCK], NEG), s_loc[BLOCK:2 * BLOCK],
                                jnp.where(next_ok, s_loc[2 * BLOCK:3 * BLOCK], NEG)], vt_loc),
                              ([s_ctx], cvt_s[vrows, :])]
                else:
                    s_own = scores_t(k_s[slot, :, cols])
                    yield 0, n_keys * cols4 * 5 // (8 * LANES * 4)
                    groups = [([s_own], vt_s[slot, vrows, :])]
                snk = jnp.concatenate(
                    [jnp.full((1, BLOCK), sink_ref[kv * Q_PER_KV + g] * LOG2E, F32) for g in range(Q_PER_KV)], axis=1)
                if tokens:
                    token = tokens.pop(0)
                    while tokens:
                        token = token + tokens.pop(0)
                    snk = snk + jnp.concatenate([token] * Q_PER_KV, axis=1)
                mx = snk
                for ss, _ in groups:
                    for sc in ss:
                        mx = jnp.maximum(mx, jnp.max(sc, axis=0, keepdims=True))
                probs = []
                for ss, vt in groups:
                    es = [jnp.exp2(sc - mx).astype(BF16) for sc in ss]
                    probs.append((es[0] if len(es) == 1 else jnp.concatenate(es, axis=0), vt))
                yield _mxu_cost(LANES, n_keys, cols4), 120
                o_t = jnp.zeros((LANES + ONES_ROWS, cols4), F32)
                for e_all, vt in probs:
                    vt_ones = jnp.concatenate([vt, jnp.ones((ONES_ROWS, vt.shape[1]), BF16)], axis=0)
                    o_t = o_t + jnp.dot(vt_ones, e_all, preferred_element_type=F32)
                den = jnp.exp2(snk - mx) + o_t[LANES:LANES + 1]
                o_t = o_t[0:LANES] * (1.0 / den)
                o01 = jnp.where(upper, o_t[:, 0:BLOCK], o_t[:, BLOCK:2 * BLOCK]).T
                o23 = jnp.where(upper, o_t[:, 2 * BLOCK:3 * BLOCK], o_t[:, 3 * BLOCK:4 * BLOCK]).T
                a_s[own, CONV_WIDTH + kv * GROUP_W:CONV_WIDTH + (kv + 1) * GROUP_W] = (
                    jnp.concatenate([o01, o23], axis=1) * gb_s[slot, own, cols]).astype(BF16)

        yield from _out_proj_items(lambda: a_s[...], wout_ref, xa_ref, moda_ref, y_ref)

    staged = ((u_w, u_s), (gate_w, gate_s), (q_w, q_s), (k_w, k_s), (vt_w, vt_s), (gb_w, gb_s))
    _pipeline_steps(s, n_chunks, lag, ring, (h_w, h_s), staged, EVEN_ORDER, normalize, project, finish)


def _chunk_specs(n_chunks, nc, lag, per_seq_mod):
    last = n_chunks - 1
    norm_chunk = lambda s: jnp.minimum(s, last)
    proj_chunk = lambda s: jnp.clip(s - 1, 0, last)
    fin_chunk = lambda s: jnp.clip(s - 1 - lag, 0, last)
    seq_of = (lambda j: j // nc) if per_seq_mod else (lambda j: 0)
    return dict(
        x_norm=pl.BlockSpec((1, TM, D_MODEL), lambda s: (norm_chunk(s), 0, 0)),
        x_fin=pl.BlockSpec((1, TM, D_MODEL), lambda s: (fin_chunk(s), 0, 0)),
        mod_norm=pl.BlockSpec((1, 1, 3 * D_MODEL), lambda s: (seq_of(norm_chunk(s)), 0, 0)),
        mod_fin=pl.BlockSpec((1, 1, 3 * D_MODEL), lambda s: (seq_of(fin_chunk(s)), 0, 0)),
        proj_chunk=proj_chunk, fin_chunk=fin_chunk)


def _const_spec(shape):
    return pl.BlockSpec(shape, lambda c: (0,) * len(shape))


def _even_layer(x, mod, norm_g, w_in, conv_w, conv_b, q_g, k_g, sink, w_out, rope=None, ctx=None):
    n_seq, seq_len, _ = x.shape
    windowed = rope is not None
    nc = seq_len // TM
    n_chunks = n_seq * nc
    lag, ring = (2, 4) if windowed else (1, 2)
    assert windowed or nc == 1
    sp = _chunk_specs(n_chunks, nc, lag, windowed)
    xc = x.reshape(n_chunks, TM, D_MODEL)
    in_specs = [
        sp["x_norm"], sp["x_fin"], sp["mod_norm"], sp["mod_fin"],
        _const_spec((1, D_MODEL)),
        _const_spec((D_MODEL, EVEN_IN)),
        _const_spec((3, CONV_WIDTH)),
        _const_spec((1, CONV_WIDTH)),
        _const_spec((1, LANES)),
        _const_spec((1, LANES)),
        pl.BlockSpec(memory_space=pltpu.SMEM),
        _const_spec((D_MODEL, D_MODEL)),
    ]
    args = [xc, xc, mod, mod, norm_g.reshape(1, D_MODEL), w_in, conv_w, conv_b.reshape(1, CONV_WIDTH),
            jnp.tile(q_g, LANES // HEAD_DIM).reshape(1, LANES), jnp.tile(k_g, LANES // HEAD_DIM).reshape(1, LANES),
            sink, w_out]
    per_chunk = [
        ((TM, CONV_WIDTH), F32),
        ((TM, CONV_WIDTH), F32),
        ((TM, ATTN_WIDTH), BF16),
        ((TM, 2 * GROUP_W), BF16),
        ((2 * LANES, TM), BF16),
        ((TM, ATTN_WIDTH), F32),
    ]
    scratch = (
        [pltpu.VMEM((TM, D_MODEL), BF16)] * 2
        + [pltpu.VMEM((ring,) + shape, dt) for shape, dt in per_chunk]
        + [pltpu.VMEM(shape, dt) for shape, dt in per_chunk]
        + [pltpu.VMEM((TM, D_MODEL), BF16)])
    y_shape = jax.ShapeDtypeStruct(xc.shape, F32)
    y_spec = pl.BlockSpec((1, TM, D_MODEL), lambda c: (sp["fin_chunk"](c), 0, 0))
    if windowed:
        cos, sin = rope
        ck, cv = ctx
        rope_spec = pl.BlockSpec((TM, LANES), lambda c: (sp["proj_chunk"](c) % nc, 0))
        ctx_spec = pl.BlockSpec((1,) + ck.shape[1:], lambda c: (sp["fin_chunk"](c) // nc, 0, 0))
        in_specs += [rope_spec, rope_spec, ctx_spec, ctx_spec]
        args += [cos, sin, ck, cv]
        scratch += [pltpu.VMEM((ck.shape[1], 2 * GROUP_W), BF16), pltpu.VMEM((2 * LANES, cv.shape[1]), BF16)]
        out_shape, out_specs = y_shape, y_spec
    else:
        kv_shape = jax.ShapeDtypeStruct((n_chunks, TM, KV_WIDTH), F32)
        kv_spec = pl.BlockSpec((1, TM, KV_WIDTH), lambda c: (sp["proj_chunk"](c), 0, 0))
        out_shape, out_specs = (y_shape, kv_shape, kv_shape), (y_spec, kv_spec, kv_spec)
    out = pl.pallas_call(
        functools.partial(_even_kernel, nc=nc, n_chunks=n_chunks, lag=lag, ring=ring, windowed=windowed),
        out_shape=out_shape,
        grid=(n_chunks + lag + 1,),
        in_specs=in_specs,
        out_specs=out_specs,
        scratch_shapes=scratch,
        compiler_params=PIPELINE_PARAMS,
        name="even_latent" if windowed else "even_context",
    )(*args)
    if windowed:
        return out.reshape(x.shape)
    y, k, v = out
    return y.reshape(x.shape), k.reshape(n_seq, seq_len, KV_WIDTH), v.reshape(n_seq, seq_len, KV_WIDTH)


def _odd_kernel(xp_ref, xa_ref, modp_ref, moda_ref, g_ref, win_ref, pw_ref, ps_ref, wout_ref, y_ref,
                h_s, h_w, u_s, sg_s, u_w, sg_w, *, nc, n_chunks, lag, ring):
    s = pl.program_id(0)
    i = s - 1 - lag
    ci = lax.rem(i, jnp.int32(nc))

    def normalize(tokens):
        yield from _normalize_items(h_w, xp_ref, g_ref, modp_ref, tokens)

    def project(tokens):
        tile = _mxu_cost(TM, D_MODEL, NT)
        for n in range(D_MODEL // NT):
            yield tile, 10
            cols = slice(n * NT, (n + 1) * NT)
            u = jnp.dot(h_s[...], win_ref[:, cols], preferred_element_type=F32)
            u_w[:, cols] = u
        for n in range(D_MODEL // NT):
            yield tile, 110
            cols = slice(n * NT, (n + 1) * NT)
            sg = _silu(jnp.dot(h_s[...], win_ref[:, D_MODEL + n * NT:D_MODEL + (n + 1) * NT],
                               preferred_element_type=F32))
            sg_w[:, cols] = sg

    def finish(slot, tokens):
        if nc > 1:
            prev_slot = lax.rem(slot + (ring - 1), jnp.int32(ring))
            next_slot = lax.rem(slot + 1, jnp.int32(ring))
            has_prev = ci > 0
            has_next = ci < nc - 1
        else:
            prev_slot = next_slot = has_prev = has_next = None
        t = ci * TM + lax.broadcasted_iota(jnp.int32, (TM, 1), 0)
        outs = []
        for gi, w in enumerate(POOL_SIZES):
            yield _mxu_cost(TM, POOL_GROUP, POOL_GROUP), 140 + 120 * gi
            cols = slice(gi * POOL_GROUP, (gi + 1) * POOL_GROUP)
            ext = _with_halo(u_s, slot, cols, prev_slot, next_slot, has_prev, has_next)
            acc = ext
            span = 1
            while span < w:
                acc = acc + _rows_down(acc, span)
                span *= 2
            if w > 2:
                acc = _rows_up(acc, w // 2 - 1)
            cnt = (jnp.minimum(t + w // 2, nc * TM) - jnp.maximum(t - w // 2, 0)).astype(F32)
            while tokens:
                cnt = cnt + tokens.pop(0)[:, 0:1]
            pooled = acc[HALO:HALO + TM] / cnt - ext[HALO:HALO + TM]
            outs.append(jnp.dot(pooled.astype(BF16), pw_ref[gi], preferred_element_type=F32))
        yield 0, 130
        y = (jnp.concatenate(outs, axis=1) * ps_ref[...] * sg_s[slot]).astype(BF16)
        yield from _out_proj_items(lambda: y, wout_ref, xa_ref, moda_ref, y_ref)

    _pipeline_steps(s, n_chunks, lag, ring, (h_w, h_s), ((u_w, u_s), (sg_w, sg_s)), ODD_ORDER,
                    normalize, project, finish)


def _odd_layer(x, mod, per_seq_mod, norm_g, w_in, pool_w, pool_scale, w_out):
    n_seq, seq_len, _ = x.shape
    nc = seq_len // TM
    n_chunks = n_seq * nc
    lag, ring = (2, 4) if nc > 1 else (1, 2)
    sp = _chunk_specs(n_chunks, nc, lag, per_seq_mod)
    xc = x.reshape(n_chunks, TM, D_MODEL)
    in_specs = [
        sp["x_norm"], sp["x_fin"], sp["mod_norm"], sp["mod_fin"],
        _const_spec((1, D_MODEL)),
        _const_spec((D_MODEL, 2 * D_MODEL)),
        _const_spec(pool_w.shape),
        _const_spec((1, D_MODEL)),
        _const_spec((D_MODEL, D_MODEL)),
    ]
    out = pl.pallas_call(
        functools.partial(_odd_kernel, nc=nc, n_chunks=n_chunks, lag=lag, ring=ring),
        out_shape=jax.ShapeDtypeStruct(xc.shape, F32),
        grid=(n_chunks + lag + 1,),
        in_specs=in_specs,
        out_specs=pl.BlockSpec((1, TM, D_MODEL), lambda c: (sp["fin_chunk"](c), 0, 0)),
        scratch_shapes=[
            pltpu.VMEM((TM, D_MODEL), BF16),
            pltpu.VMEM((TM, D_MODEL), BF16),
            pltpu.VMEM((ring, TM, D_MODEL), F32),
            pltpu.VMEM((ring, TM, D_MODEL), F32),
            pltpu.VMEM((TM, D_MODEL), F32),
            pltpu.VMEM((TM, D_MODEL), F32),
        ],
        compiler_params=PIPELINE_PARAMS,
        name="odd_latent" if per_seq_mod else "odd_context",
    )(xc, xc, mod, mod, norm_g.reshape(1, D_MODEL), w_in, pool_w, pool_scale.reshape(1, D_MODEL), w_out)
    return out.reshape(x.shape)


def _rope_tables(seq_len):
    n_rows = seq_len // GRID_W
    row = np.repeat(np.arange(n_rows), GRID_W).astype(np.float32)
    col = np.tile(np.arange(GRID_W), n_rows).astype(np.float32)
    inv = (np.float32(ROPE_BASE) ** (-np.arange(ROPE_FREQS, dtype=np.float32) / np.float32(ROPE_FREQS))).astype(np.float32)
    ang = np.stack([row[:, None] * inv, col[:, None] * inv], axis=1)
    cos = np.broadcast_to(np.cos(ang)[:, :, None, :], (seq_len, 2, 2, ROPE_FREQS))
    sin = np.sin(ang)[:, :, None, :] * np.array([-1.0, 1.0], np.float32)[None, None, :, None]
    cos = cos.reshape(seq_len, HEAD_DIM).astype(np.float32)
    sin = sin.reshape(seq_len, HEAD_DIM).astype(np.float32)
    return jnp.asarray(np.tile(cos, (1, LANES // HEAD_DIM))), jnp.asarray(np.tile(sin, (1, LANES // HEAD_DIM)))


def kernel(x_prompt, x_sample, cache_k, cache_v, c, c_ctx, ada_w_e, ada_b_e, norm_g_e, w_in_e, conv_w, conv_b,
           q_norm_g, k_norm_g, sink, w_out_e, ada_w_o, ada_b_o, norm_g_o, w_in_o, pool_w, pool_scale, w_out_o):
    n_dec = x_sample.shape[0]
    depth = ada_w_e.shape[0] + ada_w_o.shape[0]
    assert ada_w_e.shape[0] == 1 and ada_w_o.shape[0] == 1 and n_dec + 1 <= 8
    cond8 = jnp.concatenate([c_ctx[None], c, jnp.zeros((8 - 1 - n_dec, D_MODEL), F32)], axis=0)
    mod_e, mod_o = _adaln(cond8, ada_w_e[0], ada_b_e[0], ada_w_o[0], ada_b_o[0])
    rope = _rope_tables(x_sample.shape[1])

    yp, ys = x_prompt, x_sample
    new_k, new_v = [], []
    for layer in range(depth):
        i = layer // 2
        if layer % 2 == 0:
            mod = mod_e.reshape(8, 1, 3 * D_MODEL)
            w_in = w_in_e[i].astype(BF16)
            w_out = w_out_e[i].astype(BF16)
            common = (norm_g_e[i], w_in, conv_w[i], conv_b[i], q_norm_g[i], k_norm_g[i], sink[i], w_out)
            yp, k, v = _even_layer(yp, mod[0:1], *common)
            new_k.append(k.reshape(k.shape[0], k.shape[1], N_KV_HEADS, HEAD_DIM))
            new_v.append(v.reshape(v.shape[0], v.shape[1], N_KV_HEADS, HEAD_DIM))
            ck = cache_k[:, i].reshape(n_dec, cache_k.shape[2], KV_WIDTH)
            cv = cache_v[:, i].reshape(n_dec, cache_v.shape[2], KV_WIDTH)
            ys = _even_layer(ys, mod[1:1 + n_dec], *common, rope=rope, ctx=(ck, cv))
        else:
            mod = mod_o.reshape(8, 1, 3 * D_MODEL)
            common = (norm_g_o[i], w_in_o[i].astype(BF16), pool_w[i].astype(BF16), pool_scale[i],
                      w_out_o[i].astype(BF16))
            yp = _odd_layer(yp, mod[0:1], False, *common)
            ys = _odd_layer(ys, mod[1:1 + n_dec], True, *common)
    return yp, ys, jnp.stack(new_k, axis=1), jnp.stack(new_v, axis=1)
```

```python
import functools

import jax
import jax.numpy as jnp
import numpy as np
from jax import lax
from jax.experimental import pallas as pl
from jax.experimental.pallas import tpu as pltpu

F32 = jnp.float32
BF16 = jnp.bfloat16

D_MODEL = 1024
GRID_W = 64
HEAD_DIM = 64
HEAD_SHIFT = 6
ATTN_WIDTH = D_MODEL // 2
N_Q_HEADS = ATTN_WIDTH // HEAD_DIM
N_KV_HEADS = N_Q_HEADS // 4
Q_PER_KV = N_Q_HEADS // N_KV_HEADS
KV_WIDTH = N_KV_HEADS * HEAD_DIM
CONV_WIDTH = D_MODEL - ATTN_WIDTH
WINDOW = 128
BLOCK = 128
ROPE_BASE = 10000.0
ROPE_FREQS = HEAD_DIM // 4
ATTN_SCALE = HEAD_DIM ** -0.5
LOG2E = float(np.log2(np.e))
NEG = -1e30
POOL_SIZES = (2, 4, 8, 16)
POOL_GROUP = D_MODEL // len(POOL_SIZES)
EPS = 1e-6
EVEN_SIZES = (CONV_WIDTH, CONV_WIDTH, CONV_WIDTH, CONV_WIDTH, ATTN_WIDTH, KV_WIDTH, KV_WIDTH, ATTN_WIDTH)
EVEN_IN = sum(EVEN_SIZES)
EVEN_OFFS = tuple(int(s) for s in np.cumsum((0,) + EVEN_SIZES))

TM = 256
NT = 256
NORM_ROWS = 64
EVEN_ORDER = "PF" + "FNPFPF" * 4 + "FFFF"
ODD_ORDER = "PNFP" * 4 + "FFFFF"
LANES = 128
GROUP_W = Q_PER_KV * HEAD_DIM
HALO = 8
ADA_TN = 768
VMEM_LIMIT = 56 * 1024 * 1024
PIPELINE_PARAMS = pltpu.CompilerParams(dimension_semantics=("arbitrary",), vmem_limit_bytes=VMEM_LIMIT)


def _silu(x):
    return x / (1.0 + jnp.exp(-x))


def _rows_down(x, k):
    return pltpu.roll(x, k, axis=0)


def _rows_up(x, k):
    return pltpu.roll(x, x.shape[0] - k, axis=0)


def _rms_modulate(x, g, shift, scale):
    ms = jnp.mean(x * x, axis=-1, keepdims=True)
    return (x * lax.rsqrt(ms + EPS) * g) * (1.0 + scale) + shift


def _opaque_zero(x):
    bits = pltpu.bitcast(x, jnp.uint32)
    return pltpu.bitcast((bits >> 16) >> 16, F32)


def _normalize_items(h_s, x_ref, g_ref, mod_ref, tokens):
    shift = mod_ref[0, :, 0:D_MODEL]
    scale = mod_ref[0, :, D_MODEL:2 * D_MODEL]
    for r in range(0, TM, NORM_ROWS):
        yield 0, 170
        h = _rms_modulate(x_ref[0, r:r + NORM_ROWS], g_ref[...], shift, scale)
        h_s[r:r + NORM_ROWS] = h.astype(BF16)
        tokens.append(_opaque_zero(h[0:8, 0:LANES])[0:1])


def _out_proj_items(get_a, wout_ref, x_ref, mod_ref, y_ref):
    for n in range(D_MODEL // NT):
        yield _mxu_cost(TM, D_MODEL, NT), 70
        cols = slice(n * NT, (n + 1) * NT)
        mixed = jnp.dot(get_a(), wout_ref[:, cols], preferred_element_type=F32)
        y_ref[0, :, cols] = x_ref[0, :, cols] + mod_ref[0, :, 2 * D_MODEL + n * NT:2 * D_MODEL + (n + 1) * NT] * mixed


def _mxu_cost(m, k, n):
    tiles = -(-k // NT) * -(-n // NT)
    return tiles * max(m // 2, 64) // 2


def _weave(order, **stages):
    by_letter = {name[0].upper(): stage for name, stage in stages.items()}
    for stage in by_letter.values():
        next(stage, None)
    for letter in order:
        if letter in by_letter:
            next(by_letter[letter], None)
    for stage in by_letter.values():
        for _ in stage:
            pass


def _with_halo(u_s, slot, cols, prev_slot, next_slot, has_prev, has_next):
    u = u_s[slot, :, cols]
    zero = jnp.zeros((HALO, u.shape[1]), F32)
    if prev_slot is None:
        top = bot = zero
    else:
        top = jnp.where(has_prev, u_s[prev_slot, TM - HALO:TM, cols], zero)
        bot = jnp.where(has_next, u_s[next_slot, 0:HALO, cols], zero)
    return jnp.concatenate([top, u, bot], axis=0)


def _pipeline_steps(s, n_chunks, lag, ring, normalized, staged, order, normalize, project, finish):
    fin_slot = lax.rem(s + (2 * ring - 1 - lag), jnp.int32(ring))

    @pl.when((s >= 1) & (s <= n_chunks))
    def _publish_normalized():
        h_w, h_s = normalized
        h_s[...] = h_w[...]

    @pl.when((s >= 2) & (s <= n_chunks + 1))
    def _publish_projected():
        slot = lax.rem(s + (ring - 2), jnp.int32(ring))
        for stage_ref, ring_ref in staged:
            ring_ref[slot] = stage_ref[...]

    @pl.when(s == 0)
    def _first():
        _weave(order, normalize=normalize([]))

    @pl.when((s >= 1) & (s <= lag))
    def _fill():
        _weave(order, project=project([]), normalize=normalize([]))

    @pl.when((s > lag) & (s <= n_chunks))
    def _steady():
        tokens = []
        _weave(order, finish=finish(fin_slot, tokens), project=project(tokens), normalize=normalize(tokens))

    @pl.when(s > n_chunks)
    def _drain():
        _weave(order, finish=finish(fin_slot, []))


def _adaln_kernel(cond_ref, we_ref, be_ref, wo_ref, bo_ref, oe_ref, oo_ref):
    s = _silu(cond_ref[...]).astype(BF16)
    oe_ref[...] = jnp.dot(s, we_ref[...].astype(BF16), preferred_element_type=F32) + be_ref[...]
    oo_ref[...] = jnp.dot(s, wo_ref[...].astype(BF16), preferred_element_type=F32) + bo_ref[...]


def _adaln(cond8, w_e, b_e, w_o, b_o):
    n = 3 * D_MODEL
    wspec = pl.BlockSpec((D_MODEL, ADA_TN), lambda j: (0, j))
    vspec = pl.BlockSpec((1, ADA_TN), lambda j: (0, j))
    ospec = pl.BlockSpec((8, ADA_TN), lambda j: (0, j))
    return pl.pallas_call(
        _adaln_kernel,
        out_shape=(jax.ShapeDtypeStruct((8, n), F32), jax.ShapeDtypeStruct((8, n), F32)),
        grid=(n // ADA_TN,),
        in_specs=[pl.BlockSpec((8, D_MODEL), lambda j: (0, 0)), wspec, vspec, wspec, vspec],
        out_specs=(ospec, ospec),
        compiler_params=pltpu.CompilerParams(dimension_semantics=("arbitrary",), vmem_limit_bytes=VMEM_LIMIT),
        name="adaln",
    )(cond8, w_e, b_e.reshape(1, n), w_o, b_o.reshape(1, n))


def _head_inv_rms(x, bd):
    sq = x * x
    hi = sq.astype(BF16)
    lo = (sq - hi.astype(F32)).astype(BF16)
    ms = (jnp.dot(hi, bd, preferred_element_type=F32) + jnp.dot(lo, bd, preferred_element_type=F32)) * (1.0 / HEAD_DIM)
    return lax.rsqrt(ms + EPS)


def _rope(x, cos, sin_signed, first_half):
    partner = jnp.where(first_half, pltpu.roll(x, LANES - ROPE_FREQS, axis=1), pltpu.roll(x, ROPE_FREQS, axis=1))
    return x * cos + partner * sin_signed


def _dup_kv(x):
    lane = lax.broadcasted_iota(jnp.int32, (1, LANES), 1)
    swapped = pltpu.roll(x, HEAD_DIM, axis=1)
    return jnp.where(lane < HEAD_DIM, x, swapped), jnp.where(lane < HEAD_DIM, swapped, x)


def _tiled_keys(k):
    ka, kb = _dup_kv(k)
    return jnp.concatenate([ka, ka, kb, kb], axis=1).astype(BF16)


def _values_t(v):
    va, vb = _dup_kv(v)
    return jnp.concatenate([va.T, vb.T], axis=0).astype(BF16)


def _even_kernel(*refs, nc, n_chunks, lag, ring, windowed):
    if windowed:
        (xp_ref, xa_ref, modp_ref, moda_ref, g_ref, win_ref, cw_ref, cb_ref, qg_ref, kg_ref, sink_ref, wout_ref,
         cos_ref, sin_ref, ck_ref, cv_ref,
         y_ref,
         h_s, h_w, u_s, gate_s, q_s, k_s, vt_s, gb_s, u_w, gate_w, q_w, k_w, vt_w, gb_w, a_s, p_a, p_b,
         ck_s, cvt_s) = refs
        ko_ref = vo_ref = None
    else:
        (xp_ref, xa_ref, modp_ref, moda_ref, g_ref, win_ref, cw_ref, cb_ref, qg_ref, kg_ref, sink_ref, wout_ref,
         y_ref, ko_ref, vo_ref,
         h_s, h_w, u_s, gate_s, q_s, k_s, vt_s, gb_s, u_w, gate_w, q_w, k_w, vt_w, gb_w, a_s, p_a, p_b) = refs
    p_bufs = (p_a, p_b)
    s = pl.program_id(0)
    i = s - 1 - lag
    ci = lax.rem(i, jnp.int32(nc))

    @pl.when(s == 0)
    def _init():
        k_s[...] = jnp.zeros(k_s.shape, BF16)
        vt_s[...] = jnp.zeros(vt_s.shape, BF16)

    if windowed:
        @pl.when((i >= 0) & (ci == 0))
        def _load_ctx():
            ck_s[...] = _tiled_keys(ck_ref[0])
            cvt_s[...] = _values_t(cv_ref[0])

    def normalize(tokens):
        yield from _normalize_items(h_w, xp_ref, g_ref, modp_ref, tokens)

    def project(tokens):
        o_bg, o_cg, o_xs, o_ga, o_q, o_k, _, o_gb = EVEN_OFFS[:8]

        def proj(start):
            return jnp.dot(h_s[...], win_ref[:, start:start + NT], preferred_element_type=F32)

        lane = lax.broadcasted_iota(jnp.int32, (1, LANES), 1)
        first_half = (lane & (2 * ROPE_FREQS - 1)) < ROPE_FREQS
        ri = lax.broadcasted_iota(jnp.int32, (NT, NT), 0) >> HEAD_SHIFT
        cj = lax.broadcasted_iota(jnp.int32, (NT, NT), 1) >> HEAD_SHIFT
        bd = (ri == cj).astype(BF16)

        def qk_norm(x, g):
            w = x.shape[1]
            inv = _head_inv_rms(x, bd[0:w, 0:w])
            pieces = []
            for j in range(w // LANES):
                lanes = slice(j * LANES, (j + 1) * LANES)
                xj = x[:, lanes] * inv[:, lanes] * g
                pieces.append(_rope(xj, cos_ref[...], sin_ref[...], first_half) if windowed else xj)
            return pieces

        tile = _mxu_cost(TM, D_MODEL, NT)

        def keys_values():
            kv = proj(o_k)
            k, = qk_norm(kv[:, 0:KV_WIDTH], kg_ref[...])
            v = kv[:, KV_WIDTH:2 * KV_WIDTH]
            if not windowed:
                ko_ref[0] = k
                vo_ref[0] = v
            k_w[...] = _tiled_keys(k)
            vt_w[...] = _values_t(v)
            return v

        def queries(n):
            for j, qj in enumerate(qk_norm(proj(o_q + n * NT), qg_ref[...])):
                q_w[:, n * NT + j * LANES:n * NT + (j + 1) * LANES] = (qj * (ATTN_SCALE * LOG2E)).astype(BF16)
            return qj

        def conv_gate(n):
            gate = proj(o_bg + n * NT) * _silu(proj(o_ga + n * NT))
            gate_w[:, n * NT:(n + 1) * NT] = gate
            return gate

        def conv_input(n):
            u = proj(o_cg + n * NT) * proj(o_xs + n * NT)
            u_w[:, n * NT:(n + 1) * NT] = u
            return u

        def attn_gate(n):
            gb = _silu(proj(o_gb + n * NT))
            gb_w[:, n * NT:(n + 1) * NT] = gb
            return gb

        items = [((2 * tile, 180), conv_gate, 0), ((tile, 350), keys_values, None), ((2 * tile, 40), conv_input, 0),
                 ((tile, 300), queries, 0), ((2 * tile, 180), conv_gate, 1), ((tile, 300), queries, 1),
                 ((2 * tile, 40), conv_input, 1), ((tile, 110), attn_gate, 0), ((tile, 110), attn_gate, 1)]
        for cost, emit, n in items:
            yield cost
            emit() if n is None else emit(n)

    def finish(slot, tokens):
        if windowed:
            prev_slot = lax.rem(slot + (ring - 1), jnp.int32(ring))
            next_slot = lax.rem(slot + 1, jnp.int32(ring))
            has_prev = ci > 0
            has_next = ci < nc - 1
        else:
            prev_slot = next_slot = has_prev = has_next = None

        yield 0, 400
        ext = _with_halo(u_s, slot, slice(None), prev_slot, next_slot, has_prev, has_next)
        conv = (_rows_down(ext, 1)[HALO:HALO + TM] * cw_ref[0:1, :]
                + ext[HALO:HALO + TM] * cw_ref[1:2, :]
                + _rows_up(ext, 1)[HALO:HALO + TM] * cw_ref[2:3, :]
                + cb_ref[...])
        a_s[:, 0:CONV_WIDTH] = (gate_s[slot] * conv).astype(BF16)

        lane_head = lax.broadcasted_iota(jnp.int32, (1, GROUP_W), 1) >> HEAD_SHIFT
        cols4 = Q_PER_KV * BLOCK
        n_keys = 3 * BLOCK + ck_s.shape[0] if windowed else TM
        key_i =lax.broadcasted_iota(jnp.int32, (BLOCK, cols4), 0)
        qry_i = lax.broadcasted_iota(jnp.int32, (BLOCK, cols4), 1) & (BLOCK - 1)
        upper = lax.broadcasted_iota(jnp.int32, (LANES, LANES), 0) < HEAD_DIM
        halves = (slice(0, BLOCK), slice(BLOCK, TM))
        for qb in range(TM // BLOCK):
            own = halves[qb]
            if windowed:
                if qb == 0:
                    blocks = ((prev_slot, halves[1]), (slot, halves[0]), (slot, halves[1]))
                    prev_in, next_in = has_prev, True
                else:
                    blocks = ((slot, halves[0]), (slot, halves[1]), (next_slot, halves[0]))
                    prev_in, next_in = True, has_next
                prev_ok = (key_i - qry_i) >= jnp.where(prev_in, 0, BLOCK)
                next_ok = (qry_i - key_i) >= jnp.where(next_in, 0, BLOCK)
            for kv in range(N_KV_HEADS):
                cols = slice(kv * GROUP_W, (kv + 1) * GROUP_W)
                vrows = slice(kv * LANES, (kv + 1) * LANES)
                yield _mxu_cost(n_keys, GROUP_W, cols4), 60
                qblk = q_s[slot, own, cols]
                q4 = jnp.concatenate(
                    [jnp.where(lane_head == g, qblk, jnp.zeros_like(qblk)) for g in range(Q_PER_KV)], axis=0)

                def scores_t(kk):
                    return lax.dot_general(kk, q4, (((1,), (1,)), ((), ())), preferred_element_type=F32)

                if windowed:
                    s_loc = scores_t(jnp.concatenate([k_s[sl, r, cols] for sl, r in blocks], axis=0))
                    vt_loc = jnp.concatenate([vt_s[sl, vrows, r] for sl, r in blocks], axis=1)
                    s_ctx = scores_t(ck_s[:, cols])
                    yield 0, n_keys * cols4 * 5 // (8 * LANES * 4)
                    groups = [([jnp.where(prev_ok, s_loc[0:BLOCK], NEG), s_loc[BLOCK:2 * BLOCK],
                                jnp.where(next_ok, s_loc[2 * BLOCK:3 * BLOCK], NEG)], vt_loc),
                              ([s_ctx], cvt_s[vrows, :])]
                else:
                    s_own = scores_t(k_s[slot, :, cols])
                    yield 0, n_keys * cols4 * 5 // (8 * LANES * 4)
                    groups = [([s_own], vt_s[slot, vrows, :])]
                snk = jnp.concatenate(
                    [jnp.full((1, BLOCK), sink_ref[kv * Q_PER_KV + g] * LOG2E, F32) for g in range(Q_PER_KV)], axis=1)
                if tokens:
                    token = tokens.pop(0)
                    while tokens:
                        token = token + tokens.pop(0)
                    snk = snk + jnp.concatenate([token] * Q_PER_KV, axis=1)
                mx = snk
                for ss, _ in groups:
                    for sc in ss:
                        mx = jnp.maximum(mx, jnp.max(sc, axis=0, keepdims=True))
                den = jnp.exp2(snk - mx)
                p_ref = p_bufs[(qb * N_KV_HEADS + kv) % len(p_bufs)]
                probs = []
                row = 0
                for ss, vt in groups:
                    first = row
                    for sc in ss:
                        for r in range(0, sc.shape[0], BLOCK):
                            e = jnp.exp2(sc[r:r + BLOCK] - mx)
                            den = den + jnp.sum(e, axis=0, keepdims=True)
                            p_ref[row:row + BLOCK, :] = e.astype(BF16)
                            row += BLOCK
                    probs.append(((first, row), vt))
                yield _mxu_cost(LANES, n_keys, cols4), 120
                o_t = jnp.zeros((LANES, cols4), F32)
                for (first, last), vt in probs:
                    o_t = o_t + jnp.dot(vt, p_ref[first:last, :], preferred_element_type=F32)
                o_t = o_t * (1.0 / den)
                o01 = jnp.where(upper, o_t[:, 0:BLOCK], o_t[:, BLOCK:2 * BLOCK]).T
                o23 = jnp.where(upper, o_t[:, 2 * BLOCK:3 * BLOCK], o_t[:, 3 * BLOCK:4 * BLOCK]).T
                a_s[own, CONV_WIDTH + kv * GROUP_W:CONV_WIDTH + (kv + 1) * GROUP_W] = (
                    jnp.concatenate([o01, o23], axis=1) * gb_s[slot, own, cols]).astype(BF16)

        yield from _out_proj_items(lambda: a_s[...], wout_ref, xa_ref, moda_ref, y_ref)

    staged = ((u_w, u_s), (gate_w, gate_s), (q_w, q_s), (k_w, k_s), (vt_w, vt_s), (gb_w, gb_s))
    _pipeline_steps(s, n_chunks, lag, ring, (h_w, h_s), staged, EVEN_ORDER, normalize, project, finish)


def _chunk_specs(n_chunks, nc, lag, per_seq_mod):
    last = n_chunks - 1
    norm_chunk = lambda s: jnp.minimum(s, last)
    proj_chunk = lambda s: jnp.clip(s - 1, 0, last)
    fin_chunk = lambda s: jnp.clip(s - 1 - lag, 0, last)
    seq_of = (lambda j: j // nc) if per_seq_mod else (lambda j: 0)
    return dict(
        x_norm=pl.BlockSpec((1, TM, D_MODEL), lambda s: (norm_chunk(s), 0, 0)),
        x_fin=pl.BlockSpec((1, TM, D_MODEL), lambda s: (fin_chunk(s), 0, 0)),
        mod_norm=pl.BlockSpec((1, 1, 3 * D_MODEL), lambda s: (seq_of(norm_chunk(s)), 0, 0)),
        mod_fin=pl.BlockSpec((1, 1, 3 * D_MODEL), lambda s: (seq_of(fin_chunk(s)), 0, 0)),
        proj_chunk=proj_chunk, fin_chunk=fin_chunk)


def _const_spec(shape):
    return pl.BlockSpec(shape, lambda c: (0,) * len(shape))


def _even_layer(x, mod, norm_g, w_in, conv_w, conv_b, q_g, k_g, sink, w_out, rope=None, ctx=None):
    n_seq, seq_len, _ = x.shape
    windowed = rope is not None
    nc = seq_len // TM
    n_chunks = n_seq * nc
    lag, ring = (2, 4) if windowed else (1, 2)
    assert windowed or nc == 1
    sp = _chunk_specs(n_chunks, nc, lag, windowed)
    xc = x.reshape(n_chunks, TM, D_MODEL)
    in_specs = [
        sp["x_norm"], sp["x_fin"], sp["mod_norm"], sp["mod_fin"],
        _const_spec((1, D_MODEL)),
        _const_spec((D_MODEL, EVEN_IN)),
        _const_spec((3, CONV_WIDTH)),
        _const_spec((1, CONV_WIDTH)),
        _const_spec((1, LANES)),
        _const_spec((1, LANES)),
        pl.BlockSpec(memory_space=pltpu.SMEM),
        _const_spec((D_MODEL, D_MODEL)),
    ]
    args = [xc, xc, mod, mod, norm_g.reshape(1, D_MODEL), w_in, conv_w, conv_b.reshape(1, CONV_WIDTH),
            jnp.tile(q_g, LANES // HEAD_DIM).reshape(1, LANES), jnp.tile(k_g, LANES // HEAD_DIM).reshape(1, LANES),
            sink, w_out]
    n_keys = 3 * BLOCK + ctx[0].shape[1] if windowed else TM
    per_chunk = [
        ((TM, CONV_WIDTH), F32),
        ((TM, CONV_WIDTH), F32),
        ((TM, ATTN_WIDTH), BF16),
        ((TM, 2 * GROUP_W), BF16),
        ((2 * LANES, TM), BF16),
        ((TM, ATTN_WIDTH), F32),
    ]
    scratch = (
        [pltpu.VMEM((TM, D_MODEL), BF16)] * 2
        + [pltpu.VMEM((ring,) + shape, dt) for shape, dt in per_chunk]
        + [pltpu.VMEM(shape, dt) for shape, dt in per_chunk]
        + [pltpu.VMEM((TM, D_MODEL), BF16)]
        + [pltpu.VMEM((n_keys, Q_PER_KV * BLOCK), BF16)] * 2)
    y_shape = jax.ShapeDtypeStruct(xc.shape, F32)
    y_spec = pl.BlockSpec((1, TM, D_MODEL), lambda c: (sp["fin_chunk"](c), 0, 0))
    if windowed:
        cos, sin = rope
        ck, cv = ctx
        rope_spec = pl.BlockSpec((TM, LANES), lambda c: (sp["proj_chunk"](c) % nc, 0))
        ctx_spec = pl.BlockSpec((1,) + ck.shape[1:], lambda c: (sp["fin_chunk"](c) // nc, 0, 0))
        in_specs += [rope_spec, rope_spec, ctx_spec, ctx_spec]
        args += [cos, sin, ck, cv]
        scratch += [pltpu.VMEM((ck.shape[1], 2 * GROUP_W), BF16), pltpu.VMEM((2 * LANES, cv.shape[1]), BF16)]
        out_shape, out_specs = y_shape, y_spec
    else:
        kv_shape = jax.ShapeDtypeStruct((n_chunks, TM, KV_WIDTH), F32)
        kv_spec = pl.BlockSpec((1, TM, KV_WIDTH), lambda c: (sp["proj_chunk"](c), 0, 0))
        out_shape, out_specs = (y_shape, kv_shape, kv_shape), (y_spec, kv_spec, kv_spec)
    out = pl.pallas_call(
        functools.partial(_even_kernel, nc=nc, n_chunks=n_chunks, lag=lag, ring=ring, windowed=windowed),
        out_shape=out_shape,
        grid=(n_chunks + lag + 1,),
        in_specs=in_specs,
        out_specs=out_specs,
        scratch_shapes=scratch,
        compiler_params=PIPELINE_PARAMS,
        name="even_latent" if windowed else "even_context",
    )(*args)
    if windowed:
        return out.reshape(x.shape)
    y, k, v = out
    return y.reshape(x.shape), k.reshape(n_seq, seq_len, KV_WIDTH), v.reshape(n_seq, seq_len, KV_WIDTH)


def _odd_kernel(xp_ref, xa_ref, modp_ref, moda_ref, g_ref, win_ref, pw_ref, ps_ref, wout_ref, y_ref,
                h_s, h_w, u_s, sg_s, u_w, sg_w, *, nc, n_chunks, lag, ring):
    s = pl.program_id(0)
    i = s - 1 - lag
    ci = lax.rem(i, jnp.int32(nc))

    def normalize(tokens):
        yield from _normalize_items(h_w, xp_ref, g_ref, modp_ref, tokens)

    def project(tokens):
        tile = _mxu_cost(TM, D_MODEL, NT)
        for n in range(D_MODEL // NT):
            yield tile, 10
            cols = slice(n * NT, (n + 1) * NT)
            u = jnp.dot(h_s[...], win_ref[:, cols], preferred_element_type=F32)
            u_w[:, cols] = u
        for n in range(D_MODEL // NT):
            yield tile, 110
            cols = slice(n * NT, (n + 1) * NT)
            sg = _silu(jnp.dot(h_s[...], win_ref[:, D_MODEL + n * NT:D_MODEL + (n + 1) * NT],
                               preferred_element_type=F32))
            sg_w[:, cols] = sg

    def finish(slot, tokens):
        if nc > 1:
            prev_slot = lax.rem(slot + (ring - 1), jnp.int32(ring))
            next_slot = lax.rem(slot + 1, jnp.int32(ring))
            has_prev = ci > 0
            has_next = ci < nc - 1
        else:
            prev_slot = next_slot = has_prev = has_next = None
        t = ci * TM + lax.broadcasted_iota(jnp.int32, (TM, 1), 0)
        outs = []
        for gi, w in enumerate(POOL_SIZES):
            yield _mxu_cost(TM, POOL_GROUP, POOL_GROUP), 140 + 120 * gi
            cols = slice(gi * POOL_GROUP, (gi + 1) * POOL_GROUP)
            ext = _with_halo(u_s, slot, cols, prev_slot, next_slot, has_prev, has_next)
            acc = ext
            span = 1
            while span < w:
                acc = acc + _rows_down(acc, span)
                span *= 2
            if w > 2:
                acc = _rows_up(acc, w // 2 - 1)
            cnt = (jnp.minimum(t + w // 2, nc * TM) - jnp.maximum(t - w // 2, 0)).astype(F32)
            while tokens:
                cnt = cnt + tokens.pop(0)[:, 0:1]
            pooled = acc[HALO:HALO + TM] / cnt - ext[HALO:HALO + TM]
            outs.append(jnp.dot(pooled.astype(BF16), pw_ref[gi], preferred_element_type=F32))
        yield 0, 130
        y = (jnp.concatenate(outs, axis=1) * ps_ref[...] * sg_s[slot]).astype(BF16)
        yield from _out_proj_items(lambda: y, wout_ref, xa_ref, moda_ref, y_ref)

    _pipeline_steps(s, n_chunks, lag, ring, (h_w, h_s), ((u_w, u_s), (sg_w, sg_s)), ODD_ORDER,
                    normalize, project, finish)


def _odd_layer(x, mod, per_seq_mod, norm_g, w_in, pool_w, pool_scale, w_out):
    n_seq, seq_len, _ = x.shape
    nc = seq_len // TM
    n_chunks = n_seq * nc
    lag, ring = (2, 4) if nc > 1 else (1, 2)
    sp = _chunk_specs(n_chunks, nc, lag, per_seq_mod)
    xc = x.reshape(n_chunks, TM, D_MODEL)
    in_specs = [
        sp["x_norm"], sp["x_fin"], sp["mod_norm"], sp["mod_fin"],
        _const_spec((1, D_MODEL)),
        _const_spec((D_MODEL, 2 * D_MODEL)),
        _const_spec(pool_w.shape),
        _const_spec((1, D_MODEL)),
        _const_spec((D_MODEL, D_MODEL)),
    ]
    out = pl.pallas_call(
        functools.partial(_odd_kernel, nc=nc, n_chunks=n_chunks, lag=lag, ring=ring),
        out_shape=jax.ShapeDtypeStruct(xc.shape, F32),
        grid=(n_chunks + lag + 1,),
        in_specs=in_specs,
        out_specs=pl.BlockSpec((1, TM, D_MODEL), lambda c: (sp["fin_chunk"](c), 0, 0)),
        scratch_shapes=[
            pltpu.VMEM((TM, D_MODEL), BF16),
            pltpu.VMEM((TM, D_MODEL), BF16),
            pltpu.VMEM((ring, TM, D_MODEL), F32),
            pltpu.VMEM((ring, TM, D_MODEL), F32),
            pltpu.VMEM((TM, D_MODEL), F32),
            pltpu.VMEM((TM, D_MODEL), F32),
        ],
        compiler_params=PIPELINE_PARAMS,
        name="odd_latent" if per_seq_mod else "odd_context",
    )(xc, xc, mod, mod, norm_g.reshape(1, D_MODEL), w_in, pool_w, pool_scale.reshape(1, D_MODEL), w_out)
    return out.reshape(x.shape)


def _rope_tables(seq_len):
    n_rows = seq_len // GRID_W
    row = np.repeat(np.arange(n_rows), GRID_W).astype(np.float32)
    col = np.tile(np.arange(GRID_W), n_rows).astype(np.float32)
    inv = (np.float32(ROPE_BASE) ** (-np.arange(ROPE_FREQS, dtype=np.float32) / np.float32(ROPE_FREQS))).astype(np.float32)
    ang = np.stack([row[:, None] * inv, col[:, None] * inv], axis=1)
    cos = np.broadcast_to(np.cos(ang)[:, :, None, :], (seq_len, 2, 2, ROPE_FREQS))
    sin = np.sin(ang)[:, :, None, :] * np.array([-1.0, 1.0], np.float32)[None, None, :, None]
    cos = cos.reshape(seq_len, HEAD_DIM).astype(np.float32)
    sin = sin.reshape(seq_len, HEAD_DIM).astype(np.float32)
    return jnp.asarray(np.tile(cos, (1, LANES // HEAD_DIM))), jnp.asarray(np.tile(sin, (1, LANES // HEAD_DIM)))


def kernel(x_prompt, x_sample, cache_k, cache_v, c, c_ctx, ada_w_e, ada_b_e, norm_g_e, w_in_e, conv_w, conv_b,
           q_norm_g, k_norm_g, sink, w_out_e, ada_w_o, ada_b_o, norm_g_o, w_in_o, pool_w, pool_scale, w_out_o):
    n_dec = x_sample.shape[0]
    depth = ada_w_e.shape[0] + ada_w_o.shape[0]
    assert ada_w_e.shape[0] == 1 and ada_w_o.shape[0] == 1 and n_dec + 1 <= 8
    cond8 = jnp.concatenate([c_ctx[None], c, jnp.zeros((8 - 1 - n_dec, D_MODEL), F32)], axis=0)
    mod_e, mod_o = _adaln(cond8, ada_w_e[0], ada_b_e[0], ada_w_o[0], ada_b_o[0])
    rope = _rope_tables(x_sample.shape[1])

    yp, ys = x_prompt, x_sample
    new_k, new_v = [], []
    for layer in range(depth):
        i = layer // 2
        if layer % 2 == 0:
            mod = mod_e.reshape(8, 1, 3 * D_MODEL)
            w_in = w_in_e[i].astype(BF16)
            w_out = w_out_e[i].astype(BF16)
            common = (norm_g_e[i], w_in, conv_w[i], conv_b[i], q_norm_g[i], k_norm_g[i], sink[i], w_out)
            yp, k, v = _even_layer(yp, mod[0:1], *common)
            new_k.append(k.reshape(k.shape[0], k.shape[1], N_KV_HEADS, HEAD_DIM))
            new_v.append(v.reshape(v.shape[0], v.shape[1], N_KV_HEADS, HEAD_DIM))
            ck = cache_k[:, i].reshape(n_dec, cache_k.shape[2], KV_WIDTH)
            cv = cache_v[:, i].reshape(n_dec, cache_v.shape[2], KV_WIDTH)
            ys = _even_layer(ys, mod[1:1 + n_dec], *common, rope=rope, ctx=(ck, cv))
        else:
            mod = mod_o.reshape(8, 1, 3 * D_MODEL)
            common = (norm_g_o[i], w_in_o[i].astype(BF16), pool_w[i].astype(BF16), pool_scale[i],
                      w_out_o[i].astype(BF16))
            yp = _odd_layer(yp, mod[0:1], False, *common)
            ys = _odd_layer(ys, mod[1:1 + n_dec], True, *common)
    return yp, ys, jnp.stack(new_k, axis=1), jnp.stack(new_v, axis=1)
```

```python
import functools

import jax
import jax.numpy as jnp
import numpy as np
from jax import lax
from jax.experimental import pallas as pl
from jax.experimental.pallas import tpu as pltpu

F32 = jnp.float32
BF16 = jnp.bfloat16

D_MODEL = 1024
GRID_W = 64
HEAD_DIM = 64
HEAD_SHIFT = 6
ATTN_WIDTH = D_MODEL // 2
N_Q_HEADS = ATTN_WIDTH // HEAD_DIM
N_KV_HEADS = N_Q_HEADS // 4
Q_PER_KV = N_Q_HEADS // N_KV_HEADS
KV_WIDTH = N_KV_HEADS * HEAD_DIM
CONV_WIDTH = D_MODEL - ATTN_WIDTH
WINDOW = 128
BLOCK = 128
ROPE_BASE = 10000.0
ROPE_FREQS = HEAD_DIM // 4
ATTN_SCALE = HEAD_DIM ** -0.5
LOG2E = float(np.log2(np.e))
NEG = -1e30
POOL_SIZES = (2, 4, 8, 16)
POOL_GROUP = D_MODEL // len(POOL_SIZES)
EPS = 1e-6
EVEN_SIZES = (CONV_WIDTH, CONV_WIDTH, CONV_WIDTH, CONV_WIDTH, ATTN_WIDTH, KV_WIDTH, KV_WIDTH, ATTN_WIDTH)
EVEN_IN = sum(EVEN_SIZES)
EVEN_OFFS = tuple(int(s) for s in np.cumsum((0,) + EVEN_SIZES))

TM = 256
TM_LONG = 512
NT = 256
NORM_ROWS = 64
EVEN_ORDER = "PF" + "FNPFPF" * 4 + "FFFF"
ODD_ORDER = "PNFP" * 4 + "FFFFF"
LANES = 128
GROUP_W = Q_PER_KV * HEAD_DIM
HALO = 8
ADA_TN = 768
VMEM_LIMIT = 56 * 1024 * 1024
PIPELINE_PARAMS = pltpu.CompilerParams(dimension_semantics=("arbitrary",), vmem_limit_bytes=VMEM_LIMIT)


def _silu(x):
    return x / (1.0 + jnp.exp(-x))


def _rows_down(x, k):
    return pltpu.roll(x, k, axis=0)


def _rows_up(x, k):
    return pltpu.roll(x, x.shape[0] - k, axis=0)


def _rms_modulate(x, g, shift, scale):
    ms = jnp.mean(x * x, axis=-1, keepdims=True)
    return (x * lax.rsqrt(ms + EPS) * g) * (1.0 + scale) + shift


def _opaque_zero(x):
    bits = pltpu.bitcast(x, jnp.uint32)
    return pltpu.bitcast((bits >> 16) >> 16, F32)


def _normalize_items(h_s, x_ref, g_ref, mod_ref, tokens):
    shift = mod_ref[0, :, 0:D_MODEL]
    scale = mod_ref[0, :, D_MODEL:2 * D_MODEL]
    for r in range(0, h_s.shape[0], NORM_ROWS):
        yield 0, 170
        h = _rms_modulate(x_ref[0, r:r + NORM_ROWS], g_ref[...], shift, scale)
        h_s[r:r + NORM_ROWS] = h.astype(BF16)
        tokens.append(_opaque_zero(h[0:8, 0:LANES])[0:1])


def _out_proj_items(get_a, wout_ref, x_ref, mod_ref, y_ref):
    for n in range(D_MODEL // NT):
        yield _mxu_cost(y_ref.shape[1], D_MODEL, NT), 70
        cols = slice(n * NT, (n + 1) * NT)
        mixed = jnp.dot(get_a(), wout_ref[:, cols], preferred_element_type=F32)
        y_ref[0, :, cols] = x_ref[0, :, cols] + mod_ref[0, :, 2 * D_MODEL + n * NT:2 * D_MODEL + (n + 1) * NT] * mixed


def _mxu_cost(m, k, n):
    tiles = -(-k // NT) * -(-n // NT)
    return tiles * max(m // 2, 64) // 2


def _weave(order, **stages):
    by_letter = {name[0].upper(): stage for name, stage in stages.items()}
    for stage in by_letter.values():
        next(stage, None)
    for letter in order:
        if letter in by_letter:
            next(by_letter[letter], None)
    for stage in by_letter.values():
        for _ in stage:
            pass


def _with_halo(u_s, slot, cols, prev_slot, next_slot, has_prev, has_next):
    u = u_s[slot, :, cols]
    zero = jnp.zeros((HALO, u.shape[1]), F32)
    if prev_slot is None:
        top = bot = zero
    else:
        rows = u_s.shape[1]
        top = jnp.where(has_prev, u_s[prev_slot, rows - HALO:rows, cols], zero)
        bot = jnp.where(has_next, u_s[next_slot, 0:HALO, cols], zero)
    return jnp.concatenate([top, u, bot], axis=0)


def _pipeline_steps(s, n_chunks, lag, ring, normalized, staged, order, normalize, project, finish):
    fin_slot = lax.rem(s + (2 * ring - 1 - lag), jnp.int32(ring))

    @pl.when((s >= 1) & (s <= n_chunks))
    def _publish_normalized():
        h_w, h_s = normalized
        h_s[...] = h_w[...]

    @pl.when((s >= 2) & (s <= n_chunks + 1))
    def _publish_projected():
        slot = lax.rem(s + (ring - 2), jnp.int32(ring))
        for stage_ref, ring_ref in staged:
            ring_ref[slot] = stage_ref[...]

    @pl.when(s == 0)
    def _first():
        _weave(order, normalize=normalize([]))

    @pl.when((s >= 1) & (s <= lag))
    def _fill():
        _weave(order, project=project([]), normalize=normalize([]))

    @pl.when((s > lag) & (s <= n_chunks))
    def _steady():
        tokens = []
        _weave(order, finish=finish(fin_slot, tokens), project=project(tokens), normalize=normalize(tokens))

    @pl.when(s > n_chunks)
    def _drain():
        _weave(order, finish=finish(fin_slot, []))


def _adaln_kernel(cond_ref, we_ref, be_ref, wo_ref, bo_ref, oe_ref, oo_ref):
    s = _silu(cond_ref[...]).astype(BF16)
    oe_ref[...] = jnp.dot(s, we_ref[...].astype(BF16), preferred_element_type=F32) + be_ref[...]
    oo_ref[...] = jnp.dot(s, wo_ref[...].astype(BF16), preferred_element_type=F32) + bo_ref[...]


def _adaln(cond8, w_e, b_e, w_o, b_o):
    n = 3 * D_MODEL
    wspec = pl.BlockSpec((D_MODEL, ADA_TN), lambda j: (0, j))
    vspec = pl.BlockSpec((1, ADA_TN), lambda j: (0, j))
    ospec = pl.BlockSpec((8, ADA_TN), lambda j: (0, j))
    return pl.pallas_call(
        _adaln_kernel,
        out_shape=(jax.ShapeDtypeStruct((8, n), F32), jax.ShapeDtypeStruct((8, n), F32)),
        grid=(n // ADA_TN,),
        in_specs=[pl.BlockSpec((8, D_MODEL), lambda j: (0, 0)), wspec, vspec, wspec, vspec],
        out_specs=(ospec, ospec),
        compiler_params=pltpu.CompilerParams(dimension_semantics=("arbitrary",), vmem_limit_bytes=VMEM_LIMIT),
        name="adaln",
    )(cond8, w_e, b_e.reshape(1, n), w_o, b_o.reshape(1, n))


def _head_inv_rms(x, bd):
    sq = x * x
    hi = sq.astype(BF16)
    lo = (sq - hi.astype(F32)).astype(BF16)
    ms = (jnp.dot(hi, bd, preferred_element_type=F32) + jnp.dot(lo, bd, preferred_element_type=F32)) * (1.0 / HEAD_DIM)
    return lax.rsqrt(ms + EPS)


def _rope(x, cos, sin_signed, first_half):
    partner = jnp.where(first_half, pltpu.roll(x, LANES - ROPE_FREQS, axis=1), pltpu.roll(x, ROPE_FREQS, axis=1))
    return x * cos + partner * sin_signed


def _dup_kv(x):
    lane = lax.broadcasted_iota(jnp.int32, (1, LANES), 1)
    swapped = pltpu.roll(x, HEAD_DIM, axis=1)
    return jnp.where(lane < HEAD_DIM, x, swapped), jnp.where(lane < HEAD_DIM, swapped, x)


def _tiled_keys(k):
    ka, kb = _dup_kv(k)
    return jnp.concatenate([ka, ka, kb, kb], axis=1).astype(BF16)


def _values_t(v):
    va, vb = _dup_kv(v)
    return jnp.concatenate([va.T, vb.T], axis=0).astype(BF16)


def _even_kernel(*refs, nc, n_chunks, lag, ring, windowed):
    if windowed:
        (xp_ref, xa_ref, modp_ref, moda_ref, g_ref, win_ref, cw_ref, cb_ref, qg_ref, kg_ref, sink_ref, wout_ref,
         cos_ref, sin_ref, ck_ref, cv_ref,
         y_ref,
         h_s, h_w, u_s, gate_s, q_s, k_s, vt_s, gb_s, u_w, gate_w, q_w, k_w, vt_w, gb_w, a_s, ck_s, cvt_s) = refs
        ko_ref = vo_ref = None
    else:
        (xp_ref, xa_ref, modp_ref, moda_ref, g_ref, win_ref, cw_ref, cb_ref, qg_ref, kg_ref, sink_ref, wout_ref,
         y_ref, ko_ref, vo_ref,
         h_s, h_w, u_s, gate_s, q_s, k_s, vt_s, gb_s, u_w, gate_w, q_w, k_w, vt_w, gb_w, a_s) = refs
    s = pl.program_id(0)
    i = s - 1 - lag
    ci = lax.rem(i, jnp.int32(nc))

    @pl.when(s == 0)
    def _init():
        k_s[...] = jnp.zeros(k_s.shape, BF16)
        vt_s[...] = jnp.zeros(vt_s.shape, BF16)

    if windowed:
        @pl.when((i >= 0) & (ci == 0))
        def _load_ctx():
            ck_s[...] = _tiled_keys(ck_ref[0])
            cvt_s[...] = _values_t(cv_ref[0])

    def normalize(tokens):
        yield from _normalize_items(h_w, xp_ref, g_ref, modp_ref, tokens)

    def project(tokens):
        o_bg, o_cg, o_xs, o_ga, o_q, o_k, _, o_gb = EVEN_OFFS[:8]

        def proj(start):
            return jnp.dot(h_s[...], win_ref[:, start:start + NT], preferred_element_type=F32)

        lane = lax.broadcasted_iota(jnp.int32, (1, LANES), 1)
        first_half = (lane & (2 * ROPE_FREQS - 1)) < ROPE_FREQS
        ri = lax.broadcasted_iota(jnp.int32, (NT, NT), 0) >> HEAD_SHIFT
        cj = lax.broadcasted_iota(jnp.int32, (NT, NT), 1) >> HEAD_SHIFT
        bd = (ri == cj).astype(BF16)

        def qk_norm(x, g):
            w = x.shape[1]
            inv = _head_inv_rms(x, bd[0:w, 0:w])
            pieces = []
            for j in range(w // LANES):
                lanes = slice(j * LANES, (j + 1) * LANES)
                xj = x[:, lanes] * inv[:, lanes] * g
                pieces.append(_rope(xj, cos_ref[...], sin_ref[...], first_half) if windowed else xj)
            return pieces

        tile = _mxu_cost(TM, D_MODEL, NT)

        def keys_values():
            kv = proj(o_k)
            k, = qk_norm(kv[:, 0:KV_WIDTH], kg_ref[...])
            v = kv[:, KV_WIDTH:2 * KV_WIDTH]
            if not windowed:
                ko_ref[0] = k
                vo_ref[0] = v
            k_w[...] = _tiled_keys(k)
            vt_w[...] = _values_t(v)
            return v

        def queries(n):
            for j, qj in enumerate(qk_norm(proj(o_q + n * NT), qg_ref[...])):
                q_w[:, n * NT + j * LANES:n * NT + (j + 1) * LANES] = (qj * (ATTN_SCALE * LOG2E)).astype(BF16)
            return qj

        def conv_gate(n):
            gate = proj(o_bg + n * NT) * _silu(proj(o_ga + n * NT))
            gate_w[:, n * NT:(n + 1) * NT] = gate
            return gate

        def conv_input(n):
            u = proj(o_cg + n * NT) * proj(o_xs + n * NT)
            u_w[:, n * NT:(n + 1) * NT] = u
            return u

        def attn_gate(n):
            gb = _silu(proj(o_gb + n * NT))
            gb_w[:, n * NT:(n + 1) * NT] = gb
            return gb

        items = [((2 * tile, 180), conv_gate, 0), ((tile, 350), keys_values, None), ((2 * tile, 40), conv_input, 0),
                 ((tile, 300), queries, 0), ((2 * tile, 180), conv_gate, 1), ((tile, 300), queries, 1),
                 ((2 * tile, 40), conv_input, 1), ((tile, 110), attn_gate, 0), ((tile, 110), attn_gate, 1)]
        for cost, emit, n in items:
            yield cost
            emit() if n is None else emit(n)

    def finish(slot, tokens):
        if windowed:
            prev_slot = lax.rem(slot + (ring - 1), jnp.int32(ring))
            next_slot = lax.rem(slot + 1, jnp.int32(ring))
            has_prev = ci > 0
            has_next = ci < nc - 1
        else:
            prev_slot = next_slot = has_prev = has_next = None

        yield 0, 400
        ext = _with_halo(u_s, slot, slice(None), prev_slot, next_slot, has_prev, has_next)
        conv = (_rows_down(ext, 1)[HALO:HALO + TM] * cw_ref[0:1, :]
                + ext[HALO:HALO + TM] * cw_ref[1:2, :]
                + _rows_up(ext, 1)[HALO:HALO + TM] * cw_ref[2:3, :]
                + cb_ref[...])
        a_s[:, 0:CONV_WIDTH] = (gate_s[slot] * conv).astype(BF16)

        lane_head = lax.broadcasted_iota(jnp.int32, (1, GROUP_W), 1) >> HEAD_SHIFT
        cols4 = Q_PER_KV * BLOCK
        n_keys = 3 * BLOCK + ck_s.shape[0] if windowed else TM
        key_i =lax.broadcasted_iota(jnp.int32, (BLOCK, cols4), 0)
        qry_i = lax.broadcasted_iota(jnp.int32, (BLOCK, cols4), 1) & (BLOCK - 1)
        upper = lax.broadcasted_iota(jnp.int32, (LANES, LANES), 0) < HEAD_DIM
        halves = (slice(0, BLOCK), slice(BLOCK, TM))
        for qb in range(TM // BLOCK):
            own = halves[qb]
            if windowed:
                if qb == 0:
                    blocks = ((prev_slot, halves[1]), (slot, halves[0]), (slot, halves[1]))
                    prev_in, next_in = has_prev, True
                else:
                    blocks = ((slot, halves[0]), (slot, halves[1]), (next_slot, halves[0]))
                    prev_in, next_in = True, has_next
                prev_ok = (key_i - qry_i) >= jnp.where(prev_in, 0, BLOCK)
                next_ok = (qry_i - key_i) >= jnp.where(next_in, 0, BLOCK)
            for kv in range(N_KV_HEADS):
                cols = slice(kv * GROUP_W, (kv + 1) * GROUP_W)
                vrows = slice(kv * LANES, (kv + 1) * LANES)
                yield _mxu_cost(n_keys, GROUP_W, cols4), 60
                qblk = q_s[slot, own, cols]
                q4 = jnp.concatenate(
                    [jnp.where(lane_head == g, qblk, jnp.zeros_like(qblk)) for g in range(Q_PER_KV)], axis=0)

                def scores_t(kk):
                    return lax.dot_general(kk, q4, (((1,), (1,)), ((), ())), preferred_element_type=F32)

                if windowed:
                    s_loc = scores_t(jnp.concatenate([k_s[sl, r, cols] for sl, r in blocks], axis=0))
                    vt_loc = jnp.concatenate([vt_s[sl, vrows, r] for sl, r in blocks], axis=1)
                    s_ctx = scores_t(ck_s[:, cols])
                    yield 0, n_keys * cols4 * 5 // (8 * LANES * 4)
                    groups = [([jnp.where(prev_ok, s_loc[0:BLOCK], NEG), s_loc[BLOCK:2 * BLOCK],
                                jnp.where(next_ok, s_loc[2 * BLOCK:3 * BLOCK], NEG)], vt_loc),
                              ([s_ctx], cvt_s[vrows, :])]
                else:
                    s_own = scores_t(k_s[slot, :, cols])
                    yield 0, n_keys * cols4 * 5 // (8 * LANES * 4)
                    groups = [([s_own], vt_s[slot, vrows, :])]
                snk = jnp.concatenate(
                    [jnp.full((1, BLOCK), sink_ref[kv * Q_PER_KV + g] * LOG2E, F32) for g in range(Q_PER_KV)], axis=1)
                if tokens:
                    token = tokens.pop(0)
                    while tokens:
                        token = token + tokens.pop(0)
                    snk = snk + jnp.concatenate([token] * Q_PER_KV, axis=1)
                mx = snk
                for ss, _ in groups:
                    for sc in ss:
                        mx = jnp.maximum(mx, jnp.max(sc, axis=0, keepdims=True))
                den = jnp.exp2(snk - mx)
                probs = []
                for ss, vt in groups:
                    es = []
                    for sc in ss:
                        e = jnp.exp2(sc - mx)
                        den = den + jnp.sum(e, axis=0, keepdims=True)
                        es.append(e.astype(BF16))
                    probs.append((es[0] if len(es) == 1 else jnp.concatenate(es, axis=0), vt))
                yield _mxu_cost(LANES, n_keys, cols4), 120
                o_t = jnp.zeros((LANES, cols4), F32)
                for e_all, vt in probs:
                    o_t = o_t + jnp.dot(vt, e_all, preferred_element_type=F32)
                o_t = o_t * (1.0 / den)
                o01 = jnp.where(upper, o_t[:, 0:BLOCK], o_t[:, BLOCK:2 * BLOCK]).T
                o23 = jnp.where(upper, o_t[:, 2 * BLOCK:3 * BLOCK], o_t[:, 3 * BLOCK:4 * BLOCK]).T
                a_s[own, CONV_WIDTH + kv * GROUP_W:CONV_WIDTH + (kv + 1) * GROUP_W] = (
                    jnp.concatenate([o01, o23], axis=1) * gb_s[slot, own, cols]).astype(BF16)

        yield from _out_proj_items(lambda: a_s[...], wout_ref, xa_ref, moda_ref, y_ref)

    staged = ((u_w, u_s), (gate_w, gate_s), (q_w, q_s), (k_w, k_s), (vt_w, vt_s), (gb_w, gb_s))
    _pipeline_steps(s, n_chunks, lag, ring, (h_w, h_s), staged, EVEN_ORDER, normalize, project, finish)


def _chunk_specs(n_chunks, nc, lag, per_seq_mod, tm=TM):
    last = n_chunks - 1
    norm_chunk = lambda s: jnp.minimum(s, last)
    proj_chunk = lambda s: jnp.clip(s - 1, 0, last)
    fin_chunk = lambda s: jnp.clip(s - 1 - lag, 0, last)
    seq_of = (lambda j: j // nc) if per_seq_mod else (lambda j: 0)
    return dict(
        x_norm=pl.BlockSpec((1, tm, D_MODEL), lambda s: (norm_chunk(s), 0, 0)),
        x_fin=pl.BlockSpec((1, tm, D_MODEL), lambda s: (fin_chunk(s), 0, 0)),
        mod_norm=pl.BlockSpec((1, 1, 3 * D_MODEL), lambda s: (seq_of(norm_chunk(s)), 0, 0)),
        mod_fin=pl.BlockSpec((1, 1, 3 * D_MODEL), lambda s: (seq_of(fin_chunk(s)), 0, 0)),
        proj_chunk=proj_chunk, fin_chunk=fin_chunk)


def _const_spec(shape):
    return pl.BlockSpec(shape, lambda c: (0,) * len(shape))


def _even_layer(x, mod, norm_g, w_in, conv_w, conv_b, q_g, k_g, sink, w_out, rope=None, ctx=None):
    n_seq, seq_len, _ = x.shape
    windowed = rope is not None
    nc = seq_len // TM
    n_chunks = n_seq * nc
    lag, ring = (2, 4) if windowed else (1, 2)
    assert windowed or nc == 1
    sp = _chunk_specs(n_chunks, nc, lag, windowed)
    xc = x.reshape(n_chunks, TM, D_MODEL)
    in_specs = [
        sp["x_norm"], sp["x_fin"], sp["mod_norm"], sp["mod_fin"],
        _const_spec((1, D_MODEL)),
        _const_spec((D_MODEL, EVEN_IN)),
        _const_spec((3, CONV_WIDTH)),
        _const_spec((1, CONV_WIDTH)),
        _const_spec((1, LANES)),
        _const_spec((1, LANES)),
        pl.BlockSpec(memory_space=pltpu.SMEM),
        _const_spec((D_MODEL, D_MODEL)),
    ]
    args = [xc, xc, mod, mod, norm_g.reshape(1, D_MODEL), w_in, conv_w, conv_b.reshape(1, CONV_WIDTH),
            jnp.tile(q_g, LANES // HEAD_DIM).reshape(1, LANES), jnp.tile(k_g, LANES // HEAD_DIM).reshape(1, LANES),
            sink, w_out]
    per_chunk = [
        ((TM, CONV_WIDTH), F32),
        ((TM, CONV_WIDTH), F32),
        ((TM, ATTN_WIDTH), BF16),
        ((TM, 2 * GROUP_W), BF16),
        ((2 * LANES, TM), BF16),
        ((TM, ATTN_WIDTH), F32),
    ]
    scratch = (
        [pltpu.VMEM((TM, D_MODEL), BF16)] * 2
        + [pltpu.VMEM((ring,) + shape, dt) for shape, dt in per_chunk]
        + [pltpu.VMEM(shape, dt) for shape, dt in per_chunk]
        + [pltpu.VMEM((TM, D_MODEL), BF16)])
    y_shape = jax.ShapeDtypeStruct(xc.shape, F32)
    y_spec = pl.BlockSpec((1, TM, D_MODEL), lambda c: (sp["fin_chunk"](c), 0, 0))
    if windowed:
        cos, sin = rope
        ck, cv = ctx
        rope_spec = pl.BlockSpec((TM, LANES), lambda c: (sp["proj_chunk"](c) % nc, 0))
        ctx_spec = pl.BlockSpec((1,) + ck.shape[1:], lambda c: (sp["fin_chunk"](c) // nc, 0, 0))
        in_specs += [rope_spec, rope_spec, ctx_spec, ctx_spec]
        args += [cos, sin, ck, cv]
        scratch += [pltpu.VMEM((ck.shape[1], 2 * GROUP_W), BF16), pltpu.VMEM((2 * LANES, cv.shape[1]), BF16)]
        out_shape, out_specs = y_shape, y_spec
    else:
        kv_shape = jax.ShapeDtypeStruct((n_chunks, TM, KV_WIDTH), F32)
        kv_spec = pl.BlockSpec((1, TM, KV_WIDTH), lambda c: (sp["proj_chunk"](c), 0, 0))
        out_shape, out_specs = (y_shape, kv_shape, kv_shape), (y_spec, kv_spec, kv_spec)
    out = pl.pallas_call(
        functools.partial(_even_kernel, nc=nc, n_chunks=n_chunks, lag=lag, ring=ring, windowed=windowed),
        out_shape=out_shape,
        grid=(n_chunks + lag + 1,),
        in_specs=in_specs,
        out_specs=out_specs,
        scratch_shapes=scratch,
        compiler_params=PIPELINE_PARAMS,
        name="even_latent" if windowed else "even_context",
    )(*args)
    if windowed:
        return out.reshape(x.shape)
    y, k, v = out
    return y.reshape(x.shape), k.reshape(n_seq, seq_len, KV_WIDTH), v.reshape(n_seq, seq_len, KV_WIDTH)


def _odd_kernel(xp_ref, xa_ref, modp_ref, moda_ref, g_ref, win_ref, pw_ref, ps_ref, wout_ref, y_ref,
                h_s, h_w, u_s, sg_s, u_w, sg_w, *, nc, n_chunks, lag, ring, order):
    tm = h_s.shape[0]
    s = pl.program_id(0)
    i = s - 1 - lag
    ci = lax.rem(i, jnp.int32(nc))

    def normalize(tokens):
        yield from _normalize_items(h_w, xp_ref, g_ref, modp_ref, tokens)

    def project(tokens):
        tile = _mxu_cost(tm, D_MODEL, NT)
        for n in range(D_MODEL // NT):
            yield tile, 10
            cols = slice(n * NT, (n + 1) * NT)
            u = jnp.dot(h_s[...], win_ref[:, cols], preferred_element_type=F32)
            u_w[:, cols] = u
        for n in range(D_MODEL // NT):
            yield tile, 110
            cols = slice(n * NT, (n + 1) * NT)
            sg = _silu(jnp.dot(h_s[...], win_ref[:, D_MODEL + n * NT:D_MODEL + (n + 1) * NT],
                               preferred_element_type=F32))
            sg_w[:, cols] = sg

    def finish(slot, tokens):
        if nc > 1:
            prev_slot = lax.rem(slot + (ring - 1), jnp.int32(ring))
            next_slot = lax.rem(slot + 1, jnp.int32(ring))
            has_prev = ci > 0
            has_next = ci < nc - 1
        else:
            prev_slot = next_slot = has_prev = has_next = None
        t = ci * tm + lax.broadcasted_iota(jnp.int32, (tm, 1), 0)
        outs = []
        for gi, w in enumerate(POOL_SIZES):
            yield _mxu_cost(tm, POOL_GROUP, POOL_GROUP), 140 + 120 * gi
            cols = slice(gi * POOL_GROUP, (gi + 1) * POOL_GROUP)
            ext = _with_halo(u_s, slot, cols, prev_slot, next_slot, has_prev, has_next)
            acc = ext
            span = 1
            while span < w:
                acc = acc + _rows_down(acc, span)
                span *= 2
            if w > 2:
                acc = _rows_up(acc, w // 2 - 1)
            cnt = (jnp.minimum(t + w // 2, nc * tm) - jnp.maximum(t - w // 2, 0)).astype(F32)
            while tokens:
                cnt = cnt + tokens.pop(0)[:, 0:1]
            pooled = acc[HALO:HALO + tm] / cnt - ext[HALO:HALO + tm]
            outs.append(jnp.dot(pooled.astype(BF16), pw_ref[gi], preferred_element_type=F32))
        yield 0, 130
        y = (jnp.concatenate(outs, axis=1) * ps_ref[...] * sg_s[slot]).astype(BF16)
        yield from _out_proj_items(lambda: y, wout_ref, xa_ref, moda_ref, y_ref)

    _pipeline_steps(s, n_chunks, lag, ring, (h_w, h_s), ((u_w, u_s), (sg_w, sg_s)), order,
                    normalize, project, finish)


def _odd_layer(x, mod, per_seq_mod, norm_g, w_in, pool_w, pool_scale, w_out):
    n_seq, seq_len, _ = x.shape
    tm = TM_LONG if seq_len % TM_LONG == 0 else TM
    nc = seq_len // tm
    n_chunks = n_seq * nc
    lag, ring = (2, 3) if nc > 1 else (1, 2)
    order = ("P" + "N" * (tm // TM) + "FP") * 4 + "FFFFF"
    sp = _chunk_specs(n_chunks, nc, lag, per_seq_mod, tm)
    xc = x.reshape(n_chunks, tm, D_MODEL)
    in_specs = [
        sp["x_norm"], sp["x_fin"], sp["mod_norm"], sp["mod_fin"],
        _const_spec((1, D_MODEL)),
        _const_spec((D_MODEL, 2 * D_MODEL)),
        _const_spec(pool_w.shape),
        _const_spec((1, D_MODEL)),
        _const_spec((D_MODEL, D_MODEL)),
    ]
    out = pl.pallas_call(
        functools.partial(_odd_kernel, nc=nc, n_chunks=n_chunks, lag=lag, ring=ring, order=order),
        out_shape=jax.ShapeDtypeStruct(xc.shape, F32),
        grid=(n_chunks + lag + 1,),
        in_specs=in_specs,
        out_specs=pl.BlockSpec((1, tm, D_MODEL), lambda c: (sp["fin_chunk"](c), 0, 0)),
        scratch_shapes=[
            pltpu.VMEM((tm, D_MODEL), BF16),
            pltpu.VMEM((tm, D_MODEL), BF16),
            pltpu.VMEM((ring, tm, D_MODEL), F32),
            pltpu.VMEM((ring, tm, D_MODEL), F32),
            pltpu.VMEM((tm, D_MODEL), F32),
            pltpu.VMEM((tm, D_MODEL), F32),
        ],
        compiler_params=PIPELINE_PARAMS,
        name="odd_latent" if per_seq_mod else "odd_context",
    )(xc, xc, mod, mod, norm_g.reshape(1, D_MODEL), w_in, pool_w, pool_scale.reshape(1, D_MODEL), w_out)
    return out.reshape(x.shape)


def _rope_tables(seq_len):
    n_rows = seq_len // GRID_W
    row = np.repeat(np.arange(n_rows), GRID_W).astype(np.float32)
    col = np.tile(np.arange(GRID_W), n_rows).astype(np.float32)
    inv = (np.float32(ROPE_BASE) ** (-np.arange(ROPE_FREQS, dtype=np.float32) / np.float32(ROPE_FREQS))).astype(np.float32)
    ang = np.stack([row[:, None] * inv, col[:, None] * inv], axis=1)
    cos = np.broadcast_to(np.cos(ang)[:, :, None, :], (seq_len, 2, 2, ROPE_FREQS))
    sin = np.sin(ang)[:, :, None, :] * np.array([-1.0, 1.0], np.float32)[None, None, :, None]
    cos = cos.reshape(seq_len, HEAD_DIM).astype(np.float32)
    sin = sin.reshape(seq_len, HEAD_DIM).astype(np.float32)
    return jnp.asarray(np.tile(cos, (1, LANES // HEAD_DIM))), jnp.asarray(np.tile(sin, (1, LANES // HEAD_DIM)))


def kernel(x_prompt, x_sample, cache_k, cache_v, c, c_ctx, ada_w_e, ada_b_e, norm_g_e, w_in_e, conv_w, conv_b,
           q_norm_g, k_norm_g, sink, w_out_e, ada_w_o, ada_b_o, norm_g_o, w_in_o, pool_w, pool_scale, w_out_o):
    n_dec = x_sample.shape[0]
    depth = ada_w_e.shape[0] + ada_w_o.shape[0]
    assert ada_w_e.shape[0] == 1 and ada_w_o.shape[0] == 1 and n_dec + 1 <= 8
    cond8 = jnp.concatenate([c_ctx[None], c, jnp.zeros((8 - 1 - n_dec, D_MODEL), F32)], axis=0)
    mod_e, mod_o = _adaln(cond8, ada_w_e[0], ada_b_e[0], ada_w_o[0], ada_b_o[0])
    rope = _rope_tables(x_sample.shape[1])

    yp, ys = x_prompt, x_sample
    new_k, new_v = [], []
    for layer in range(depth):
        i = layer // 2
        if layer % 2 == 0:
            mod = mod_e.reshape(8, 1, 3 * D_MODEL)
            w_in = w_in_e[i].astype(BF16)
            w_out = w_out_e[i].astype(BF16)
            common = (norm_g_e[i], w_in, conv_w[i], conv_b[i], q_norm_g[i], k_norm_g[i], sink[i], w_out)
            yp, k, v = _even_layer(yp, mod[0:1], *common)
            new_k.append(k.reshape(k.shape[0], k.shape[1], N_KV_HEADS, HEAD_DIM))
            new_v.append(v.reshape(v.shape[0], v.shape[1], N_KV_HEADS, HEAD_DIM))
            ck = cache_k[:, i].reshape(n_dec, cache_k.shape[2], KV_WIDTH)
            cv = cache_v[:, i].reshape(n_dec, cache_v.shape[2], KV_WIDTH)
            ys = _even_layer(ys, mod[1:1 + n_dec], *common, rope=rope, ctx=(ck, cv))
        else:
            mod = mod_o.reshape(8, 1, 3 * D_MODEL)
            common = (norm_g_o[i], w_in_o[i].astype(BF16), pool_w[i].astype(BF16), pool_scale[i],
                      w_out_o[i].astype(BF16))
            yp = _odd_layer(yp, mod[0:1], False, *common)
            ys = _odd_layer(ys, mod[1:1 + n_dec], True, *common)
    return yp, ys, jnp.stack(new_k, axis=1), jnp.stack(new_v, axis=1)
```

```python
import functools

import jax
import jax.numpy as jnp
import numpy as np
from jax import lax
from jax.experimental import pallas as pl
from jax.experimental.pallas import tpu as pltpu

F32 = jnp.float32
BF16 = jnp.bfloat16

D_MODEL = 1024
GRID_W = 64
HEAD_DIM = 64
HEAD_SHIFT = 6
ATTN_WIDTH = D_MODEL // 2
N_Q_HEADS = ATTN_WIDTH // HEAD_DIM
N_KV_HEADS = N_Q_HEADS // 4
Q_PER_KV = N_Q_HEADS // N_KV_HEADS
KV_WIDTH = N_KV_HEADS * HEAD_DIM
CONV_WIDTH = D_MODEL - ATTN_WIDTH
WINDOW = 128
BLOCK = 128
ROPE_BASE = 10000.0
ROPE_FREQS = HEAD_DIM // 4
ATTN_SCALE = HEAD_DIM ** -0.5
LOG2E = float(np.log2(np.e))
NEG = -1e30
POOL_SIZES = (2, 4, 8, 16)
POOL_GROUP = D_MODEL // len(POOL_SIZES)
EPS = 1e-6
EVEN_SIZES = (CONV_WIDTH, CONV_WIDTH, CONV_WIDTH, CONV_WIDTH, ATTN_WIDTH, KV_WIDTH, KV_WIDTH, ATTN_WIDTH)
EVEN_IN = sum(EVEN_SIZES)
EVEN_OFFS = tuple(int(s) for s in np.cumsum((0,) + EVEN_SIZES))

TM = 256
NT = 256
NORM_ROWS = 64
EVEN_ORDER_LATENT = "PF" + "F" + "PFNFPF" * 3 + "PNFPF" + "FFFF"
EVEN_ORDER_CONTEXT = "PF" + "F" + "FNPFPF" * 3 + "NPFPF" + "FFFF"
ODD_ORDER = "PNFP" * 4 + "FFFFF"
LANES = 128
GROUP_W = Q_PER_KV * HEAD_DIM
HALO = 8
ADA_TN = 768
VMEM_LIMIT = 56 * 1024 * 1024
PIPELINE_PARAMS = pltpu.CompilerParams(dimension_semantics=("arbitrary",), vmem_limit_bytes=VMEM_LIMIT)


def _silu(x):
    return x / (1.0 + jnp.exp(-x))


def _rows_down(x, k):
    return pltpu.roll(x, k, axis=0)


def _rows_up(x, k):
    return pltpu.roll(x, x.shape[0] - k, axis=0)


def _rms_modulate(x, g, shift, scale):
    ms = jnp.mean(x * x, axis=-1, keepdims=True)
    return (x * lax.rsqrt(ms + EPS) * g) * (1.0 + scale) + shift


def _opaque_zero(x):
    bits = pltpu.bitcast(x, jnp.uint32)
    return pltpu.bitcast((bits >> 16) >> 16, F32)


def _normalize_items(h_s, x_ref, g_ref, mod_ref, tokens):
    shift = mod_ref[0, :, 0:D_MODEL]
    scale = mod_ref[0, :, D_MODEL:2 * D_MODEL]
    for r in range(0, h_s.shape[0], NORM_ROWS):
        yield
        h = _rms_modulate(x_ref[0, r:r + NORM_ROWS], g_ref[...], shift, scale)
        h_s[r:r + NORM_ROWS] = h.astype(BF16)
        tokens.append(_opaque_zero(h[0:8, 0:LANES])[0:1])


def _out_proj_items(get_a, wout_ref, x_ref, mod_ref, y_ref):
    for n in range(D_MODEL // NT):
        yield
        cols = slice(n * NT, (n + 1) * NT)
        mixed = jnp.dot(get_a(), wout_ref[:, cols], preferred_element_type=F32)
        y_ref[0, :, cols] = x_ref[0, :, cols] + mod_ref[0, :, 2 * D_MODEL + n * NT:2 * D_MODEL + (n + 1) * NT] * mixed


def _weave(order, **stages):
    by_letter = {name[0].upper(): stage for name, stage in stages.items()}
    for stage in by_letter.values():
        next(stage, None)
    for letter in order:
        if letter in by_letter:
            next(by_letter[letter], None)
    for stage in by_letter.values():
        for _ in stage:
            pass


def _with_halo(u_s, slot, cols, prev_slot, next_slot, has_prev, has_next):
    u = u_s[slot, :, cols]
    zero = jnp.zeros((HALO, u.shape[1]), F32)
    if prev_slot is None:
        top = bot = zero
    else:
        rows = u_s.shape[1]
        top = jnp.where(has_prev, u_s[prev_slot, rows - HALO:rows, cols], zero)
        bot = jnp.where(has_next, u_s[next_slot, 0:HALO, cols], zero)
    return jnp.concatenate([top, u, bot], axis=0)


def _pipeline_steps(s, n_chunks, lag, ring, normalized, staged, order, normalize, project, finish):
    fin_slot = lax.rem(s + (2 * ring - 1 - lag), jnp.int32(ring))

    @pl.when((s >= 1) & (s <= n_chunks))
    def _publish_normalized():
        h_w, h_s = normalized
        h_s[...] = h_w[...]

    @pl.when((s >= 2) & (s <= n_chunks + 1))
    def _publish_projected():
        slot = lax.rem(s + (ring - 2), jnp.int32(ring))
        for stage_ref, ring_ref in staged:
            ring_ref[slot] = stage_ref[...]

    @pl.when(s == 0)
    def _first():
        _weave(order, normalize=normalize([]))

    @pl.when((s >= 1) & (s <= lag))
    def _fill():
        _weave(order, project=project(), normalize=normalize([]))

    @pl.when((s > lag) & (s <= n_chunks))
    def _steady():
        tokens = []
        _weave(order, finish=finish(fin_slot, tokens), project=project(), normalize=normalize(tokens))

    @pl.when(s > n_chunks)
    def _drain():
        _weave(order, finish=finish(fin_slot, []))


def _adaln_kernel(cond_ref, we_ref, be_ref, wo_ref, bo_ref, oe_ref, oo_ref):
    s = _silu(cond_ref[...]).astype(BF16)
    oe_ref[...] = jnp.dot(s, we_ref[...].astype(BF16), preferred_element_type=F32) + be_ref[...]
    oo_ref[...] = jnp.dot(s, wo_ref[...].astype(BF16), preferred_element_type=F32) + bo_ref[...]


def _adaln(cond8, w_e, b_e, w_o, b_o):
    n = 3 * D_MODEL
    wspec = pl.BlockSpec((D_MODEL, ADA_TN), lambda j: (0, j))
    vspec = pl.BlockSpec((1, ADA_TN), lambda j: (0, j))
    ospec = pl.BlockSpec((8, ADA_TN), lambda j: (0, j))
    return pl.pallas_call(
        _adaln_kernel,
        out_shape=(jax.ShapeDtypeStruct((8, n), F32), jax.ShapeDtypeStruct((8, n), F32)),
        grid=(n // ADA_TN,),
        in_specs=[pl.BlockSpec((8, D_MODEL), lambda j: (0, 0)), wspec, vspec, wspec, vspec],
        out_specs=(ospec, ospec),
        compiler_params=pltpu.CompilerParams(dimension_semantics=("arbitrary",), vmem_limit_bytes=VMEM_LIMIT),
        name="adaln",
    )(cond8, w_e, b_e.reshape(1, n), w_o, b_o.reshape(1, n))


def _head_inv_rms(x, bd):
    sq = x * x
    hi = sq.astype(BF16)
    lo = (sq - hi.astype(F32)).astype(BF16)
    ms = (jnp.dot(hi, bd, preferred_element_type=F32) + jnp.dot(lo, bd, preferred_element_type=F32)) * (1.0 / HEAD_DIM)
    return lax.rsqrt(ms + EPS)


def _rope(x, cos, sin_signed, first_half):
    partner = jnp.where(first_half, pltpu.roll(x, LANES - ROPE_FREQS, axis=1), pltpu.roll(x, ROPE_FREQS, axis=1))
    return x * cos + partner * sin_signed


def _dup_kv(x):
    lane = lax.broadcasted_iota(jnp.int32, (1, LANES), 1)
    swapped = pltpu.roll(x, HEAD_DIM, axis=1)
    return jnp.where(lane < HEAD_DIM, x, swapped), jnp.where(lane < HEAD_DIM, swapped, x)


def _tiled_keys(k):
    ka, kb = _dup_kv(k)
    return jnp.concatenate([ka, ka, kb, kb], axis=1).astype(BF16)


def _values_t(v):
    va, vb = _dup_kv(v)
    return jnp.concatenate([va.T, vb.T], axis=0).astype(BF16)


def _even_kernel(*refs, nc, n_chunks, lag, ring, windowed):
    if windowed:
        (xp_ref, xa_ref, modp_ref, moda_ref, g_ref, win_ref, cw_ref, cb_ref, qg_ref, kg_ref, sink_ref, wout_ref,
         cos_ref, sin_ref, ck_ref, cv_ref,
         y_ref,
         h_s, h_w, u_s, gate_s, q_s, k_s, vt_s, gb_s, u_w, gate_w, q_w, k_w, vt_w, gb_w, a_s, ck_s, cvt_s) = refs
        ko_ref = vo_ref = None
    else:
        (xp_ref, xa_ref, modp_ref, moda_ref, g_ref, win_ref, cw_ref, cb_ref, qg_ref, kg_ref, sink_ref, wout_ref,
         y_ref, ko_ref, vo_ref,
         h_s, h_w, u_s, gate_s, q_s, k_s, vt_s, gb_s, u_w, gate_w, q_w, k_w, vt_w, gb_w, a_s) = refs
    s = pl.program_id(0)
    i = s - 1 - lag
    ci = lax.rem(i, jnp.int32(nc))

    @pl.when(s == 0)
    def _init():
        k_s[...] = jnp.zeros(k_s.shape, BF16)
        vt_s[...] = jnp.zeros(vt_s.shape, BF16)

    if windowed:
        @pl.when((i >= 0) & (ci == 0))
        def _load_ctx():
            ck_s[...] = _tiled_keys(ck_ref[0])
            cvt_s[...] = _values_t(cv_ref[0])

    def normalize(tokens):
        yield from _normalize_items(h_w, xp_ref, g_ref, modp_ref, tokens)

    def project():
        o_bg, o_cg, o_xs, o_ga, o_q, o_k, _, o_gb = EVEN_OFFS[:8]

        def proj(start):
            return jnp.dot(h_s[...], win_ref[:, start:start + NT], preferred_element_type=F32)

        lane = lax.broadcasted_iota(jnp.int32, (1, LANES), 1)
        first_half = (lane & (2 * ROPE_FREQS - 1)) < ROPE_FREQS
        ri = lax.broadcasted_iota(jnp.int32, (NT, NT), 0) >> HEAD_SHIFT
        cj = lax.broadcasted_iota(jnp.int32, (NT, NT), 1) >> HEAD_SHIFT
        bd = (ri == cj).astype(BF16)

        def qk_norm(x, g):
            w = x.shape[1]
            inv = _head_inv_rms(x, bd[0:w, 0:w])
            pieces = []
            for j in range(w // LANES):
                lanes = slice(j * LANES, (j + 1) * LANES)
                xj = x[:, lanes] * inv[:, lanes] * g
                pieces.append(_rope(xj, cos_ref[...], sin_ref[...], first_half) if windowed else xj)
            return pieces

        def keys_values(_):
            kv = proj(o_k)
            k, = qk_norm(kv[:, 0:KV_WIDTH], kg_ref[...])
            v = kv[:, KV_WIDTH:2 * KV_WIDTH]
            if not windowed:
                ko_ref[0] = k
                vo_ref[0] = v
            k_w[...] = _tiled_keys(k)
            vt_w[...] = _values_t(v)

        def queries(n):
            for j, qj in enumerate(qk_norm(proj(o_q + n * NT), qg_ref[...])):
                q_w[:, n * NT + j * LANES:n * NT + (j + 1) * LANES] = (qj * (ATTN_SCALE * LOG2E)).astype(BF16)

        def conv_gate(n):
            gate_w[:, n * NT:(n + 1) * NT] = proj(o_bg + n * NT) * _silu(proj(o_ga + n * NT))

        def conv_input(n):
            u_w[:, n * NT:(n + 1) * NT] = proj(o_cg + n * NT) * proj(o_xs + n * NT)

        def attn_gate(n):
            gb_w[:, n * NT:(n + 1) * NT] = _silu(proj(o_gb + n * NT))

        items = [(conv_gate, 0), (keys_values, 0), (conv_input, 0), (queries, 0), (conv_gate, 1), (queries, 1),
                 (conv_input, 1), (attn_gate, 0), (attn_gate, 1)]
        for emit, n in items:
            yield
            emit(n)

    def finish(slot, tokens):
        if windowed:
            prev_slot = lax.rem(slot + (ring - 1), jnp.int32(ring))
            next_slot = lax.rem(slot + 1, jnp.int32(ring))
            has_prev = ci > 0
            has_next = ci < nc - 1
        else:
            prev_slot = next_slot = has_prev = has_next = None

        yield
        ext = _with_halo(u_s, slot, slice(None), prev_slot, next_slot, has_prev, has_next)
        conv = (_rows_down(ext, 1)[HALO:HALO + TM] * cw_ref[0:1, :]
                + ext[HALO:HALO + TM] * cw_ref[1:2, :]
                + _rows_up(ext, 1)[HALO:HALO + TM] * cw_ref[2:3, :]
                + cb_ref[...])
        a_s[:, 0:CONV_WIDTH] = (gate_s[slot] * conv).astype(BF16)

        lane_head = lax.broadcasted_iota(jnp.int32, (1, GROUP_W), 1) >> HEAD_SHIFT
        cols4 = Q_PER_KV * BLOCK
        key_i =lax.broadcasted_iota(jnp.int32, (BLOCK, cols4), 0)
        qry_i = lax.broadcasted_iota(jnp.int32, (BLOCK, cols4), 1) & (BLOCK - 1)
        upper = lax.broadcasted_iota(jnp.int32, (LANES, LANES), 0) < HEAD_DIM
        halves = (slice(0, BLOCK), slice(BLOCK, TM))
        def block_masks(qb):
            if qb == 0:
                blocks = ((prev_slot, halves[1]), (slot, halves[0]), (slot, halves[1]))
                prev_in, next_in = has_prev, True
            else:
                blocks = ((slot, halves[0]), (slot, halves[1]), (next_slot, halves[0]))
                prev_in, next_in = True, has_next
            prev_ok = (key_i - qry_i) >= jnp.where(prev_in, 0, BLOCK)
            next_ok = (qry_i - key_i) >= jnp.where(next_in, 0, BLOCK)
            return blocks, prev_ok, next_ok

        def scores(qb, kv):
            cols = slice(kv * GROUP_W, (kv + 1) * GROUP_W)
            vrows = slice(kv * LANES, (kv + 1) * LANES)
            qblk = q_s[slot, halves[qb], cols]
            q4 = jnp.concatenate(
                [jnp.where(lane_head == g, qblk, jnp.zeros_like(qblk)) for g in range(Q_PER_KV)], axis=0)

            def scores_t(kk):
                return lax.dot_general(kk, q4, (((1,), (1,)), ((), ())), preferred_element_type=F32)

            if not windowed:
                return [([scores_t(k_s[slot, :, cols])], vt_s[slot, vrows, :])]
            blocks, prev_ok, next_ok = block_masks(qb)
            s_loc = scores_t(jnp.concatenate([k_s[sl, r, cols] for sl, r in blocks], axis=0))
            vt_loc = jnp.concatenate([vt_s[sl, vrows, r] for sl, r in blocks], axis=1)
            s_ctx = scores_t(ck_s[:, cols])
            return [([jnp.where(prev_ok, s_loc[0:BLOCK], NEG), s_loc[BLOCK:2 * BLOCK],
                      jnp.where(next_ok, s_loc[2 * BLOCK:3 * BLOCK], NEG)], vt_loc),
                    ([s_ctx], cvt_s[vrows, :])]

        def softmax(kv, groups):
            snk = jnp.concatenate(
                [jnp.full((1, BLOCK), sink_ref[kv * Q_PER_KV + g] * LOG2E, F32) for g in range(Q_PER_KV)], axis=1)
            if tokens:
                token = tokens.pop(0)
                while tokens:
                    token = token + tokens.pop(0)
                snk = snk + jnp.concatenate([token] * Q_PER_KV, axis=1)
            mx = snk
            for ss, _ in groups:
                for sc in ss:
                    mx = jnp.maximum(mx, jnp.max(sc, axis=0, keepdims=True))
            den = jnp.exp2(snk - mx)
            probs = []
            for ss, vt in groups:
                es = []
                for sc in ss:
                    e = jnp.exp2(sc - mx)
                    den = den + jnp.sum(e, axis=0, keepdims=True)
                    es.append(e.astype(BF16))
                probs.append((es[0] if len(es) == 1 else jnp.concatenate(es, axis=0), vt))
            return probs, den

        def values(qb, kv, probs, den):
            o_t = jnp.zeros((LANES, cols4), F32)
            for e_all, vt in probs:
                o_t = o_t + jnp.dot(vt, e_all, preferred_element_type=F32)
            o_t = o_t * (1.0 / den)
            o01 = jnp.where(upper, o_t[:, 0:BLOCK], o_t[:, BLOCK:2 * BLOCK]).T
            o23 = jnp.where(upper, o_t[:, 2 * BLOCK:3 * BLOCK], o_t[:, 3 * BLOCK:4 * BLOCK]).T
            a_s[halves[qb], CONV_WIDTH + kv * GROUP_W:CONV_WIDTH + (kv + 1) * GROUP_W] = (
                jnp.concatenate([o01, o23], axis=1) * gb_s[slot, halves[qb], kv * GROUP_W:(kv + 1) * GROUP_W]
            ).astype(BF16)

        pairs = [(qb, kv) for qb in range(TM // BLOCK) for kv in range(N_KV_HEADS)]
        yield
        ahead = scores(*pairs[0])
        for t, (qb, kv) in enumerate(pairs):
            groups = ahead
            if t + 1 < len(pairs):
                yield
                ahead = scores(*pairs[t + 1])
            yield
            probs, den = softmax(kv, groups)
            yield
            values(qb, kv, probs, den)

        yield from _out_proj_items(lambda: a_s[...], wout_ref, xa_ref, moda_ref, y_ref)

    staged = ((u_w, u_s), (gate_w, gate_s), (q_w, q_s), (k_w, k_s), (vt_w, vt_s), (gb_w, gb_s))
    order = EVEN_ORDER_LATENT if windowed else EVEN_ORDER_CONTEXT
    _pipeline_steps(s, n_chunks, lag, ring, (h_w, h_s), staged, order, normalize, project, finish)


def _chunk_specs(n_chunks, nc, lag, per_seq_mod, tm=TM):
    last = n_chunks - 1
    norm_chunk = lambda s: jnp.minimum(s, last)
    proj_chunk = lambda s: jnp.clip(s - 1, 0, last)
    fin_chunk = lambda s: jnp.clip(s - 1 - lag, 0, last)
    seq_of = (lambda j: j // nc) if per_seq_mod else (lambda j: 0)
    return dict(
        x_norm=pl.BlockSpec((1, tm, D_MODEL), lambda s: (norm_chunk(s), 0, 0)),
        x_fin=pl.BlockSpec((1, tm, D_MODEL), lambda s: (fin_chunk(s), 0, 0)),
        mod_norm=pl.BlockSpec((1, 1, 3 * D_MODEL), lambda s: (seq_of(norm_chunk(s)), 0, 0)),
        mod_fin=pl.BlockSpec((1, 1, 3 * D_MODEL), lambda s: (seq_of(fin_chunk(s)), 0, 0)),
        proj_chunk=proj_chunk, fin_chunk=fin_chunk)


def _const_spec(shape):
    return pl.BlockSpec(shape, lambda c: (0,) * len(shape))


def _even_layer(x, mod, norm_g, w_in, conv_w, conv_b, q_g, k_g, sink, w_out, rope=None, ctx=None):
    n_seq, seq_len, _ = x.shape
    windowed = rope is not None
    nc = seq_len // TM
    n_chunks = n_seq * nc
    lag, ring = (2, 4) if windowed else (1, 2)
    assert windowed or nc == 1
    sp = _chunk_specs(n_chunks, nc, lag, windowed)
    xc = x.reshape(n_chunks, TM, D_MODEL)
    in_specs = [
        sp["x_norm"], sp["x_fin"], sp["mod_norm"], sp["mod_fin"],
        _const_spec((1, D_MODEL)),
        _const_spec((D_MODEL, EVEN_IN)),
        _const_spec((3, CONV_WIDTH)),
        _const_spec((1, CONV_WIDTH)),
        _const_spec((1, LANES)),
        _const_spec((1, LANES)),
        pl.BlockSpec(memory_space=pltpu.SMEM),
        _const_spec((D_MODEL, D_MODEL)),
    ]
    args = [xc, xc, mod, mod, norm_g.reshape(1, D_MODEL), w_in, conv_w, conv_b.reshape(1, CONV_WIDTH),
            jnp.tile(q_g, LANES // HEAD_DIM).reshape(1, LANES), jnp.tile(k_g, LANES // HEAD_DIM).reshape(1, LANES),
            sink, w_out]
    per_chunk = [
        ((TM, CONV_WIDTH), F32),
        ((TM, CONV_WIDTH), F32),
        ((TM, ATTN_WIDTH), BF16),
        ((TM, 2 * GROUP_W), BF16),
        ((2 * LANES, TM), BF16),
        ((TM, ATTN_WIDTH), F32),
    ]
    scratch = (
        [pltpu.VMEM((TM, D_MODEL), BF16)] * 2
        + [pltpu.VMEM((ring,) + shape, dt) for shape, dt in per_chunk]
        + [pltpu.VMEM(shape, dt) for shape, dt in per_chunk]
        + [pltpu.VMEM((TM, D_MODEL), BF16)])
    y_shape = jax.ShapeDtypeStruct(xc.shape, F32)
    y_spec = pl.BlockSpec((1, TM, D_MODEL), lambda c: (sp["fin_chunk"](c), 0, 0))
    if windowed:
        cos, sin = rope
        ck, cv = ctx
        rope_spec = pl.BlockSpec((TM, LANES), lambda c: (sp["proj_chunk"](c) % nc, 0))
        ctx_spec = pl.BlockSpec((1,) + ck.shape[1:], lambda c: (sp["fin_chunk"](c) // nc, 0, 0))
        in_specs += [rope_spec, rope_spec, ctx_spec, ctx_spec]
        args += [cos, sin, ck, cv]
        scratch += [pltpu.VMEM((ck.shape[1], 2 * GROUP_W), BF16), pltpu.VMEM((2 * LANES, cv.shape[1]), BF16)]
        out_shape, out_specs = y_shape, y_spec
    else:
        kv_shape = jax.ShapeDtypeStruct((n_chunks, TM, KV_WIDTH), F32)
        kv_spec = pl.BlockSpec((1, TM, KV_WIDTH), lambda c: (sp["proj_chunk"](c), 0, 0))
        out_shape, out_specs = (y_shape, kv_shape, kv_shape), (y_spec, kv_spec, kv_spec)
    out = pl.pallas_call(
        functools.partial(_even_kernel, nc=nc, n_chunks=n_chunks, lag=lag, ring=ring, windowed=windowed),
        out_shape=out_shape,
        grid=(n_chunks + lag + 1,),
        in_specs=in_specs,
        out_specs=out_specs,
        scratch_shapes=scratch,
        compiler_params=PIPELINE_PARAMS,
        name="even_latent" if windowed else "even_context",
    )(*args)
    if windowed:
        return out.reshape(x.shape)
    y, k, v = out
    return y.reshape(x.shape), k.reshape(n_seq, seq_len, KV_WIDTH), v.reshape(n_seq, seq_len, KV_WIDTH)


def _odd_kernel(xp_ref, xa_ref, modp_ref, moda_ref, g_ref, win_ref, pw_ref, ps_ref, wout_ref, y_ref,
                h_s, h_w, u_s, sg_s, u_w, sg_w, *, nc, n_chunks, lag, ring):
    tm = h_s.shape[0]
    s = pl.program_id(0)
    i = s - 1 - lag
    ci = lax.rem(i, jnp.int32(nc))

    def normalize(tokens):
        yield from _normalize_items(h_w, xp_ref, g_ref, modp_ref, tokens)

    def project():
        for n in range(D_MODEL // NT):
            yield
            cols = slice(n * NT, (n + 1) * NT)
            u = jnp.dot(h_s[...], win_ref[:, cols], preferred_element_type=F32)
            u_w[:, cols] = u
        for n in range(D_MODEL // NT):
            yield
            cols = slice(n * NT, (n + 1) * NT)
            sg = _silu(jnp.dot(h_s[...], win_ref[:, D_MODEL + n * NT:D_MODEL + (n + 1) * NT],
                               preferred_element_type=F32))
            sg_w[:, cols] = sg

    def finish(slot, tokens):
        if nc > 1:
            prev_slot = lax.rem(slot + (ring - 1), jnp.int32(ring))
            next_slot = lax.rem(slot + 1, jnp.int32(ring))
            has_prev = ci > 0
            has_next = ci < nc - 1
        else:
            prev_slot = next_slot = has_prev = has_next = None
        t = ci * tm + lax.broadcasted_iota(jnp.int32, (tm, 1), 0)
        outs = []
        for gi, w in enumerate(POOL_SIZES):
            yield
            cols = slice(gi * POOL_GROUP, (gi + 1) * POOL_GROUP)
            ext = _with_halo(u_s, slot, cols, prev_slot, next_slot, has_prev, has_next)
            acc = ext
            span = 1
            while span < w:
                acc = acc + _rows_down(acc, span)
                span *= 2
            if w > 2:
                acc = _rows_up(acc, w // 2 - 1)
            cnt = (jnp.minimum(t + w // 2, nc * tm) - jnp.maximum(t - w // 2, 0)).astype(F32)
            while tokens:
                cnt = cnt + tokens.pop(0)[:, 0:1]
            pooled = acc[HALO:HALO + tm] / cnt - ext[HALO:HALO + tm]
            outs.append(jnp.dot(pooled.astype(BF16), pw_ref[gi], preferred_element_type=F32))
        yield
        y = (jnp.concatenate(outs, axis=1) * ps_ref[...] * sg_s[slot]).astype(BF16)
        yield from _out_proj_items(lambda: y, wout_ref, xa_ref, moda_ref, y_ref)

    _pipeline_steps(s, n_chunks, lag, ring, (h_w, h_s), ((u_w, u_s), (sg_w, sg_s)), ODD_ORDER,
                    normalize, project, finish)


def _odd_layer(x, mod, per_seq_mod, norm_g, w_in, pool_w, pool_scale, w_out):
    n_seq, seq_len, _ = x.shape
    tm = TM
    nc = seq_len // tm
    n_chunks = n_seq * nc
    lag, ring = (2, 4) if nc > 1 else (1, 2)
    sp = _chunk_specs(n_chunks, nc, lag, per_seq_mod, tm)
    xc = x.reshape(n_chunks, tm, D_MODEL)
    in_specs = [
        sp["x_norm"], sp["x_fin"], sp["mod_norm"], sp["mod_fin"],
        _const_spec((1, D_MODEL)),
        _const_spec((D_MODEL, 2 * D_MODEL)),
        _const_spec(pool_w.shape),
        _const_spec((1, D_MODEL)),
        _const_spec((D_MODEL, D_MODEL)),
    ]
    out = pl.pallas_call(
        functools.partial(_odd_kernel, nc=nc, n_chunks=n_chunks, lag=lag, ring=ring),
        out_shape=jax.ShapeDtypeStruct(xc.shape, F32),
        grid=(n_chunks + lag + 1,),
        in_specs=in_specs,
        out_specs=pl.BlockSpec((1, tm, D_MODEL), lambda c: (sp["fin_chunk"](c), 0, 0)),
        scratch_shapes=[
            pltpu.VMEM((tm, D_MODEL), BF16),
            pltpu.VMEM((tm, D_MODEL), BF16),
            pltpu.VMEM((ring, tm, D_MODEL), F32),
            pltpu.VMEM((ring, tm, D_MODEL), F32),
            pltpu.VMEM((tm, D_MODEL), F32),
            pltpu.VMEM((tm, D_MODEL), F32),
        ],
        compiler_params=PIPELINE_PARAMS,
        name="odd_latent" if per_seq_mod else "odd_context",
    )(xc, xc, mod, mod, norm_g.reshape(1, D_MODEL), w_in, pool_w, pool_scale.reshape(1, D_MODEL), w_out)
    return out.reshape(x.shape)


def _rope_tables(seq_len):
    n_rows = seq_len // GRID_W
    row = np.repeat(np.arange(n_rows), GRID_W).astype(np.float32)
    col = np.tile(np.arange(GRID_W), n_rows).astype(np.float32)
    inv = (np.float32(ROPE_BASE) ** (-np.arange(ROPE_FREQS, dtype=np.float32) / np.float32(ROPE_FREQS))).astype(np.float32)
    ang = np.stack([row[:, None] * inv, col[:, None] * inv], axis=1)
    cos = np.broadcast_to(np.cos(ang)[:, :, None, :], (seq_len, 2, 2, ROPE_FREQS))
    sin = np.sin(ang)[:, :, None, :] * np.array([-1.0, 1.0], np.float32)[None, None, :, None]
    cos = cos.reshape(seq_len, HEAD_DIM).astype(np.float32)
    sin = sin.reshape(seq_len, HEAD_DIM).astype(np.float32)
    return jnp.asarray(np.tile(cos, (1, LANES // HEAD_DIM))), jnp.asarray(np.tile(sin, (1, LANES // HEAD_DIM)))


def kernel(x_prompt, x_sample, cache_k, cache_v, c, c_ctx, ada_w_e, ada_b_e, norm_g_e, w_in_e, conv_w, conv_b,
           q_norm_g, k_norm_g, sink, w_out_e, ada_w_o, ada_b_o, norm_g_o, w_in_o, pool_w, pool_scale, w_out_o):
    n_dec = x_sample.shape[0]
    depth = ada_w_e.shape[0] + ada_w_o.shape[0]
    assert ada_w_e.shape[0] == 1 and ada_w_o.shape[0] == 1 and n_dec + 1 <= 8
    cond8 = jnp.concatenate([c_ctx[None], c, jnp.zeros((8 - 1 - n_dec, D_MODEL), F32)], axis=0)
    mod_e, mod_o = _adaln(cond8, ada_w_e[0], ada_b_e[0], ada_w_o[0], ada_b_o[0])
    rope = _rope_tables(x_sample.shape[1])

    yp, ys = x_prompt, x_sample
    new_k, new_v = [], []
    for layer in range(depth):
        i = layer // 2
        if layer % 2 == 0:
            mod = mod_e.reshape(8, 1, 3 * D_MODEL)
            w_in = w_in_e[i].astype(BF16)
            w_out = w_out_e[i].astype(BF16)
            common = (norm_g_e[i], w_in, conv_w[i], conv_b[i], q_norm_g[i], k_norm_g[i], sink[i], w_out)
            yp, k, v = _even_layer(yp, mod[0:1], *common)
            new_k.append(k.reshape(k.shape[0], k.shape[1], N_KV_HEADS, HEAD_DIM))
            new_v.append(v.reshape(v.shape[0], v.shape[1], N_KV_HEADS, HEAD_DIM))
            ck = cache_k[:, i].reshape(n_dec, cache_k.shape[2], KV_WIDTH)
            cv = cache_v[:, i].reshape(n_dec, cache_v.shape[2], KV_WIDTH)
            ys = _even_layer(ys, mod[1:1 + n_dec], *common, rope=rope, ctx=(ck, cv))
        else:
            mod = mod_o.reshape(8, 1, 3 * D_MODEL)
            common = (norm_g_o[i], w_in_o[i].astype(BF16), pool_w[i].astype(BF16), pool_scale[i],
                      w_out_o[i].astype(BF16))
            yp = _odd_layer(yp, mod[0:1], False, *common)
            ys = _odd_layer(ys, mod[1:1 + n_dec], True, *common)
    return yp, ys, jnp.stack(new_k, axis=1), jnp.stack(new_v, axis=1)
```

```python
import functools

import jax
import jax.numpy as jnp
import numpy as np
from jax import lax
from jax.experimental import pallas as pl
from jax.experimental.pallas import tpu as pltpu

F32 = jnp.float32
BF16 = jnp.bfloat16

D_MODEL = 1024
GRID_W = 64
HEAD_DIM = 64
HEAD_SHIFT = 6
ATTN_WIDTH = D_MODEL // 2
N_Q_HEADS = ATTN_WIDTH // HEAD_DIM
N_KV_HEADS = N_Q_HEADS // 4
Q_PER_KV = N_Q_HEADS // N_KV_HEADS
KV_WIDTH = N_KV_HEADS * HEAD_DIM
CONV_WIDTH = D_MODEL - ATTN_WIDTH
WINDOW = 128
BLOCK = 128
ROPE_BASE = 10000.0
ROPE_FREQS = HEAD_DIM // 4
ATTN_SCALE = HEAD_DIM ** -0.5
LOG2E = float(np.log2(np.e))
NEG = -1e30
POOL_SIZES = (2, 4, 8, 16)
POOL_GROUP = D_MODEL // len(POOL_SIZES)
EPS = 1e-6
EVEN_SIZES = (CONV_WIDTH, CONV_WIDTH, CONV_WIDTH, CONV_WIDTH, ATTN_WIDTH, KV_WIDTH, KV_WIDTH, ATTN_WIDTH)
EVEN_IN = sum(EVEN_SIZES)
EVEN_OFFS = tuple(int(s) for s in np.cumsum((0,) + EVEN_SIZES))

TM = 256
NT = 256
NORM_ROWS = 64
EVEN_ORDER_LATENT = "PF" + "F" + "PFNFPF" * 3 + "PNFPF" + "FFFF"
EVEN_ORDER_CONTEXT = "PF" + "F" + "FNPFPF" * 3 + "NPFPF" + "FFFF"
ODD_ORDER = "PNFP" * 4 + "FFFFF"
LANES = 128
GROUP_W = Q_PER_KV * HEAD_DIM
HALO = 8
ADA_TN = 768
VMEM_LIMIT = 56 * 1024 * 1024
PIPELINE_PARAMS = pltpu.CompilerParams(dimension_semantics=("arbitrary",), vmem_limit_bytes=VMEM_LIMIT)


def _silu(x):
    return x / (1.0 + jnp.exp(-x))


def _rows_down(x, k):
    return pltpu.roll(x, k, axis=0)


def _rows_up(x, k):
    return pltpu.roll(x, x.shape[0] - k, axis=0)


def _rms_modulate(x, g, shift, scale):
    ms = jnp.mean(x * x, axis=-1, keepdims=True)
    return (x * lax.rsqrt(ms + EPS) * g) * (1.0 + scale) + shift


def _cast_columns(src_ref, dst_ref):
    for n in range(src_ref.shape[-1] // NT):
        dst_ref[..., n * NT:(n + 1) * NT] = src_ref[..., n * NT:(n + 1) * NT].astype(dst_ref.dtype)


def _opaque_zero(x):
    bits = pltpu.bitcast(x, jnp.uint32)
    return pltpu.bitcast((bits >> 16) >> 16, F32)


def _normalize_items(h_s, x_ref, g_ref, mod_ref, tokens):
    shift = mod_ref[0, :, 0:D_MODEL]
    scale = mod_ref[0, :, D_MODEL:2 * D_MODEL]
    for r in range(0, h_s.shape[0], NORM_ROWS):
        yield
        h = _rms_modulate(x_ref[0, r:r + NORM_ROWS], g_ref[...], shift, scale)
        h_s[r:r + NORM_ROWS] = h.astype(BF16)
        tokens.append(_opaque_zero(h[0:8, 0:LANES])[0:1])


def _out_proj_items(get_a, wout_ref, x_ref, mod_ref, y_ref):
    for n in range(D_MODEL // NT):
        yield
        cols = slice(n * NT, (n + 1) * NT)
        mixed = jnp.dot(get_a(), wout_ref[:, cols], preferred_element_type=F32)
        y_ref[0, :, cols] = x_ref[0, :, cols] + mod_ref[0, :, 2 * D_MODEL + n * NT:2 * D_MODEL + (n + 1) * NT] * mixed


def _weave(order, **stages):
    by_letter = {name[0].upper(): stage for name, stage in stages.items()}
    for stage in by_letter.values():
        next(stage, None)
    for letter in order:
        if letter in by_letter:
            next(by_letter[letter], None)
    for stage in by_letter.values():
        for _ in stage:
            pass


def _with_halo(u_s, slot, cols, prev_slot, next_slot, has_prev, has_next):
    u = u_s[slot, :, cols]
    zero = jnp.zeros((HALO, u.shape[1]), F32)
    if prev_slot is None:
        top = bot = zero
    else:
        rows = u_s.shape[1]
        top = jnp.where(has_prev, u_s[prev_slot, rows - HALO:rows, cols], zero)
        bot = jnp.where(has_next, u_s[next_slot, 0:HALO, cols], zero)
    return jnp.concatenate([top, u, bot], axis=0)


def _pipeline_steps(s, n_chunks, lag, ring, normalized, staged, order, normalize, project, finish):
    fin_slot = lax.rem(s + (2 * ring - 1 - lag), jnp.int32(ring))

    @pl.when((s >= 1) & (s <= n_chunks))
    def _publish_normalized():
        h_w, h_s = normalized
        h_s[...] = h_w[...]

    @pl.when((s >= 2) & (s <= n_chunks + 1))
    def _publish_projected():
        slot = lax.rem(s + (ring - 2), jnp.int32(ring))
        for stage_ref, ring_ref in staged:
            ring_ref[slot] = stage_ref[...]

    @pl.when(s == 0)
    def _first():
        _weave(order, normalize=normalize([]))

    @pl.when((s >= 1) & (s <= lag))
    def _fill():
        _weave(order, project=project(), normalize=normalize([]))

    @pl.when((s > lag) & (s <= n_chunks))
    def _steady():
        tokens = []
        _weave(order, finish=finish(fin_slot, tokens), project=project(), normalize=normalize(tokens))

    @pl.when(s > n_chunks)
    def _drain():
        _weave(order, finish=finish(fin_slot, []))


def _adaln_kernel(cond_ref, we_ref, be_ref, wo_ref, bo_ref, oe_ref, oo_ref):
    s = _silu(cond_ref[...]).astype(BF16)
    oe_ref[...] = jnp.dot(s, we_ref[...].astype(BF16), preferred_element_type=F32) + be_ref[...]
    oo_ref[...] = jnp.dot(s, wo_ref[...].astype(BF16), preferred_element_type=F32) + bo_ref[...]


def _adaln(cond8, w_e, b_e, w_o, b_o):
    n = 3 * D_MODEL
    wspec = pl.BlockSpec((D_MODEL, ADA_TN), lambda j: (0, j))
    vspec = pl.BlockSpec((1, ADA_TN), lambda j: (0, j))
    ospec = pl.BlockSpec((8, ADA_TN), lambda j: (0, j))
    return pl.pallas_call(
        _adaln_kernel,
        out_shape=(jax.ShapeDtypeStruct((8, n), F32), jax.ShapeDtypeStruct((8, n), F32)),
        grid=(n // ADA_TN,),
        in_specs=[pl.BlockSpec((8, D_MODEL), lambda j: (0, 0)), wspec, vspec, wspec, vspec],
        out_specs=(ospec, ospec),
        compiler_params=pltpu.CompilerParams(dimension_semantics=("arbitrary",), vmem_limit_bytes=VMEM_LIMIT),
        name="adaln",
    )(cond8, w_e, b_e.reshape(1, n), w_o, b_o.reshape(1, n))


def _head_inv_rms(x, bd):
    sq = x * x
    hi = sq.astype(BF16)
    lo = (sq - hi.astype(F32)).astype(BF16)
    ms = (jnp.dot(hi, bd, preferred_element_type=F32) + jnp.dot(lo, bd, preferred_element_type=F32)) * (1.0 / HEAD_DIM)
    return lax.rsqrt(ms + EPS)


def _rope(x, cos, sin_signed, first_half):
    partner = jnp.where(first_half, pltpu.roll(x, LANES - ROPE_FREQS, axis=1), pltpu.roll(x, ROPE_FREQS, axis=1))
    return x * cos + partner * sin_signed


def _dup_kv(x):
    lane = lax.broadcasted_iota(jnp.int32, (1, LANES), 1)
    swapped = pltpu.roll(x, HEAD_DIM, axis=1)
    return jnp.where(lane < HEAD_DIM, x, swapped), jnp.where(lane < HEAD_DIM, swapped, x)


def _tiled_keys(k):
    ka, kb = _dup_kv(k)
    return jnp.concatenate([ka, ka, kb, kb], axis=1).astype(BF16)


def _values_t(v):
    va, vb = _dup_kv(v)
    return jnp.concatenate([va.T, vb.T], axis=0).astype(BF16)


def _even_kernel(*refs, nc, n_chunks, lag, ring, windowed):
    if windowed:
        (xp_ref, xa_ref, modp_ref, moda_ref, g_ref, win_ref, cw_ref, cb_ref, qg_ref, kg_ref, sink_ref, wout_ref,
         cos_ref, sin_ref, ck_ref, cv_ref,
         y_ref,
         h_s, h_w, u_s, gate_s, q_s, k_s, vt_s, gb_s, u_w, gate_w, q_w, k_w, vt_w, gb_w, a_s, ck_s, cvt_s) = refs
        ko_ref = vo_ref = None
    else:
        (xp_ref, xa_ref, modp_ref, moda_ref, g_ref, win_f32, cw_ref, cb_ref, qg_ref, kg_ref, sink_ref, wout_f32,
         y_ref, ko_ref, vo_ref, win_ref, wout_ref,
         h_s, h_w, u_s, gate_s, q_s, k_s, vt_s, gb_s, u_w, gate_w, q_w, k_w, vt_w, gb_w, a_s) = refs
    s = pl.program_id(0)
    i = s - 1 - lag
    ci = lax.rem(i, jnp.int32(nc))

    @pl.when(s == 0)
    def _init():
        k_s[...] = jnp.zeros(k_s.shape, BF16)
        vt_s[...] = jnp.zeros(vt_s.shape, BF16)
        if not windowed:
            _cast_columns(win_f32, win_ref)
            _cast_columns(wout_f32, wout_ref)

    if windowed:
        @pl.when((i >= 0) & (ci == 0))
        def _load_ctx():
            ck_s[...] = _tiled_keys(ck_ref[0])
            cvt_s[...] = _values_t(cv_ref[0])

    def normalize(tokens):
        yield from _normalize_items(h_w, xp_ref, g_ref, modp_ref, tokens)

    def project():
        o_bg, o_cg, o_xs, o_ga, o_q, o_k, _, o_gb = EVEN_OFFS[:8]

        def proj(start):
            return jnp.dot(h_s[...], win_ref[:, start:start + NT], preferred_element_type=F32)

        lane = lax.broadcasted_iota(jnp.int32, (1, LANES), 1)
        first_half = (lane & (2 * ROPE_FREQS - 1)) < ROPE_FREQS
        ri = lax.broadcasted_iota(jnp.int32, (NT, NT), 0) >> HEAD_SHIFT
        cj = lax.broadcasted_iota(jnp.int32, (NT, NT), 1) >> HEAD_SHIFT
        bd = (ri == cj).astype(BF16)

        def qk_norm(x, g):
            w = x.shape[1]
            inv = _head_inv_rms(x, bd[0:w, 0:w])
            pieces = []
            for j in range(w // LANES):
                lanes = slice(j * LANES, (j + 1) * LANES)
                xj = x[:, lanes] * inv[:, lanes] * g
                pieces.append(_rope(xj, cos_ref[...], sin_ref[...], first_half) if windowed else xj)
            return pieces

        def keys_values(_):
            kv = proj(o_k)
            k, = qk_norm(kv[:, 0:KV_WIDTH], kg_ref[...])
            v = kv[:, KV_WIDTH:2 * KV_WIDTH]
            if not windowed:
                ko_ref[0] = k
                vo_ref[0] = v
            k_w[...] = _tiled_keys(k)
            vt_w[...] = _values_t(v)

        def queries(n):
            for j, qj in enumerate(qk_norm(proj(o_q + n * NT), qg_ref[...])):
                q_w[:, n * NT + j * LANES:n * NT + (j + 1) * LANES] = (qj * (ATTN_SCALE * LOG2E)).astype(BF16)

        def conv_gate(n):
            gate_w[:, n * NT:(n + 1) * NT] = proj(o_bg + n * NT) * _silu(proj(o_ga + n * NT))

        def conv_input(n):
            u_w[:, n * NT:(n + 1) * NT] = proj(o_cg + n * NT) * proj(o_xs + n * NT)

        def attn_gate(n):
            gb_w[:, n * NT:(n + 1) * NT] = _silu(proj(o_gb + n * NT))

        items = [(conv_gate, 0), (keys_values, 0), (conv_input, 0), (queries, 0), (conv_gate, 1), (queries, 1),
                 (conv_input, 1), (attn_gate, 0), (attn_gate, 1)]
        for emit, n in items:
            yield
            emit(n)

    def finish(slot, tokens):
        if windowed:
            prev_slot = lax.rem(slot + (ring - 1), jnp.int32(ring))
            next_slot = lax.rem(slot + 1, jnp.int32(ring))
            has_prev = ci > 0
            has_next = ci < nc - 1
        else:
            prev_slot = next_slot = has_prev = has_next = None

        yield
        ext = _with_halo(u_s, slot, slice(None), prev_slot, next_slot, has_prev, has_next)
        conv = (_rows_down(ext, 1)[HALO:HALO + TM] * cw_ref[0:1, :]
                + ext[HALO:HALO + TM] * cw_ref[1:2, :]
                + _rows_up(ext, 1)[HALO:HALO + TM] * cw_ref[2:3, :]
                + cb_ref[...])
        a_s[:, 0:CONV_WIDTH] = (gate_s[slot] * conv).astype(BF16)

        lane_head = lax.broadcasted_iota(jnp.int32, (1, GROUP_W), 1) >> HEAD_SHIFT
        cols4 = Q_PER_KV * BLOCK
        key_i =lax.broadcasted_iota(jnp.int32, (BLOCK, cols4), 0)
        qry_i = lax.broadcasted_iota(jnp.int32, (BLOCK, cols4), 1) & (BLOCK - 1)
        upper = lax.broadcasted_iota(jnp.int32, (LANES, LANES), 0) < HEAD_DIM
        halves = (slice(0, BLOCK), slice(BLOCK, TM))
        def block_masks(qb):
            if qb == 0:
                blocks = ((prev_slot, halves[1]), (slot, halves[0]), (slot, halves[1]))
                prev_in, next_in = has_prev, True
            else:
                blocks = ((slot, halves[0]), (slot, halves[1]), (next_slot, halves[0]))
                prev_in, next_in = True, has_next
            prev_ok = (key_i - qry_i) >= jnp.where(prev_in, 0, BLOCK)
            next_ok = (qry_i - key_i) >= jnp.where(next_in, 0, BLOCK)
            return blocks, prev_ok, next_ok

        def scores(qb, kv):
            cols = slice(kv * GROUP_W, (kv + 1) * GROUP_W)
            vrows = slice(kv * LANES, (kv + 1) * LANES)
            qblk = q_s[slot, halves[qb], cols]
            q4 = jnp.concatenate(
                [jnp.where(lane_head == g, qblk, jnp.zeros_like(qblk)) for g in range(Q_PER_KV)], axis=0)

            def scores_t(kk):
                return lax.dot_general(kk, q4, (((1,), (1,)), ((), ())), preferred_element_type=F32)

            if not windowed:
                return [([scores_t(k_s[slot, :, cols])], vt_s[slot, vrows, :])]
            blocks, prev_ok, next_ok = block_masks(qb)
            s_loc = scores_t(jnp.concatenate([k_s[sl, r, cols] for sl, r in blocks], axis=0))
            vt_loc = jnp.concatenate([vt_s[sl, vrows, r] for sl, r in blocks], axis=1)
            s_ctx = scores_t(ck_s[:, cols])
            return [([jnp.where(prev_ok, s_loc[0:BLOCK], NEG), s_loc[BLOCK:2 * BLOCK],
                      jnp.where(next_ok, s_loc[2 * BLOCK:3 * BLOCK], NEG)], vt_loc),
                    ([s_ctx], cvt_s[vrows, :])]

        def softmax(kv, groups):
            snk = jnp.concatenate(
                [jnp.full((1, BLOCK), sink_ref[kv * Q_PER_KV + g] * LOG2E, F32) for g in range(Q_PER_KV)], axis=1)
            if tokens:
                token = tokens.pop(0)
                while tokens:
                    token = token + tokens.pop(0)
                snk = snk + jnp.concatenate([token] * Q_PER_KV, axis=1)
            mx = snk
            for ss, _ in groups:
                for sc in ss:
                    mx = jnp.maximum(mx, jnp.max(sc, axis=0, keepdims=True))
            den = jnp.exp2(snk - mx)
            probs = []
            for ss, vt in groups:
                es = []
                for sc in ss:
                    e = jnp.exp2(sc - mx)
                    den = den + jnp.sum(e, axis=0, keepdims=True)
                    es.append(e.astype(BF16))
                probs.append((es[0] if len(es) == 1 else jnp.concatenate(es, axis=0), vt))
            return probs, den

        def values(qb, kv, probs, den):
            o_t = jnp.zeros((LANES, cols4), F32)
            for e_all, vt in probs:
                o_t = o_t + jnp.dot(vt, e_all, preferred_element_type=F32)
            o_t = o_t * (1.0 / den)
            o01 = jnp.where(upper, o_t[:, 0:BLOCK], o_t[:, BLOCK:2 * BLOCK]).T
            o23 = jnp.where(upper, o_t[:, 2 * BLOCK:3 * BLOCK], o_t[:, 3 * BLOCK:4 * BLOCK]).T
            a_s[halves[qb], CONV_WIDTH + kv * GROUP_W:CONV_WIDTH + (kv + 1) * GROUP_W] = (
                jnp.concatenate([o01, o23], axis=1) * gb_s[slot, halves[qb], kv * GROUP_W:(kv + 1) * GROUP_W]
            ).astype(BF16)

        pairs = [(qb, kv) for qb in range(TM // BLOCK) for kv in range(N_KV_HEADS)]
        yield
        ahead = scores(*pairs[0])
        for t, (qb, kv) in enumerate(pairs):
            groups = ahead
            if t + 1 < len(pairs):
                yield
                ahead = scores(*pairs[t + 1])
            yield
            probs, den = softmax(kv, groups)
            yield
            values(qb, kv, probs, den)

        yield from _out_proj_items(lambda: a_s[...], wout_ref, xa_ref, moda_ref, y_ref)

    staged = ((u_w, u_s), (gate_w, gate_s), (q_w, q_s), (k_w, k_s), (vt_w, vt_s), (gb_w, gb_s))
    order = EVEN_ORDER_LATENT if windowed else EVEN_ORDER_CONTEXT
    _pipeline_steps(s, n_chunks, lag, ring, (h_w, h_s), staged, order, normalize, project, finish)


def _chunk_specs(n_chunks, nc, lag, per_seq_mod, tm=TM):
    last = n_chunks - 1
    norm_chunk = lambda s: jnp.minimum(s, last)
    proj_chunk = lambda s: jnp.clip(s - 1, 0, last)
    fin_chunk = lambda s: jnp.clip(s - 1 - lag, 0, last)
    seq_of = (lambda j: j // nc) if per_seq_mod else (lambda j: 0)
    return dict(
        x_norm=pl.BlockSpec((1, tm, D_MODEL), lambda s: (norm_chunk(s), 0, 0)),
        x_fin=pl.BlockSpec((1, tm, D_MODEL), lambda s: (fin_chunk(s), 0, 0)),
        mod_norm=pl.BlockSpec((1, 1, 3 * D_MODEL), lambda s: (seq_of(norm_chunk(s)), 0, 0)),
        mod_fin=pl.BlockSpec((1, 1, 3 * D_MODEL), lambda s: (seq_of(fin_chunk(s)), 0, 0)),
        proj_chunk=proj_chunk, fin_chunk=fin_chunk)


def _const_spec(shape):
    return pl.BlockSpec(shape, lambda c: (0,) * len(shape))


def _even_layer(x, mod, norm_g, w_in, conv_w, conv_b, q_g, k_g, sink, w_out, rope=None, ctx=None):
    n_seq, seq_len, _ = x.shape
    windowed = rope is not None
    nc = seq_len // TM
    n_chunks = n_seq * nc
    lag, ring = (2, 4) if windowed else (1, 2)
    assert windowed or nc == 1
    sp = _chunk_specs(n_chunks, nc, lag, windowed)
    xc = x.reshape(n_chunks, TM, D_MODEL)
    in_specs = [
        sp["x_norm"], sp["x_fin"], sp["mod_norm"], sp["mod_fin"],
        _const_spec((1, D_MODEL)),
        _const_spec((D_MODEL, EVEN_IN)),
        _const_spec((3, CONV_WIDTH)),
        _const_spec((1, CONV_WIDTH)),
        _const_spec((1, LANES)),
        _const_spec((1, LANES)),
        pl.BlockSpec(memory_space=pltpu.SMEM),
        _const_spec((D_MODEL, D_MODEL)),
    ]
    args = [xc, xc, mod, mod, norm_g.reshape(1, D_MODEL), w_in, conv_w, conv_b.reshape(1, CONV_WIDTH),
            jnp.tile(q_g, LANES // HEAD_DIM).reshape(1, LANES), jnp.tile(k_g, LANES // HEAD_DIM).reshape(1, LANES),
            sink, w_out]
    per_chunk = [
        ((TM, CONV_WIDTH), F32),
        ((TM, CONV_WIDTH), F32),
        ((TM, ATTN_WIDTH), BF16),
        ((TM, 2 * GROUP_W), BF16),
        ((2 * LANES, TM), BF16),
        ((TM, ATTN_WIDTH), F32),
    ]
    scratch = (
        [pltpu.VMEM((TM, D_MODEL), BF16)] * 2
        + [pltpu.VMEM((ring,) + shape, dt) for shape, dt in per_chunk]
        + [pltpu.VMEM(shape, dt) for shape, dt in per_chunk]
        + [pltpu.VMEM((TM, D_MODEL), BF16)])
    y_shape = jax.ShapeDtypeStruct(xc.shape, F32)
    y_spec = pl.BlockSpec((1, TM, D_MODEL), lambda c: (sp["fin_chunk"](c), 0, 0))
    if windowed:
        cos, sin = rope
        ck, cv = ctx
        rope_spec = pl.BlockSpec((TM, LANES), lambda c: (sp["proj_chunk"](c) % nc, 0))
        ctx_spec = pl.BlockSpec((1,) + ck.shape[1:], lambda c: (sp["fin_chunk"](c) // nc, 0, 0))
        in_specs += [rope_spec, rope_spec, ctx_spec, ctx_spec]
        args += [cos, sin, ck, cv]
        scratch += [pltpu.VMEM((ck.shape[1], 2 * GROUP_W), BF16), pltpu.VMEM((2 * LANES, cv.shape[1]), BF16)]
        out_shape, out_specs = y_shape, y_spec
    else:
        kv_shape = jax.ShapeDtypeStruct((n_chunks, TM, KV_WIDTH), F32)
        kv_spec = pl.BlockSpec((1, TM, KV_WIDTH), lambda c: (sp["proj_chunk"](c), 0, 0))
        out_shape = (y_shape, kv_shape, kv_shape,
                     jax.ShapeDtypeStruct(w_in.shape, BF16), jax.ShapeDtypeStruct(w_out.shape, BF16))
        out_specs = (y_spec, kv_spec, kv_spec, _const_spec(w_in.shape), _const_spec(w_out.shape))
    out = pl.pallas_call(
        functools.partial(_even_kernel, nc=nc, n_chunks=n_chunks, lag=lag, ring=ring, windowed=windowed),
        out_shape=out_shape,
        grid=(n_chunks + lag + 1,),
        in_specs=in_specs,
        out_specs=out_specs,
        scratch_shapes=scratch,
        compiler_params=PIPELINE_PARAMS,
        name="even_latent" if windowed else "even_context",
    )(*args)
    if windowed:
        return out.reshape(x.shape)
    y, k, v, w_in_bf16, w_out_bf16 = out
    return (y.reshape(x.shape), k.reshape(n_seq, seq_len, KV_WIDTH), v.reshape(n_seq, seq_len, KV_WIDTH),
            w_in_bf16, w_out_bf16)


def _odd_kernel(*refs, nc, n_chunks, lag, ring, cast_weights):
    if cast_weights:
        (xp_ref, xa_ref, modp_ref, moda_ref, g_ref, win_f32, pw_f32, ps_ref, wout_f32,
         y_ref, win_ref, pw_ref, wout_ref, h_s, h_w, u_s, sg_s, u_w, sg_w) = refs
    else:
        (xp_ref, xa_ref, modp_ref, moda_ref, g_ref, win_ref, pw_ref, ps_ref, wout_ref,
         y_ref, h_s, h_w, u_s, sg_s, u_w, sg_w) = refs
    tm = h_s.shape[0]
    s = pl.program_id(0)
    i = s - 1 - lag
    ci = lax.rem(i, jnp.int32(nc))

    if cast_weights:
        @pl.when(s == 0)
        def _cast():
            _cast_columns(win_f32, win_ref)
            _cast_columns(pw_f32, pw_ref)
            _cast_columns(wout_f32, wout_ref)

    def normalize(tokens):
        yield from _normalize_items(h_w, xp_ref, g_ref, modp_ref, tokens)

    def project():
        for n in range(D_MODEL // NT):
            yield
            cols = slice(n * NT, (n + 1) * NT)
            u = jnp.dot(h_s[...], win_ref[:, cols], preferred_element_type=F32)
            u_w[:, cols] = u
        for n in range(D_MODEL // NT):
            yield
            cols = slice(n * NT, (n + 1) * NT)
            sg = _silu(jnp.dot(h_s[...], win_ref[:, D_MODEL + n * NT:D_MODEL + (n + 1) * NT],
                               preferred_element_type=F32))
            sg_w[:, cols] = sg

    def finish(slot, tokens):
        if nc > 1:
            prev_slot = lax.rem(slot + (ring - 1), jnp.int32(ring))
            next_slot = lax.rem(slot + 1, jnp.int32(ring))
            has_prev = ci > 0
            has_next = ci < nc - 1
        else:
            prev_slot = next_slot = has_prev = has_next = None
        t = ci * tm + lax.broadcasted_iota(jnp.int32, (tm, 1), 0)
        outs = []
        for gi, w in enumerate(POOL_SIZES):
            yield
            cols = slice(gi * POOL_GROUP, (gi + 1) * POOL_GROUP)
            ext = _with_halo(u_s, slot, cols, prev_slot, next_slot, has_prev, has_next)
            acc = ext
            span = 1
            while span < w:
                acc = acc + _rows_down(acc, span)
                span *= 2
            if w > 2:
                acc = _rows_up(acc, w // 2 - 1)
            cnt = (jnp.minimum(t + w // 2, nc * tm) - jnp.maximum(t - w // 2, 0)).astype(F32)
            while tokens:
                cnt = cnt + tokens.pop(0)[:, 0:1]
            pooled = acc[HALO:HALO + tm] / cnt - ext[HALO:HALO + tm]
            outs.append(jnp.dot(pooled.astype(BF16), pw_ref[gi], preferred_element_type=F32))
        yield
        y = (jnp.concatenate(outs, axis=1) * ps_ref[...] * sg_s[slot]).astype(BF16)
        yield from _out_proj_items(lambda: y, wout_ref, xa_ref, moda_ref, y_ref)

    _pipeline_steps(s, n_chunks, lag, ring, (h_w, h_s), ((u_w, u_s), (sg_w, sg_s)), ODD_ORDER,
                    normalize, project, finish)


def _odd_layer(x, mod, per_seq_mod, norm_g, w_in, pool_w, pool_scale, w_out):
    n_seq, seq_len, _ = x.shape
    tm = TM
    nc = seq_len // tm
    n_chunks = n_seq * nc
    lag, ring = (2, 4) if nc > 1 else (1, 2)
    sp = _chunk_specs(n_chunks, nc, lag, per_seq_mod, tm)
    xc = x.reshape(n_chunks, tm, D_MODEL)
    in_specs = [
        sp["x_norm"], sp["x_fin"], sp["mod_norm"], sp["mod_fin"],
        _const_spec((1, D_MODEL)),
        _const_spec((D_MODEL, 2 * D_MODEL)),
        _const_spec(pool_w.shape),
        _const_spec((1, D_MODEL)),
        _const_spec((D_MODEL, D_MODEL)),
    ]
    cast_weights = w_in.dtype != BF16
    out_shape = jax.ShapeDtypeStruct(xc.shape, F32)
    out_specs = pl.BlockSpec((1, tm, D_MODEL), lambda c: (sp["fin_chunk"](c), 0, 0))
    if cast_weights:
        weights = (w_in, pool_w, w_out)
        out_shape = (out_shape,) + tuple(jax.ShapeDtypeStruct(w.shape, BF16) for w in weights)
        out_specs = (out_specs,) + tuple(_const_spec(w.shape) for w in weights)
    out = pl.pallas_call(
        functools.partial(_odd_kernel, nc=nc, n_chunks=n_chunks, lag=lag, ring=ring, cast_weights=cast_weights),
        out_shape=out_shape,
        grid=(n_chunks + lag + 1,),
        in_specs=in_specs,
        out_specs=out_specs,
        scratch_shapes=[
            pltpu.VMEM((tm, D_MODEL), BF16),
            pltpu.VMEM((tm, D_MODEL), BF16),
            pltpu.VMEM((ring, tm, D_MODEL), F32),
            pltpu.VMEM((ring, tm, D_MODEL), F32),
            pltpu.VMEM((tm, D_MODEL), F32),
            pltpu.VMEM((tm, D_MODEL), F32),
        ],
        compiler_params=PIPELINE_PARAMS,
        name="odd_latent" if per_seq_mod else "odd_context",
    )(xc, xc, mod, mod, norm_g.reshape(1, D_MODEL), w_in, pool_w, pool_scale.reshape(1, D_MODEL), w_out)
    if cast_weights:
        y, *bf16_weights = out
        return y.reshape(x.shape), bf16_weights
    return out.reshape(x.shape)


def _rope_tables(seq_len):
    n_rows = seq_len // GRID_W
    row = np.repeat(np.arange(n_rows), GRID_W).astype(np.float32)
    col = np.tile(np.arange(GRID_W), n_rows).astype(np.float32)
    inv = (np.float32(ROPE_BASE) ** (-np.arange(ROPE_FREQS, dtype=np.float32) / np.float32(ROPE_FREQS))).astype(np.float32)
    ang = np.stack([row[:, None] * inv, col[:, None] * inv], axis=1)
    cos = np.broadcast_to(np.cos(ang)[:, :, None, :], (seq_len, 2, 2, ROPE_FREQS))
    sin = np.sin(ang)[:, :, None, :] * np.array([-1.0, 1.0], np.float32)[None, None, :, None]
    cos = cos.reshape(seq_len, HEAD_DIM).astype(np.float32)
    sin = sin.reshape(seq_len, HEAD_DIM).astype(np.float32)
    return jnp.asarray(np.tile(cos, (1, LANES // HEAD_DIM))), jnp.asarray(np.tile(sin, (1, LANES // HEAD_DIM)))


def kernel(x_prompt, x_sample, cache_k, cache_v, c, c_ctx, ada_w_e, ada_b_e, norm_g_e, w_in_e, conv_w, conv_b,
           q_norm_g, k_norm_g, sink, w_out_e, ada_w_o, ada_b_o, norm_g_o, w_in_o, pool_w, pool_scale, w_out_o):
    n_dec = x_sample.shape[0]
    depth = ada_w_e.shape[0] + ada_w_o.shape[0]
    assert ada_w_e.shape[0] == 1 and ada_w_o.shape[0] == 1 and n_dec + 1 <= 8
    cond8 = jnp.concatenate([c_ctx[None], c, jnp.zeros((8 - 1 - n_dec, D_MODEL), F32)], axis=0)
    mod_e, mod_o = _adaln(cond8, ada_w_e[0], ada_b_e[0], ada_w_o[0], ada_b_o[0])
    rope = _rope_tables(x_sample.shape[1])

    yp, ys = x_prompt, x_sample
    new_k, new_v = [], []
    for layer in range(depth):
        i = layer // 2
        if layer % 2 == 0:
            mod = mod_e.reshape(8, 1, 3 * D_MODEL)
            small = (conv_w[i], conv_b[i], q_norm_g[i], k_norm_g[i], sink[i])
            yp, k, v, w_in, w_out = _even_layer(yp, mod[0:1], norm_g_e[i], w_in_e[i], *small, w_out_e[i])
            new_k.append(k.reshape(k.shape[0], k.shape[1], N_KV_HEADS, HEAD_DIM))
            new_v.append(v.reshape(v.shape[0], v.shape[1], N_KV_HEADS, HEAD_DIM))
            ck = cache_k[:, i].reshape(n_dec, cache_k.shape[2], KV_WIDTH)
            cv = cache_v[:, i].reshape(n_dec, cache_v.shape[2], KV_WIDTH)
            ys = _even_layer(ys, mod[1:1 + n_dec], norm_g_e[i], w_in, *small, w_out, rope=rope, ctx=(ck, cv))
        else:
            mod = mod_o.reshape(8, 1, 3 * D_MODEL)
            yp, (w_in, w_pool, w_out) = _odd_layer(yp, mod[0:1], False, norm_g_o[i], w_in_o[i], pool_w[i],
                                                   pool_scale[i], w_out_o[i])
            ys = _odd_layer(ys, mod[1:1 + n_dec], True, norm_g_o[i], w_in, w_pool, pool_scale[i], w_out)
    return yp, ys, jnp.stack(new_k, axis=1), jnp.stack(new_v, axis=1)
```

```python
import functools

import jax
import jax.numpy as jnp
import numpy as np
from jax import lax
from jax.experimental import pallas as pl
from jax.experimental.pallas import tpu as pltpu

F32 = jnp.float32
BF16 = jnp.bfloat16

D_MODEL = 1024
GRID_W = 64
HEAD_DIM = 64
HEAD_SHIFT = 6
ATTN_WIDTH = D_MODEL // 2
N_Q_HEADS = ATTN_WIDTH // HEAD_DIM
N_KV_HEADS = N_Q_HEADS // 4
Q_PER_KV = N_Q_HEADS // N_KV_HEADS
KV_WIDTH = N_KV_HEADS * HEAD_DIM
CONV_WIDTH = D_MODEL - ATTN_WIDTH
WINDOW = 128
BLOCK = 128
ROPE_BASE = 10000.0
ROPE_FREQS = HEAD_DIM // 4
ATTN_SCALE = HEAD_DIM ** -0.5
LOG2E = float(np.log2(np.e))
NEG = -1e30
POOL_SIZES = (2, 4, 8, 16)
POOL_GROUP = D_MODEL // len(POOL_SIZES)
EPS = 1e-6
EVEN_SIZES = (CONV_WIDTH, CONV_WIDTH, CONV_WIDTH, CONV_WIDTH, ATTN_WIDTH, KV_WIDTH, KV_WIDTH, ATTN_WIDTH)
EVEN_IN = sum(EVEN_SIZES)
EVEN_OFFS = tuple(int(s) for s in np.cumsum((0,) + EVEN_SIZES))

TM = 256
NT = 256
NORM_ROWS = 64
EVEN_ORDER_LATENT = "PF" + "F" + "PFNFPF" * 3 + "PNFPF" + "FFFF"
EVEN_ORDER_CONTEXT = "PF" + "F" + "FNPFPF" * 3 + "NPFPF" + "FFFF"
ODD_ORDER = "PNFP" * 4 + "FFFFF"
LANES = 128
GROUP_W = Q_PER_KV * HEAD_DIM
HALO = 8
ADA_TN = 768
VMEM_LIMIT = 56 * 1024 * 1024
PIPELINE_PARAMS = pltpu.CompilerParams(dimension_semantics=("arbitrary",), vmem_limit_bytes=VMEM_LIMIT)


def _silu(x):
    return x / (1.0 + jnp.exp(-x))


def _rows_down(x, k):
    return pltpu.roll(x, k, axis=0)


def _rows_up(x, k):
    return pltpu.roll(x, x.shape[0] - k, axis=0)


def _rms_modulate(x, g, shift, scale):
    ms = jnp.mean(x * x, axis=-1, keepdims=True)
    return (x * lax.rsqrt(ms + EPS) * g) * (1.0 + scale) + shift


def _cast_columns(src_ref, dst_ref):
    for n in range(dst_ref.shape[0]):
        dst_ref[n] = src_ref[:, n * NT:(n + 1) * NT].astype(dst_ref.dtype)


def _tiled_shape(w):
    return (w.shape[1] // NT, w.shape[0], NT)


def _opaque_zero(x):
    bits = pltpu.bitcast(x, jnp.uint32)
    return pltpu.bitcast((bits >> 16) >> 16, F32)


def _normalize_items(h_s, x_ref, g_ref, mod_ref, tokens):
    shift = mod_ref[0, :, 0:D_MODEL]
    scale = mod_ref[0, :, D_MODEL:2 * D_MODEL]
    for r in range(0, h_s.shape[0], NORM_ROWS):
        yield
        h = _rms_modulate(x_ref[0, r:r + NORM_ROWS], g_ref[...], shift, scale)
        h_s[r:r + NORM_ROWS] = h.astype(BF16)
        tokens.append(_opaque_zero(h[0:8, 0:LANES])[0:1])


def _out_proj_items(get_a, wout_ref, x_ref, mod_ref, y_ref):
    for n in range(D_MODEL // NT):
        yield
        cols = slice(n * NT, (n + 1) * NT)
        mixed = jnp.dot(get_a(), wout_ref[n], preferred_element_type=F32)
        y_ref[0, :, cols] = x_ref[0, :, cols] + mod_ref[0, :, 2 * D_MODEL + n * NT:2 * D_MODEL + (n + 1) * NT] * mixed


def _weave(order, **stages):
    by_letter = {name[0].upper(): stage for name, stage in stages.items()}
    for stage in by_letter.values():
        next(stage, None)
    for letter in order:
        if letter in by_letter:
            next(by_letter[letter], None)
    for stage in by_letter.values():
        for _ in stage:
            pass


def _with_halo(u_s, slot, cols, prev_slot, next_slot, has_prev, has_next):
    u = u_s[slot, :, cols]
    zero = jnp.zeros((HALO, u.shape[1]), F32)
    if prev_slot is None:
        top = bot = zero
    else:
        rows = u_s.shape[1]
        top = jnp.where(has_prev, u_s[prev_slot, rows - HALO:rows, cols], zero)
        bot = jnp.where(has_next, u_s[next_slot, 0:HALO, cols], zero)
    return jnp.concatenate([top, u, bot], axis=0)


def _pipeline_steps(s, n_chunks, lag, ring, normalized, staged, order, normalize, project, finish):
    fin_slot = lax.rem(s + (2 * ring - 1 - lag), jnp.int32(ring))

    @pl.when((s >= 1) & (s <= n_chunks))
    def _publish_normalized():
        h_w, h_s = normalized
        h_s[...] = h_w[...]

    @pl.when((s >= 2) & (s <= n_chunks + 1))
    def _publish_projected():
        slot = lax.rem(s + (ring - 2), jnp.int32(ring))
        for stage_ref, ring_ref in staged:
            ring_ref[slot] = stage_ref[...]

    @pl.when(s == 0)
    def _first():
        _weave(order, normalize=normalize([]))

    @pl.when((s >= 1) & (s <= lag))
    def _fill():
        _weave(order, project=project(), normalize=normalize([]))

    @pl.when((s > lag) & (s <= n_chunks))
    def _steady():
        tokens = []
        _weave(order, finish=finish(fin_slot, tokens), project=project(), normalize=normalize(tokens))

    @pl.when(s > n_chunks)
    def _drain():
        _weave(order, finish=finish(fin_slot, []))


def _adaln_kernel(cond_ref, we_ref, be_ref, wo_ref, bo_ref, oe_ref, oo_ref):
    s = _silu(cond_ref[...]).astype(BF16)
    oe_ref[...] = jnp.dot(s, we_ref[...].astype(BF16), preferred_element_type=F32) + be_ref[...]
    oo_ref[...] = jnp.dot(s, wo_ref[...].astype(BF16), preferred_element_type=F32) + bo_ref[...]


def _adaln(cond8, w_e, b_e, w_o, b_o):
    n = 3 * D_MODEL
    wspec = pl.BlockSpec((D_MODEL, ADA_TN), lambda j: (0, j))
    vspec = pl.BlockSpec((1, ADA_TN), lambda j: (0, j))
    ospec = pl.BlockSpec((8, ADA_TN), lambda j: (0, j))
    return pl.pallas_call(
        _adaln_kernel,
        out_shape=(jax.ShapeDtypeStruct((8, n), F32), jax.ShapeDtypeStruct((8, n), F32)),
        grid=(n // ADA_TN,),
        in_specs=[pl.BlockSpec((8, D_MODEL), lambda j: (0, 0)), wspec, vspec, wspec, vspec],
        out_specs=(ospec, ospec),
        compiler_params=pltpu.CompilerParams(dimension_semantics=("arbitrary",), vmem_limit_bytes=VMEM_LIMIT),
        name="adaln",
    )(cond8, w_e, b_e.reshape(1, n), w_o, b_o.reshape(1, n))


def _head_inv_rms(x, bd):
    sq = x * x
    hi = sq.astype(BF16)
    lo = (sq - hi.astype(F32)).astype(BF16)
    ms = (jnp.dot(hi, bd, preferred_element_type=F32) + jnp.dot(lo, bd, preferred_element_type=F32)) * (1.0 / HEAD_DIM)
    return lax.rsqrt(ms + EPS)


def _rope(x, cos, sin_signed, first_half):
    partner = jnp.where(first_half, pltpu.roll(x, LANES - ROPE_FREQS, axis=1), pltpu.roll(x, ROPE_FREQS, axis=1))
    return x * cos + partner * sin_signed


def _dup_kv(x):
    lane = lax.broadcasted_iota(jnp.int32, (1, LANES), 1)
    swapped = pltpu.roll(x, HEAD_DIM, axis=1)
    return jnp.where(lane < HEAD_DIM, x, swapped), jnp.where(lane < HEAD_DIM, swapped, x)


def _tiled_keys(k):
    ka, kb = _dup_kv(k)
    return jnp.concatenate([ka, ka, kb, kb], axis=1).astype(BF16)


def _values_t(v):
    va, vb = _dup_kv(v)
    return jnp.concatenate([va.T, vb.T], axis=0).astype(BF16)


def _even_kernel(*refs, nc, n_chunks, lag, ring, windowed):
    if windowed:
        (xp_ref, xa_ref, modp_ref, moda_ref, g_ref, win_ref, cw_ref, cb_ref, qg_ref, kg_ref, sink_ref, wout_ref,
         cos_ref, sin_ref, ck_ref, cv_ref,
         y_ref,
         h_s, h_w, u_s, gate_s, q_s, k_s, vt_s, gb_s, u_w, gate_w, q_w, k_w, vt_w, gb_w, a_s, ck_s, cvt_s) = refs
        ko_ref = vo_ref = None
    else:
        (xp_ref, xa_ref, modp_ref, moda_ref, g_ref, win_f32, cw_ref, cb_ref, qg_ref, kg_ref, sink_ref, wout_f32,
         y_ref, ko_ref, vo_ref, win_ref, wout_ref,
         h_s, h_w, u_s, gate_s, q_s, k_s, vt_s, gb_s, u_w, gate_w, q_w, k_w, vt_w, gb_w, a_s) = refs
    s = pl.program_id(0)
    i = s - 1 - lag
    ci = lax.rem(i, jnp.int32(nc))

    @pl.when(s == 0)
    def _init():
        k_s[...] = jnp.zeros(k_s.shape, BF16)
        vt_s[...] = jnp.zeros(vt_s.shape, BF16)
        if not windowed:
            _cast_columns(win_f32, win_ref)
            _cast_columns(wout_f32, wout_ref)

    if windowed:
        @pl.when((i >= 0) & (ci == 0))
        def _load_ctx():
            ck_s[...] = _tiled_keys(ck_ref[0])
            cvt_s[...] = _values_t(cv_ref[0])

    def normalize(tokens):
        yield from _normalize_items(h_w, xp_ref, g_ref, modp_ref, tokens)

    def project():
        o_bg, o_cg, o_xs, o_ga, o_q, o_k, _, o_gb = EVEN_OFFS[:8]

        def proj(start):
            return jnp.dot(h_s[...], win_ref[start // NT], preferred_element_type=F32)

        lane = lax.broadcasted_iota(jnp.int32, (1, LANES), 1)
        first_half = (lane & (2 * ROPE_FREQS - 1)) < ROPE_FREQS
        ri = lax.broadcasted_iota(jnp.int32, (NT, NT), 0) >> HEAD_SHIFT
        cj = lax.broadcasted_iota(jnp.int32, (NT, NT), 1) >> HEAD_SHIFT
        bd = (ri == cj).astype(BF16)

        def qk_norm(x, g):
            w = x.shape[1]
            inv = _head_inv_rms(x, bd[0:w, 0:w])
            pieces = []
            for j in range(w // LANES):
                lanes = slice(j * LANES, (j + 1) * LANES)
                xj = x[:, lanes] * inv[:, lanes] * g
                pieces.append(_rope(xj, cos_ref[...], sin_ref[...], first_half) if windowed else xj)
            return pieces

        def keys_values(_):
            kv = proj(o_k)
            k, = qk_norm(kv[:, 0:KV_WIDTH], kg_ref[...])
            v = kv[:, KV_WIDTH:2 * KV_WIDTH]
            if not windowed:
                ko_ref[0] = k
                vo_ref[0] = v
            k_w[...] = _tiled_keys(k)
            vt_w[...] = _values_t(v)

        def queries(n):
            for j, qj in enumerate(qk_norm(proj(o_q + n * NT), qg_ref[...])):
                q_w[:, n * NT + j * LANES:n * NT + (j + 1) * LANES] = (qj * (ATTN_SCALE * LOG2E)).astype(BF16)

        def conv_gate(n):
            gate_w[:, n * NT:(n + 1) * NT] = proj(o_bg + n * NT) * _silu(proj(o_ga + n * NT))

        def conv_input(n):
            u_w[:, n * NT:(n + 1) * NT] = proj(o_cg + n * NT) * proj(o_xs + n * NT)

        def attn_gate(n):
            gb_w[:, n * NT:(n + 1) * NT] = _silu(proj(o_gb + n * NT))

        items = [(conv_gate, 0), (keys_values, 0), (conv_input, 0), (queries, 0), (conv_gate, 1), (queries, 1),
                 (conv_input, 1), (attn_gate, 0), (attn_gate, 1)]
        for emit, n in items:
            yield
            emit(n)

    def finish(slot, tokens):
        if windowed:
            prev_slot = lax.rem(slot + (ring - 1), jnp.int32(ring))
            next_slot = lax.rem(slot + 1, jnp.int32(ring))
            has_prev = ci > 0
            has_next = ci < nc - 1
        else:
            prev_slot = next_slot = has_prev = has_next = None

        yield
        ext = _with_halo(u_s, slot, slice(None), prev_slot, next_slot, has_prev, has_next)
        conv = (_rows_down(ext, 1)[HALO:HALO + TM] * cw_ref[0:1, :]
                + ext[HALO:HALO + TM] * cw_ref[1:2, :]
                + _rows_up(ext, 1)[HALO:HALO + TM] * cw_ref[2:3, :]
                + cb_ref[...])
        a_s[:, 0:CONV_WIDTH] = (gate_s[slot] * conv).astype(BF16)

        lane_head = lax.broadcasted_iota(jnp.int32, (1, GROUP_W), 1) >> HEAD_SHIFT
        cols4 = Q_PER_KV * BLOCK
        key_i =lax.broadcasted_iota(jnp.int32, (BLOCK, cols4), 0)
        qry_i = lax.broadcasted_iota(jnp.int32, (BLOCK, cols4), 1) & (BLOCK - 1)
        upper = lax.broadcasted_iota(jnp.int32, (LANES, LANES), 0) < HEAD_DIM
        halves = (slice(0, BLOCK), slice(BLOCK, TM))
        def block_masks(qb):
            if qb == 0:
                blocks = ((prev_slot, halves[1]), (slot, halves[0]), (slot, halves[1]))
                prev_in, next_in = has_prev, True
            else:
                blocks = ((slot, halves[0]), (slot, halves[1]), (next_slot, halves[0]))
                prev_in, next_in = True, has_next
            prev_ok = (key_i - qry_i) >= jnp.where(prev_in, 0, BLOCK)
            next_ok = (qry_i - key_i) >= jnp.where(next_in, 0, BLOCK)
            return blocks, prev_ok, next_ok

        def scores(qb, kv):
            cols = slice(kv * GROUP_W, (kv + 1) * GROUP_W)
            vrows = slice(kv * LANES, (kv + 1) * LANES)
            qblk = q_s[slot, halves[qb], cols]
            q4 = jnp.concatenate(
                [jnp.where(lane_head == g, qblk, jnp.zeros_like(qblk)) for g in range(Q_PER_KV)], axis=0)

            def scores_t(kk):
                return lax.dot_general(kk, q4, (((1,), (1,)), ((), ())), preferred_element_type=F32)

            if not windowed:
                return [([scores_t(k_s[slot, :, cols])], vt_s[slot, vrows, :])]
            blocks, prev_ok, next_ok = block_masks(qb)
            s_loc = scores_t(jnp.concatenate([k_s[sl, r, cols] for sl, r in blocks], axis=0))
            vt_loc = jnp.concatenate([vt_s[sl, vrows, r] for sl, r in blocks], axis=1)
            s_ctx = scores_t(ck_s[:, cols])
            return [([jnp.where(prev_ok, s_loc[0:BLOCK], NEG), s_loc[BLOCK:2 * BLOCK],
                      jnp.where(next_ok, s_loc[2 * BLOCK:3 * BLOCK], NEG)], vt_loc),
                    ([s_ctx], cvt_s[vrows, :])]

        def softmax(kv, groups):
            snk = jnp.concatenate(
                [jnp.full((1, BLOCK), sink_ref[kv * Q_PER_KV + g] * LOG2E, F32) for g in range(Q_PER_KV)], axis=1)
            if tokens:
                token = tokens.pop(0)
                while tokens:
                    token = token + tokens.pop(0)
                snk = snk + jnp.concatenate([token] * Q_PER_KV, axis=1)
            mx = snk
            for ss, _ in groups:
                for sc in ss:
                    mx = jnp.maximum(mx, jnp.max(sc, axis=0, keepdims=True))
            den = jnp.exp2(snk - mx)
            probs = []
            for ss, vt in groups:
                es = []
                for sc in ss:
                    e = jnp.exp2(sc - mx)
                    den = den + jnp.sum(e, axis=0, keepdims=True)
                    es.append(e.astype(BF16))
                probs.append((es[0] if len(es) == 1 else jnp.concatenate(es, axis=0), vt))
            return probs, den

        def values(qb, kv, probs, den):
            o_t = jnp.zeros((LANES, cols4), F32)
            for e_all, vt in probs:
                o_t = o_t + jnp.dot(vt, e_all, preferred_element_type=F32)
            o_t = o_t * (1.0 / den)
            o01 = jnp.where(upper, o_t[:, 0:BLOCK], o_t[:, BLOCK:2 * BLOCK]).T
            o23 = jnp.where(upper, o_t[:, 2 * BLOCK:3 * BLOCK], o_t[:, 3 * BLOCK:4 * BLOCK]).T
            a_s[halves[qb], CONV_WIDTH + kv * GROUP_W:CONV_WIDTH + (kv + 1) * GROUP_W] = (
                jnp.concatenate([o01, o23], axis=1) * gb_s[slot, halves[qb], kv * GROUP_W:(kv + 1) * GROUP_W]
            ).astype(BF16)

        pairs = [(qb, kv) for qb in range(TM // BLOCK) for kv in range(N_KV_HEADS)]
        yield
        ahead = scores(*pairs[0])
        for t, (qb, kv) in enumerate(pairs):
            groups = ahead
            if t + 1 < len(pairs):
                yield
                ahead = scores(*pairs[t + 1])
            yield
            probs, den = softmax(kv, groups)
            yield
            values(qb, kv, probs, den)

        yield from _out_proj_items(lambda: a_s[...], wout_ref, xa_ref, moda_ref, y_ref)

    staged = ((u_w, u_s), (gate_w, gate_s), (q_w, q_s), (k_w, k_s), (vt_w, vt_s), (gb_w, gb_s))
    order = EVEN_ORDER_LATENT if windowed else EVEN_ORDER_CONTEXT
    _pipeline_steps(s, n_chunks, lag, ring, (h_w, h_s), staged, order, normalize, project, finish)


def _chunk_specs(n_chunks, nc, lag, per_seq_mod, tm=TM):
    last = n_chunks - 1
    norm_chunk = lambda s: jnp.minimum(s, last)
    proj_chunk = lambda s: jnp.clip(s - 1, 0, last)
    fin_chunk = lambda s: jnp.clip(s - 1 - lag, 0, last)
    seq_of = (lambda j: j // nc) if per_seq_mod else (lambda j: 0)
    return dict(
        x_norm=pl.BlockSpec((1, tm, D_MODEL), lambda s: (norm_chunk(s), 0, 0)),
        x_fin=pl.BlockSpec((1, tm, D_MODEL), lambda s: (fin_chunk(s), 0, 0)),
        mod_norm=pl.BlockSpec((1, 1, 3 * D_MODEL), lambda s: (seq_of(norm_chunk(s)), 0, 0)),
        mod_fin=pl.BlockSpec((1, 1, 3 * D_MODEL), lambda s: (seq_of(fin_chunk(s)), 0, 0)),
        proj_chunk=proj_chunk, fin_chunk=fin_chunk)


def _const_spec(shape):
    return pl.BlockSpec(shape, lambda c: (0,) * len(shape))


def _even_layer(x, mod, norm_g, w_in, conv_w, conv_b, q_g, k_g, sink, w_out, rope=None, ctx=None):
    n_seq, seq_len, _ = x.shape
    windowed = rope is not None
    nc = seq_len // TM
    n_chunks = n_seq * nc
    lag, ring = (2, 4) if windowed else (1, 2)
    assert windowed or nc == 1
    sp = _chunk_specs(n_chunks, nc, lag, windowed)
    xc = x.reshape(n_chunks, TM, D_MODEL)
    in_specs = [
        sp["x_norm"], sp["x_fin"], sp["mod_norm"], sp["mod_fin"],
        _const_spec((1, D_MODEL)),
        _const_spec(w_in.shape),
        _const_spec((3, CONV_WIDTH)),
        _const_spec((1, CONV_WIDTH)),
        _const_spec((1, LANES)),
        _const_spec((1, LANES)),
        pl.BlockSpec(memory_space=pltpu.SMEM),
        _const_spec(w_out.shape),
    ]
    args = [xc, xc, mod, mod, norm_g.reshape(1, D_MODEL), w_in, conv_w, conv_b.reshape(1, CONV_WIDTH),
            jnp.tile(q_g, LANES // HEAD_DIM).reshape(1, LANES), jnp.tile(k_g, LANES // HEAD_DIM).reshape(1, LANES),
            sink, w_out]
    per_chunk = [
        ((TM, CONV_WIDTH), F32),
        ((TM, CONV_WIDTH), F32),
        ((TM, ATTN_WIDTH), BF16),
        ((TM, 2 * GROUP_W), BF16),
        ((2 * LANES, TM), BF16),
        ((TM, ATTN_WIDTH), F32),
    ]
    scratch = (
        [pltpu.VMEM((TM, D_MODEL), BF16)] * 2
        + [pltpu.VMEM((ring,) + shape, dt) for shape, dt in per_chunk]
        + [pltpu.VMEM(shape, dt) for shape, dt in per_chunk]
        + [pltpu.VMEM((TM, D_MODEL), BF16)])
    y_shape = jax.ShapeDtypeStruct(xc.shape, F32)
    y_spec = pl.BlockSpec((1, TM, D_MODEL), lambda c: (sp["fin_chunk"](c), 0, 0))
    if windowed:
        cos, sin = rope
        ck, cv = ctx
        rope_spec = pl.BlockSpec((TM, LANES), lambda c: (sp["proj_chunk"](c) % nc, 0))
        ctx_spec = pl.BlockSpec((1,) + ck.shape[1:], lambda c: (sp["fin_chunk"](c) // nc, 0, 0))
        in_specs += [rope_spec, rope_spec, ctx_spec, ctx_spec]
        args += [cos, sin, ck, cv]
        scratch += [pltpu.VMEM((ck.shape[1], 2 * GROUP_W), BF16), pltpu.VMEM((2 * LANES, cv.shape[1]), BF16)]
        out_shape, out_specs = y_shape, y_spec
    else:
        kv_shape = jax.ShapeDtypeStruct((n_chunks, TM, KV_WIDTH), F32)
        kv_spec = pl.BlockSpec((1, TM, KV_WIDTH), lambda c: (sp["proj_chunk"](c), 0, 0))
        out_shape = (y_shape, kv_shape, kv_shape,
                     jax.ShapeDtypeStruct(_tiled_shape(w_in), BF16), jax.ShapeDtypeStruct(_tiled_shape(w_out), BF16))
        out_specs = (y_spec, kv_spec, kv_spec, _const_spec(_tiled_shape(w_in)), _const_spec(_tiled_shape(w_out)))
    out = pl.pallas_call(
        functools.partial(_even_kernel, nc=nc, n_chunks=n_chunks, lag=lag, ring=ring, windowed=windowed),
        out_shape=out_shape,
        grid=(n_chunks + lag + 1,),
        in_specs=in_specs,
        out_specs=out_specs,
        scratch_shapes=scratch,
        compiler_params=PIPELINE_PARAMS,
        name="even_latent" if windowed else "even_context",
    )(*args)
    if windowed:
        return out.reshape(x.shape)
    y, k, v, w_in_bf16, w_out_bf16 = out
    return (y.reshape(x.shape), k.reshape(n_seq, seq_len, KV_WIDTH), v.reshape(n_seq, seq_len, KV_WIDTH),
            w_in_bf16, w_out_bf16)


def _odd_kernel(*refs, nc, n_chunks, lag, ring, cast_weights):
    if cast_weights:
        (xp_ref, xa_ref, modp_ref, moda_ref, g_ref, win_f32, pw_f32, ps_ref, wout_f32,
         y_ref, win_ref, pw_ref, wout_ref, h_s, h_w, u_s, sg_s, u_w, sg_w) = refs
    else:
        (xp_ref, xa_ref, modp_ref, moda_ref, g_ref, win_ref, pw_ref, ps_ref, wout_ref,
         y_ref, h_s, h_w, u_s, sg_s, u_w, sg_w) = refs
    tm = h_s.shape[0]
    s = pl.program_id(0)
    i = s - 1 - lag
    ci = lax.rem(i, jnp.int32(nc))

    if cast_weights:
        @pl.when(s == 0)
        def _cast():
            _cast_columns(win_f32, win_ref)
            _cast_columns(wout_f32, wout_ref)
            for gi in range(pw_ref.shape[0]):
                pw_ref[gi] = pw_f32[gi].astype(BF16)

    def normalize(tokens):
        yield from _normalize_items(h_w, xp_ref, g_ref, modp_ref, tokens)

    def project():
        for n in range(D_MODEL // NT):
            yield
            cols = slice(n * NT, (n + 1) * NT)
            u = jnp.dot(h_s[...], win_ref[n], preferred_element_type=F32)
            u_w[:, cols] = u
        for n in range(D_MODEL // NT):
            yield
            cols = slice(n * NT, (n + 1) * NT)
            sg = _silu(jnp.dot(h_s[...], win_ref[D_MODEL // NT + n], preferred_element_type=F32))
            sg_w[:, cols] = sg

    def finish(slot, tokens):
        if nc > 1:
            prev_slot = lax.rem(slot + (ring - 1), jnp.int32(ring))
            next_slot = lax.rem(slot + 1, jnp.int32(ring))
            has_prev = ci > 0
            has_next = ci < nc - 1
        else:
            prev_slot = next_slot = has_prev = has_next = None
        t = ci * tm + lax.broadcasted_iota(jnp.int32, (tm, 1), 0)
        outs = []
        for gi, w in enumerate(POOL_SIZES):
            yield
            cols = slice(gi * POOL_GROUP, (gi + 1) * POOL_GROUP)
            ext = _with_halo(u_s, slot, cols, prev_slot, next_slot, has_prev, has_next)
            acc = ext
            span = 1
            while span < w:
                acc = acc + _rows_down(acc, span)
                span *= 2
            if w > 2:
                acc = _rows_up(acc, w // 2 - 1)
            cnt = (jnp.minimum(t + w // 2, nc * tm) - jnp.maximum(t - w // 2, 0)).astype(F32)
            while tokens:
                cnt = cnt + tokens.pop(0)[:, 0:1]
            pooled = acc[HALO:HALO + tm] / cnt - ext[HALO:HALO + tm]
            outs.append(jnp.dot(pooled.astype(BF16), pw_ref[gi], preferred_element_type=F32))
        yield
        y = (jnp.concatenate(outs, axis=1) * ps_ref[...] * sg_s[slot]).astype(BF16)
        yield from _out_proj_items(lambda: y, wout_ref, xa_ref, moda_ref, y_ref)

    _pipeline_steps(s, n_chunks, lag, ring, (h_w, h_s), ((u_w, u_s), (sg_w, sg_s)), ODD_ORDER,
                    normalize, project, finish)


def _odd_layer(x, mod, per_seq_mod, norm_g, w_in, pool_w, pool_scale, w_out):
    n_seq, seq_len, _ = x.shape
    tm = TM
    nc = seq_len // tm
    n_chunks = n_seq * nc
    lag, ring = (2, 4) if nc > 1 else (1, 2)
    sp = _chunk_specs(n_chunks, nc, lag, per_seq_mod, tm)
    xc = x.reshape(n_chunks, tm, D_MODEL)
    in_specs = [
        sp["x_norm"], sp["x_fin"], sp["mod_norm"], sp["mod_fin"],
        _const_spec((1, D_MODEL)),
        _const_spec(w_in.shape),
        _const_spec(pool_w.shape),
        _const_spec((1, D_MODEL)),
        _const_spec(w_out.shape),
    ]
    cast_weights = w_in.dtype != BF16
    out_shape = jax.ShapeDtypeStruct(xc.shape, F32)
    out_specs = pl.BlockSpec((1, tm, D_MODEL), lambda c: (sp["fin_chunk"](c), 0, 0))
    if cast_weights:
        shapes = (_tiled_shape(w_in), pool_w.shape, _tiled_shape(w_out))
        out_shape = (out_shape,) + tuple(jax.ShapeDtypeStruct(shape, BF16) for shape in shapes)
        out_specs = (out_specs,) + tuple(_const_spec(shape) for shape in shapes)
    out = pl.pallas_call(
        functools.partial(_odd_kernel, nc=nc, n_chunks=n_chunks, lag=lag, ring=ring, cast_weights=cast_weights),
        out_shape=out_shape,
        grid=(n_chunks + lag + 1,),
        in_specs=in_specs,
        out_specs=out_specs,
        scratch_shapes=[
            pltpu.VMEM((tm, D_MODEL), BF16),
            pltpu.VMEM((tm, D_MODEL), BF16),
            pltpu.VMEM((ring, tm, D_MODEL), F32),
            pltpu.VMEM((ring, tm, D_MODEL), F32),
            pltpu.VMEM((tm, D_MODEL), F32),
            pltpu.VMEM((tm, D_MODEL), F32),
        ],
        compiler_params=PIPELINE_PARAMS,
        name="odd_latent" if per_seq_mod else "odd_context",
    )(xc, xc, mod, mod, norm_g.reshape(1, D_MODEL), w_in, pool_w, pool_scale.reshape(1, D_MODEL), w_out)
    if cast_weights:
        y, *bf16_weights = out
        return y.reshape(x.shape), bf16_weights
    return out.reshape(x.shape)


def _rope_tables(seq_len):
    n_rows = seq_len // GRID_W
    row = np.repeat(np.arange(n_rows), GRID_W).astype(np.float32)
    col = np.tile(np.arange(GRID_W), n_rows).astype(np.float32)
    inv = (np.float32(ROPE_BASE) ** (-np.arange(ROPE_FREQS, dtype=np.float32) / np.float32(ROPE_FREQS))).astype(np.float32)
    ang = np.stack([row[:, None] * inv, col[:, None] * inv], axis=1)
    cos = np.broadcast_to(np.cos(ang)[:, :, None, :], (seq_len, 2, 2, ROPE_FREQS))
    sin = np.sin(ang)[:, :, None, :] * np.array([-1.0, 1.0], np.float32)[None, None, :, None]
    cos = cos.reshape(seq_len, HEAD_DIM).astype(np.float32)
    sin = sin.reshape(seq_len, HEAD_DIM).astype(np.float32)
    return jnp.asarray(np.tile(cos, (1, LANES // HEAD_DIM))), jnp.asarray(np.tile(sin, (1, LANES // HEAD_DIM)))


def kernel(x_prompt, x_sample, cache_k, cache_v, c, c_ctx, ada_w_e, ada_b_e, norm_g_e, w_in_e, conv_w, conv_b,
           q_norm_g, k_norm_g, sink, w_out_e, ada_w_o, ada_b_o, norm_g_o, w_in_o, pool_w, pool_scale, w_out_o):
    n_dec = x_sample.shape[0]
    depth = ada_w_e.shape[0] + ada_w_o.shape[0]
    assert ada_w_e.shape[0] == 1 and ada_w_o.shape[0] == 1 and n_dec + 1 <= 8
    cond8 = jnp.concatenate([c_ctx[None], c, jnp.zeros((8 - 1 - n_dec, D_MODEL), F32)], axis=0)
    mod_e, mod_o = _adaln(cond8, ada_w_e[0], ada_b_e[0], ada_w_o[0], ada_b_o[0])
    rope = _rope_tables(x_sample.shape[1])

    yp, ys = x_prompt, x_sample
    new_k, new_v = [], []
    for layer in range(depth):
        i = layer // 2
        if layer % 2 == 0:
            mod = mod_e.reshape(8, 1, 3 * D_MODEL)
            small = (conv_w[i], conv_b[i], q_norm_g[i], k_norm_g[i], sink[i])
            yp, k, v, w_in, w_out = _even_layer(yp, mod[0:1], norm_g_e[i], w_in_e[i], *small, w_out_e[i])
            new_k.append(k.reshape(k.shape[0], k.shape[1], N_KV_HEADS, HEAD_DIM))
            new_v.append(v.reshape(v.shape[0], v.shape[1], N_KV_HEADS, HEAD_DIM))
            ck = cache_k[:, i].reshape(n_dec, cache_k.shape[2], KV_WIDTH)
            cv = cache_v[:, i].reshape(n_dec, cache_v.shape[2], KV_WIDTH)
            ys = _even_layer(ys, mod[1:1 + n_dec], norm_g_e[i], w_in, *small, w_out, rope=rope, ctx=(ck, cv))
        else:
            mod = mod_o.reshape(8, 1, 3 * D_MODEL)
            yp, (w_in, w_pool, w_out) = _odd_layer(yp, mod[0:1], False, norm_g_o[i], w_in_o[i], pool_w[i],
                                                   pool_scale[i], w_out_o[i])
            ys = _odd_layer(ys, mod[1:1 + n_dec], True, norm_g_o[i], w_in, w_pool, pool_scale[i], w_out)
    return yp, ys, jnp.stack(new_k, axis=1), jnp.stack(new_v, axis=1)
```

```python
import functools

import jax
import jax.numpy as jnp
import numpy as np
from jax import lax
from jax.experimental import pallas as pl
from jax.experimental.pallas import tpu as pltpu

F32 = jnp.float32
BF16 = jnp.bfloat16

D_MODEL = 1024
GRID_W = 64
HEAD_DIM = 64
HEAD_SHIFT = 6
ATTN_WIDTH = D_MODEL // 2
N_Q_HEADS = ATTN_WIDTH // HEAD_DIM
N_KV_HEADS = N_Q_HEADS // 4
Q_PER_KV = N_Q_HEADS // N_KV_HEADS
KV_WIDTH = N_KV_HEADS * HEAD_DIM
CONV_WIDTH = D_MODEL - ATTN_WIDTH
WINDOW = 128
BLOCK = 128
ROPE_BASE = 10000.0
ROPE_FREQS = HEAD_DIM // 4
ATTN_SCALE = HEAD_DIM ** -0.5
LOG2E = float(np.log2(np.e))
NEG = -1e30
POOL_SIZES = (2, 4, 8, 16)
POOL_GROUP = D_MODEL // len(POOL_SIZES)
EPS = 1e-6
EVEN_SIZES = (CONV_WIDTH, CONV_WIDTH, CONV_WIDTH, CONV_WIDTH, ATTN_WIDTH, KV_WIDTH, KV_WIDTH, ATTN_WIDTH)
EVEN_IN = sum(EVEN_SIZES)
EVEN_OFFS = tuple(int(s) for s in np.cumsum((0,) + EVEN_SIZES))

TM = 256
NT = 256
NORM_ROWS = 64
EVEN_ORDER_LATENT = "PF" + "F" + "PFNFPF" * 3 + "PNFPF" + "FFFF"
EVEN_ORDER_CONTEXT = "PF" + "F" + "FNPFPF" * 3 + "NPFPF" + "FFFF"
ODD_ORDER = "PNFP" * 4 + "FFFFF"
LANES = 128
GROUP_W = Q_PER_KV * HEAD_DIM
HALO = 8
ADA_TN = 768
VMEM_LIMIT = 56 * 1024 * 1024
PIPELINE_PARAMS = pltpu.CompilerParams(dimension_semantics=("arbitrary",), vmem_limit_bytes=VMEM_LIMIT)


def _silu(x):
    half = 0.5 * x
    return half + half * jnp.tanh(half)


def _rows_down(x, k):
    return pltpu.roll(x, k, axis=0)


def _rows_up(x, k):
    return pltpu.roll(x, x.shape[0] - k, axis=0)


def _rms_modulate(x, g, shift, scale):
    ms = jnp.mean(x * x, axis=-1, keepdims=True)
    return (x * lax.rsqrt(ms + EPS) * g) * (1.0 + scale) + shift


def _cast_columns(src_ref, dst_ref):
    for n in range(dst_ref.shape[0]):
        dst_ref[n] = src_ref[:, n * NT:(n + 1) * NT].astype(dst_ref.dtype)


def _tiled_shape(w):
    return (w.shape[1] // NT, w.shape[0], NT)


def _opaque_zero(x):
    bits = pltpu.bitcast(x, jnp.uint32)
    return pltpu.bitcast((bits >> 16) >> 16, F32)


def _normalize_items(h_s, x_ref, g_ref, mod_ref, tokens):
    shift = mod_ref[0, :, 0:D_MODEL]
    scale = mod_ref[0, :, D_MODEL:2 * D_MODEL]
    for r in range(0, h_s.shape[0], NORM_ROWS):
        yield
        h = _rms_modulate(x_ref[0, r:r + NORM_ROWS], g_ref[...], shift, scale)
        h_s[r:r + NORM_ROWS] = h.astype(BF16)
        tokens.append(_opaque_zero(h[0:8, 0:LANES])[0:1])


def _out_proj_items(get_a, wout_ref, x_ref, mod_ref, y_ref):
    for n in range(D_MODEL // NT):
        yield
        cols = slice(n * NT, (n + 1) * NT)
        mixed = jnp.dot(get_a(), wout_ref[n], preferred_element_type=F32)
        y_ref[0, :, cols] = x_ref[0, :, cols] + mod_ref[0, :, 2 * D_MODEL + n * NT:2 * D_MODEL + (n + 1) * NT] * mixed


def _weave(order, **stages):
    by_letter = {name[0].upper(): stage for name, stage in stages.items()}
    for stage in by_letter.values():
        next(stage, None)
    for letter in order:
        if letter in by_letter:
            next(by_letter[letter], None)
    for stage in by_letter.values():
        for _ in stage:
            pass


def _with_halo(u_s, slot, cols, prev_slot, next_slot, has_prev, has_next):
    u = u_s[slot, :, cols]
    zero = jnp.zeros((HALO, u.shape[1]), F32)
    if prev_slot is None:
        top = bot = zero
    else:
        rows = u_s.shape[1]
        top = jnp.where(has_prev, u_s[prev_slot, rows - HALO:rows, cols], zero)
        bot = jnp.where(has_next, u_s[next_slot, 0:HALO, cols], zero)
    return jnp.concatenate([top, u, bot], axis=0)


def _pipeline_steps(s, n_chunks, lag, ring, normalized, staged, order, normalize, project, finish):
    fin_slot = lax.rem(s + (2 * ring - 1 - lag), jnp.int32(ring))

    @pl.when((s >= 1) & (s <= n_chunks))
    def _publish_normalized():
        h_w, h_s = normalized
        h_s[...] = h_w[...]

    @pl.when((s >= 2) & (s <= n_chunks + 1))
    def _publish_projected():
        slot = lax.rem(s + (ring - 2), jnp.int32(ring))
        for stage_ref, ring_ref in staged:
            ring_ref[slot] = stage_ref[...]

    @pl.when(s == 0)
    def _first():
        _weave(order, normalize=normalize([]))

    @pl.when((s >= 1) & (s <= lag))
    def _fill():
        _weave(order, project=project(), normalize=normalize([]))

    @pl.when((s > lag) & (s <= n_chunks))
    def _steady():
        tokens = []
        _weave(order, finish=finish(fin_slot, tokens), project=project(), normalize=normalize(tokens))

    @pl.when(s > n_chunks)
    def _drain():
        _weave(order, finish=finish(fin_slot, []))


def _adaln_kernel(cond_ref, we_ref, be_ref, wo_ref, bo_ref, oe_ref, oo_ref):
    s = _silu(cond_ref[...]).astype(BF16)
    oe_ref[...] = jnp.dot(s, we_ref[...].astype(BF16), preferred_element_type=F32) + be_ref[...]
    oo_ref[...] = jnp.dot(s, wo_ref[...].astype(BF16), preferred_element_type=F32) + bo_ref[...]


def _adaln(cond8, w_e, b_e, w_o, b_o):
    n = 3 * D_MODEL
    wspec = pl.BlockSpec((D_MODEL, ADA_TN), lambda j: (0, j))
    vspec = pl.BlockSpec((1, ADA_TN), lambda j: (0, j))
    ospec = pl.BlockSpec((8, ADA_TN), lambda j: (0, j))
    return pl.pallas_call(
        _adaln_kernel,
        out_shape=(jax.ShapeDtypeStruct((8, n), F32), jax.ShapeDtypeStruct((8, n), F32)),
        grid=(n // ADA_TN,),
        in_specs=[pl.BlockSpec((8, D_MODEL), lambda j: (0, 0)), wspec, vspec, wspec, vspec],
        out_specs=(ospec, ospec),
        compiler_params=pltpu.CompilerParams(dimension_semantics=("arbitrary",), vmem_limit_bytes=VMEM_LIMIT),
        name="adaln",
    )(cond8, w_e, b_e.reshape(1, n), w_o, b_o.reshape(1, n))


def _head_inv_rms(x, bd):
    sq = x * x
    hi = sq.astype(BF16)
    lo = (sq - hi.astype(F32)).astype(BF16)
    ms = (jnp.dot(hi, bd, preferred_element_type=F32) + jnp.dot(lo, bd, preferred_element_type=F32)) * (1.0 / HEAD_DIM)
    return lax.rsqrt(ms + EPS)


def _rope(x, cos, sin_signed, first_half):
    partner = jnp.where(first_half, pltpu.roll(x, LANES - ROPE_FREQS, axis=1), pltpu.roll(x, ROPE_FREQS, axis=1))
    return x * cos + partner * sin_signed


def _dup_kv(x):
    lane = lax.broadcasted_iota(jnp.int32, (1, LANES), 1)
    swapped = pltpu.roll(x, HEAD_DIM, axis=1)
    return jnp.where(lane < HEAD_DIM, x, swapped), jnp.where(lane < HEAD_DIM, swapped, x)


def _tiled_keys(k):
    ka, kb = _dup_kv(k)
    return jnp.concatenate([ka, ka, kb, kb], axis=1).astype(BF16)


def _values_t(v):
    va, vb = _dup_kv(v)
    return jnp.concatenate([va.T, vb.T], axis=0).astype(BF16)


def _even_kernel(*refs, nc, n_chunks, lag, ring, windowed):
    if windowed:
        (xp_ref, xa_ref, modp_ref, moda_ref, g_ref, win_ref, cw_ref, cb_ref, qg_ref, kg_ref, sink_ref, wout_ref,
         cos_ref, sin_ref, ck_ref, cv_ref,
         y_ref,
         h_s, h_w, u_s, gate_s, q_s, k_s, vt_s, gb_s, u_w, gate_w, q_w, k_w, vt_w, gb_w, a_s, ck_s, cvt_s) = refs
        ko_ref = vo_ref = None
    else:
        (xp_ref, xa_ref, modp_ref, moda_ref, g_ref, win_f32, cw_ref, cb_ref, qg_ref, kg_ref, sink_ref, wout_f32,
         y_ref, ko_ref, vo_ref, win_ref, wout_ref,
         h_s, h_w, u_s, gate_s, q_s, k_s, vt_s, gb_s, u_w, gate_w, q_w, k_w, vt_w, gb_w, a_s) = refs
    s = pl.program_id(0)
    i = s - 1 - lag
    ci = lax.rem(i, jnp.int32(nc))

    @pl.when(s == 0)
    def _init():
        k_s[...] = jnp.zeros(k_s.shape, BF16)
        vt_s[...] = jnp.zeros(vt_s.shape, BF16)
        if not windowed:
            _cast_columns(win_f32, win_ref)
            _cast_columns(wout_f32, wout_ref)

    if windowed:
        @pl.when((i >= 0) & (ci == 0))
        def _load_ctx():
            ck_s[...] = _tiled_keys(ck_ref[0])
            cvt_s[...] = _values_t(cv_ref[0])

    def normalize(tokens):
        yield from _normalize_items(h_w, xp_ref, g_ref, modp_ref, tokens)

    def project():
        o_bg, o_cg, o_xs, o_ga, o_q, o_k, _, o_gb = EVEN_OFFS[:8]

        def proj(start):
            return jnp.dot(h_s[...], win_ref[start // NT], preferred_element_type=F32)

        lane = lax.broadcasted_iota(jnp.int32, (1, LANES), 1)
        first_half = (lane & (2 * ROPE_FREQS - 1)) < ROPE_FREQS
        ri = lax.broadcasted_iota(jnp.int32, (NT, NT), 0) >> HEAD_SHIFT
        cj = lax.broadcasted_iota(jnp.int32, (NT, NT), 1) >> HEAD_SHIFT
        bd = (ri == cj).astype(BF16)

        def qk_norm(x, g):
            w = x.shape[1]
            inv = _head_inv_rms(x, bd[0:w, 0:w])
            pieces = []
            for j in range(w // LANES):
                lanes = slice(j * LANES, (j + 1) * LANES)
                xj = x[:, lanes] * inv[:, lanes] * g
                pieces.append(_rope(xj, cos_ref[...], sin_ref[...], first_half) if windowed else xj)
            return pieces

        def keys_values(_):
            kv = proj(o_k)
            k, = qk_norm(kv[:, 0:KV_WIDTH], kg_ref[...])
            v = kv[:, KV_WIDTH:2 * KV_WIDTH]
            if not windowed:
                ko_ref[0] = k
                vo_ref[0] = v
            k_w[...] = _tiled_keys(k)
            vt_w[...] = _values_t(v)

        def queries(n):
            for j, qj in enumerate(qk_norm(proj(o_q + n * NT), qg_ref[...])):
                q_w[:, n * NT + j * LANES:n * NT + (j + 1) * LANES] = (qj * (ATTN_SCALE * LOG2E)).astype(BF16)

        def conv_gate(n):
            gate_w[:, n * NT:(n + 1) * NT] = proj(o_bg + n * NT) * _silu(proj(o_ga + n * NT))

        def conv_input(n):
            u_w[:, n * NT:(n + 1) * NT] = proj(o_cg + n * NT) * proj(o_xs + n * NT)

        def attn_gate(n):
            gb_w[:, n * NT:(n + 1) * NT] = _silu(proj(o_gb + n * NT))

        items = [(conv_gate, 0), (keys_values, 0), (conv_input, 0), (queries, 0), (conv_gate, 1), (queries, 1),
                 (conv_input, 1), (attn_gate, 0), (attn_gate, 1)]
        for emit, n in items:
            yield
            emit(n)

    def finish(slot, tokens):
        if windowed:
            prev_slot = lax.rem(slot + (ring - 1), jnp.int32(ring))
            next_slot = lax.rem(slot + 1, jnp.int32(ring))
            has_prev = ci > 0
            has_next = ci < nc - 1
        else:
            prev_slot = next_slot = has_prev = has_next = None

        yield
        ext = _with_halo(u_s, slot, slice(None), prev_slot, next_slot, has_prev, has_next)
        conv = (_rows_down(ext, 1)[HALO:HALO + TM] * cw_ref[0:1, :]
                + ext[HALO:HALO + TM] * cw_ref[1:2, :]
                + _rows_up(ext, 1)[HALO:HALO + TM] * cw_ref[2:3, :]
                + cb_ref[...])
        a_s[:, 0:CONV_WIDTH] = (gate_s[slot] * conv).astype(BF16)

        lane_head = lax.broadcasted_iota(jnp.int32, (1, GROUP_W), 1) >> HEAD_SHIFT
        cols4 = Q_PER_KV * BLOCK
        key_i =lax.broadcasted_iota(jnp.int32, (BLOCK, cols4), 0)
        qry_i = lax.broadcasted_iota(jnp.int32, (BLOCK, cols4), 1) & (BLOCK - 1)
        upper = lax.broadcasted_iota(jnp.int32, (LANES, LANES), 0) < HEAD_DIM
        halves = (slice(0, BLOCK), slice(BLOCK, TM))
        @functools.lru_cache(maxsize=None)
        def block_masks(qb):
            if qb == 0:
                blocks = ((prev_slot, halves[1]), (slot, halves[0]), (slot, halves[1]))
                prev_in, next_in = has_prev, True
            else:
                blocks = ((slot, halves[0]), (slot, halves[1]), (next_slot, halves[0]))
                prev_in, next_in = True, has_next
            prev_ok = (key_i - qry_i) >= jnp.where(prev_in, 0, BLOCK)
            next_ok = (qry_i - key_i) >= jnp.where(next_in, 0, BLOCK)
            return blocks, prev_ok, next_ok

        def scores(qb, kv):
            cols = slice(kv * GROUP_W, (kv + 1) * GROUP_W)
            vrows = slice(kv * LANES, (kv + 1) * LANES)
            qblk = q_s[slot, halves[qb], cols]
            q4 = jnp.concatenate(
                [jnp.where(lane_head == g, qblk, jnp.zeros_like(qblk)) for g in range(Q_PER_KV)], axis=0)

            def scores_t(kk):
                return lax.dot_general(kk, q4, (((1,), (1,)), ((), ())), preferred_element_type=F32)

            if not windowed:
                return [([scores_t(k_s[slot, :, cols])], vt_s[slot, vrows, :])]
            blocks, prev_ok, next_ok = block_masks(qb)
            s_loc = scores_t(jnp.concatenate([k_s[sl, r, cols] for sl, r in blocks], axis=0))
            vt_loc = jnp.concatenate([vt_s[sl, vrows, r] for sl, r in blocks], axis=1)
            s_ctx = scores_t(ck_s[:, cols])
            return [([jnp.where(prev_ok, s_loc[0:BLOCK], NEG), s_loc[BLOCK:2 * BLOCK],
                      jnp.where(next_ok, s_loc[2 * BLOCK:3 * BLOCK], NEG)], vt_loc),
                    ([s_ctx], cvt_s[vrows, :])]

        def softmax(kv, groups):
            snk = jnp.concatenate(
                [jnp.full((1, BLOCK), sink_ref[kv * Q_PER_KV + g] * LOG2E, F32) for g in range(Q_PER_KV)], axis=1)
            if tokens:
                token = tokens.pop(0)
                while tokens:
                    token = token + tokens.pop(0)
                snk = snk + jnp.concatenate([token] * Q_PER_KV, axis=1)
            mx = snk
            for ss, _ in groups:
                for sc in ss:
                    mx = jnp.maximum(mx, jnp.max(sc, axis=0, keepdims=True))
            den = jnp.exp2(snk - mx)
            probs = []
            for ss, vt in groups:
                es = []
                for sc in ss:
                    e = jnp.exp2(sc - mx)
                    den = den + jnp.sum(e, axis=0, keepdims=True)
                    es.append(e.astype(BF16))
                probs.append((es[0] if len(es) == 1 else jnp.concatenate(es, axis=0), vt))
            return probs, den

        def values(qb, kv, probs, den):
            o_t = jnp.zeros((LANES, cols4), F32)
            for e_all, vt in probs:
                o_t = o_t + jnp.dot(vt, e_all, preferred_element_type=F32)
            o_t = o_t * (1.0 / den)
            o01 = jnp.where(upper, o_t[:, 0:BLOCK], o_t[:, BLOCK:2 * BLOCK]).T
            o23 = jnp.where(upper, o_t[:, 2 * BLOCK:3 * BLOCK], o_t[:, 3 * BLOCK:4 * BLOCK]).T
            a_s[halves[qb], CONV_WIDTH + kv * GROUP_W:CONV_WIDTH + (kv + 1) * GROUP_W] = (
                jnp.concatenate([o01, o23], axis=1) * gb_s[slot, halves[qb], kv * GROUP_W:(kv + 1) * GROUP_W]
            ).astype(BF16)

        pairs = [(qb, kv) for qb in range(TM // BLOCK) for kv in range(N_KV_HEADS)]
        yield
        ahead = scores(*pairs[0])
        for t, (qb, kv) in enumerate(pairs):
            groups = ahead
            if t + 1 < len(pairs):
                yield
                ahead = scores(*pairs[t + 1])
            yield
            probs, den = softmax(kv, groups)
            yield
            values(qb, kv, probs, den)

        yield from _out_proj_items(lambda: a_s[...], wout_ref, xa_ref, moda_ref, y_ref)

    staged = ((u_w, u_s), (gate_w, gate_s), (q_w, q_s), (k_w, k_s), (vt_w, vt_s), (gb_w, gb_s))
    order = EVEN_ORDER_LATENT if windowed else EVEN_ORDER_CONTEXT
    _pipeline_steps(s, n_chunks, lag, ring, (h_w, h_s), staged, order, normalize, project, finish)


def _chunk_specs(n_chunks, nc, lag, per_seq_mod, tm=TM):
    last = n_chunks - 1
    norm_chunk = lambda s: jnp.minimum(s, last)
    proj_chunk = lambda s: jnp.clip(s - 1, 0, last)
    fin_chunk = lambda s: jnp.clip(s - 1 - lag, 0, last)
    seq_of = (lambda j: j // nc) if per_seq_mod else (lambda j: 0)
    return dict(
        x_norm=pl.BlockSpec((1, tm, D_MODEL), lambda s: (norm_chunk(s), 0, 0)),
        x_fin=pl.BlockSpec((1, tm, D_MODEL), lambda s: (fin_chunk(s), 0, 0)),
        mod_norm=pl.BlockSpec((1, 1, 3 * D_MODEL), lambda s: (seq_of(norm_chunk(s)), 0, 0)),
        mod_fin=pl.BlockSpec((1, 1, 3 * D_MODEL), lambda s: (seq_of(fin_chunk(s)), 0, 0)),
        proj_chunk=proj_chunk, fin_chunk=fin_chunk)


def _const_spec(shape):
    return pl.BlockSpec(shape, lambda c: (0,) * len(shape))


def _even_layer(x, mod, norm_g, w_in, conv_w, conv_b, q_g, k_g, sink, w_out, rope=None, ctx=None):
    n_seq, seq_len, _ = x.shape
    windowed = rope is not None
    nc = seq_len // TM
    n_chunks = n_seq * nc
    lag, ring = (2, 4) if windowed else (1, 2)
    assert windowed or nc == 1
    sp = _chunk_specs(n_chunks, nc, lag, windowed)
    xc = x.reshape(n_chunks, TM, D_MODEL)
    in_specs = [
        sp["x_norm"], sp["x_fin"], sp["mod_norm"], sp["mod_fin"],
        _const_spec((1, D_MODEL)),
        _const_spec(w_in.shape),
        _const_spec((3, CONV_WIDTH)),
        _const_spec((1, CONV_WIDTH)),
        _const_spec((1, LANES)),
        _const_spec((1, LANES)),
        pl.BlockSpec(memory_space=pltpu.SMEM),
        _const_spec(w_out.shape),
    ]
    args = [xc, xc, mod, mod, norm_g.reshape(1, D_MODEL), w_in, conv_w, conv_b.reshape(1, CONV_WIDTH),
            jnp.tile(q_g, LANES // HEAD_DIM).reshape(1, LANES), jnp.tile(k_g, LANES // HEAD_DIM).reshape(1, LANES),
            sink, w_out]
    per_chunk = [
        ((TM, CONV_WIDTH), F32),
        ((TM, CONV_WIDTH), F32),
        ((TM, ATTN_WIDTH), BF16),
        ((TM, 2 * GROUP_W), BF16),
        ((2 * LANES, TM), BF16),
        ((TM, ATTN_WIDTH), F32),
    ]
    scratch = (
        [pltpu.VMEM((TM, D_MODEL), BF16)] * 2
        + [pltpu.VMEM((ring,) + shape, dt) for shape, dt in per_chunk]
        + [pltpu.VMEM(shape, dt) for shape, dt in per_chunk]
        + [pltpu.VMEM((TM, D_MODEL), BF16)])
    y_shape = jax.ShapeDtypeStruct(xc.shape, F32)
    y_spec = pl.BlockSpec((1, TM, D_MODEL), lambda c: (sp["fin_chunk"](c), 0, 0))
    if windowed:
        cos, sin = rope
        ck, cv = ctx
        rope_spec = pl.BlockSpec((TM, LANES), lambda c: (sp["proj_chunk"](c) % nc, 0))
        ctx_spec = pl.BlockSpec((1,) + ck.shape[1:], lambda c: (sp["fin_chunk"](c) // nc, 0, 0))
        in_specs += [rope_spec, rope_spec, ctx_spec, ctx_spec]
        args += [cos, sin, ck, cv]
        scratch += [pltpu.VMEM((ck.shape[1], 2 * GROUP_W), BF16), pltpu.VMEM((2 * LANES, cv.shape[1]), BF16)]
        out_shape, out_specs = y_shape, y_spec
    else:
        kv_shape = jax.ShapeDtypeStruct((n_chunks, TM, KV_WIDTH), F32)
        kv_spec = pl.BlockSpec((1, TM, KV_WIDTH), lambda c: (sp["proj_chunk"](c), 0, 0))
        out_shape = (y_shape, kv_shape, kv_shape,
                     jax.ShapeDtypeStruct(_tiled_shape(w_in), BF16), jax.ShapeDtypeStruct(_tiled_shape(w_out), BF16))
        out_specs = (y_spec, kv_spec, kv_spec, _const_spec(_tiled_shape(w_in)), _const_spec(_tiled_shape(w_out)))
    out = pl.pallas_call(
        functools.partial(_even_kernel, nc=nc, n_chunks=n_chunks, lag=lag, ring=ring, windowed=windowed),
        out_shape=out_shape,
        grid=(n_chunks + lag + 1,),
        in_specs=in_specs,
        out_specs=out_specs,
        scratch_shapes=scratch,
        compiler_params=PIPELINE_PARAMS,
        name="even_latent" if windowed else "even_context",
    )(*args)
    if windowed:
        return out.reshape(x.shape)
    y, k, v, w_in_bf16, w_out_bf16 = out
    return (y.reshape(x.shape), k.reshape(n_seq, seq_len, KV_WIDTH), v.reshape(n_seq, seq_len, KV_WIDTH),
            w_in_bf16, w_out_bf16)


def _odd_kernel(*refs, nc, n_chunks, lag, ring, cast_weights):
    if cast_weights:
        (xp_ref, xa_ref, modp_ref, moda_ref, g_ref, win_f32, pw_f32, ps_ref, wout_f32,
         y_ref, win_ref, pw_ref, wout_ref, h_s, h_w, u_s, sg_s, u_w, sg_w) = refs
    else:
        (xp_ref, xa_ref, modp_ref, moda_ref, g_ref, win_ref, pw_ref, ps_ref, wout_ref,
         y_ref, h_s, h_w, u_s, sg_s, u_w, sg_w) = refs
    tm = h_s.shape[0]
    s = pl.program_id(0)
    i = s - 1 - lag
    ci = lax.rem(i, jnp.int32(nc))

    if cast_weights:
        @pl.when(s == 0)
        def _cast():
            _cast_columns(win_f32, win_ref)
            _cast_columns(wout_f32, wout_ref)
            for gi in range(pw_ref.shape[0]):
                pw_ref[gi] = pw_f32[gi].astype(BF16)

    def normalize(tokens):
        yield from _normalize_items(h_w, xp_ref, g_ref, modp_ref, tokens)

    def project():
        for n in range(D_MODEL // NT):
            yield
            cols = slice(n * NT, (n + 1) * NT)
            u = jnp.dot(h_s[...], win_ref[n], preferred_element_type=F32)
            u_w[:, cols] = u
        for n in range(D_MODEL // NT):
            yield
            cols = slice(n * NT, (n + 1) * NT)
            sg = _silu(jnp.dot(h_s[...], win_ref[D_MODEL // NT + n], preferred_element_type=F32))
            sg_w[:, cols] = sg

    def finish(slot, tokens):
        if nc > 1:
            prev_slot = lax.rem(slot + (ring - 1), jnp.int32(ring))
            next_slot = lax.rem(slot + 1, jnp.int32(ring))
            has_prev = ci > 0
            has_next = ci < nc - 1
        else:
            prev_slot = next_slot = has_prev = has_next = None
        t = ci * tm + lax.broadcasted_iota(jnp.int32, (tm, 1), 0)
        outs = []
        for gi, w in enumerate(POOL_SIZES):
            yield
            cols = slice(gi * POOL_GROUP, (gi + 1) * POOL_GROUP)
            ext = _with_halo(u_s, slot, cols, prev_slot, next_slot, has_prev, has_next)
            acc = ext
            span = 1
            while span < w:
                acc = acc + _rows_down(acc, span)
                span *= 2
            if w > 2:
                acc = _rows_up(acc, w // 2 - 1)
            cnt = (jnp.minimum(t + w // 2, nc * tm) - jnp.maximum(t - w // 2, 0)).astype(F32)
            while tokens:
                cnt = cnt + tokens.pop(0)[:, 0:1]
            pooled = acc[HALO:HALO + tm] / cnt - ext[HALO:HALO + tm]
            outs.append(jnp.dot(pooled.astype(BF16), pw_ref[gi], preferred_element_type=F32))
        yield
        y = (jnp.concatenate(outs, axis=1) * ps_ref[...] * sg_s[slot]).astype(BF16)
        yield from _out_proj_items(lambda: y, wout_ref, xa_ref, moda_ref, y_ref)

    _pipeline_steps(s, n_chunks, lag, ring, (h_w, h_s), ((u_w, u_s), (sg_w, sg_s)), ODD_ORDER,
                    normalize, project, finish)


def _odd_layer(x, mod, per_seq_mod, norm_g, w_in, pool_w, pool_scale, w_out):
    n_seq, seq_len, _ = x.shape
    tm = TM
    nc = seq_len // tm
    n_chunks = n_seq * nc
    lag, ring = (2, 4) if nc > 1 else (1, 2)
    sp = _chunk_specs(n_chunks, nc, lag, per_seq_mod, tm)
    xc = x.reshape(n_chunks, tm, D_MODEL)
    in_specs = [
        sp["x_norm"], sp["x_fin"], sp["mod_norm"], sp["mod_fin"],
        _const_spec((1, D_MODEL)),
        _const_spec(w_in.shape),
        _const_spec(pool_w.shape),
        _const_spec((1, D_MODEL)),
        _const_spec(w_out.shape),
    ]
    cast_weights = w_in.dtype != BF16
    out_shape = jax.ShapeDtypeStruct(xc.shape, F32)
    out_specs = pl.BlockSpec((1, tm, D_MODEL), lambda c: (sp["fin_chunk"](c), 0, 0))
    if cast_weights:
        shapes = (_tiled_shape(w_in), pool_w.shape, _tiled_shape(w_out))
        out_shape = (out_shape,) + tuple(jax.ShapeDtypeStruct(shape, BF16) for shape in shapes)
        out_specs = (out_specs,) + tuple(_const_spec(shape) for shape in shapes)
    out = pl.pallas_call(
        functools.partial(_odd_kernel, nc=nc, n_chunks=n_chunks, lag=lag, ring=ring, cast_weights=cast_weights),
        out_shape=out_shape,
        grid=(n_chunks + lag + 1,),
        in_specs=in_specs,
        out_specs=out_specs,
        scratch_shapes=[
            pltpu.VMEM((tm, D_MODEL), BF16),
            pltpu.VMEM((tm, D_MODEL), BF16),
            pltpu.VMEM((ring, tm, D_MODEL), F32),
            pltpu.VMEM((ring, tm, D_MODEL), F32),
            pltpu.VMEM((tm, D_MODEL), F32),
            pltpu.VMEM((tm, D_MODEL), F32),
        ],
        compiler_params=PIPELINE_PARAMS,
        name="odd_latent" if per_seq_mod else "odd_context",
    )(xc, xc, mod, mod, norm_g.reshape(1, D_MODEL), w_in, pool_w, pool_scale.reshape(1, D_MODEL), w_out)
    if cast_weights:
        y, *bf16_weights = out
        return y.reshape(x.shape), bf16_weights
    return out.reshape(x.shape)


def _rope_tables(seq_len):
    n_rows = seq_len // GRID_W
    row = np.repeat(np.arange(n_rows), GRID_W).astype(np.float32)
    col = np.tile(np.arange(GRID_W), n_rows).astype(np.float32)
    inv = (np.float32(ROPE_BASE) ** (-np.arange(ROPE_FREQS, dtype=np.float32) / np.float32(ROPE_FREQS))).astype(np.float32)
    ang = np.stack([row[:, None] * inv, col[:, None] * inv], axis=1)
    cos = np.broadcast_to(np.cos(ang)[:, :, None, :], (seq_len, 2, 2, ROPE_FREQS))
    sin = np.sin(ang)[:, :, None, :] * np.array([-1.0, 1.0], np.float32)[None, None, :, None]
    cos = cos.reshape(seq_len, HEAD_DIM).astype(np.float32)
    sin = sin.reshape(seq_len, HEAD_DIM).astype(np.float32)
    return jnp.asarray(np.tile(cos, (1, LANES // HEAD_DIM))), jnp.asarray(np.tile(sin, (1, LANES // HEAD_DIM)))


def kernel(x_prompt, x_sample, cache_k, cache_v, c, c_ctx, ada_w_e, ada_b_e, norm_g_e, w_in_e, conv_w, conv_b,
           q_norm_g, k_norm_g, sink, w_out_e, ada_w_o, ada_b_o, norm_g_o, w_in_o, pool_w, pool_scale, w_out_o):
    n_dec = x_sample.shape[0]
    depth = ada_w_e.shape[0] + ada_w_o.shape[0]
    assert ada_w_e.shape[0] == 1 and ada_w_o.shape[0] == 1 and n_dec + 1 <= 8
    cond8 = jnp.concatenate([c_ctx[None], c, jnp.zeros((8 - 1 - n_dec, D_MODEL), F32)], axis=0)
    mod_e, mod_o = _adaln(cond8, ada_w_e[0], ada_b_e[0], ada_w_o[0], ada_b_o[0])
    rope = _rope_tables(x_sample.shape[1])

    yp, ys = x_prompt, x_sample
    new_k, new_v = [], []
    for layer in range(depth):
        i = layer // 2
        if layer % 2 == 0:
            mod = mod_e.reshape(8, 1, 3 * D_MODEL)
            small = (conv_w[i], conv_b[i], q_norm_g[i], k_norm_g[i], sink[i])
            yp, k, v, w_in, w_out = _even_layer(yp, mod[0:1], norm_g_e[i], w_in_e[i], *small, w_out_e[i])
            new_k.append(k.reshape(k.shape[0], k.shape[1], N_KV_HEADS, HEAD_DIM))
            new_v.append(v.reshape(v.shape[0], v.shape[1], N_KV_HEADS, HEAD_DIM))
            ck = cache_k[:, i].reshape(n_dec, cache_k.shape[2], KV_WIDTH)
            cv = cache_v[:, i].reshape(n_dec, cache_v.shape[2], KV_WIDTH)
            ys = _even_layer(ys, mod[1:1 + n_dec], norm_g_e[i], w_in, *small, w_out, rope=rope, ctx=(ck, cv))
        else:
            mod = mod_o.reshape(8, 1, 3 * D_MODEL)
            yp, (w_in, w_pool, w_out) = _odd_layer(yp, mod[0:1], False, norm_g_o[i], w_in_o[i], pool_w[i],
                                                   pool_scale[i], w_out_o[i])
            ys = _odd_layer(ys, mod[1:1 + n_dec], True, norm_g_o[i], w_in, w_pool, pool_scale[i], w_out)
    return yp, ys, jnp.stack(new_k, axis=1), jnp.stack(new_v, axis=1)
```

```python
import functools

import jax
import jax.numpy as jnp
import numpy as np
from jax import lax
from jax.experimental import pallas as pl
from jax.experimental.pallas import tpu as pltpu

F32 = jnp.float32
BF16 = jnp.bfloat16

D_MODEL = 1024
GRID_W = 64
HEAD_DIM = 64
HEAD_SHIFT = 6
ATTN_WIDTH = D_MODEL // 2
N_Q_HEADS = ATTN_WIDTH // HEAD_DIM
N_KV_HEADS = N_Q_HEADS // 4
Q_PER_KV = N_Q_HEADS // N_KV_HEADS
KV_WIDTH = N_KV_HEADS * HEAD_DIM
CONV_WIDTH = D_MODEL - ATTN_WIDTH
WINDOW = 128
BLOCK = 128
ROPE_BASE = 10000.0
ROPE_FREQS = HEAD_DIM // 4
ATTN_SCALE = HEAD_DIM ** -0.5
LOG2E = float(np.log2(np.e))
NEG = -1e30
POOL_SIZES = (2, 4, 8, 16)
POOL_GROUP = D_MODEL // len(POOL_SIZES)
EPS = 1e-6
EVEN_SIZES = (CONV_WIDTH, CONV_WIDTH, CONV_WIDTH, CONV_WIDTH, ATTN_WIDTH, KV_WIDTH, KV_WIDTH, ATTN_WIDTH)
EVEN_IN = sum(EVEN_SIZES)
EVEN_OFFS = tuple(int(s) for s in np.cumsum((0,) + EVEN_SIZES))

TM = 256
NT = 256
NORM_ROWS = 64
EVEN_ORDER_LATENT = "PF" + "F" + "PFNFPF" * 3 + "PNFPF" + "FFFF"
EVEN_ORDER_CONTEXT = "PF" + "F" + "FNPFPF" * 3 + "NPFPF" + "FFFF"
ODD_ORDER = "PNFP" * 4 + "FFFFF"
LANES = 128
GROUP_W = Q_PER_KV * HEAD_DIM
HALO = 8
ADA_TN = 768
VMEM_LIMIT = 56 * 1024 * 1024
PIPELINE_PARAMS = pltpu.CompilerParams(dimension_semantics=("arbitrary",), vmem_limit_bytes=VMEM_LIMIT)


def _silu(x):
    half = 0.5 * x
    return half + half * jnp.tanh(half)


def _rows_down(x, k):
    return pltpu.roll(x, k, axis=0)


def _rows_up(x, k):
    return pltpu.roll(x, x.shape[0] - k, axis=0)


def _rms_modulate(x, g, shift, scale):
    ms = jnp.mean(x * x, axis=-1, keepdims=True)
    return (x * lax.rsqrt(ms + EPS) * g) * (1.0 + scale) + shift


def _cast_columns(src_ref, dst_ref):
    for n in range(dst_ref.shape[0]):
        dst_ref[n] = src_ref[:, n * NT:(n + 1) * NT].astype(dst_ref.dtype)


def _tiled_shape(w):
    return (w.shape[1] // NT, w.shape[0], NT)


def _opaque_zero(x):
    bits = pltpu.bitcast(x, jnp.uint32)
    return pltpu.bitcast((bits >> 16) >> 16, F32)


def _normalize_items(h_s, x_ref, g_ref, mod_ref, tokens):
    shift = mod_ref[0, :, 0:D_MODEL]
    scale = mod_ref[0, :, D_MODEL:2 * D_MODEL]
    for r in range(0, h_s.shape[0], NORM_ROWS):
        yield
        h = _rms_modulate(x_ref[0, r:r + NORM_ROWS], g_ref[...], shift, scale)
        h_s[r:r + NORM_ROWS] = h.astype(BF16)
        tokens.append(_opaque_zero(h[0:8, 0:LANES])[0:1])


def _out_proj_items(get_a, wout_ref, x_ref, mod_ref, y_ref):
    for n in range(D_MODEL // NT):
        yield
        cols = slice(n * NT, (n + 1) * NT)
        mixed = jnp.dot(get_a(), wout_ref[n], preferred_element_type=F32)
        y_ref[0, :, cols] = x_ref[0, :, cols] + mod_ref[0, :, 2 * D_MODEL + n * NT:2 * D_MODEL + (n + 1) * NT] * mixed


def _weave(order, **stages):
    by_letter = {name[0].upper(): stage for name, stage in stages.items()}
    for stage in by_letter.values():
        next(stage, None)
    for letter in order:
        if letter in by_letter:
            next(by_letter[letter], None)
    for stage in by_letter.values():
        for _ in stage:
            pass


def _with_halo(u_s, slot, cols, prev_slot, next_slot, has_prev, has_next):
    u = u_s[slot, :, cols]
    zero = jnp.zeros((HALO, u.shape[1]), F32)
    if prev_slot is None:
        top = bot = zero
    else:
        rows = u_s.shape[1]
        top = jnp.where(has_prev, u_s[prev_slot, rows - HALO:rows, cols], zero)
        bot = jnp.where(has_next, u_s[next_slot, 0:HALO, cols], zero)
    return jnp.concatenate([top, u, bot], axis=0)


def _pipeline_steps(s, n_chunks, lag, ring, normalized, staged, order, normalize, project, finish):
    fin_slot = lax.rem(s + (2 * ring - 1 - lag), jnp.int32(ring))

    @pl.when((s >= 1) & (s <= n_chunks))
    def _publish_normalized():
        h_w, h_s = normalized
        h_s[...] = h_w[...]

    @pl.when((s >= 2) & (s <= n_chunks + 1))
    def _publish_projected():
        slot = lax.rem(s + (ring - 2), jnp.int32(ring))
        for stage_ref, ring_ref in staged:
            ring_ref[slot] = stage_ref[...]

    @pl.when(s == 0)
    def _first():
        _weave(order, normalize=normalize([]))

    @pl.when((s >= 1) & (s <= lag))
    def _fill():
        _weave(order, project=project(), normalize=normalize([]))

    @pl.when((s > lag) & (s <= n_chunks))
    def _steady():
        tokens = []
        _weave(order, finish=finish(fin_slot, tokens), project=project(), normalize=normalize(tokens))

    @pl.when(s > n_chunks)
    def _drain():
        _weave(order, finish=finish(fin_slot, []))


def _adaln_kernel(cond_ref, we_ref, be_ref, wo_ref, bo_ref, oe_ref, oo_ref):
    s = _silu(cond_ref[...]).astype(BF16)
    oe_ref[...] = jnp.dot(s, we_ref[...].astype(BF16), preferred_element_type=F32) + be_ref[...]
    oo_ref[...] = jnp.dot(s, wo_ref[...].astype(BF16), preferred_element_type=F32) + bo_ref[...]


def _adaln(cond8, w_e, b_e, w_o, b_o):
    n = 3 * D_MODEL
    wspec = pl.BlockSpec((D_MODEL, ADA_TN), lambda j: (0, j))
    vspec = pl.BlockSpec((1, ADA_TN), lambda j: (0, j))
    ospec = pl.BlockSpec((8, ADA_TN), lambda j: (0, j))
    return pl.pallas_call(
        _adaln_kernel,
        out_shape=(jax.ShapeDtypeStruct((8, n), F32), jax.ShapeDtypeStruct((8, n), F32)),
        grid=(n // ADA_TN,),
        in_specs=[pl.BlockSpec((8, D_MODEL), lambda j: (0, 0)), wspec, vspec, wspec, vspec],
        out_specs=(ospec, ospec),
        compiler_params=pltpu.CompilerParams(dimension_semantics=("arbitrary",), vmem_limit_bytes=VMEM_LIMIT),
        name="adaln",
    )(cond8, w_e, b_e.reshape(1, n), w_o, b_o.reshape(1, n))


def _head_inv_rms(x, bd):
    sq = x * x
    hi = sq.astype(BF16)
    lo = (sq - hi.astype(F32)).astype(BF16)
    ms = (jnp.dot(hi, bd, preferred_element_type=F32) + jnp.dot(lo, bd, preferred_element_type=F32)) * (1.0 / HEAD_DIM)
    return lax.rsqrt(ms + EPS)


def _rope(x, cos, sin_signed, first_half):
    partner = jnp.where(first_half, pltpu.roll(x, LANES - ROPE_FREQS, axis=1), pltpu.roll(x, ROPE_FREQS, axis=1))
    return x * cos + partner * sin_signed


def _dup_kv(x):
    lane = lax.broadcasted_iota(jnp.int32, (1, LANES), 1)
    swapped = pltpu.roll(x, HEAD_DIM, axis=1)
    return jnp.where(lane < HEAD_DIM, x, swapped), jnp.where(lane < HEAD_DIM, swapped, x)


def _tiled_keys(k):
    ka, kb = _dup_kv(k)
    return jnp.concatenate([ka, ka, kb, kb], axis=1).astype(BF16)


def _values_t(v):
    va, vb = _dup_kv(v)
    return jnp.concatenate([va.T, vb.T], axis=0).astype(BF16)


def _even_kernel(*refs, nc, n_chunks, lag, ring, windowed):
    if windowed:
        (xp_ref, xa_ref, modp_ref, moda_ref, g_ref, win_ref, cw_ref, cb_ref, qg_ref, kg_ref, sink_ref, wout_ref,
         cos_ref, sin_ref, ck_ref, cv_ref,
         y_ref,
         h_s, h_w, u_s, gate_s, q_s, k_s, vt_s, gb_s, u_w, gate_w, q_w, k_w, vt_w, gb_w, a_s, ck_s, cvt_s) = refs
        ko_ref = vo_ref = None
    else:
        (xp_ref, xa_ref, modp_ref, moda_ref, g_ref, win_f32, cw_ref, cb_ref, qg_ref, kg_ref, sink_ref, wout_f32,
         y_ref, ko_ref, vo_ref, win_ref, wout_ref,
         h_s, h_w, u_s, gate_s, q_s, k_s, vt_s, gb_s, u_w, gate_w, q_w, k_w, vt_w, gb_w, a_s) = refs
    s = pl.program_id(0)
    i = s - 1 - lag
    ci = lax.rem(i, jnp.int32(nc))

    @pl.when(s == 0)
    def _init():
        k_s[...] = jnp.zeros(k_s.shape, BF16)
        vt_s[...] = jnp.zeros(vt_s.shape, BF16)
        if not windowed:
            _cast_columns(win_f32, win_ref)
            _cast_columns(wout_f32, wout_ref)

    if windowed:
        @pl.when((i >= 0) & (ci == 0))
        def _load_ctx():
            ck_s[...] = _tiled_keys(ck_ref[0])
            cvt_s[...] = _values_t(cv_ref[0])

    def normalize(tokens):
        yield from _normalize_items(h_w, xp_ref, g_ref, modp_ref, tokens)

    def project():
        o_bg, o_cg, o_xs, o_ga, o_q, o_k, _, o_gb = EVEN_OFFS[:8]

        def proj(start):
            return jnp.dot(h_s[...], win_ref[start // NT], preferred_element_type=F32)

        lane = lax.broadcasted_iota(jnp.int32, (1, LANES), 1)
        first_half = (lane & (2 * ROPE_FREQS - 1)) < ROPE_FREQS
        ri = lax.broadcasted_iota(jnp.int32, (NT, NT), 0) >> HEAD_SHIFT
        cj = lax.broadcasted_iota(jnp.int32, (NT, NT), 1) >> HEAD_SHIFT
        bd = (ri == cj).astype(BF16)

        def qk_norm(x, g):
            w = x.shape[1]
            inv = _head_inv_rms(x, bd[0:w, 0:w])
            pieces = []
            for j in range(w // LANES):
                lanes = slice(j * LANES, (j + 1) * LANES)
                xj = x[:, lanes] * inv[:, lanes] * g
                pieces.append(_rope(xj, cos_ref[...], sin_ref[...], first_half) if windowed else xj)
            return pieces

        def keys_values(_):
            kv = proj(o_k)
            k, = qk_norm(kv[:, 0:KV_WIDTH], kg_ref[...])
            v = kv[:, KV_WIDTH:2 * KV_WIDTH]
            if not windowed:
                ko_ref[0] = k
                vo_ref[0] = v
            k_w[...] = _tiled_keys(k)
            vt_w[...] = _values_t(v)

        def queries(n):
            for j, qj in enumerate(qk_norm(proj(o_q + n * NT), qg_ref[...])):
                q_w[:, n * NT + j * LANES:n * NT + (j + 1) * LANES] = (qj * (ATTN_SCALE * LOG2E)).astype(BF16)

        def conv_gate(n):
            gate_w[:, n * NT:(n + 1) * NT] = proj(o_bg + n * NT) * _silu(proj(o_ga + n * NT))

        def conv_input(n):
            u_w[:, n * NT:(n + 1) * NT] = proj(o_cg + n * NT) * proj(o_xs + n * NT)

        def attn_gate(n):
            gb_w[:, n * NT:(n + 1) * NT] = _silu(proj(o_gb + n * NT))

        items = [(conv_gate, 0), (keys_values, 0), (conv_input, 0), (queries, 0), (conv_gate, 1), (queries, 1),
                 (conv_input, 1), (attn_gate, 0), (attn_gate, 1)]
        for emit, n in items:
            yield
            emit(n)

    def finish(slot, tokens):
        if windowed:
            prev_slot = lax.rem(slot + (ring - 1), jnp.int32(ring))
            next_slot = lax.rem(slot + 1, jnp.int32(ring))
            has_prev = ci > 0
            has_next = ci < nc - 1
        else:
            prev_slot = next_slot = has_prev = has_next = None

        yield
        ext = _with_halo(u_s, slot, slice(None), prev_slot, next_slot, has_prev, has_next)
        conv = (_rows_down(ext, 1)[HALO:HALO + TM] * cw_ref[0:1, :]
                + ext[HALO:HALO + TM] * cw_ref[1:2, :]
                + _rows_up(ext, 1)[HALO:HALO + TM] * cw_ref[2:3, :]
                + cb_ref[...])
        a_s[:, 0:CONV_WIDTH] = (gate_s[slot] * conv).astype(BF16)

        lane_head = lax.broadcasted_iota(jnp.int32, (1, GROUP_W), 1) >> HEAD_SHIFT
        cols4 = Q_PER_KV * BLOCK
        key_i =lax.broadcasted_iota(jnp.int32, (BLOCK, cols4), 0)
        qry_i = lax.broadcasted_iota(jnp.int32, (BLOCK, cols4), 1) & (BLOCK - 1)
        upper = lax.broadcasted_iota(jnp.int32, (LANES, LANES), 0) < HEAD_DIM
        halves = (slice(0, BLOCK), slice(BLOCK, TM))
        @functools.lru_cache(maxsize=None)
        def block_masks(qb):
            if qb == 0:
                blocks = ((prev_slot, halves[1]), (slot, halves[0]), (slot, halves[1]))
                prev_in, next_in = has_prev, True
            else:
                blocks = ((slot, halves[0]), (slot, halves[1]), (next_slot, halves[0]))
                prev_in, next_in = True, has_next
            prev_ok = (key_i - qry_i) >= jnp.where(prev_in, 0, BLOCK)
            next_ok = (qry_i - key_i) >= jnp.where(next_in, 0, BLOCK)
            return blocks, prev_ok, next_ok

        def scores(qb, kv):
            cols = slice(kv * GROUP_W, (kv + 1) * GROUP_W)
            vrows = slice(kv * LANES, (kv + 1) * LANES)
            qblk = q_s[slot, halves[qb], cols]
            q4 = jnp.concatenate(
                [jnp.where(lane_head == g, qblk, jnp.zeros_like(qblk)) for g in range(Q_PER_KV)], axis=0)

            def scores_t(kk):
                return lax.dot_general(kk, q4, (((1,), (1,)), ((), ())), preferred_element_type=F32)

            if not windowed:
                return [([scores_t(k_s[slot, :, cols])], vt_s[slot, vrows, :])]
            blocks, prev_ok, next_ok = block_masks(qb)
            s_loc = scores_t(jnp.concatenate([k_s[sl, r, cols] for sl, r in blocks], axis=0))
            vt_loc = jnp.concatenate([vt_s[sl, vrows, r] for sl, r in blocks], axis=1)
            s_ctx = scores_t(ck_s[:, cols])
            return [([jnp.where(prev_ok, s_loc[0:BLOCK], NEG), s_loc[BLOCK:2 * BLOCK],
                      jnp.where(next_ok, s_loc[2 * BLOCK:3 * BLOCK], NEG)], vt_loc),
                    ([s_ctx], cvt_s[vrows, :])]

        def softmax(kv, groups):
            snk = jnp.concatenate(
                [jnp.full((1, BLOCK), sink_ref[kv * Q_PER_KV + g] * LOG2E, F32) for g in range(Q_PER_KV)], axis=1)
            if tokens:
                token = tokens.pop(0)
                while tokens:
                    token = token + tokens.pop(0)
                snk = snk + jnp.concatenate([token] * Q_PER_KV, axis=1)
            mx = snk
            for ss, _ in groups:
                for sc in ss:
                    mx = jnp.maximum(mx, jnp.max(sc, axis=0, keepdims=True))
            den = jnp.exp2(snk - mx)
            probs = []
            for ss, vt in groups:
                es = []
                for sc in ss:
                    e = jnp.exp2(sc - mx)
                    den = den + jnp.sum(e, axis=0, keepdims=True)
                    es.append(e.astype(BF16))
                probs.append((es[0] if len(es) == 1 else jnp.concatenate(es, axis=0), vt))
            return probs, den

        def values(qb, kv, probs, den):
            o_t = jnp.zeros((LANES, cols4), F32)
            for e_all, vt in probs:
                o_t = o_t + jnp.dot(vt, e_all, preferred_element_type=F32)
            o_t = o_t * (1.0 / den)
            o01 = jnp.where(upper, o_t[:, 0:BLOCK], o_t[:, BLOCK:2 * BLOCK]).T
            o23 = jnp.where(upper, o_t[:, 2 * BLOCK:3 * BLOCK], o_t[:, 3 * BLOCK:4 * BLOCK]).T
            a_s[halves[qb], CONV_WIDTH + kv * GROUP_W:CONV_WIDTH + (kv + 1) * GROUP_W] = (
                jnp.concatenate([o01, o23], axis=1) * gb_s[slot, halves[qb], kv * GROUP_W:(kv + 1) * GROUP_W]
            ).astype(BF16)

        pairs = [(qb, kv) for qb in range(TM // BLOCK) for kv in range(N_KV_HEADS)]
        yield
        ahead = scores(*pairs[0])
        for t, (qb, kv) in enumerate(pairs):
            groups = ahead
            if t + 1 < len(pairs):
                yield
                ahead = scores(*pairs[t + 1])
            yield
            probs, den = softmax(kv, groups)
            yield
            values(qb, kv, probs, den)

        yield from _out_proj_items(lambda: a_s[...], wout_ref, xa_ref, moda_ref, y_ref)

    staged = ((u_w, u_s), (gate_w, gate_s), (q_w, q_s), (k_w, k_s), (vt_w, vt_s), (gb_w, gb_s))
    order = EVEN_ORDER_LATENT if windowed else EVEN_ORDER_CONTEXT
    _pipeline_steps(s, n_chunks, lag, ring, (h_w, h_s), staged, order, normalize, project, finish)


def _chunk_specs(n_chunks, nc, lag, per_seq_mod, tm=TM):
    last = n_chunks - 1
    norm_chunk = lambda s: jnp.minimum(s, last)
    proj_chunk = lambda s: jnp.clip(s - 1, 0, last)
    fin_chunk = lambda s: jnp.clip(s - 1 - lag, 0, last)
    seq_of = (lambda j: 1 + j // nc) if per_seq_mod else (lambda j: 0)
    return dict(
        x_norm=pl.BlockSpec((1, tm, D_MODEL), lambda s: (norm_chunk(s), 0, 0)),
        x_fin=pl.BlockSpec((1, tm, D_MODEL), lambda s: (fin_chunk(s), 0, 0)),
        mod_norm=pl.BlockSpec((1, 1, 3 * D_MODEL), lambda s: (seq_of(norm_chunk(s)), 0, 0)),
        mod_fin=pl.BlockSpec((1, 1, 3 * D_MODEL), lambda s: (seq_of(fin_chunk(s)), 0, 0)),
        proj_chunk=proj_chunk, fin_chunk=fin_chunk)


def _const_spec(shape):
    return pl.BlockSpec(shape, lambda c: (0,) * len(shape))


def _even_layer(x, mod, norm_g, w_in, conv_w, conv_b, q_g, k_g, sink, w_out, rope=None, ctx=None):
    n_seq, seq_len, _ = x.shape
    windowed = rope is not None
    nc = seq_len // TM
    n_chunks = n_seq * nc
    lag, ring = (2, 4) if windowed else (1, 2)
    assert windowed or nc == 1
    sp = _chunk_specs(n_chunks, nc, lag, windowed)
    xc = x.reshape(n_chunks, TM, D_MODEL)
    in_specs = [
        sp["x_norm"], sp["x_fin"], sp["mod_norm"], sp["mod_fin"],
        _const_spec((1, D_MODEL)),
        _const_spec(w_in.shape),
        _const_spec((3, CONV_WIDTH)),
        _const_spec((1, CONV_WIDTH)),
        _const_spec((1, LANES)),
        _const_spec((1, LANES)),
        pl.BlockSpec(memory_space=pltpu.SMEM),
        _const_spec(w_out.shape),
    ]
    args = [xc, xc, mod, mod, norm_g.reshape(1, D_MODEL), w_in, conv_w, conv_b.reshape(1, CONV_WIDTH),
            q_g, k_g, sink, w_out]
    per_chunk = [
        ((TM, CONV_WIDTH), F32),
        ((TM, CONV_WIDTH), F32),
        ((TM, ATTN_WIDTH), BF16),
        ((TM, 2 * GROUP_W), BF16),
        ((2 * LANES, TM), BF16),
        ((TM, ATTN_WIDTH), F32),
    ]
    scratch = (
        [pltpu.VMEM((TM, D_MODEL), BF16)] * 2
        + [pltpu.VMEM((ring,) + shape, dt) for shape, dt in per_chunk]
        + [pltpu.VMEM(shape, dt) for shape, dt in per_chunk]
        + [pltpu.VMEM((TM, D_MODEL), BF16)])
    y_shape = jax.ShapeDtypeStruct(xc.shape, F32)
    y_spec = pl.BlockSpec((1, TM, D_MODEL), lambda c: (sp["fin_chunk"](c), 0, 0))
    if windowed:
        cos, sin = rope
        ck, cv = ctx
        rope_spec = pl.BlockSpec((TM, LANES), lambda c: (sp["proj_chunk"](c) % nc, 0))
        ctx_spec = pl.BlockSpec((1,) + ck.shape[1:], lambda c: (sp["fin_chunk"](c) // nc, 0, 0))
        in_specs += [rope_spec, rope_spec, ctx_spec, ctx_spec]
        args += [cos, sin, ck, cv]
        scratch += [pltpu.VMEM((ck.shape[1], 2 * GROUP_W), BF16), pltpu.VMEM((2 * LANES, cv.shape[1]), BF16)]
        out_shape, out_specs = y_shape, y_spec
    else:
        kv_shape = jax.ShapeDtypeStruct((n_chunks, TM, KV_WIDTH), F32)
        kv_spec = pl.BlockSpec((1, TM, KV_WIDTH), lambda c: (sp["proj_chunk"](c), 0, 0))
        out_shape = (y_shape, kv_shape, kv_shape,
                     jax.ShapeDtypeStruct(_tiled_shape(w_in), BF16), jax.ShapeDtypeStruct(_tiled_shape(w_out), BF16))
        out_specs = (y_spec, kv_spec, kv_spec, _const_spec(_tiled_shape(w_in)), _const_spec(_tiled_shape(w_out)))
    out = pl.pallas_call(
        functools.partial(_even_kernel, nc=nc, n_chunks=n_chunks, lag=lag, ring=ring, windowed=windowed),
        out_shape=out_shape,
        grid=(n_chunks + lag + 1,),
        in_specs=in_specs,
        out_specs=out_specs,
        scratch_shapes=scratch,
        compiler_params=PIPELINE_PARAMS,
        name="even_latent" if windowed else "even_context",
    )(*args)
    if windowed:
        return out.reshape(x.shape)
    y, k, v, w_in_bf16, w_out_bf16 = out
    return (y.reshape(x.shape), k.reshape(n_seq, seq_len, KV_WIDTH), v.reshape(n_seq, seq_len, KV_WIDTH),
            w_in_bf16, w_out_bf16)


def _odd_kernel(*refs, nc, n_chunks, lag, ring, cast_weights):
    if cast_weights:
        (xp_ref, xa_ref, modp_ref, moda_ref, g_ref, win_f32, pw_f32, ps_ref, wout_f32,
         y_ref, win_ref, pw_ref, wout_ref, h_s, h_w, u_s, sg_s, u_w, sg_w) = refs
    else:
        (xp_ref, xa_ref, modp_ref, moda_ref, g_ref, win_ref, pw_ref, ps_ref, wout_ref,
         y_ref, h_s, h_w, u_s, sg_s, u_w, sg_w) = refs
    tm = h_s.shape[0]
    s = pl.program_id(0)
    i = s - 1 - lag
    ci = lax.rem(i, jnp.int32(nc))

    if cast_weights:
        @pl.when(s == 0)
        def _cast():
            _cast_columns(win_f32, win_ref)
            _cast_columns(wout_f32, wout_ref)
            for gi in range(pw_ref.shape[0]):
                pw_ref[gi] = pw_f32[gi].astype(BF16)

    def normalize(tokens):
        yield from _normalize_items(h_w, xp_ref, g_ref, modp_ref, tokens)

    def project():
        for n in range(D_MODEL // NT):
            yield
            cols = slice(n * NT, (n + 1) * NT)
            u = jnp.dot(h_s[...], win_ref[n], preferred_element_type=F32)
            u_w[:, cols] = u
        for n in range(D_MODEL // NT):
            yield
            cols = slice(n * NT, (n + 1) * NT)
            sg = _silu(jnp.dot(h_s[...], win_ref[D_MODEL // NT + n], preferred_element_type=F32))
            sg_w[:, cols] = sg

    def finish(slot, tokens):
        if nc > 1:
            prev_slot = lax.rem(slot + (ring - 1), jnp.int32(ring))
            next_slot = lax.rem(slot + 1, jnp.int32(ring))
            has_prev = ci > 0
            has_next = ci < nc - 1
        else:
            prev_slot = next_slot = has_prev = has_next = None
        t = ci * tm + lax.broadcasted_iota(jnp.int32, (tm, 1), 0)
        outs = []
        for gi, w in enumerate(POOL_SIZES):
            yield
            cols = slice(gi * POOL_GROUP, (gi + 1) * POOL_GROUP)
            ext = _with_halo(u_s, slot, cols, prev_slot, next_slot, has_prev, has_next)
            acc = ext
            span = 1
            while span < w:
                acc = acc + _rows_down(acc, span)
                span *= 2
            if w > 2:
                acc = _rows_up(acc, w // 2 - 1)
            cnt = (jnp.minimum(t + w // 2, nc * tm) - jnp.maximum(t - w // 2, 0)).astype(F32)
            while tokens:
                cnt = cnt + tokens.pop(0)[:, 0:1]
            pooled = acc[HALO:HALO + tm] / cnt - ext[HALO:HALO + tm]
            outs.append(jnp.dot(pooled.astype(BF16), pw_ref[gi], preferred_element_type=F32))
        yield
        y = (jnp.concatenate(outs, axis=1) * ps_ref[...] * sg_s[slot]).astype(BF16)
        yield from _out_proj_items(lambda: y, wout_ref, xa_ref, moda_ref, y_ref)

    _pipeline_steps(s, n_chunks, lag, ring, (h_w, h_s), ((u_w, u_s), (sg_w, sg_s)), ODD_ORDER,
                    normalize, project, finish)


def _odd_layer(x, mod, per_seq_mod, norm_g, w_in, pool_w, pool_scale, w_out):
    n_seq, seq_len, _ = x.shape
    tm = TM
    nc = seq_len // tm
    n_chunks = n_seq * nc
    lag, ring = (2, 4) if nc > 1 else (1, 2)
    sp = _chunk_specs(n_chunks, nc, lag, per_seq_mod, tm)
    xc = x.reshape(n_chunks, tm, D_MODEL)
    in_specs = [
        sp["x_norm"], sp["x_fin"], sp["mod_norm"], sp["mod_fin"],
        _const_spec((1, D_MODEL)),
        _const_spec(w_in.shape),
        _const_spec(pool_w.shape),
        _const_spec((1, D_MODEL)),
        _const_spec(w_out.shape),
    ]
    cast_weights = w_in.dtype != BF16
    out_shape = jax.ShapeDtypeStruct(xc.shape, F32)
    out_specs = pl.BlockSpec((1, tm, D_MODEL), lambda c: (sp["fin_chunk"](c), 0, 0))
    if cast_weights:
        shapes = (_tiled_shape(w_in), pool_w.shape, _tiled_shape(w_out))
        out_shape = (out_shape,) + tuple(jax.ShapeDtypeStruct(shape, BF16) for shape in shapes)
        out_specs = (out_specs,) + tuple(_const_spec(shape) for shape in shapes)
    out = pl.pallas_call(
        functools.partial(_odd_kernel, nc=nc, n_chunks=n_chunks, lag=lag, ring=ring, cast_weights=cast_weights),
        out_shape=out_shape,
        grid=(n_chunks + lag + 1,),
        in_specs=in_specs,
        out_specs=out_specs,
        scratch_shapes=[
            pltpu.VMEM((tm, D_MODEL), BF16),
            pltpu.VMEM((tm, D_MODEL), BF16),
            pltpu.VMEM((ring, tm, D_MODEL), F32),
            pltpu.VMEM((ring, tm, D_MODEL), F32),
            pltpu.VMEM((tm, D_MODEL), F32),
            pltpu.VMEM((tm, D_MODEL), F32),
        ],
        compiler_params=PIPELINE_PARAMS,
        name="odd_latent" if per_seq_mod else "odd_context",
    )(xc, xc, mod, mod, norm_g.reshape(1, D_MODEL), w_in, pool_w, pool_scale.reshape(1, D_MODEL), w_out)
    if cast_weights:
        y, *bf16_weights = out
        return y.reshape(x.shape), bf16_weights
    return out.reshape(x.shape)


def _rope_tables(seq_len):
    n_rows = seq_len // GRID_W
    row = np.repeat(np.arange(n_rows), GRID_W).astype(np.float32)
    col = np.tile(np.arange(GRID_W), n_rows).astype(np.float32)
    inv = (np.float32(ROPE_BASE) ** (-np.arange(ROPE_FREQS, dtype=np.float32) / np.float32(ROPE_FREQS))).astype(np.float32)
    ang = np.stack([row[:, None] * inv, col[:, None] * inv], axis=1)
    cos = np.broadcast_to(np.cos(ang)[:, :, None, :], (seq_len, 2, 2, ROPE_FREQS))
    sin = np.sin(ang)[:, :, None, :] * np.array([-1.0, 1.0], np.float32)[None, None, :, None]
    cos = cos.reshape(seq_len, HEAD_DIM).astype(np.float32)
    sin = sin.reshape(seq_len, HEAD_DIM).astype(np.float32)
    return jnp.asarray(np.tile(cos, (1, LANES // HEAD_DIM))), jnp.asarray(np.tile(sin, (1, LANES // HEAD_DIM)))


def kernel(x_prompt, x_sample, cache_k, cache_v, c, c_ctx, ada_w_e, ada_b_e, norm_g_e, w_in_e, conv_w, conv_b,
           q_norm_g, k_norm_g, sink, w_out_e, ada_w_o, ada_b_o, norm_g_o, w_in_o, pool_w, pool_scale, w_out_o):
    n_dec = x_sample.shape[0]
    depth = ada_w_e.shape[0] + ada_w_o.shape[0]
    assert ada_w_e.shape[0] == 1 and ada_w_o.shape[0] == 1 and n_dec + 1 <= 8
    cond8 = jnp.concatenate([c_ctx[None], c, jnp.zeros((8 - 1 - n_dec, D_MODEL), F32)], axis=0)
    mod_e, mod_o = _adaln(cond8, ada_w_e[0], ada_b_e[0], ada_w_o[0], ada_b_o[0])
    rope = _rope_tables(x_sample.shape[1])

    yp, ys = x_prompt, x_sample
    new_k, new_v = [], []
    for layer in range(depth):
        i = layer // 2
        if layer % 2 == 0:
            mod = mod_e.reshape(8, 1, 3 * D_MODEL)
            gains = [jnp.tile(g[i], LANES // HEAD_DIM).reshape(1, LANES) for g in (q_norm_g, k_norm_g)]
            small = (conv_w[i], conv_b[i], *gains, sink[i])
            yp, k, v, w_in, w_out = _even_layer(yp, mod, norm_g_e[i], w_in_e[i], *small, w_out_e[i])
            new_k.append(k.reshape(k.shape[0], k.shape[1], N_KV_HEADS, HEAD_DIM))
            new_v.append(v.reshape(v.shape[0], v.shape[1], N_KV_HEADS, HEAD_DIM))
            ck = cache_k[:, i].reshape(n_dec, cache_k.shape[2], KV_WIDTH)
            cv = cache_v[:, i].reshape(n_dec, cache_v.shape[2], KV_WIDTH)
            ys = _even_layer(ys, mod, norm_g_e[i], w_in, *small, w_out, rope=rope, ctx=(ck, cv))
        else:
            mod = mod_o.reshape(8, 1, 3 * D_MODEL)
            yp, (w_in, w_pool, w_out) = _odd_layer(yp, mod, False, norm_g_o[i], w_in_o[i], pool_w[i],
                                                   pool_scale[i], w_out_o[i])
            ys = _odd_layer(ys, mod, True, norm_g_o[i], w_in, w_pool, pool_scale[i], w_out)
    return yp, ys, jnp.stack(new_k, axis=1), jnp.stack(new_v, axis=1)
```

```python
import functools

import jax
import jax.numpy as jnp
import numpy as np
from jax import lax
from jax.experimental import pallas as pl
from jax.experimental.pallas import tpu as pltpu

F32 = jnp.float32
BF16 = jnp.bfloat16

D_MODEL = 1024
GRID_W = 64
HEAD_DIM = 64
HEAD_SHIFT = 6
ATTN_WIDTH = D_MODEL // 2
N_Q_HEADS = ATTN_WIDTH // HEAD_DIM
N_KV_HEADS = N_Q_HEADS // 4
Q_PER_KV = N_Q_HEADS // N_KV_HEADS
KV_WIDTH = N_KV_HEADS * HEAD_DIM
CONV_WIDTH = D_MODEL - ATTN_WIDTH
WINDOW = 128
BLOCK = 128
ROPE_BASE = 10000.0
ROPE_FREQS = HEAD_DIM // 4
ATTN_SCALE = HEAD_DIM ** -0.5
LOG2E = float(np.log2(np.e))
NEG = -1e30
POOL_SIZES = (2, 4, 8, 16)
POOL_GROUP = D_MODEL // len(POOL_SIZES)
EPS = 1e-6
EVEN_SIZES = (CONV_WIDTH, CONV_WIDTH, CONV_WIDTH, CONV_WIDTH, ATTN_WIDTH, KV_WIDTH, KV_WIDTH, ATTN_WIDTH)
EVEN_IN = sum(EVEN_SIZES)
EVEN_OFFS = tuple(int(s) for s in np.cumsum((0,) + EVEN_SIZES))

TM = 256
NT = 256
NORM_ROWS = 64
EVEN_ORDER_LATENT = "PF" + "F" + "PFNFPF" * 3 + "PNFPF" + "FFFF"
EVEN_ORDER_CONTEXT = "PF" + "F" + "FNPFPF" * 3 + "NPFPF" + "FFFF"
ODD_ORDER = "PNFP" * 4 + "FFFFF"
LANES = 128
GROUP_W = Q_PER_KV * HEAD_DIM
HALO = 8
ADA_TN = 768
VMEM_LIMIT = 56 * 1024 * 1024
PIPELINE_PARAMS = pltpu.CompilerParams(dimension_semantics=("arbitrary",), vmem_limit_bytes=VMEM_LIMIT)


def _silu(x):
    half = 0.5 * x
    return half + half * jnp.tanh(half)


def _rows_down(x, k):
    return pltpu.roll(x, k, axis=0)


def _rows_up(x, k):
    return pltpu.roll(x, x.shape[0] - k, axis=0)


def _rms_modulate(x, g, shift, scale):
    ms = jnp.mean(x * x, axis=-1, keepdims=True)
    return (x * lax.rsqrt(ms + EPS) * g) * (1.0 + scale) + shift


def _cast_columns(src_ref, dst_ref):
    for n in range(dst_ref.shape[0]):
        dst_ref[n] = src_ref[:, n * NT:(n + 1) * NT].astype(dst_ref.dtype)


def _tiled_shape(w):
    return (w.shape[1] // NT, w.shape[0], NT)


def _opaque_zero(x):
    bits = pltpu.bitcast(x, jnp.uint32)
    return pltpu.bitcast((bits >> 16) >> 16, F32)


def _normalize_items(h_s, x_ref, g_ref, mod_ref, tokens):
    shift = mod_ref[0, :, 0:D_MODEL]
    scale = mod_ref[0, :, D_MODEL:2 * D_MODEL]
    for r in range(0, h_s.shape[0], NORM_ROWS):
        yield
        h = _rms_modulate(x_ref[0, r:r + NORM_ROWS], g_ref[...], shift, scale)
        h_s[r:r + NORM_ROWS] = h.astype(BF16)
        tokens.append(_opaque_zero(h[0:8, 0:LANES])[0:1])


def _out_proj_items(get_a, wout_ref, x_ref, mod_ref, y_ref):
    for n in range(D_MODEL // NT):
        yield
        cols = slice(n * NT, (n + 1) * NT)
        mixed = jnp.dot(get_a(), wout_ref[n], preferred_element_type=F32)
        y_ref[0, :, cols] = x_ref[0, :, cols] + mod_ref[0, :, 2 * D_MODEL + n * NT:2 * D_MODEL + (n + 1) * NT] * mixed


def _weave(order, **stages):
    by_letter = {name[0].upper(): stage for name, stage in stages.items()}
    for stage in by_letter.values():
        next(stage, None)
    for letter in order:
        if letter in by_letter:
            next(by_letter[letter], None)
    for stage in by_letter.values():
        for _ in stage:
            pass


def _with_halo(u_s, slot, cols, prev_slot, next_slot, has_prev, has_next):
    u = u_s[slot, :, cols]
    zero = jnp.zeros((HALO, u.shape[1]), F32)
    if prev_slot is None:
        top = bot = zero
    else:
        rows = u_s.shape[1]
        top = jnp.where(has_prev, u_s[prev_slot, rows - HALO:rows, cols], zero)
        bot = jnp.where(has_next, u_s[next_slot, 0:HALO, cols], zero)
    return jnp.concatenate([top, u, bot], axis=0)


def _pipeline_steps(s, n_chunks, lag, ring, normalized, staged, order, normalize, project, finish):
    fin_slot = lax.rem(s + (2 * ring - 1 - lag), jnp.int32(ring))

    @pl.when((s >= 1) & (s <= n_chunks))
    def _publish_normalized():
        h_w, h_s = normalized
        h_s[...] = h_w[...]

    @pl.when((s >= 2) & (s <= n_chunks + 1))
    def _publish_projected():
        slot = lax.rem(s + (ring - 2), jnp.int32(ring))
        for stage_ref, ring_ref in staged:
            ring_ref[slot] = stage_ref[...]

    @pl.when(s == 0)
    def _first():
        _weave(order, normalize=normalize([]))

    @pl.when((s >= 1) & (s <= lag))
    def _fill():
        _weave(order, project=project(), normalize=normalize([]))

    @pl.when((s > lag) & (s <= n_chunks))
    def _steady():
        tokens = []
        _weave(order, finish=finish(fin_slot, tokens), project=project(), normalize=normalize(tokens))

    @pl.when(s > n_chunks)
    def _drain():
        _weave(order, finish=finish(fin_slot, []))


def _adaln_kernel(cond_ref, we_ref, be_ref, wo_ref, bo_ref, oe_ref, oo_ref):
    s = _silu(cond_ref[...]).astype(BF16)
    oe = jnp.dot(s, we_ref[...].astype(BF16), preferred_element_type=F32) + be_ref[...]
    oo = jnp.dot(s, wo_ref[...].astype(BF16), preferred_element_type=F32) + bo_ref[...]
    for r in range(oe_ref.shape[0]):
        oe_ref[r] = oe[r:r + 1]
        oo_ref[r] = oo[r:r + 1]


def _adaln(cond8, w_e, b_e, w_o, b_o):
    n = 3 * D_MODEL
    wspec = pl.BlockSpec((D_MODEL, ADA_TN), lambda j: (0, j))
    vspec = pl.BlockSpec((1, ADA_TN), lambda j: (0, j))
    ospec = pl.BlockSpec((8, 1, ADA_TN), lambda j: (0, 0, j))
    return pl.pallas_call(
        _adaln_kernel,
        out_shape=(jax.ShapeDtypeStruct((8, 1, n), F32), jax.ShapeDtypeStruct((8, 1, n), F32)),
        grid=(n // ADA_TN,),
        in_specs=[pl.BlockSpec((8, D_MODEL), lambda j: (0, 0)), wspec, vspec, wspec, vspec],
        out_specs=(ospec, ospec),
        compiler_params=pltpu.CompilerParams(dimension_semantics=("arbitrary",), vmem_limit_bytes=VMEM_LIMIT),
        name="adaln",
    )(cond8, w_e, b_e.reshape(1, n), w_o, b_o.reshape(1, n))


def _head_inv_rms(x, bd):
    sq = x * x
    hi = sq.astype(BF16)
    lo = (sq - hi.astype(F32)).astype(BF16)
    ms = (jnp.dot(hi, bd, preferred_element_type=F32) + jnp.dot(lo, bd, preferred_element_type=F32)) * (1.0 / HEAD_DIM)
    return lax.rsqrt(ms + EPS)


def _rope(x, cos, sin_signed, first_half):
    partner = jnp.where(first_half, pltpu.roll(x, LANES - ROPE_FREQS, axis=1), pltpu.roll(x, ROPE_FREQS, axis=1))
    return x * cos + partner * sin_signed


def _dup_kv(x):
    lane = lax.broadcasted_iota(jnp.int32, (1, LANES), 1)
    swapped = pltpu.roll(x, HEAD_DIM, axis=1)
    return jnp.where(lane < HEAD_DIM, x, swapped), jnp.where(lane < HEAD_DIM, swapped, x)


def _tiled_keys(k):
    ka, kb = _dup_kv(k)
    return jnp.concatenate([ka, ka, kb, kb], axis=1).astype(BF16)


def _values_t(v):
    va, vb = _dup_kv(v)
    return jnp.concatenate([va.T, vb.T], axis=0).astype(BF16)


def _even_kernel(*refs, nc, n_chunks, lag, ring, windowed):
    if windowed:
        (xp_ref, xa_ref, modp_ref, moda_ref, g_ref, win_ref, cw_ref, cb_ref, qg_ref, kg_ref, sink_ref, wout_ref,
         cos_ref, sin_ref, ck_ref, cv_ref,
         y_ref,
         h_s, h_w, u_s, gate_s, q_s, k_s, vt_s, gb_s, u_w, gate_w, q_w, k_w, vt_w, gb_w, a_s, ck_s, cvt_s) = refs
        ko_ref = vo_ref = None
    else:
        (xp_ref, xa_ref, modp_ref, moda_ref, g_ref, win_f32, cw_ref, cb_ref, qg_ref, kg_ref, sink_ref, wout_f32,
         y_ref, ko_ref, vo_ref, win_ref, wout_ref,
         h_s, h_w, u_s, gate_s, q_s, k_s, vt_s, gb_s, u_w, gate_w, q_w, k_w, vt_w, gb_w, a_s) = refs
    s = pl.program_id(0)
    i = s - 1 - lag
    ci = lax.rem(i, jnp.int32(nc))

    @pl.when(s == 0)
    def _init():
        k_s[...] = jnp.zeros(k_s.shape, BF16)
        vt_s[...] = jnp.zeros(vt_s.shape, BF16)
        if not windowed:
            _cast_columns(win_f32, win_ref)
            _cast_columns(wout_f32, wout_ref)

    if windowed:
        @pl.when((i >= 0) & (ci == 0))
        def _load_ctx():
            ck_s[...] = _tiled_keys(ck_ref[0])
            cvt_s[...] = _values_t(cv_ref[0])

    def normalize(tokens):
        yield from _normalize_items(h_w, xp_ref, g_ref, modp_ref, tokens)

    def project():
        o_bg, o_cg, o_xs, o_ga, o_q, o_k, _, o_gb = EVEN_OFFS[:8]

        def proj(start):
            return jnp.dot(h_s[...], win_ref[start // NT], preferred_element_type=F32)

        lane = lax.broadcasted_iota(jnp.int32, (1, LANES), 1)
        first_half = (lane & (2 * ROPE_FREQS - 1)) < ROPE_FREQS
        ri = lax.broadcasted_iota(jnp.int32, (NT, NT), 0) >> HEAD_SHIFT
        cj = lax.broadcasted_iota(jnp.int32, (NT, NT), 1) >> HEAD_SHIFT
        bd = (ri == cj).astype(BF16)

        def qk_norm(x, g):
            w = x.shape[1]
            inv = _head_inv_rms(x, bd[0:w, 0:w])
            pieces = []
            for j in range(w // LANES):
                lanes = slice(j * LANES, (j + 1) * LANES)
                xj = x[:, lanes] * inv[:, lanes] * g
                pieces.append(_rope(xj, cos_ref[...], sin_ref[...], first_half) if windowed else xj)
            return pieces

        def keys_values(_):
            kv = proj(o_k)
            k, = qk_norm(kv[:, 0:KV_WIDTH], kg_ref[...])
            v = kv[:, KV_WIDTH:2 * KV_WIDTH]
            if not windowed:
                ko_ref[0] = k
                vo_ref[0] = v
            k_w[...] = _tiled_keys(k)
            vt_w[...] = _values_t(v)

        def queries(n):
            for j, qj in enumerate(qk_norm(proj(o_q + n * NT), qg_ref[...])):
                q_w[:, n * NT + j * LANES:n * NT + (j + 1) * LANES] = (qj * (ATTN_SCALE * LOG2E)).astype(BF16)

        def conv_gate(n):
            gate_w[:, n * NT:(n + 1) * NT] = proj(o_bg + n * NT) * _silu(proj(o_ga + n * NT))

        def conv_input(n):
            u_w[:, n * NT:(n + 1) * NT] = proj(o_cg + n * NT) * proj(o_xs + n * NT)

        def attn_gate(n):
            gb_w[:, n * NT:(n + 1) * NT] = _silu(proj(o_gb + n * NT))

        items = [(conv_gate, 0), (keys_values, 0), (conv_input, 0), (queries, 0), (conv_gate, 1), (queries, 1),
                 (conv_input, 1), (attn_gate, 0), (attn_gate, 1)]
        for emit, n in items:
            yield
            emit(n)

    def finish(slot, tokens):
        if windowed:
            prev_slot = lax.rem(slot + (ring - 1), jnp.int32(ring))
            next_slot = lax.rem(slot + 1, jnp.int32(ring))
            has_prev = ci > 0
            has_next = ci < nc - 1
        else:
            prev_slot = next_slot = has_prev = has_next = None

        yield
        ext = _with_halo(u_s, slot, slice(None), prev_slot, next_slot, has_prev, has_next)
        conv = (_rows_down(ext, 1)[HALO:HALO + TM] * cw_ref[0:1, :]
                + ext[HALO:HALO + TM] * cw_ref[1:2, :]
                + _rows_up(ext, 1)[HALO:HALO + TM] * cw_ref[2:3, :]
                + cb_ref[...])
        a_s[:, 0:CONV_WIDTH] = (gate_s[slot] * conv).astype(BF16)

        lane_head = lax.broadcasted_iota(jnp.int32, (1, GROUP_W), 1) >> HEAD_SHIFT
        cols4 = Q_PER_KV * BLOCK
        key_i =lax.broadcasted_iota(jnp.int32, (BLOCK, cols4), 0)
        qry_i = lax.broadcasted_iota(jnp.int32, (BLOCK, cols4), 1) & (BLOCK - 1)
        upper = lax.broadcasted_iota(jnp.int32, (LANES, LANES), 0) < HEAD_DIM
        halves = (slice(0, BLOCK), slice(BLOCK, TM))
        @functools.lru_cache(maxsize=None)
        def block_masks(qb):
            if qb == 0:
                blocks = ((prev_slot, halves[1]), (slot, halves[0]), (slot, halves[1]))
                prev_in, next_in = has_prev, True
            else:
                blocks = ((slot, halves[0]), (slot, halves[1]), (next_slot, halves[0]))
                prev_in, next_in = True, has_next
            prev_ok = (key_i - qry_i) >= jnp.where(prev_in, 0, BLOCK)
            next_ok = (qry_i - key_i) >= jnp.where(next_in, 0, BLOCK)
            return blocks, prev_ok, next_ok

        def scores(qb, kv):
            cols = slice(kv * GROUP_W, (kv + 1) * GROUP_W)
            vrows = slice(kv * LANES, (kv + 1) * LANES)
            qblk = q_s[slot, halves[qb], cols]
            q4 = jnp.concatenate(
                [jnp.where(lane_head == g, qblk, jnp.zeros_like(qblk)) for g in range(Q_PER_KV)], axis=0)

            def scores_t(kk):
                return lax.dot_general(kk, q4, (((1,), (1,)), ((), ())), preferred_element_type=F32)

            if not windowed:
                return [([scores_t(k_s[slot, :, cols])], vt_s[slot, vrows, :])]
            blocks, prev_ok, next_ok = block_masks(qb)
            s_loc = scores_t(jnp.concatenate([k_s[sl, r, cols] for sl, r in blocks], axis=0))
            vt_loc = jnp.concatenate([vt_s[sl, vrows, r] for sl, r in blocks], axis=1)
            s_ctx = scores_t(ck_s[:, cols])
            return [([jnp.where(prev_ok, s_loc[0:BLOCK], NEG), s_loc[BLOCK:2 * BLOCK],
                      jnp.where(next_ok, s_loc[2 * BLOCK:3 * BLOCK], NEG)], vt_loc),
                    ([s_ctx], cvt_s[vrows, :])]

        def softmax(kv, groups):
            snk = jnp.concatenate(
                [jnp.full((1, BLOCK), sink_ref[kv * Q_PER_KV + g] * LOG2E, F32) for g in range(Q_PER_KV)], axis=1)
            if tokens:
                token = tokens.pop(0)
                while tokens:
                    token = token + tokens.pop(0)
                snk = snk + jnp.concatenate([token] * Q_PER_KV, axis=1)
            mx = snk
            for ss, _ in groups:
                for sc in ss:
                    mx = jnp.maximum(mx, jnp.max(sc, axis=0, keepdims=True))
            den = jnp.exp2(snk - mx)
            probs = []
            for ss, vt in groups:
                es = []
                for sc in ss:
                    e = jnp.exp2(sc - mx)
                    den = den + jnp.sum(e, axis=0, keepdims=True)
                    es.append(e.astype(BF16))
                probs.append((es[0] if len(es) == 1 else jnp.concatenate(es, axis=0), vt))
            return probs, den

        def values(qb, kv, probs, den):
            o_t = jnp.zeros((LANES, cols4), F32)
            for e_all, vt in probs:
                o_t = o_t + jnp.dot(vt, e_all, preferred_element_type=F32)
            o_t = o_t * (1.0 / den)
            o01 = jnp.where(upper, o_t[:, 0:BLOCK], o_t[:, BLOCK:2 * BLOCK]).T
            o23 = jnp.where(upper, o_t[:, 2 * BLOCK:3 * BLOCK], o_t[:, 3 * BLOCK:4 * BLOCK]).T
            a_s[halves[qb], CONV_WIDTH + kv * GROUP_W:CONV_WIDTH + (kv + 1) * GROUP_W] = (
                jnp.concatenate([o01, o23], axis=1) * gb_s[slot, halves[qb], kv * GROUP_W:(kv + 1) * GROUP_W]
            ).astype(BF16)

        pairs = [(qb, kv) for qb in range(TM // BLOCK) for kv in range(N_KV_HEADS)]
        yield
        ahead = scores(*pairs[0])
        for t, (qb, kv) in enumerate(pairs):
            groups = ahead
            if t + 1 < len(pairs):
                yield
                ahead = scores(*pairs[t + 1])
            yield
            probs, den = softmax(kv, groups)
            yield
            values(qb, kv, probs, den)

        yield from _out_proj_items(lambda: a_s[...], wout_ref, xa_ref, moda_ref, y_ref)

    staged = ((u_w, u_s), (gate_w, gate_s), (q_w, q_s), (k_w, k_s), (vt_w, vt_s), (gb_w, gb_s))
    order = EVEN_ORDER_LATENT if windowed else EVEN_ORDER_CONTEXT
    _pipeline_steps(s, n_chunks, lag, ring, (h_w, h_s), staged, order, normalize, project, finish)


def _chunk_specs(n_chunks, nc, lag, per_seq_mod, tm=TM):
    last = n_chunks - 1
    norm_chunk = lambda s: jnp.minimum(s, last)
    proj_chunk = lambda s: jnp.clip(s - 1, 0, last)
    fin_chunk = lambda s: jnp.clip(s - 1 - lag, 0, last)
    seq_of = (lambda j: 1 + j // nc) if per_seq_mod else (lambda j: 0)
    return dict(
        x_norm=pl.BlockSpec((1, tm, D_MODEL), lambda s: (norm_chunk(s), 0, 0)),
        x_fin=pl.BlockSpec((1, tm, D_MODEL), lambda s: (fin_chunk(s), 0, 0)),
        mod_norm=pl.BlockSpec((1, 1, 3 * D_MODEL), lambda s: (seq_of(norm_chunk(s)), 0, 0)),
        mod_fin=pl.BlockSpec((1, 1, 3 * D_MODEL), lambda s: (seq_of(fin_chunk(s)), 0, 0)),
        proj_chunk=proj_chunk, fin_chunk=fin_chunk)


def _const_spec(shape):
    return pl.BlockSpec(shape, lambda c: (0,) * len(shape))


def _even_layer(x, mod, norm_g, w_in, conv_w, conv_b, q_g, k_g, sink, w_out, rope=None, ctx=None):
    n_seq, seq_len, _ = x.shape
    windowed = rope is not None
    nc = seq_len // TM
    n_chunks = n_seq * nc
    lag, ring = (2, 4) if windowed else (1, 2)
    assert windowed or nc == 1
    sp = _chunk_specs(n_chunks, nc, lag, windowed)
    xc = x.reshape(n_chunks, TM, D_MODEL)
    in_specs = [
        sp["x_norm"], sp["x_fin"], sp["mod_norm"], sp["mod_fin"],
        _const_spec((1, D_MODEL)),
        _const_spec(w_in.shape),
        _const_spec((3, CONV_WIDTH)),
        _const_spec((1, CONV_WIDTH)),
        _const_spec((1, LANES)),
        _const_spec((1, LANES)),
        pl.BlockSpec(memory_space=pltpu.SMEM),
        _const_spec(w_out.shape),
    ]
    args = [xc, xc, mod, mod, norm_g.reshape(1, D_MODEL), w_in, conv_w, conv_b.reshape(1, CONV_WIDTH),
            q_g, k_g, sink, w_out]
    per_chunk = [
        ((TM, CONV_WIDTH), F32),
        ((TM, CONV_WIDTH), F32),
        ((TM, ATTN_WIDTH), BF16),
        ((TM, 2 * GROUP_W), BF16),
        ((2 * LANES, TM), BF16),
        ((TM, ATTN_WIDTH), F32),
    ]
    scratch = (
        [pltpu.VMEM((TM, D_MODEL), BF16)] * 2
        + [pltpu.VMEM((ring,) + shape, dt) for shape, dt in per_chunk]
        + [pltpu.VMEM(shape, dt) for shape, dt in per_chunk]
        + [pltpu.VMEM((TM, D_MODEL), BF16)])
    y_shape = jax.ShapeDtypeStruct(xc.shape, F32)
    y_spec = pl.BlockSpec((1, TM, D_MODEL), lambda c: (sp["fin_chunk"](c), 0, 0))
    if windowed:
        cos, sin = rope
        ck, cv = ctx
        rope_spec = pl.BlockSpec((TM, LANES), lambda c: (sp["proj_chunk"](c) % nc, 0))
        ctx_spec = pl.BlockSpec((1,) + ck.shape[1:], lambda c: (sp["fin_chunk"](c) // nc, 0, 0))
        in_specs += [rope_spec, rope_spec, ctx_spec, ctx_spec]
        args += [cos, sin, ck, cv]
        scratch += [pltpu.VMEM((ck.shape[1], 2 * GROUP_W), BF16), pltpu.VMEM((2 * LANES, cv.shape[1]), BF16)]
        out_shape, out_specs = y_shape, y_spec
    else:
        kv_shape = jax.ShapeDtypeStruct((n_chunks, TM, KV_WIDTH), F32)
        kv_spec = pl.BlockSpec((1, TM, KV_WIDTH), lambda c: (sp["proj_chunk"](c), 0, 0))
        out_shape = (y_shape, kv_shape, kv_shape,
                     jax.ShapeDtypeStruct(_tiled_shape(w_in), BF16), jax.ShapeDtypeStruct(_tiled_shape(w_out), BF16))
        out_specs = (y_spec, kv_spec, kv_spec, _const_spec(_tiled_shape(w_in)), _const_spec(_tiled_shape(w_out)))
    out = pl.pallas_call(
        functools.partial(_even_kernel, nc=nc, n_chunks=n_chunks, lag=lag, ring=ring, windowed=windowed),
        out_shape=out_shape,
        grid=(n_chunks + lag + 1,),
        in_specs=in_specs,
        out_specs=out_specs,
        scratch_shapes=scratch,
        compiler_params=PIPELINE_PARAMS,
        name="even_latent" if windowed else "even_context",
    )(*args)
    if windowed:
        return out.reshape(x.shape)
    y, k, v, w_in_bf16, w_out_bf16 = out
    return (y.reshape(x.shape), k.reshape(n_seq, seq_len, KV_WIDTH), v.reshape(n_seq, seq_len, KV_WIDTH),
            w_in_bf16, w_out_bf16)


def _odd_kernel(*refs, nc, n_chunks, lag, ring, cast_weights):
    if cast_weights:
        (xp_ref, xa_ref, modp_ref, moda_ref, g_ref, win_f32, pw_f32, ps_ref, wout_f32,
         y_ref, win_ref, pw_ref, wout_ref, h_s, h_w, u_s, sg_s, u_w, sg_w) = refs
    else:
        (xp_ref, xa_ref, modp_ref, moda_ref, g_ref, win_ref, pw_ref, ps_ref, wout_ref,
         y_ref, h_s, h_w, u_s, sg_s, u_w, sg_w) = refs
    tm = h_s.shape[0]
    s = pl.program_id(0)
    i = s - 1 - lag
    ci = lax.rem(i, jnp.int32(nc))

    if cast_weights:
        @pl.when(s == 0)
        def _cast():
            _cast_columns(win_f32, win_ref)
            _cast_columns(wout_f32, wout_ref)
            for gi in range(pw_ref.shape[0]):
                pw_ref[gi] = pw_f32[gi].astype(BF16)

    def normalize(tokens):
        yield from _normalize_items(h_w, xp_ref, g_ref, modp_ref, tokens)

    def project():
        for n in range(D_MODEL // NT):
            yield
            cols = slice(n * NT, (n + 1) * NT)
            u = jnp.dot(h_s[...], win_ref[n], preferred_element_type=F32)
            u_w[:, cols] = u
        for n in range(D_MODEL // NT):
            yield
            cols = slice(n * NT, (n + 1) * NT)
            sg = _silu(jnp.dot(h_s[...], win_ref[D_MODEL // NT + n], preferred_element_type=F32))
            sg_w[:, cols] = sg

    def finish(slot, tokens):
        if nc > 1:
            prev_slot = lax.rem(slot + (ring - 1), jnp.int32(ring))
            next_slot = lax.rem(slot + 1, jnp.int32(ring))
            has_prev = ci > 0
            has_next = ci < nc - 1
        else:
            prev_slot = next_slot = has_prev = has_next = None
        t = ci * tm + lax.broadcasted_iota(jnp.int32, (tm, 1), 0)
        outs = []
        for gi, w in enumerate(POOL_SIZES):
            yield
            cols = slice(gi * POOL_GROUP, (gi + 1) * POOL_GROUP)
            ext = _with_halo(u_s, slot, cols, prev_slot, next_slot, has_prev, has_next)
            acc = ext
            span = 1
            while span < w:
                acc = acc + _rows_down(acc, span)
                span *= 2
            if w > 2:
                acc = _rows_up(acc, w // 2 - 1)
            cnt = (jnp.minimum(t + w // 2, nc * tm) - jnp.maximum(t - w // 2, 0)).astype(F32)
            while tokens:
                cnt = cnt + tokens.pop(0)[:, 0:1]
            pooled = acc[HALO:HALO + tm] / cnt - ext[HALO:HALO + tm]
            outs.append(jnp.dot(pooled.astype(BF16), pw_ref[gi], preferred_element_type=F32))
        yield
        y = (jnp.concatenate(outs, axis=1) * ps_ref[...] * sg_s[slot]).astype(BF16)
        yield from _out_proj_items(lambda: y, wout_ref, xa_ref, moda_ref, y_ref)

    _pipeline_steps(s, n_chunks, lag, ring, (h_w, h_s), ((u_w, u_s), (sg_w, sg_s)), ODD_ORDER,
                    normalize, project, finish)


def _odd_layer(x, mod, per_seq_mod, norm_g, w_in, pool_w, pool_scale, w_out):
    n_seq, seq_len, _ = x.shape
    tm = TM
    nc = seq_len // tm
    n_chunks = n_seq * nc
    lag, ring = (2, 4) if nc > 1 else (1, 2)
    sp = _chunk_specs(n_chunks, nc, lag, per_seq_mod, tm)
    xc = x.reshape(n_chunks, tm, D_MODEL)
    in_specs = [
        sp["x_norm"], sp["x_fin"], sp["mod_norm"], sp["mod_fin"],
        _const_spec((1, D_MODEL)),
        _const_spec(w_in.shape),
        _const_spec(pool_w.shape),
        _const_spec((1, D_MODEL)),
        _const_spec(w_out.shape),
    ]
    cast_weights = w_in.dtype != BF16
    out_shape = jax.ShapeDtypeStruct(xc.shape, F32)
    out_specs = pl.BlockSpec((1, tm, D_MODEL), lambda c: (sp["fin_chunk"](c), 0, 0))
    if cast_weights:
        shapes = (_tiled_shape(w_in), pool_w.shape, _tiled_shape(w_out))
        out_shape = (out_shape,) + tuple(jax.ShapeDtypeStruct(shape, BF16) for shape in shapes)
        out_specs = (out_specs,) + tuple(_const_spec(shape) for shape in shapes)
    out = pl.pallas_call(
        functools.partial(_odd_kernel, nc=nc, n_chunks=n_chunks, lag=lag, ring=ring, cast_weights=cast_weights),
        out_shape=out_shape,
        grid=(n_chunks + lag + 1,),
        in_specs=in_specs,
        out_specs=out_specs,
        scratch_shapes=[
            pltpu.VMEM((tm, D_MODEL), BF16),
            pltpu.VMEM((tm, D_MODEL), BF16),
            pltpu.VMEM((ring, tm, D_MODEL), F32),
            pltpu.VMEM((ring, tm, D_MODEL), F32),
            pltpu.VMEM((tm, D_MODEL), F32),
            pltpu.VMEM((tm, D_MODEL), F32),
        ],
        compiler_params=PIPELINE_PARAMS,
        name="odd_latent" if per_seq_mod else "odd_context",
    )(xc, xc, mod, mod, norm_g.reshape(1, D_MODEL), w_in, pool_w, pool_scale.reshape(1, D_MODEL), w_out)
    if cast_weights:
        y, *bf16_weights = out
        return y.reshape(x.shape), bf16_weights
    return out.reshape(x.shape)


def _rope_tables(seq_len):
    n_rows = seq_len // GRID_W
    row = np.repeat(np.arange(n_rows), GRID_W).astype(np.float32)
    col = np.tile(np.arange(GRID_W), n_rows).astype(np.float32)
    inv = (np.float32(ROPE_BASE) ** (-np.arange(ROPE_FREQS, dtype=np.float32) / np.float32(ROPE_FREQS))).astype(np.float32)
    ang = np.stack([row[:, None] * inv, col[:, None] * inv], axis=1)
    cos = np.broadcast_to(np.cos(ang)[:, :, None, :], (seq_len, 2, 2, ROPE_FREQS))
    sin = np.sin(ang)[:, :, None, :] * np.array([-1.0, 1.0], np.float32)[None, None, :, None]
    cos = cos.reshape(seq_len, HEAD_DIM).astype(np.float32)
    sin = sin.reshape(seq_len, HEAD_DIM).astype(np.float32)
    return jnp.asarray(np.tile(cos, (1, LANES // HEAD_DIM))), jnp.asarray(np.tile(sin, (1, LANES // HEAD_DIM)))


def kernel(x_prompt, x_sample, cache_k, cache_v, c, c_ctx, ada_w_e, ada_b_e, norm_g_e, w_in_e, conv_w, conv_b,
           q_norm_g, k_norm_g, sink, w_out_e, ada_w_o, ada_b_o, norm_g_o, w_in_o, pool_w, pool_scale, w_out_o):
    n_dec = x_sample.shape[0]
    depth = ada_w_e.shape[0] + ada_w_o.shape[0]
    assert ada_w_e.shape[0] == 1 and ada_w_o.shape[0] == 1 and n_dec + 1 <= 8
    cond8 = jnp.concatenate([c_ctx[None], c, jnp.zeros((8 - 1 - n_dec, D_MODEL), F32)], axis=0)
    mod_e, mod_o = _adaln(cond8, ada_w_e[0], ada_b_e[0], ada_w_o[0], ada_b_o[0])
    rope = _rope_tables(x_sample.shape[1])

    yp, ys = x_prompt, x_sample
    new_k, new_v = [], []
    for layer in range(depth):
        i = layer // 2
        if layer % 2 == 0:
            mod = mod_e
            gains = [jnp.tile(g[i], LANES // HEAD_DIM).reshape(1, LANES) for g in (q_norm_g, k_norm_g)]
            small = (conv_w[i], conv_b[i], *gains, sink[i])
            yp, k, v, w_in, w_out = _even_layer(yp, mod, norm_g_e[i], w_in_e[i], *small, w_out_e[i])
            new_k.append(k.reshape(k.shape[0], k.shape[1], N_KV_HEADS, HEAD_DIM))
            new_v.append(v.reshape(v.shape[0], v.shape[1], N_KV_HEADS, HEAD_DIM))
            ck = cache_k[:, i].reshape(n_dec, cache_k.shape[2], KV_WIDTH)
            cv = cache_v[:, i].reshape(n_dec, cache_v.shape[2], KV_WIDTH)
            ys = _even_layer(ys, mod, norm_g_e[i], w_in, *small, w_out, rope=rope, ctx=(ck, cv))
        else:
            mod = mod_o
            yp, (w_in, w_pool, w_out) = _odd_layer(yp, mod, False, norm_g_o[i], w_in_o[i], pool_w[i],
                                                   pool_scale[i], w_out_o[i])
            ys = _odd_layer(ys, mod, True, norm_g_o[i], w_in, w_pool, pool_scale[i], w_out)
    return yp, ys, jnp.stack(new_k, axis=1), jnp.stack(new_v, axis=1)
```

```python
import functools

import jax
import jax.numpy as jnp
import numpy as np
from jax import lax
from jax.experimental import pallas as pl
from jax.experimental.pallas import tpu as pltpu

F32 = jnp.float32
BF16 = jnp.bfloat16

D_MODEL = 1024
GRID_W = 64
HEAD_DIM = 64
HEAD_SHIFT = 6
ATTN_WIDTH = D_MODEL // 2
N_Q_HEADS = ATTN_WIDTH // HEAD_DIM
N_KV_HEADS = N_Q_HEADS // 4
Q_PER_KV = N_Q_HEADS // N_KV_HEADS
KV_WIDTH = N_KV_HEADS * HEAD_DIM
CONV_WIDTH = D_MODEL - ATTN_WIDTH
WINDOW = 128
BLOCK = 128
ROPE_BASE = 10000.0
ROPE_FREQS = HEAD_DIM // 4
ATTN_SCALE = HEAD_DIM ** -0.5
LOG2E = float(np.log2(np.e))
NEG = -1e30
POOL_SIZES = (2, 4, 8, 16)
POOL_GROUP = D_MODEL // len(POOL_SIZES)
EPS = 1e-6
EVEN_SIZES = (CONV_WIDTH, CONV_WIDTH, CONV_WIDTH, CONV_WIDTH, ATTN_WIDTH, KV_WIDTH, KV_WIDTH, ATTN_WIDTH)
EVEN_IN = sum(EVEN_SIZES)
EVEN_OFFS = tuple(int(s) for s in np.cumsum((0,) + EVEN_SIZES))

TM = 256
NT = 256
NORM_ROWS = 64
EVEN_ORDER_LATENT = "PF" + "F" + "PFNFPF" * 3 + "PNFPF" + "FFFF"
EVEN_ORDER_CONTEXT = "PF" + "F" + "FNPFPF" * 3 + "NPFPF" + "FFFF"
ODD_ORDER = "PNFP" * 4 + "FFFFF"
LANES = 128
GROUP_W = Q_PER_KV * HEAD_DIM
HALO = 8
ADA_TN = 768
VMEM_LIMIT = 56 * 1024 * 1024
PIPELINE_PARAMS = pltpu.CompilerParams(dimension_semantics=("arbitrary",), vmem_limit_bytes=VMEM_LIMIT)


def _silu(x):
    half = 0.5 * x
    return half + half * jnp.tanh(half)


def _rows_down(x, k):
    return pltpu.roll(x, k, axis=0)


def _rows_up(x, k):
    return pltpu.roll(x, x.shape[0] - k, axis=0)


def _rms_modulate(x, gain, shift):
    ms = jnp.mean(x * x, axis=-1, keepdims=True)
    return x * lax.rsqrt(ms + EPS) * gain + shift


def _cast_columns(src_ref, dst_ref):
    for n in range(dst_ref.shape[0]):
        dst_ref[n] = src_ref[:, n * NT:(n + 1) * NT].astype(dst_ref.dtype)


def _tiled_shape(w):
    return (w.shape[1] // NT, w.shape[0], NT)


def _opaque_zero(x):
    bits = pltpu.bitcast(x, jnp.uint32)
    return pltpu.bitcast((bits >> 16) >> 16, F32)


def _normalize_items(h_s, x_ref, g_ref, mod_ref, tokens):
    shift = mod_ref[0, :, 0:D_MODEL]
    gain = g_ref[...] * (1.0 + mod_ref[0, :, D_MODEL:2 * D_MODEL])
    for r in range(0, h_s.shape[0], NORM_ROWS):
        yield
        h = _rms_modulate(x_ref[0, r:r + NORM_ROWS], gain, shift)
        h_s[r:r + NORM_ROWS] = h.astype(BF16)
        tokens.append(_opaque_zero(h[0:8, 0:LANES])[0:1])


def _out_proj_items(get_a, wout_ref, x_ref, mod_ref, y_ref):
    for n in range(D_MODEL // NT):
        yield
        cols = slice(n * NT, (n + 1) * NT)
        mixed = jnp.dot(get_a(), wout_ref[n], preferred_element_type=F32)
        y_ref[0, :, cols] = x_ref[0, :, cols] + mod_ref[0, :, 2 * D_MODEL + n * NT:2 * D_MODEL + (n + 1) * NT] * mixed


def _weave(order, **stages):
    by_letter = {name[0].upper(): stage for name, stage in stages.items()}
    for stage in by_letter.values():
        next(stage, None)
    for letter in order:
        if letter in by_letter:
            next(by_letter[letter], None)
    for stage in by_letter.values():
        for _ in stage:
            pass


def _with_halo(u_s, slot, cols, prev_slot, next_slot, has_prev, has_next):
    u = u_s[slot, :, cols]
    zero = jnp.zeros((HALO, u.shape[1]), F32)
    if prev_slot is None:
        top = bot = zero
    else:
        rows = u_s.shape[1]
        top = jnp.where(has_prev, u_s[prev_slot, rows - HALO:rows, cols], zero)
        bot = jnp.where(has_next, u_s[next_slot, 0:HALO, cols], zero)
    return jnp.concatenate([top, u, bot], axis=0)


def _pipeline_steps(s, n_chunks, lag, ring, normalized, staged, order, normalize, project, finish):
    fin_slot = lax.rem(s + (2 * ring - 1 - lag), jnp.int32(ring))

    @pl.when((s >= 1) & (s <= n_chunks))
    def _publish_normalized():
        h_w, h_s = normalized
        h_s[...] = h_w[...]

    @pl.when((s >= 2) & (s <= n_chunks + 1))
    def _publish_projected():
        slot = lax.rem(s + (ring - 2), jnp.int32(ring))
        for stage_ref, ring_ref in staged:
            ring_ref[slot] = stage_ref[...]

    @pl.when(s == 0)
    def _first():
        _weave(order, normalize=normalize([]))

    @pl.when((s >= 1) & (s <= lag))
    def _fill():
        _weave(order, project=project(), normalize=normalize([]))

    @pl.when((s > lag) & (s <= n_chunks))
    def _steady():
        tokens = []
        _weave(order, finish=finish(fin_slot, tokens), project=project(), normalize=normalize(tokens))

    @pl.when(s > n_chunks)
    def _drain():
        _weave(order, finish=finish(fin_slot, []))


def _adaln_kernel(cond_ref, we_ref, be_ref, wo_ref, bo_ref, oe_ref, oo_ref):
    s = _silu(cond_ref[...]).astype(BF16)
    oe = jnp.dot(s, we_ref[...].astype(BF16), preferred_element_type=F32) + be_ref[...]
    oo = jnp.dot(s, wo_ref[...].astype(BF16), preferred_element_type=F32) + bo_ref[...]
    for r in range(oe_ref.shape[0]):
        oe_ref[r] = oe[r:r + 1]
        oo_ref[r] = oo[r:r + 1]


def _adaln(cond8, w_e, b_e, w_o, b_o):
    n = 3 * D_MODEL
    wspec = pl.BlockSpec((D_MODEL, ADA_TN), lambda j: (0, j))
    vspec = pl.BlockSpec((1, ADA_TN), lambda j: (0, j))
    ospec = pl.BlockSpec((8, 1, ADA_TN), lambda j: (0, 0, j))
    return pl.pallas_call(
        _adaln_kernel,
        out_shape=(jax.ShapeDtypeStruct((8, 1, n), F32), jax.ShapeDtypeStruct((8, 1, n), F32)),
        grid=(n // ADA_TN,),
        in_specs=[pl.BlockSpec((8, D_MODEL), lambda j: (0, 0)), wspec, vspec, wspec, vspec],
        out_specs=(ospec, ospec),
        compiler_params=pltpu.CompilerParams(dimension_semantics=("arbitrary",), vmem_limit_bytes=VMEM_LIMIT),
        name="adaln",
    )(cond8, w_e, b_e.reshape(1, n), w_o, b_o.reshape(1, n))


def _head_inv_rms(x, bd):
    ms = jnp.dot((x * x).astype(BF16), bd, preferred_element_type=F32) * (1.0 / HEAD_DIM)
    return lax.rsqrt(ms + EPS)


def _rope(x, cos, sin_signed, first_half):
    partner = jnp.where(first_half, pltpu.roll(x, LANES - ROPE_FREQS, axis=1), pltpu.roll(x, ROPE_FREQS, axis=1))
    return x * cos + partner * sin_signed


def _dup_kv(x):
    lane = lax.broadcasted_iota(jnp.int32, (1, LANES), 1)
    swapped = pltpu.roll(x, HEAD_DIM, axis=1)
    return jnp.where(lane < HEAD_DIM, x, swapped), jnp.where(lane < HEAD_DIM, swapped, x)


def _tiled_keys(k):
    ka, kb = _dup_kv(k)
    return jnp.concatenate([ka, ka, kb, kb], axis=1).astype(BF16)


def _values_t(v):
    va, vb = _dup_kv(v)
    return jnp.concatenate([va.T, vb.T], axis=0).astype(BF16)


def _even_kernel(*refs, nc, n_chunks, lag, ring, windowed):
    if windowed:
        (xp_ref, xa_ref, modp_ref, moda_ref, g_ref, win_ref, cw_ref, cb_ref, qg_ref, kg_ref, sink_ref, wout_ref,
         cos_ref, sin_ref, ck_ref, cv_ref,
         y_ref,
         h_s, h_w, u_s, gate_s, q_s, k_s, vt_s, gb_s, u_w, gate_w, q_w, k_w, vt_w, gb_w, a_s, ck_s, cvt_s) = refs
        ko_ref = vo_ref = None
    else:
        (xp_ref, xa_ref, modp_ref, moda_ref, g_ref, win_f32, cw_ref, cb_ref, qg_ref, kg_ref, sink_ref, wout_f32,
         y_ref, ko_ref, vo_ref, win_ref, wout_ref,
         h_s, h_w, u_s, gate_s, q_s, k_s, vt_s, gb_s, u_w, gate_w, q_w, k_w, vt_w, gb_w, a_s) = refs
    s = pl.program_id(0)
    i = s - 1 - lag
    ci = lax.rem(i, jnp.int32(nc))

    @pl.when(s == 0)
    def _init():
        k_s[...] = jnp.zeros(k_s.shape, BF16)
        vt_s[...] = jnp.zeros(vt_s.shape, BF16)
        if not windowed:
            _cast_columns(win_f32, win_ref)
            _cast_columns(wout_f32, wout_ref)

    if windowed:
        @pl.when((i >= 0) & (ci == 0))
        def _load_ctx():
            ck_s[...] = _tiled_keys(ck_ref[0])
            cvt_s[...] = _values_t(cv_ref[0])

    def normalize(tokens):
        yield from _normalize_items(h_w, xp_ref, g_ref, modp_ref, tokens)

    def project():
        o_bg, o_cg, o_xs, o_ga, o_q, o_k, _, o_gb = EVEN_OFFS[:8]

        def proj(start):
            return jnp.dot(h_s[...], win_ref[start // NT], preferred_element_type=F32)

        lane = lax.broadcasted_iota(jnp.int32, (1, LANES), 1)
        first_half = (lane & (2 * ROPE_FREQS - 1)) < ROPE_FREQS
        ri = lax.broadcasted_iota(jnp.int32, (NT, NT), 0) >> HEAD_SHIFT
        cj = lax.broadcasted_iota(jnp.int32, (NT, NT), 1) >> HEAD_SHIFT
        bd = (ri == cj).astype(BF16)

        def qk_norm(x, g):
            w = x.shape[1]
            inv = _head_inv_rms(x, bd[0:w, 0:w])
            pieces = []
            for j in range(w // LANES):
                lanes = slice(j * LANES, (j + 1) * LANES)
                xj = x[:, lanes] * inv[:, lanes] * g
                pieces.append(_rope(xj, cos_ref[...], sin_ref[...], first_half) if windowed else xj)
            return pieces

        def keys_values(_):
            kv = proj(o_k)
            k, = qk_norm(kv[:, 0:KV_WIDTH], kg_ref[...])
            v = kv[:, KV_WIDTH:2 * KV_WIDTH]
            if not windowed:
                ko_ref[0] = k
                vo_ref[0] = v
            k_w[...] = _tiled_keys(k)
            vt_w[...] = _values_t(v)

        def queries(n):
            for j, qj in enumerate(qk_norm(proj(o_q + n * NT), qg_ref[...])):
                q_w[:, n * NT + j * LANES:n * NT + (j + 1) * LANES] = (qj * (ATTN_SCALE * LOG2E)).astype(BF16)

        def conv_gate(n):
            gate_w[:, n * NT:(n + 1) * NT] = proj(o_bg + n * NT) * _silu(proj(o_ga + n * NT))

        def conv_input(n):
            u_w[:, n * NT:(n + 1) * NT] = proj(o_cg + n * NT) * proj(o_xs + n * NT)

        def attn_gate(n):
            gb_w[:, n * NT:(n + 1) * NT] = _silu(proj(o_gb + n * NT))

        items = [(conv_gate, 0), (keys_values, 0), (conv_input, 0), (queries, 0), (conv_gate, 1), (queries, 1),
                 (conv_input, 1), (attn_gate, 0), (attn_gate, 1)]
        for emit, n in items:
            yield
            emit(n)

    def finish(slot, tokens):
        if windowed:
            prev_slot = lax.rem(slot + (ring - 1), jnp.int32(ring))
            next_slot = lax.rem(slot + 1, jnp.int32(ring))
            has_prev = ci > 0
            has_next = ci < nc - 1
        else:
            prev_slot = next_slot = has_prev = has_next = None

        yield
        ext = _with_halo(u_s, slot, slice(None), prev_slot, next_slot, has_prev, has_next)
        conv = (_rows_down(ext, 1)[HALO:HALO + TM] * cw_ref[0:1, :]
                + ext[HALO:HALO + TM] * cw_ref[1:2, :]
                + _rows_up(ext, 1)[HALO:HALO + TM] * cw_ref[2:3, :]
                + cb_ref[...])
        a_s[:, 0:CONV_WIDTH] = (gate_s[slot] * conv).astype(BF16)

        lane_head = lax.broadcasted_iota(jnp.int32, (1, GROUP_W), 1) >> HEAD_SHIFT
        cols4 = Q_PER_KV * BLOCK
        key_i =lax.broadcasted_iota(jnp.int32, (BLOCK, cols4), 0)
        qry_i = lax.broadcasted_iota(jnp.int32, (BLOCK, cols4), 1) & (BLOCK - 1)
        upper = lax.broadcasted_iota(jnp.int32, (LANES, LANES), 0) < HEAD_DIM
        halves = (slice(0, BLOCK), slice(BLOCK, TM))
        @functools.lru_cache(maxsize=None)
        def block_masks(qb):
            if qb == 0:
                blocks = ((prev_slot, halves[1]), (slot, halves[0]), (slot, halves[1]))
                prev_in, next_in = has_prev, True
            else:
                blocks = ((slot, halves[0]), (slot, halves[1]), (next_slot, halves[0]))
                prev_in, next_in = True, has_next
            prev_ok = (key_i - qry_i) >= jnp.where(prev_in, 0, BLOCK)
            next_ok = (qry_i - key_i) >= jnp.where(next_in, 0, BLOCK)
            return blocks, prev_ok, next_ok

        def scores(qb, kv):
            cols = slice(kv * GROUP_W, (kv + 1) * GROUP_W)
            vrows = slice(kv * LANES, (kv + 1) * LANES)
            qblk = q_s[slot, halves[qb], cols]
            q4 = jnp.concatenate(
                [jnp.where(lane_head == g, qblk, jnp.zeros_like(qblk)) for g in range(Q_PER_KV)], axis=0)

            def scores_t(kk):
                return lax.dot_general(kk, q4, (((1,), (1,)), ((), ())), preferred_element_type=F32)

            if not windowed:
                return [([scores_t(k_s[slot, :, cols])], vt_s[slot, vrows, :])]
            blocks, prev_ok, next_ok = block_masks(qb)
            s_loc = scores_t(jnp.concatenate([k_s[sl, r, cols] for sl, r in blocks], axis=0))
            vt_loc = jnp.concatenate([vt_s[sl, vrows, r] for sl, r in blocks], axis=1)
            s_ctx = scores_t(ck_s[:, cols])
            return [([jnp.where(prev_ok, s_loc[0:BLOCK], NEG), s_loc[BLOCK:2 * BLOCK],
                      jnp.where(next_ok, s_loc[2 * BLOCK:3 * BLOCK], NEG)], vt_loc),
                    ([s_ctx], cvt_s[vrows, :])]

        def softmax(kv, groups):
            snk = jnp.concatenate(
                [jnp.full((1, BLOCK), sink_ref[kv * Q_PER_KV + g] * LOG2E, F32) for g in range(Q_PER_KV)], axis=1)
            if tokens:
                token = tokens.pop(0)
                while tokens:
                    token = token + tokens.pop(0)
                snk = snk + jnp.concatenate([token] * Q_PER_KV, axis=1)
            mx = snk
            for ss, _ in groups:
                for sc in ss:
                    mx = jnp.maximum(mx, jnp.max(sc, axis=0, keepdims=True))
            den = jnp.exp2(snk - mx)
            probs = []
            for ss, vt in groups:
                es = []
                for sc in ss:
                    e = jnp.exp2(sc - mx)
                    den = den + jnp.sum(e, axis=0, keepdims=True)
                    es.append(e.astype(BF16))
                probs.append((es[0] if len(es) == 1 else jnp.concatenate(es, axis=0), vt))
            return probs, den

        def values(qb, kv, probs, den):
            o_t = jnp.zeros((LANES, cols4), F32)
            for e_all, vt in probs:
                o_t = o_t + jnp.dot(vt, e_all, preferred_element_type=F32)
            o_t = o_t * (1.0 / den)
            o01 = jnp.where(upper, o_t[:, 0:BLOCK], o_t[:, BLOCK:2 * BLOCK]).T
            o23 = jnp.where(upper, o_t[:, 2 * BLOCK:3 * BLOCK], o_t[:, 3 * BLOCK:4 * BLOCK]).T
            a_s[halves[qb], CONV_WIDTH + kv * GROUP_W:CONV_WIDTH + (kv + 1) * GROUP_W] = (
                jnp.concatenate([o01, o23], axis=1) * gb_s[slot, halves[qb], kv * GROUP_W:(kv + 1) * GROUP_W]
            ).astype(BF16)

        pairs = [(qb, kv) for qb in range(TM // BLOCK) for kv in range(N_KV_HEADS)]
        yield
        ahead = scores(*pairs[0])
        for t, (qb, kv) in enumerate(pairs):
            groups = ahead
            if t + 1 < len(pairs):
                yield
                ahead = scores(*pairs[t + 1])
            yield
            probs, den = softmax(kv, groups)
            yield
            values(qb, kv, probs, den)

        yield from _out_proj_items(lambda: a_s[...], wout_ref, xa_ref, moda_ref, y_ref)

    staged = ((u_w, u_s), (gate_w, gate_s), (q_w, q_s), (k_w, k_s), (vt_w, vt_s), (gb_w, gb_s))
    order = EVEN_ORDER_LATENT if windowed else EVEN_ORDER_CONTEXT
    _pipeline_steps(s, n_chunks, lag, ring, (h_w, h_s), staged, order, normalize, project, finish)


def _chunk_specs(n_chunks, nc, lag, per_seq_mod, tm=TM):
    last = n_chunks - 1
    norm_chunk = lambda s: jnp.minimum(s, last)
    proj_chunk = lambda s: jnp.clip(s - 1, 0, last)
    fin_chunk = lambda s: jnp.clip(s - 1 - lag, 0, last)
    seq_of = (lambda j: 1 + j // nc) if per_seq_mod else (lambda j: 0)
    return dict(
        x_norm=pl.BlockSpec((1, tm, D_MODEL), lambda s: (norm_chunk(s), 0, 0)),
        x_fin=pl.BlockSpec((1, tm, D_MODEL), lambda s: (fin_chunk(s), 0, 0)),
        mod_norm=pl.BlockSpec((1, 1, 3 * D_MODEL), lambda s: (seq_of(norm_chunk(s)), 0, 0)),
        mod_fin=pl.BlockSpec((1, 1, 3 * D_MODEL), lambda s: (seq_of(fin_chunk(s)), 0, 0)),
        proj_chunk=proj_chunk, fin_chunk=fin_chunk)


def _const_spec(shape):
    return pl.BlockSpec(shape, lambda c: (0,) * len(shape))


def _even_layer(x, mod, norm_g, w_in, conv_w, conv_b, q_g, k_g, sink, w_out, rope=None, ctx=None):
    n_seq, seq_len, _ = x.shape
    windowed = rope is not None
    nc = seq_len // TM
    n_chunks = n_seq * nc
    lag, ring = (2, 4) if windowed else (1, 2)
    assert windowed or nc == 1
    sp = _chunk_specs(n_chunks, nc, lag, windowed)
    xc = x.reshape(n_chunks, TM, D_MODEL)
    in_specs = [
        sp["x_norm"], sp["x_fin"], sp["mod_norm"], sp["mod_fin"],
        _const_spec((1, D_MODEL)),
        _const_spec(w_in.shape),
        _const_spec((3, CONV_WIDTH)),
        _const_spec((1, CONV_WIDTH)),
        _const_spec((1, LANES)),
        _const_spec((1, LANES)),
        pl.BlockSpec(memory_space=pltpu.SMEM),
        _const_spec(w_out.shape),
    ]
    args = [xc, xc, mod, mod, norm_g.reshape(1, D_MODEL), w_in, conv_w, conv_b.reshape(1, CONV_WIDTH),
            q_g, k_g, sink, w_out]
    per_chunk = [
        ((TM, CONV_WIDTH), F32),
        ((TM, CONV_WIDTH), F32),
        ((TM, ATTN_WIDTH), BF16),
        ((TM, 2 * GROUP_W), BF16),
        ((2 * LANES, TM), BF16),
        ((TM, ATTN_WIDTH), F32),
    ]
    scratch = (
        [pltpu.VMEM((TM, D_MODEL), BF16)] * 2
        + [pltpu.VMEM((ring,) + shape, dt) for shape, dt in per_chunk]
        + [pltpu.VMEM(shape, dt) for shape, dt in per_chunk]
        + [pltpu.VMEM((TM, D_MODEL), BF16)])
    y_shape = jax.ShapeDtypeStruct(xc.shape, F32)
    y_spec = pl.BlockSpec((1, TM, D_MODEL), lambda c: (sp["fin_chunk"](c), 0, 0))
    if windowed:
        cos, sin = rope
        ck, cv = ctx
        rope_spec = pl.BlockSpec((TM, LANES), lambda c: (sp["proj_chunk"](c) % nc, 0))
        ctx_spec = pl.BlockSpec((1,) + ck.shape[1:], lambda c: (sp["fin_chunk"](c) // nc, 0, 0))
        in_specs += [rope_spec, rope_spec, ctx_spec, ctx_spec]
        args += [cos, sin, ck, cv]
        scratch += [pltpu.VMEM((ck.shape[1], 2 * GROUP_W), BF16), pltpu.VMEM((2 * LANES, cv.shape[1]), BF16)]
        out_shape, out_specs = y_shape, y_spec
    else:
        kv_shape = jax.ShapeDtypeStruct((n_chunks, TM, KV_WIDTH), F32)
        kv_spec = pl.BlockSpec((1, TM, KV_WIDTH), lambda c: (sp["proj_chunk"](c), 0, 0))
        out_shape = (y_shape, kv_shape, kv_shape,
                     jax.ShapeDtypeStruct(_tiled_shape(w_in), BF16), jax.ShapeDtypeStruct(_tiled_shape(w_out), BF16))
        out_specs = (y_spec, kv_spec, kv_spec, _const_spec(_tiled_shape(w_in)), _const_spec(_tiled_shape(w_out)))
    out = pl.pallas_call(
        functools.partial(_even_kernel, nc=nc, n_chunks=n_chunks, lag=lag, ring=ring, windowed=windowed),
        out_shape=out_shape,
        grid=(n_chunks + lag + 1,),
        in_specs=in_specs,
        out_specs=out_specs,
        scratch_shapes=scratch,
        compiler_params=PIPELINE_PARAMS,
        name="even_latent" if windowed else "even_context",
    )(*args)
    if windowed:
        return out.reshape(x.shape)
    y, k, v, w_in_bf16, w_out_bf16 = out
    return (y.reshape(x.shape), k.reshape(n_seq, seq_len, KV_WIDTH), v.reshape(n_seq, seq_len, KV_WIDTH),
            w_in_bf16, w_out_bf16)


def _odd_kernel(*refs, nc, n_chunks, lag, ring, cast_weights):
    if cast_weights:
        (xp_ref, xa_ref, modp_ref, moda_ref, g_ref, win_f32, pw_f32, ps_ref, wout_f32,
         y_ref, win_ref, pw_ref, wout_ref, h_s, h_w, u_s, sg_s, u_w, sg_w) = refs
    else:
        (xp_ref, xa_ref, modp_ref, moda_ref, g_ref, win_ref, pw_ref, ps_ref, wout_ref,
         y_ref, h_s, h_w, u_s, sg_s, u_w, sg_w) = refs
    tm = h_s.shape[0]
    s = pl.program_id(0)
    i = s - 1 - lag
    ci = lax.rem(i, jnp.int32(nc))

    if cast_weights:
        @pl.when(s == 0)
        def _cast():
            _cast_columns(win_f32, win_ref)
            _cast_columns(wout_f32, wout_ref)
            for gi in range(pw_ref.shape[0]):
                pw_ref[gi] = pw_f32[gi].astype(BF16)

    def normalize(tokens):
        yield from _normalize_items(h_w, xp_ref, g_ref, modp_ref, tokens)

    def project():
        for n in range(D_MODEL // NT):
            yield
            cols = slice(n * NT, (n + 1) * NT)
            u = jnp.dot(h_s[...], win_ref[n], preferred_element_type=F32)
            u_w[:, cols] = u
        for n in range(D_MODEL // NT):
            yield
            cols = slice(n * NT, (n + 1) * NT)
            sg = _silu(jnp.dot(h_s[...], win_ref[D_MODEL // NT + n], preferred_element_type=F32))
            sg_w[:, cols] = sg

    def finish(slot, tokens):
        if nc > 1:
            prev_slot = lax.rem(slot + (ring - 1), jnp.int32(ring))
            next_slot = lax.rem(slot + 1, jnp.int32(ring))
            has_prev = ci > 0
            has_next = ci < nc - 1
        else:
            prev_slot = next_slot = has_prev = has_next = None
        t = ci * tm + lax.broadcasted_iota(jnp.int32, (tm, 1), 0)
        outs = []
        for gi, w in enumerate(POOL_SIZES):
            yield
            cols = slice(gi * POOL_GROUP, (gi + 1) * POOL_GROUP)
            ext = _with_halo(u_s, slot, cols, prev_slot, next_slot, has_prev, has_next)
            acc = ext
            span = 1
            while span < w:
                acc = acc + _rows_down(acc, span)
                span *= 2
            if w > 2:
                acc = _rows_up(acc, w // 2 - 1)
            cnt = (jnp.minimum(t + w // 2, nc * tm) - jnp.maximum(t - w // 2, 0)).astype(F32)
            while tokens:
                cnt = cnt + tokens.pop(0)[:, 0:1]
            pooled = acc[HALO:HALO + tm] / cnt - ext[HALO:HALO + tm]
            outs.append(jnp.dot(pooled.astype(BF16), pw_ref[gi], preferred_element_type=F32))
        yield
        y = (jnp.concatenate(outs, axis=1) * ps_ref[...] * sg_s[slot]).astype(BF16)
        yield from _out_proj_items(lambda: y, wout_ref, xa_ref, moda_ref, y_ref)

    _pipeline_steps(s, n_chunks, lag, ring, (h_w, h_s), ((u_w, u_s), (sg_w, sg_s)), ODD_ORDER,
                    normalize, project, finish)


def _odd_layer(x, mod, per_seq_mod, norm_g, w_in, pool_w, pool_scale, w_out):
    n_seq, seq_len, _ = x.shape
    tm = TM
    nc = seq_len // tm
    n_chunks = n_seq * nc
    lag, ring = (2, 4) if nc > 1 else (1, 2)
    sp = _chunk_specs(n_chunks, nc, lag, per_seq_mod, tm)
    xc = x.reshape(n_chunks, tm, D_MODEL)
    in_specs = [
        sp["x_norm"], sp["x_fin"], sp["mod_norm"], sp["mod_fin"],
        _const_spec((1, D_MODEL)),
        _const_spec(w_in.shape),
        _const_spec(pool_w.shape),
        _const_spec((1, D_MODEL)),
        _const_spec(w_out.shape),
    ]
    cast_weights = w_in.dtype != BF16
    out_shape = jax.ShapeDtypeStruct(xc.shape, F32)
    out_specs = pl.BlockSpec((1, tm, D_MODEL), lambda c: (sp["fin_chunk"](c), 0, 0))
    if cast_weights:
        shapes = (_tiled_shape(w_in), pool_w.shape, _tiled_shape(w_out))
        out_shape = (out_shape,) + tuple(jax.ShapeDtypeStruct(shape, BF16) for shape in shapes)
        out_specs = (out_specs,) + tuple(_const_spec(shape) for shape in shapes)
    out = pl.pallas_call(
        functools.partial(_odd_kernel, nc=nc, n_chunks=n_chunks, lag=lag, ring=ring, cast_weights=cast_weights),
        out_shape=out_shape,
        grid=(n_chunks + lag + 1,),
        in_specs=in_specs,
        out_specs=out_specs,
        scratch_shapes=[
            pltpu.VMEM((tm, D_MODEL), BF16),
            pltpu.VMEM((tm, D_MODEL), BF16),
            pltpu.VMEM((ring, tm, D_MODEL), F32),
            pltpu.VMEM((ring, tm, D_MODEL), F32),
            pltpu.VMEM((tm, D_MODEL), F32),
            pltpu.VMEM((tm, D_MODEL), F32),
        ],
        compiler_params=PIPELINE_PARAMS,
        name="odd_latent" if per_seq_mod else "odd_context",
    )(xc, xc, mod, mod, norm_g.reshape(1, D_MODEL), w_in, pool_w, pool_scale.reshape(1, D_MODEL), w_out)
    if cast_weights:
        y, *bf16_weights = out
        return y.reshape(x.shape), bf16_weights
    return out.reshape(x.shape)


def _rope_tables(seq_len):
    n_rows = seq_len // GRID_W
    row = np.repeat(np.arange(n_rows), GRID_W).astype(np.float32)
    col = np.tile(np.arange(GRID_W), n_rows).astype(np.float32)
    inv = (np.float32(ROPE_BASE) ** (-np.arange(ROPE_FREQS, dtype=np.float32) / np.float32(ROPE_FREQS))).astype(np.float32)
    ang = np.stack([row[:, None] * inv, col[:, None] * inv], axis=1)
    cos = np.broadcast_to(np.cos(ang)[:, :, None, :], (seq_len, 2, 2, ROPE_FREQS))
    sin = np.sin(ang)[:, :, None, :] * np.array([-1.0, 1.0], np.float32)[None, None, :, None]
    cos = cos.reshape(seq_len, HEAD_DIM).astype(np.float32)
    sin = sin.reshape(seq_len, HEAD_DIM).astype(np.float32)
    return jnp.asarray(np.tile(cos, (1, LANES // HEAD_DIM))), jnp.asarray(np.tile(sin, (1, LANES // HEAD_DIM)))


def kernel(x_prompt, x_sample, cache_k, cache_v, c, c_ctx, ada_w_e, ada_b_e, norm_g_e, w_in_e, conv_w, conv_b,
           q_norm_g, k_norm_g, sink, w_out_e, ada_w_o, ada_b_o, norm_g_o, w_in_o, pool_w, pool_scale, w_out_o):
    n_dec = x_sample.shape[0]
    depth = ada_w_e.shape[0] + ada_w_o.shape[0]
    assert ada_w_e.shape[0] == 1 and ada_w_o.shape[0] == 1 and n_dec + 1 <= 8
    cond8 = jnp.concatenate([c_ctx[None], c, jnp.zeros((8 - 1 - n_dec, D_MODEL), F32)], axis=0)
    mod_e, mod_o = _adaln(cond8, ada_w_e[0], ada_b_e[0], ada_w_o[0], ada_b_o[0])
    rope = _rope_tables(x_sample.shape[1])

    yp, ys = x_prompt, x_sample
    new_k, new_v = [], []
    for layer in range(depth):
        i = layer // 2
        if layer % 2 == 0:
            mod = mod_e
            gains = [jnp.tile(g[i], LANES // HEAD_DIM).reshape(1, LANES) for g in (q_norm_g, k_norm_g)]
            small = (conv_w[i], conv_b[i], *gains, sink[i])
            yp, k, v, w_in, w_out = _even_layer(yp, mod, norm_g_e[i], w_in_e[i], *small, w_out_e[i])
            new_k.append(k.reshape(k.shape[0], k.shape[1], N_KV_HEADS, HEAD_DIM))
            new_v.append(v.reshape(v.shape[0], v.shape[1], N_KV_HEADS, HEAD_DIM))
            ck = cache_k[:, i].reshape(n_dec, cache_k.shape[2], KV_WIDTH)
            cv = cache_v[:, i].reshape(n_dec, cache_v.shape[2], KV_WIDTH)
            ys = _even_layer(ys, mod, norm_g_e[i], w_in, *small, w_out, rope=rope, ctx=(ck, cv))
        else:
            mod = mod_o
            yp, (w_in, w_pool, w_out) = _odd_layer(yp, mod, False, norm_g_o[i], w_in_o[i], pool_w[i],
                                                   pool_scale[i], w_out_o[i])
            ys = _odd_layer(ys, mod, True, norm_g_o[i], w_in, w_pool, pool_scale[i], w_out)
    return yp, ys, jnp.stack(new_k, axis=1), jnp.stack(new_v, axis=1)
```

```python
import functools

import jax
import jax.numpy as jnp
import numpy as np
from jax import lax
from jax.experimental import pallas as pl
from jax.experimental.pallas import tpu as pltpu

F32 = jnp.float32
BF16 = jnp.bfloat16

D_MODEL = 1024
GRID_W = 64
HEAD_DIM = 64
HEAD_SHIFT = 6
ATTN_WIDTH = D_MODEL // 2
N_Q_HEADS = ATTN_WIDTH // HEAD_DIM
N_KV_HEADS = N_Q_HEADS // 4
Q_PER_KV = N_Q_HEADS // N_KV_HEADS
KV_WIDTH = N_KV_HEADS * HEAD_DIM
CONV_WIDTH = D_MODEL - ATTN_WIDTH
WINDOW = 128
BLOCK = 128
ROPE_BASE = 10000.0
ROPE_FREQS = HEAD_DIM // 4
ATTN_SCALE = HEAD_DIM ** -0.5
LOG2E = float(np.log2(np.e))
NEG = -1e30
POOL_SIZES = (2, 4, 8, 16)
POOL_GROUP = D_MODEL // len(POOL_SIZES)
EPS = 1e-6
EVEN_SIZES = (CONV_WIDTH, CONV_WIDTH, CONV_WIDTH, CONV_WIDTH, ATTN_WIDTH, KV_WIDTH, KV_WIDTH, ATTN_WIDTH)
EVEN_IN = sum(EVEN_SIZES)
EVEN_OFFS = tuple(int(s) for s in np.cumsum((0,) + EVEN_SIZES))

TM = 256
NT = 256
NORM_ROWS = 64
EVEN_ORDER_LATENT = "PF" + "F" + "PFNFPF" * 3 + "PNFPF" + "FFFF"
EVEN_ORDER_CONTEXT = "PF" + "F" + "FNPFPF" * 3 + "NPFPF" + "FFFF"
ODD_ORDER = "PNFP" * 4 + "FFFFF"
LANES = 128
GROUP_W = Q_PER_KV * HEAD_DIM
HALO = 8
ADA_TN = 768
VMEM_LIMIT = 56 * 1024 * 1024
PIPELINE_PARAMS = pltpu.CompilerParams(dimension_semantics=("arbitrary",), vmem_limit_bytes=VMEM_LIMIT)


def _silu(x):
    half = 0.5 * x
    return half + half * jnp.tanh(half)


def _rows_down(x, k):
    return pltpu.roll(x, k, axis=0)


def _rows_up(x, k):
    return pltpu.roll(x, x.shape[0] - k, axis=0)


def _rms_modulate(x, gain, shift):
    ms = jnp.mean(x * x, axis=-1, keepdims=True)
    return x * lax.rsqrt(ms + EPS) * gain + shift


def _cast_columns(src_ref, dst_ref):
    for n in range(dst_ref.shape[0]):
        dst_ref[n] = src_ref[:, n * NT:(n + 1) * NT].astype(dst_ref.dtype)


def _tiled_shape(w):
    return (w.shape[1] // NT, w.shape[0], NT)


def _opaque_zero(x):
    bits = pltpu.bitcast(x, jnp.uint32)
    return pltpu.bitcast((bits >> 16) >> 16, F32)


def _normalize_items(h_s, x_ref, g_ref, mod_ref, tokens):
    shift = mod_ref[0, :, 0:D_MODEL]
    gain = g_ref[...] * (1.0 + mod_ref[0, :, D_MODEL:2 * D_MODEL])
    for r in range(0, h_s.shape[0], NORM_ROWS):
        yield
        h = _rms_modulate(x_ref[0, r:r + NORM_ROWS], gain, shift)
        h_s[r:r + NORM_ROWS] = h.astype(BF16)
        tokens.append(_opaque_zero(h[0:8, 0:LANES])[0:1])


def _out_proj_items(get_a, wout_ref, x_ref, mod_ref, y_ref):
    for n in range(D_MODEL // NT):
        yield
        cols = slice(n * NT, (n + 1) * NT)
        mixed = jnp.dot(get_a(), wout_ref[n], preferred_element_type=F32)
        y_ref[0, :, cols] = x_ref[0, :, cols] + mod_ref[0, :, 2 * D_MODEL + n * NT:2 * D_MODEL + (n + 1) * NT] * mixed


def _weave(order, **stages):
    by_letter = {name[0].upper(): stage for name, stage in stages.items()}
    for stage in by_letter.values():
        next(stage, None)
    for letter in order:
        if letter in by_letter:
            next(by_letter[letter], None)
    for stage in by_letter.values():
        for _ in stage:
            pass


def _with_halo(u_s, slot, cols, prev_slot, next_slot, has_prev, has_next):
    u = u_s[slot, :, cols]
    zero = jnp.zeros((HALO, u.shape[1]), F32)
    if prev_slot is None:
        top = bot = zero
    else:
        rows = u_s.shape[1]
        top = jnp.where(has_prev, u_s[prev_slot, rows - HALO:rows, cols], zero)
        bot = jnp.where(has_next, u_s[next_slot, 0:HALO, cols], zero)
    return jnp.concatenate([top, u, bot], axis=0)


def _pipeline_steps(s, n_chunks, lag, ring, normalized, staged, order, normalize, project, finish):
    fin_slot = lax.rem(s + (2 * ring - 1 - lag), jnp.int32(ring))

    @pl.when((s >= 1) & (s <= n_chunks))
    def _publish_normalized():
        h_w, h_s = normalized
        h_s[...] = h_w[...]

    @pl.when((s >= 2) & (s <= n_chunks + 1))
    def _publish_projected():
        slot = lax.rem(s + (ring - 2), jnp.int32(ring))
        for stage_ref, ring_ref in staged:
            ring_ref[slot] = stage_ref[...]

    @pl.when(s == 0)
    def _first():
        _weave(order, normalize=normalize([]))

    @pl.when((s >= 1) & (s <= lag))
    def _fill():
        _weave(order, project=project(), normalize=normalize([]))

    @pl.when((s > lag) & (s <= n_chunks))
    def _steady():
        tokens = []
        _weave(order, finish=finish(fin_slot, tokens), project=project(), normalize=normalize(tokens))

    @pl.when(s > n_chunks)
    def _drain():
        _weave(order, finish=finish(fin_slot, []))


def _adaln_kernel(cctx_ref, c_ref, we_ref, be_ref, wo_ref, bo_ref, oe_ref, oo_ref, cond_s):
    n_dec = c_ref.shape[0]
    cond_s[...] = jnp.zeros(cond_s.shape, F32)
    cond_s[0:1] = cctx_ref[...]
    cond_s[1:1 + n_dec] = c_ref[...]
    s = _silu(cond_s[...]).astype(BF16)
    oe = jnp.dot(s, we_ref[...].astype(BF16), preferred_element_type=F32) + be_ref[...]
    oo = jnp.dot(s, wo_ref[...].astype(BF16), preferred_element_type=F32) + bo_ref[...]
    for r in range(oe_ref.shape[0]):
        oe_ref[r] = oe[r:r + 1]
        oo_ref[r] = oo[r:r + 1]


def _adaln(c_ctx, c, w_e, b_e, w_o, b_o):
    n = 3 * D_MODEL
    n_dec = c.shape[0]
    wspec = pl.BlockSpec((D_MODEL, ADA_TN), lambda j: (0, j))
    vspec = pl.BlockSpec((1, ADA_TN), lambda j: (0, j))
    ospec = pl.BlockSpec((8, 1, ADA_TN), lambda j: (0, 0, j))
    return pl.pallas_call(
        _adaln_kernel,
        out_shape=(jax.ShapeDtypeStruct((8, 1, n), F32), jax.ShapeDtypeStruct((8, 1, n), F32)),
        grid=(n // ADA_TN,),
        in_specs=[pl.BlockSpec((1, D_MODEL), lambda j: (0, 0)), pl.BlockSpec((n_dec, D_MODEL), lambda j: (0, 0)),
                  wspec, vspec, wspec, vspec],
        out_specs=(ospec, ospec),
        scratch_shapes=[pltpu.VMEM((8, D_MODEL), F32)],
        compiler_params=pltpu.CompilerParams(dimension_semantics=("arbitrary",), vmem_limit_bytes=VMEM_LIMIT),
        name="adaln",
    )(c_ctx.reshape(1, D_MODEL), c, w_e, b_e.reshape(1, n), w_o, b_o.reshape(1, n))


def _head_inv_rms(x, bd):
    ms = jnp.dot((x * x).astype(BF16), bd, preferred_element_type=F32) * (1.0 / HEAD_DIM)
    return lax.rsqrt(ms + EPS)


def _rope(x, cos, sin_signed, first_half):
    partner = jnp.where(first_half, pltpu.roll(x, LANES - ROPE_FREQS, axis=1), pltpu.roll(x, ROPE_FREQS, axis=1))
    return x * cos + partner * sin_signed


def _dup_kv(x):
    lane = lax.broadcasted_iota(jnp.int32, (1, LANES), 1)
    swapped = pltpu.roll(x, HEAD_DIM, axis=1)
    return jnp.where(lane < HEAD_DIM, x, swapped), jnp.where(lane < HEAD_DIM, swapped, x)


def _tiled_keys(k):
    ka, kb = _dup_kv(k)
    return jnp.concatenate([ka, ka, kb, kb], axis=1).astype(BF16)


def _values_t(v):
    va, vb = _dup_kv(v)
    return jnp.concatenate([va.T, vb.T], axis=0).astype(BF16)


def _even_kernel(*refs, nc, n_chunks, lag, ring, windowed):
    if windowed:
        (xp_ref, xa_ref, modp_ref, moda_ref, g_ref, win_ref, cw_ref, cb_ref, qg_ref, kg_ref, sink_ref, wout_ref,
         cos_ref, sin_ref, ck_ref, cv_ref,
         y_ref,
         h_s, h_w, u_s, gate_s, q_s, k_s, vt_s, gb_s, u_w, gate_w, q_w, k_w, vt_w, gb_w, a_s, ck_s, cvt_s) = refs
        ko_ref = vo_ref = None
    else:
        (xp_ref, xa_ref, modp_ref, moda_ref, g_ref, win_f32, cw_ref, cb_ref, qg_ref, kg_ref, sink_ref, wout_f32,
         y_ref, ko_ref, vo_ref, win_ref, wout_ref,
         h_s, h_w, u_s, gate_s, q_s, k_s, vt_s, gb_s, u_w, gate_w, q_w, k_w, vt_w, gb_w, a_s) = refs
    s = pl.program_id(0)
    i = s - 1 - lag
    ci = lax.rem(i, jnp.int32(nc))

    @pl.when(s == 0)
    def _init():
        k_s[...] = jnp.zeros(k_s.shape, BF16)
        vt_s[...] = jnp.zeros(vt_s.shape, BF16)
        if not windowed:
            _cast_columns(win_f32, win_ref)
            _cast_columns(wout_f32, wout_ref)

    if windowed:
        @pl.when((i >= 0) & (ci == 0))
        def _load_ctx():
            ck_s[...] = _tiled_keys(ck_ref[0])
            cvt_s[...] = _values_t(cv_ref[0])

    def normalize(tokens):
        yield from _normalize_items(h_w, xp_ref, g_ref, modp_ref, tokens)

    def project():
        o_bg, o_cg, o_xs, o_ga, o_q, o_k, _, o_gb = EVEN_OFFS[:8]

        def proj(start):
            return jnp.dot(h_s[...], win_ref[start // NT], preferred_element_type=F32)

        lane = lax.broadcasted_iota(jnp.int32, (1, LANES), 1)
        first_half = (lane & (2 * ROPE_FREQS - 1)) < ROPE_FREQS
        ri = lax.broadcasted_iota(jnp.int32, (NT, NT), 0) >> HEAD_SHIFT
        cj = lax.broadcasted_iota(jnp.int32, (NT, NT), 1) >> HEAD_SHIFT
        bd = (ri == cj).astype(BF16)

        def qk_norm(x, g):
            w = x.shape[1]
            inv = _head_inv_rms(x, bd[0:w, 0:w])
            pieces = []
            for j in range(w // LANES):
                lanes = slice(j * LANES, (j + 1) * LANES)
                xj = x[:, lanes] * inv[:, lanes] * g
                pieces.append(_rope(xj, cos_ref[...], sin_ref[...], first_half) if windowed else xj)
            return pieces

        def keys_values(_):
            kv = proj(o_k)
            k, = qk_norm(kv[:, 0:KV_WIDTH], kg_ref[...])
            v = kv[:, KV_WIDTH:2 * KV_WIDTH]
            if not windowed:
                ko_ref[0] = k
                vo_ref[0] = v
            k_w[...] = _tiled_keys(k)
            vt_w[...] = _values_t(v)

        def queries(n):
            for j, qj in enumerate(qk_norm(proj(o_q + n * NT), qg_ref[...])):
                q_w[:, n * NT + j * LANES:n * NT + (j + 1) * LANES] = (qj * (ATTN_SCALE * LOG2E)).astype(BF16)

        def conv_gate(n):
            gate_w[:, n * NT:(n + 1) * NT] = proj(o_bg + n * NT) * _silu(proj(o_ga + n * NT))

        def conv_input(n):
            u_w[:, n * NT:(n + 1) * NT] = proj(o_cg + n * NT) * proj(o_xs + n * NT)

        def attn_gate(n):
            gb_w[:, n * NT:(n + 1) * NT] = _silu(proj(o_gb + n * NT))

        items = [(conv_gate, 0), (keys_values, 0), (conv_input, 0), (queries, 0), (conv_gate, 1), (queries, 1),
                 (conv_input, 1), (attn_gate, 0), (attn_gate, 1)]
        for emit, n in items:
            yield
            emit(n)

    def finish(slot, tokens):
        if windowed:
            prev_slot = lax.rem(slot + (ring - 1), jnp.int32(ring))
            next_slot = lax.rem(slot + 1, jnp.int32(ring))
            has_prev = ci > 0
            has_next = ci < nc - 1
        else:
            prev_slot = next_slot = has_prev = has_next = None

        yield
        ext = _with_halo(u_s, slot, slice(None), prev_slot, next_slot, has_prev, has_next)
        conv = (_rows_down(ext, 1)[HALO:HALO + TM] * cw_ref[0:1, :]
                + ext[HALO:HALO + TM] * cw_ref[1:2, :]
                + _rows_up(ext, 1)[HALO:HALO + TM] * cw_ref[2:3, :]
                + cb_ref[...])
        a_s[:, 0:CONV_WIDTH] = (gate_s[slot] * conv).astype(BF16)

        lane_head = lax.broadcasted_iota(jnp.int32, (1, GROUP_W), 1) >> HEAD_SHIFT
        cols4 = Q_PER_KV * BLOCK
        key_i =lax.broadcasted_iota(jnp.int32, (BLOCK, cols4), 0)
        qry_i = lax.broadcasted_iota(jnp.int32, (BLOCK, cols4), 1) & (BLOCK - 1)
        upper = lax.broadcasted_iota(jnp.int32, (LANES, LANES), 0) < HEAD_DIM
        halves = (slice(0, BLOCK), slice(BLOCK, TM))
        @functools.lru_cache(maxsize=None)
        def block_masks(qb):
            if qb == 0:
                blocks = ((prev_slot, halves[1]), (slot, halves[0]), (slot, halves[1]))
                prev_in, next_in = has_prev, True
            else:
                blocks = ((slot, halves[0]), (slot, halves[1]), (next_slot, halves[0]))
                prev_in, next_in = True, has_next
            prev_ok = (key_i - qry_i) >= jnp.where(prev_in, 0, BLOCK)
            next_ok = (qry_i - key_i) >= jnp.where(next_in, 0, BLOCK)
            return blocks, prev_ok, next_ok

        def scores(qb, kv):
            cols = slice(kv * GROUP_W, (kv + 1) * GROUP_W)
            vrows = slice(kv * LANES, (kv + 1) * LANES)
            qblk = q_s[slot, halves[qb], cols]
            q4 = jnp.concatenate(
                [jnp.where(lane_head == g, qblk, jnp.zeros_like(qblk)) for g in range(Q_PER_KV)], axis=0)

            def scores_t(kk):
                return lax.dot_general(kk, q4, (((1,), (1,)), ((), ())), preferred_element_type=F32)

            if not windowed:
                return [([scores_t(k_s[slot, :, cols])], vt_s[slot, vrows, :])]
            blocks, prev_ok, next_ok = block_masks(qb)
            s_loc = scores_t(jnp.concatenate([k_s[sl, r, cols] for sl, r in blocks], axis=0))
            vt_loc = jnp.concatenate([vt_s[sl, vrows, r] for sl, r in blocks], axis=1)
            s_ctx = scores_t(ck_s[:, cols])
            return [([jnp.where(prev_ok, s_loc[0:BLOCK], NEG), s_loc[BLOCK:2 * BLOCK],
                      jnp.where(next_ok, s_loc[2 * BLOCK:3 * BLOCK], NEG)], vt_loc),
                    ([s_ctx], cvt_s[vrows, :])]

        def softmax(kv, groups):
            snk = jnp.concatenate(
                [jnp.full((1, BLOCK), sink_ref[kv * Q_PER_KV + g] * LOG2E, F32) for g in range(Q_PER_KV)], axis=1)
            if tokens:
                token = tokens.pop(0)
                while tokens:
                    token = token + tokens.pop(0)
                snk = snk + jnp.concatenate([token] * Q_PER_KV, axis=1)
            mx = snk
            for ss, _ in groups:
                for sc in ss:
                    mx = jnp.maximum(mx, jnp.max(sc, axis=0, keepdims=True))
            den = jnp.exp2(snk - mx)
            probs = []
            for ss, vt in groups:
                es = []
                for sc in ss:
                    e = jnp.exp2(sc - mx)
                    den = den + jnp.sum(e, axis=0, keepdims=True)
                    es.append(e.astype(BF16))
                probs.append((es[0] if len(es) == 1 else jnp.concatenate(es, axis=0), vt))
            return probs, den

        def values(qb, kv, probs, den):
            o_t = jnp.zeros((LANES, cols4), F32)
            for e_all, vt in probs:
                o_t = o_t + jnp.dot(vt, e_all, preferred_element_type=F32)
            o_t = o_t * (1.0 / den)
            o01 = jnp.where(upper, o_t[:, 0:BLOCK], o_t[:, BLOCK:2 * BLOCK]).T
            o23 = jnp.where(upper, o_t[:, 2 * BLOCK:3 * BLOCK], o_t[:, 3 * BLOCK:4 * BLOCK]).T
            a_s[halves[qb], CONV_WIDTH + kv * GROUP_W:CONV_WIDTH + (kv + 1) * GROUP_W] = (
                jnp.concatenate([o01, o23], axis=1) * gb_s[slot, halves[qb], kv * GROUP_W:(kv + 1) * GROUP_W]
            ).astype(BF16)

        pairs = [(qb, kv) for qb in range(TM // BLOCK) for kv in range(N_KV_HEADS)]
        yield
        ahead = scores(*pairs[0])
        for t, (qb, kv) in enumerate(pairs):
            groups = ahead
            if t + 1 < len(pairs):
                yield
                ahead = scores(*pairs[t + 1])
            yield
            probs, den = softmax(kv, groups)
            yield
            values(qb, kv, probs, den)

        yield from _out_proj_items(lambda: a_s[...], wout_ref, xa_ref, moda_ref, y_ref)

    staged = ((u_w, u_s), (gate_w, gate_s), (q_w, q_s), (k_w, k_s), (vt_w, vt_s), (gb_w, gb_s))
    order = EVEN_ORDER_LATENT if windowed else EVEN_ORDER_CONTEXT
    _pipeline_steps(s, n_chunks, lag, ring, (h_w, h_s), staged, order, normalize, project, finish)


def _chunk_specs(n_chunks, nc, lag, per_seq_mod, tm=TM):
    last = n_chunks - 1
    norm_chunk = lambda s: jnp.minimum(s, last)
    proj_chunk = lambda s: jnp.clip(s - 1, 0, last)
    fin_chunk = lambda s: jnp.clip(s - 1 - lag, 0, last)
    seq_of = (lambda j: 1 + j // nc) if per_seq_mod else (lambda j: 0)
    return dict(
        x_norm=pl.BlockSpec((1, tm, D_MODEL), lambda s: (norm_chunk(s), 0, 0)),
        x_fin=pl.BlockSpec((1, tm, D_MODEL), lambda s: (fin_chunk(s), 0, 0)),
        mod_norm=pl.BlockSpec((1, 1, 3 * D_MODEL), lambda s: (seq_of(norm_chunk(s)), 0, 0)),
        mod_fin=pl.BlockSpec((1, 1, 3 * D_MODEL), lambda s: (seq_of(fin_chunk(s)), 0, 0)),
        proj_chunk=proj_chunk, fin_chunk=fin_chunk)


def _const_spec(shape):
    return pl.BlockSpec(shape, lambda c: (0,) * len(shape))


def _even_layer(x, mod, norm_g, w_in, conv_w, conv_b, q_g, k_g, sink, w_out, rope=None, ctx=None):
    n_seq, seq_len, _ = x.shape
    windowed = rope is not None
    nc = seq_len // TM
    n_chunks = n_seq * nc
    lag, ring = (2, 4) if windowed else (1, 2)
    assert windowed or nc == 1
    sp = _chunk_specs(n_chunks, nc, lag, windowed)
    xc = x.reshape(n_chunks, TM, D_MODEL)
    in_specs = [
        sp["x_norm"], sp["x_fin"], sp["mod_norm"], sp["mod_fin"],
        _const_spec((1, D_MODEL)),
        _const_spec(w_in.shape),
        _const_spec((3, CONV_WIDTH)),
        _const_spec((1, CONV_WIDTH)),
        _const_spec((1, LANES)),
        _const_spec((1, LANES)),
        pl.BlockSpec(memory_space=pltpu.SMEM),
        _const_spec(w_out.shape),
    ]
    args = [xc, xc, mod, mod, norm_g.reshape(1, D_MODEL), w_in, conv_w, conv_b.reshape(1, CONV_WIDTH),
            q_g, k_g, sink, w_out]
    per_chunk = [
        ((TM, CONV_WIDTH), F32),
        ((TM, CONV_WIDTH), F32),
        ((TM, ATTN_WIDTH), BF16),
        ((TM, 2 * GROUP_W), BF16),
        ((2 * LANES, TM), BF16),
        ((TM, ATTN_WIDTH), F32),
    ]
    scratch = (
        [pltpu.VMEM((TM, D_MODEL), BF16)] * 2
        + [pltpu.VMEM((ring,) + shape, dt) for shape, dt in per_chunk]
        + [pltpu.VMEM(shape, dt) for shape, dt in per_chunk]
        + [pltpu.VMEM((TM, D_MODEL), BF16)])
    y_shape = jax.ShapeDtypeStruct(xc.shape, F32)
    y_spec = pl.BlockSpec((1, TM, D_MODEL), lambda c: (sp["fin_chunk"](c), 0, 0))
    if windowed:
        cos, sin = rope
        ck, cv = ctx
        rope_spec = pl.BlockSpec((TM, LANES), lambda c: (sp["proj_chunk"](c) % nc, 0))
        ctx_spec = pl.BlockSpec((1,) + ck.shape[1:], lambda c: (sp["fin_chunk"](c) // nc, 0, 0))
        in_specs += [rope_spec, rope_spec, ctx_spec, ctx_spec]
        args += [cos, sin, ck, cv]
        scratch += [pltpu.VMEM((ck.shape[1], 2 * GROUP_W), BF16), pltpu.VMEM((2 * LANES, cv.shape[1]), BF16)]
        out_shape, out_specs = y_shape, y_spec
    else:
        kv_shape = jax.ShapeDtypeStruct((n_chunks, TM, KV_WIDTH), F32)
        kv_spec = pl.BlockSpec((1, TM, KV_WIDTH), lambda c: (sp["proj_chunk"](c), 0, 0))
        out_shape = (y_shape, kv_shape, kv_shape,
                     jax.ShapeDtypeStruct(_tiled_shape(w_in), BF16), jax.ShapeDtypeStruct(_tiled_shape(w_out), BF16))
        out_specs = (y_spec, kv_spec, kv_spec, _const_spec(_tiled_shape(w_in)), _const_spec(_tiled_shape(w_out)))
    out = pl.pallas_call(
        functools.partial(_even_kernel, nc=nc, n_chunks=n_chunks, lag=lag, ring=ring, windowed=windowed),
        out_shape=out_shape,
        grid=(n_chunks + lag + 1,),
        in_specs=in_specs,
        out_specs=out_specs,
        scratch_shapes=scratch,
        compiler_params=PIPELINE_PARAMS,
        name="even_latent" if windowed else "even_context",
    )(*args)
    if windowed:
        return out.reshape(x.shape)
    y, k, v, w_in_bf16, w_out_bf16 = out
    return (y.reshape(x.shape), k.reshape(n_seq, seq_len, KV_WIDTH), v.reshape(n_seq, seq_len, KV_WIDTH),
            w_in_bf16, w_out_bf16)


def _odd_kernel(*refs, nc, n_chunks, lag, ring, cast_weights):
    if cast_weights:
        (xp_ref, xa_ref, modp_ref, moda_ref, g_ref, win_f32, pw_f32, ps_ref, wout_f32,
         y_ref, win_ref, pw_ref, wout_ref, h_s, h_w, u_s, sg_s, u_w, sg_w) = refs
    else:
        (xp_ref, xa_ref, modp_ref, moda_ref, g_ref, win_ref, pw_ref, ps_ref, wout_ref,
         y_ref, h_s, h_w, u_s, sg_s, u_w, sg_w) = refs
    tm = h_s.shape[0]
    s = pl.program_id(0)
    i = s - 1 - lag
    ci = lax.rem(i, jnp.int32(nc))

    if cast_weights:
        @pl.when(s == 0)
        def _cast():
            _cast_columns(win_f32, win_ref)
            _cast_columns(wout_f32, wout_ref)
            for gi in range(pw_ref.shape[0]):
                pw_ref[gi] = pw_f32[gi].astype(BF16)

    def normalize(tokens):
        yield from _normalize_items(h_w, xp_ref, g_ref, modp_ref, tokens)

    def project():
        for n in range(D_MODEL // NT):
            yield
            cols = slice(n * NT, (n + 1) * NT)
            u = jnp.dot(h_s[...], win_ref[n], preferred_element_type=F32)
            u_w[:, cols] = u
        for n in range(D_MODEL // NT):
            yield
            cols = slice(n * NT, (n + 1) * NT)
            sg = _silu(jnp.dot(h_s[...], win_ref[D_MODEL // NT + n], preferred_element_type=F32))
            sg_w[:, cols] = sg

    def finish(slot, tokens):
        if nc > 1:
            prev_slot = lax.rem(slot + (ring - 1), jnp.int32(ring))
            next_slot = lax.rem(slot + 1, jnp.int32(ring))
            has_prev = ci > 0
            has_next = ci < nc - 1
        else:
            prev_slot = next_slot = has_prev = has_next = None
        t = ci * tm + lax.broadcasted_iota(jnp.int32, (tm, 1), 0)
        outs = []
        for gi, w in enumerate(POOL_SIZES):
            yield
            cols = slice(gi * POOL_GROUP, (gi + 1) * POOL_GROUP)
            ext = _with_halo(u_s, slot, cols, prev_slot, next_slot, has_prev, has_next)
            acc = ext
            span = 1
            while span < w:
                acc = acc + _rows_down(acc, span)
                span *= 2
            if w > 2:
                acc = _rows_up(acc, w // 2 - 1)
            cnt = (jnp.minimum(t + w // 2, nc * tm) - jnp.maximum(t - w // 2, 0)).astype(F32)
            while tokens:
                cnt = cnt + tokens.pop(0)[:, 0:1]
            pooled = acc[HALO:HALO + tm] / cnt - ext[HALO:HALO + tm]
            outs.append(jnp.dot(pooled.astype(BF16), pw_ref[gi], preferred_element_type=F32))
        yield
        y = (jnp.concatenate(outs, axis=1) * ps_ref[...] * sg_s[slot]).astype(BF16)
        yield from _out_proj_items(lambda: y, wout_ref, xa_ref, moda_ref, y_ref)

    _pipeline_steps(s, n_chunks, lag, ring, (h_w, h_s), ((u_w, u_s), (sg_w, sg_s)), ODD_ORDER,
                    normalize, project, finish)


def _odd_layer(x, mod, per_seq_mod, norm_g, w_in, pool_w, pool_scale, w_out):
    n_seq, seq_len, _ = x.shape
    tm = TM
    nc = seq_len // tm
    n_chunks = n_seq * nc
    lag, ring = (2, 4) if nc > 1 else (1, 2)
    sp = _chunk_specs(n_chunks, nc, lag, per_seq_mod, tm)
    xc = x.reshape(n_chunks, tm, D_MODEL)
    in_specs = [
        sp["x_norm"], sp["x_fin"], sp["mod_norm"], sp["mod_fin"],
        _const_spec((1, D_MODEL)),
        _const_spec(w_in.shape),
        _const_spec(pool_w.shape),
        _const_spec((1, D_MODEL)),
        _const_spec(w_out.shape),
    ]
    cast_weights = w_in.dtype != BF16
    out_shape = jax.ShapeDtypeStruct(xc.shape, F32)
    out_specs = pl.BlockSpec((1, tm, D_MODEL), lambda c: (sp["fin_chunk"](c), 0, 0))
    if cast_weights:
        shapes = (_tiled_shape(w_in), pool_w.shape, _tiled_shape(w_out))
        out_shape = (out_shape,) + tuple(jax.ShapeDtypeStruct(shape, BF16) for shape in shapes)
        out_specs = (out_specs,) + tuple(_const_spec(shape) for shape in shapes)
    out = pl.pallas_call(
        functools.partial(_odd_kernel, nc=nc, n_chunks=n_chunks, lag=lag, ring=ring, cast_weights=cast_weights),
        out_shape=out_shape,
        grid=(n_chunks + lag + 1,),
        in_specs=in_specs,
        out_specs=out_specs,
        scratch_shapes=[
            pltpu.VMEM((tm, D_MODEL), BF16),
            pltpu.VMEM((tm, D_MODEL), BF16),
            pltpu.VMEM((ring, tm, D_MODEL), F32),
            pltpu.VMEM((ring, tm, D_MODEL), F32),
            pltpu.VMEM((tm, D_MODEL), F32),
            pltpu.VMEM((tm, D_MODEL), F32),
        ],
        compiler_params=PIPELINE_PARAMS,
        name="odd_latent" if per_seq_mod else "odd_context",
    )(xc, xc, mod, mod, norm_g.reshape(1, D_MODEL), w_in, pool_w, pool_scale.reshape(1, D_MODEL), w_out)
    if cast_weights:
        y, *bf16_weights = out
        return y.reshape(x.shape), bf16_weights
    return out.reshape(x.shape)


def _rope_tables(seq_len):
    n_rows = seq_len // GRID_W
    row = np.repeat(np.arange(n_rows), GRID_W).astype(np.float32)
    col = np.tile(np.arange(GRID_W), n_rows).astype(np.float32)
    inv = (np.float32(ROPE_BASE) ** (-np.arange(ROPE_FREQS, dtype=np.float32) / np.float32(ROPE_FREQS))).astype(np.float32)
    ang = np.stack([row[:, None] * inv, col[:, None] * inv], axis=1)
    cos = np.broadcast_to(np.cos(ang)[:, :, None, :], (seq_len, 2, 2, ROPE_FREQS))
    sin = np.sin(ang)[:, :, None, :] * np.array([-1.0, 1.0], np.float32)[None, None, :, None]
    cos = cos.reshape(seq_len, HEAD_DIM).astype(np.float32)
    sin = sin.reshape(seq_len, HEAD_DIM).astype(np.float32)
    return jnp.asarray(np.tile(cos, (1, LANES // HEAD_DIM))), jnp.asarray(np.tile(sin, (1, LANES // HEAD_DIM)))


def kernel(x_prompt, x_sample, cache_k, cache_v, c, c_ctx, ada_w_e, ada_b_e, norm_g_e, w_in_e, conv_w, conv_b,
           q_norm_g, k_norm_g, sink, w_out_e, ada_w_o, ada_b_o, norm_g_o, w_in_o, pool_w, pool_scale, w_out_o):
    n_dec = x_sample.shape[0]
    depth = ada_w_e.shape[0] + ada_w_o.shape[0]
    assert ada_w_e.shape[0] == 1 and ada_w_o.shape[0] == 1 and n_dec + 1 <= 8
    mod_e, mod_o = _adaln(c_ctx, c, ada_w_e[0], ada_b_e[0], ada_w_o[0], ada_b_o[0])
    rope = _rope_tables(x_sample.shape[1])

    yp, ys = x_prompt, x_sample
    new_k, new_v = [], []
    for layer in range(depth):
        i = layer // 2
        if layer % 2 == 0:
            mod = mod_e
            gains = [jnp.tile(g[i], LANES // HEAD_DIM).reshape(1, LANES) for g in (q_norm_g, k_norm_g)]
            small = (conv_w[i], conv_b[i], *gains, sink[i])
            yp, k, v, w_in, w_out = _even_layer(yp, mod, norm_g_e[i], w_in_e[i], *small, w_out_e[i])
            new_k.append(k.reshape(k.shape[0], k.shape[1], N_KV_HEADS, HEAD_DIM))
            new_v.append(v.reshape(v.shape[0], v.shape[1], N_KV_HEADS, HEAD_DIM))
            ck = cache_k[:, i].reshape(n_dec, cache_k.shape[2], KV_WIDTH)
            cv = cache_v[:, i].reshape(n_dec, cache_v.shape[2], KV_WIDTH)
            ys = _even_layer(ys, mod, norm_g_e[i], w_in, *small, w_out, rope=rope, ctx=(ck, cv))
        else:
            mod = mod_o
            yp, (w_in, w_pool, w_out) = _odd_layer(yp, mod, False, norm_g_o[i], w_in_o[i], pool_w[i],
                                                   pool_scale[i], w_out_o[i])
            ys = _odd_layer(ys, mod, True, norm_g_o[i], w_in, w_pool, pool_scale[i], w_out)
    return yp, ys, jnp.stack(new_k, axis=1), jnp.stack(new_v, axis=1)
```

```python
import functools

import jax
import jax.numpy as jnp
import numpy as np
from jax import lax
from jax.experimental import pallas as pl
from jax.experimental.pallas import tpu as pltpu

F32 = jnp.float32
BF16 = jnp.bfloat16

D_MODEL = 1024
GRID_W = 64
HEAD_DIM = 64
HEAD_SHIFT = 6
ATTN_WIDTH = D_MODEL // 2
N_Q_HEADS = ATTN_WIDTH // HEAD_DIM
N_KV_HEADS = N_Q_HEADS // 4
Q_PER_KV = N_Q_HEADS // N_KV_HEADS
KV_WIDTH = N_KV_HEADS * HEAD_DIM
CONV_WIDTH = D_MODEL - ATTN_WIDTH
WINDOW = 128
BLOCK = 128
ROPE_BASE = 10000.0
ROPE_FREQS = HEAD_DIM // 4
ATTN_SCALE = HEAD_DIM ** -0.5
LOG2E = float(np.log2(np.e))
NEG = -1e30
POOL_SIZES = (2, 4, 8, 16)
POOL_GROUP = D_MODEL // len(POOL_SIZES)
EPS = 1e-6
EVEN_SIZES = (CONV_WIDTH, CONV_WIDTH, CONV_WIDTH, CONV_WIDTH, ATTN_WIDTH, KV_WIDTH, KV_WIDTH, ATTN_WIDTH)
EVEN_IN = sum(EVEN_SIZES)
EVEN_OFFS = tuple(int(s) for s in np.cumsum((0,) + EVEN_SIZES))

TM = 256
NT = 256
NORM_ROWS = 64
EVEN_ORDER_LATENT = "PF" + "F" + "PFNFPF" * 3 + "PNFPF" + "FFFF"
EVEN_ORDER_CONTEXT = "PF" + "F" + "FNPFPF" * 3 + "NPFPF" + "FFFF"
ODD_ORDER = "PNFP" * 4 + "FFFFF"
LANES = 128
GROUP_W = Q_PER_KV * HEAD_DIM
HALO = 8
ADA_TN = 768
VMEM_LIMIT = 56 * 1024 * 1024
PIPELINE_PARAMS = pltpu.CompilerParams(dimension_semantics=("arbitrary",), vmem_limit_bytes=VMEM_LIMIT)


def _silu(x):
    half = 0.5 * x
    return half + half * jnp.tanh(half)


def _rows_down(x, k):
    return pltpu.roll(x, k, axis=0)


def _rows_up(x, k):
    return pltpu.roll(x, x.shape[0] - k, axis=0)


def _rms_modulate(x, gain, shift):
    ms = jnp.mean(x * x, axis=-1, keepdims=True)
    return x * lax.rsqrt(ms + EPS) * gain + shift


def _cast_columns(src_ref, dst_ref):
    for n in range(dst_ref.shape[0]):
        dst_ref[n] = src_ref[:, n * NT:(n + 1) * NT].astype(dst_ref.dtype)


def _tiled_shape(w):
    return (w.shape[1] // NT, w.shape[0], NT)


def _opaque_zero(x):
    bits = pltpu.bitcast(x, jnp.uint32)
    return pltpu.bitcast((bits >> 16) >> 16, F32)


def _normalize_items(h_s, x_ref, g_ref, mod_ref, tokens):
    shift = mod_ref[0, :, 0:D_MODEL]
    gain = g_ref[...] * (1.0 + mod_ref[0, :, D_MODEL:2 * D_MODEL])
    for r in range(0, h_s.shape[0], NORM_ROWS):
        yield
        h = _rms_modulate(x_ref[0, r:r + NORM_ROWS], gain, shift)
        h_s[r:r + NORM_ROWS] = h.astype(BF16)
        tokens.append(_opaque_zero(h[0:8, 0:LANES])[0:1])


def _out_proj_items(get_a, wout_ref, x_ref, mod_ref, y_ref):
    for n in range(D_MODEL // NT):
        yield
        cols = slice(n * NT, (n + 1) * NT)
        mixed = jnp.dot(get_a(), wout_ref[n], preferred_element_type=F32)
        y_ref[0, :, cols] = x_ref[0, :, cols] + mod_ref[0, :, 2 * D_MODEL + n * NT:2 * D_MODEL + (n + 1) * NT] * mixed


def _weave(order, **stages):
    by_letter = {name[0].upper(): stage for name, stage in stages.items()}
    for stage in by_letter.values():
        next(stage, None)
    for letter in order:
        if letter in by_letter:
            next(by_letter[letter], None)
    for stage in by_letter.values():
        for _ in stage:
            pass


def _with_halo(u_s, slot, cols, prev_slot, next_slot, has_prev, has_next):
    u = u_s[slot, :, cols]
    zero = jnp.zeros((HALO, u.shape[1]), F32)
    if prev_slot is None:
        top = bot = zero
    else:
        rows = u_s.shape[1]
        top = jnp.where(has_prev, u_s[prev_slot, rows - HALO:rows, cols], zero)
        bot = jnp.where(has_next, u_s[next_slot, 0:HALO, cols], zero)
    return jnp.concatenate([top, u, bot], axis=0)


def _pipeline_steps(s, n_chunks, lag, ring, normalized, staged, order, normalize, project, finish):
    fin_slot = lax.rem(s + (2 * ring - 1 - lag), jnp.int32(ring))

    @pl.when((s >= 1) & (s <= n_chunks))
    def _publish_normalized():
        h_w, h_s = normalized
        h_s[...] = h_w[...]

    @pl.when((s >= 2) & (s <= n_chunks + 1))
    def _publish_projected():
        slot = lax.rem(s + (ring - 2), jnp.int32(ring))
        for stage_ref, ring_ref in staged:
            ring_ref[slot] = stage_ref[...]

    @pl.when(s == 0)
    def _first():
        _weave(order, normalize=normalize([]))

    @pl.when((s >= 1) & (s <= lag))
    def _fill():
        _weave(order, project=project(), normalize=normalize([]))

    @pl.when((s > lag) & (s <= n_chunks))
    def _steady():
        tokens = []
        _weave(order, finish=finish(fin_slot, tokens), project=project(), normalize=normalize(tokens))

    @pl.when(s > n_chunks)
    def _drain():
        _weave(order, finish=finish(fin_slot, []))


def _adaln_kernel(cctx_ref, c_ref, we_ref, be_ref, wo_ref, bo_ref, oe_ref, oo_ref, cond_s):
    n_dec = c_ref.shape[0]
    cond_s[...] = jnp.zeros(cond_s.shape, F32)
    cond_s[0:1] = cctx_ref[...]
    cond_s[1:1 + n_dec] = c_ref[...]
    s = _silu(cond_s[...]).astype(BF16)
    oe = jnp.dot(s, we_ref[...].astype(BF16), preferred_element_type=F32) + be_ref[...]
    oo = jnp.dot(s, wo_ref[...].astype(BF16), preferred_element_type=F32) + bo_ref[...]
    for r in range(oe_ref.shape[0]):
        oe_ref[r] = oe[r:r + 1]
        oo_ref[r] = oo[r:r + 1]


def _adaln(c_ctx, c, w_e, b_e, w_o, b_o):
    n = 3 * D_MODEL
    n_dec = c.shape[0]
    wspec = pl.BlockSpec((D_MODEL, ADA_TN), lambda j: (0, j))
    vspec = pl.BlockSpec((1, ADA_TN), lambda j: (0, j))
    ospec = pl.BlockSpec((8, 1, ADA_TN), lambda j: (0, 0, j))
    return pl.pallas_call(
        _adaln_kernel,
        out_shape=(jax.ShapeDtypeStruct((8, 1, n), F32), jax.ShapeDtypeStruct((8, 1, n), F32)),
        grid=(n // ADA_TN,),
        in_specs=[pl.BlockSpec((1, D_MODEL), lambda j: (0, 0)), pl.BlockSpec((n_dec, D_MODEL), lambda j: (0, 0)),
                  wspec, vspec, wspec, vspec],
        out_specs=(ospec, ospec),
        scratch_shapes=[pltpu.VMEM((8, D_MODEL), F32)],
        compiler_params=pltpu.CompilerParams(dimension_semantics=("arbitrary",), vmem_limit_bytes=VMEM_LIMIT),
        name="adaln",
    )(c_ctx.reshape(1, D_MODEL), c, w_e, b_e.reshape(1, n), w_o, b_o.reshape(1, n))


def _head_inv_rms(x, bd):
    ms = jnp.dot((x * x).astype(BF16), bd, preferred_element_type=F32) * (1.0 / HEAD_DIM)
    return lax.rsqrt(ms + EPS)


def _rope(x, cos, sin_signed, first_half):
    partner = jnp.where(first_half, pltpu.roll(x, LANES - ROPE_FREQS, axis=1), pltpu.roll(x, ROPE_FREQS, axis=1))
    return x * cos + partner * sin_signed


def _dup_kv(x):
    lane = lax.broadcasted_iota(jnp.int32, (1, LANES), 1)
    swapped = pltpu.roll(x, HEAD_DIM, axis=1)
    return jnp.where(lane < HEAD_DIM, x, swapped), jnp.where(lane < HEAD_DIM, swapped, x)


def _tiled_keys(k):
    ka, kb = _dup_kv(k)
    return jnp.concatenate([ka, ka, kb, kb], axis=1).astype(BF16)


def _values_t(v):
    va, vb = _dup_kv(v)
    return jnp.concatenate([va.T, vb.T], axis=0).astype(BF16)


def _even_kernel(*refs, nc, n_chunks, lag, ring, windowed):
    if windowed:
        (xp_ref, xa_ref, modp_ref, moda_ref, g_ref, win_ref, cw_ref, cb_ref, qkg_ref, sink_ref, wout_ref,
         cos_ref, sin_ref, ck_ref, cv_ref,
         y_ref,
         h_s, h_w, u_s, gate_s, q_s, k_s, vt_s, gb_s, u_w, gate_w, q_w, k_w, vt_w, gb_w, a_s, ck_s, cvt_s) = refs
        ko_ref = vo_ref = None
    else:
        (xp_ref, xa_ref, modp_ref, moda_ref, g_ref, win_f32, cw_ref, cb_ref, qkg_ref, sink_ref, wout_f32,
         y_ref, ko_ref, vo_ref, win_ref, wout_ref,
         h_s, h_w, u_s, gate_s, q_s, k_s, vt_s, gb_s, u_w, gate_w, q_w, k_w, vt_w, gb_w, a_s) = refs
    s = pl.program_id(0)
    i = s - 1 - lag
    ci = lax.rem(i, jnp.int32(nc))

    @pl.when(s == 0)
    def _init():
        k_s[...] = jnp.zeros(k_s.shape, BF16)
        vt_s[...] = jnp.zeros(vt_s.shape, BF16)
        if not windowed:
            _cast_columns(win_f32, win_ref)
            _cast_columns(wout_f32, wout_ref)

    if windowed:
        @pl.when((i >= 0) & (ci == 0))
        def _load_ctx():
            ck_s[...] = _tiled_keys(ck_ref[0])
            cvt_s[...] = _values_t(cv_ref[0])

    def normalize(tokens):
        yield from _normalize_items(h_w, xp_ref, g_ref, modp_ref, tokens)

    def project():
        o_bg, o_cg, o_xs, o_ga, o_q, o_k, _, o_gb = EVEN_OFFS[:8]

        def proj(start):
            return jnp.dot(h_s[...], win_ref[start // NT], preferred_element_type=F32)

        lane = lax.broadcasted_iota(jnp.int32, (1, LANES), 1)
        first_half = (lane & (2 * ROPE_FREQS - 1)) < ROPE_FREQS
        ri = lax.broadcasted_iota(jnp.int32, (NT, NT), 0) >> HEAD_SHIFT
        cj = lax.broadcasted_iota(jnp.int32, (NT, NT), 1) >> HEAD_SHIFT
        bd = (ri == cj).astype(BF16)

        def qk_norm(x, g):
            w = x.shape[1]
            inv = _head_inv_rms(x, bd[0:w, 0:w])
            pieces = []
            for j in range(w // LANES):
                lanes = slice(j * LANES, (j + 1) * LANES)
                xj = x[:, lanes] * inv[:, lanes] * g
                pieces.append(_rope(xj, cos_ref[...], sin_ref[...], first_half) if windowed else xj)
            return pieces

        def keys_values(_):
            kv = proj(o_k)
            k, = qk_norm(kv[:, 0:KV_WIDTH], qkg_ref[1:2, :])
            v = kv[:, KV_WIDTH:2 * KV_WIDTH]
            if not windowed:
                ko_ref[0] = k
                vo_ref[0] = v
            k_w[...] = _tiled_keys(k)
            vt_w[...] = _values_t(v)

        def queries(n):
            for j, qj in enumerate(qk_norm(proj(o_q + n * NT), qkg_ref[0:1, :])):
                q_w[:, n * NT + j * LANES:n * NT + (j + 1) * LANES] = (qj * (ATTN_SCALE * LOG2E)).astype(BF16)

        def conv_gate(n):
            gate_w[:, n * NT:(n + 1) * NT] = proj(o_bg + n * NT) * _silu(proj(o_ga + n * NT))

        def conv_input(n):
            u_w[:, n * NT:(n + 1) * NT] = proj(o_cg + n * NT) * proj(o_xs + n * NT)

        def attn_gate(n):
            gb_w[:, n * NT:(n + 1) * NT] = _silu(proj(o_gb + n * NT))

        items = [(conv_gate, 0), (keys_values, 0), (conv_input, 0), (queries, 0), (conv_gate, 1), (queries, 1),
                 (conv_input, 1), (attn_gate, 0), (attn_gate, 1)]
        for emit, n in items:
            yield
            emit(n)

    def finish(slot, tokens):
        if windowed:
            prev_slot = lax.rem(slot + (ring - 1), jnp.int32(ring))
            next_slot = lax.rem(slot + 1, jnp.int32(ring))
            has_prev = ci > 0
            has_next = ci < nc - 1
        else:
            prev_slot = next_slot = has_prev = has_next = None

        yield
        ext = _with_halo(u_s, slot, slice(None), prev_slot, next_slot, has_prev, has_next)
        conv = (_rows_down(ext, 1)[HALO:HALO + TM] * cw_ref[0:1, :]
                + ext[HALO:HALO + TM] * cw_ref[1:2, :]
                + _rows_up(ext, 1)[HALO:HALO + TM] * cw_ref[2:3, :]
                + cb_ref[...])
        a_s[:, 0:CONV_WIDTH] = (gate_s[slot] * conv).astype(BF16)

        lane_head = lax.broadcasted_iota(jnp.int32, (1, GROUP_W), 1) >> HEAD_SHIFT
        cols4 = Q_PER_KV * BLOCK
        key_i =lax.broadcasted_iota(jnp.int32, (BLOCK, cols4), 0)
        qry_i = lax.broadcasted_iota(jnp.int32, (BLOCK, cols4), 1) & (BLOCK - 1)
        upper = lax.broadcasted_iota(jnp.int32, (LANES, LANES), 0) < HEAD_DIM
        halves = (slice(0, BLOCK), slice(BLOCK, TM))
        @functools.lru_cache(maxsize=None)
        def block_masks(qb):
            if qb == 0:
                blocks = ((prev_slot, halves[1]), (slot, halves[0]), (slot, halves[1]))
                prev_in, next_in = has_prev, True
            else:
                blocks = ((slot, halves[0]), (slot, halves[1]), (next_slot, halves[0]))
                prev_in, next_in = True, has_next
            prev_ok = (key_i - qry_i) >= jnp.where(prev_in, 0, BLOCK)
            next_ok = (qry_i - key_i) >= jnp.where(next_in, 0, BLOCK)
            return blocks, prev_ok, next_ok

        def scores(qb, kv):
            cols = slice(kv * GROUP_W, (kv + 1) * GROUP_W)
            vrows = slice(kv * LANES, (kv + 1) * LANES)
            qblk = q_s[slot, halves[qb], cols]
            q4 = jnp.concatenate(
                [jnp.where(lane_head == g, qblk, jnp.zeros_like(qblk)) for g in range(Q_PER_KV)], axis=0)

            def scores_t(kk):
                return lax.dot_general(kk, q4, (((1,), (1,)), ((), ())), preferred_element_type=F32)

            if not windowed:
                return [([scores_t(k_s[slot, :, cols])], vt_s[slot, vrows, :])]
            blocks, prev_ok, next_ok = block_masks(qb)
            s_loc = scores_t(jnp.concatenate([k_s[sl, r, cols] for sl, r in blocks], axis=0))
            vt_loc = jnp.concatenate([vt_s[sl, vrows, r] for sl, r in blocks], axis=1)
            s_ctx = scores_t(ck_s[:, cols])
            return [([jnp.where(prev_ok, s_loc[0:BLOCK], NEG), s_loc[BLOCK:2 * BLOCK],
                      jnp.where(next_ok, s_loc[2 * BLOCK:3 * BLOCK], NEG)], vt_loc),
                    ([s_ctx], cvt_s[vrows, :])]

        def softmax(kv, groups):
            snk = jnp.concatenate(
                [jnp.full((1, BLOCK), sink_ref[kv * Q_PER_KV + g] * LOG2E, F32) for g in range(Q_PER_KV)], axis=1)
            if tokens:
                token = tokens.pop(0)
                while tokens:
                    token = token + tokens.pop(0)
                snk = snk + jnp.concatenate([token] * Q_PER_KV, axis=1)
            mx = snk
            for ss, _ in groups:
                for sc in ss:
                    mx = jnp.maximum(mx, jnp.max(sc, axis=0, keepdims=True))
            den = jnp.exp2(snk - mx)
            probs = []
            for ss, vt in groups:
                es = []
                for sc in ss:
                    e = jnp.exp2(sc - mx)
                    den = den + jnp.sum(e, axis=0, keepdims=True)
                    es.append(e.astype(BF16))
                probs.append((es[0] if len(es) == 1 else jnp.concatenate(es, axis=0), vt))
            return probs, den

        def values(qb, kv, probs, den):
            o_t = jnp.zeros((LANES, cols4), F32)
            for e_all, vt in probs:
                o_t = o_t + jnp.dot(vt, e_all, preferred_element_type=F32)
            o_t = o_t * (1.0 / den)
            o01 = jnp.where(upper, o_t[:, 0:BLOCK], o_t[:, BLOCK:2 * BLOCK]).T
            o23 = jnp.where(upper, o_t[:, 2 * BLOCK:3 * BLOCK], o_t[:, 3 * BLOCK:4 * BLOCK]).T
            a_s[halves[qb], CONV_WIDTH + kv * GROUP_W:CONV_WIDTH + (kv + 1) * GROUP_W] = (
                jnp.concatenate([o01, o23], axis=1) * gb_s[slot, halves[qb], kv * GROUP_W:(kv + 1) * GROUP_W]
            ).astype(BF16)

        pairs = [(qb, kv) for qb in range(TM // BLOCK) for kv in range(N_KV_HEADS)]
        yield
        ahead = scores(*pairs[0])
        for t, (qb, kv) in enumerate(pairs):
            groups = ahead
            if t + 1 < len(pairs):
                yield
                ahead = scores(*pairs[t + 1])
            yield
            probs, den = softmax(kv, groups)
            yield
            values(qb, kv, probs, den)

        yield from _out_proj_items(lambda: a_s[...], wout_ref, xa_ref, moda_ref, y_ref)

    staged = ((u_w, u_s), (gate_w, gate_s), (q_w, q_s), (k_w, k_s), (vt_w, vt_s), (gb_w, gb_s))
    order = EVEN_ORDER_LATENT if windowed else EVEN_ORDER_CONTEXT
    _pipeline_steps(s, n_chunks, lag, ring, (h_w, h_s), staged, order, normalize, project, finish)


def _chunk_specs(n_chunks, nc, lag, per_seq_mod, tm=TM):
    last = n_chunks - 1
    norm_chunk = lambda s: jnp.minimum(s, last)
    proj_chunk = lambda s: jnp.clip(s - 1, 0, last)
    fin_chunk = lambda s: jnp.clip(s - 1 - lag, 0, last)
    seq_of = (lambda j: 1 + j // nc) if per_seq_mod else (lambda j: 0)
    return dict(
        x_norm=pl.BlockSpec((1, tm, D_MODEL), lambda s: (norm_chunk(s), 0, 0)),
        x_fin=pl.BlockSpec((1, tm, D_MODEL), lambda s: (fin_chunk(s), 0, 0)),
        mod_norm=pl.BlockSpec((1, 1, 3 * D_MODEL), lambda s: (seq_of(norm_chunk(s)), 0, 0)),
        mod_fin=pl.BlockSpec((1, 1, 3 * D_MODEL), lambda s: (seq_of(fin_chunk(s)), 0, 0)),
        proj_chunk=proj_chunk, fin_chunk=fin_chunk)


def _const_spec(shape):
    return pl.BlockSpec(shape, lambda c: (0,) * len(shape))


def _even_layer(x, mod, norm_g, w_in, conv_w, conv_b, qk_gains, sink, w_out, rope=None, ctx=None):
    n_seq, seq_len, _ = x.shape
    windowed = rope is not None
    nc = seq_len // TM
    n_chunks = n_seq * nc
    lag, ring = (2, 4) if windowed else (1, 2)
    assert windowed or nc == 1
    sp = _chunk_specs(n_chunks, nc, lag, windowed)
    xc = x.reshape(n_chunks, TM, D_MODEL)
    in_specs = [
        sp["x_norm"], sp["x_fin"], sp["mod_norm"], sp["mod_fin"],
        _const_spec((1, D_MODEL)),
        _const_spec(w_in.shape),
        _const_spec((3, CONV_WIDTH)),
        _const_spec((1, CONV_WIDTH)),
        _const_spec((2, LANES)),
        pl.BlockSpec(memory_space=pltpu.SMEM),
        _const_spec(w_out.shape),
    ]
    args = [xc, xc, mod, mod, norm_g.reshape(1, D_MODEL), w_in, conv_w, conv_b.reshape(1, CONV_WIDTH),
            qk_gains, sink, w_out]
    per_chunk = [
        ((TM, CONV_WIDTH), F32),
        ((TM, CONV_WIDTH), F32),
        ((TM, ATTN_WIDTH), BF16),
        ((TM, 2 * GROUP_W), BF16),
        ((2 * LANES, TM), BF16),
        ((TM, ATTN_WIDTH), F32),
    ]
    scratch = (
        [pltpu.VMEM((TM, D_MODEL), BF16)] * 2
        + [pltpu.VMEM((ring,) + shape, dt) for shape, dt in per_chunk]
        + [pltpu.VMEM(shape, dt) for shape, dt in per_chunk]
        + [pltpu.VMEM((TM, D_MODEL), BF16)])
    y_shape = jax.ShapeDtypeStruct(xc.shape, F32)
    y_spec = pl.BlockSpec((1, TM, D_MODEL), lambda c: (sp["fin_chunk"](c), 0, 0))
    if windowed:
        cos, sin = rope
        ck, cv = ctx
        rope_spec = pl.BlockSpec((TM, LANES), lambda c: (sp["proj_chunk"](c) % nc, 0))
        ctx_spec = pl.BlockSpec((1,) + ck.shape[1:], lambda c: (sp["fin_chunk"](c) // nc, 0, 0))
        in_specs += [rope_spec, rope_spec, ctx_spec, ctx_spec]
        args += [cos, sin, ck, cv]
        scratch += [pltpu.VMEM((ck.shape[1], 2 * GROUP_W), BF16), pltpu.VMEM((2 * LANES, cv.shape[1]), BF16)]
        out_shape, out_specs = y_shape, y_spec
    else:
        kv_shape = jax.ShapeDtypeStruct((n_chunks, TM, KV_WIDTH), F32)
        kv_spec = pl.BlockSpec((1, TM, KV_WIDTH), lambda c: (sp["proj_chunk"](c), 0, 0))
        out_shape = (y_shape, kv_shape, kv_shape,
                     jax.ShapeDtypeStruct(_tiled_shape(w_in), BF16), jax.ShapeDtypeStruct(_tiled_shape(w_out), BF16))
        out_specs = (y_spec, kv_spec, kv_spec, _const_spec(_tiled_shape(w_in)), _const_spec(_tiled_shape(w_out)))
    out = pl.pallas_call(
        functools.partial(_even_kernel, nc=nc, n_chunks=n_chunks, lag=lag, ring=ring, windowed=windowed),
        out_shape=out_shape,
        grid=(n_chunks + lag + 1,),
        in_specs=in_specs,
        out_specs=out_specs,
        scratch_shapes=scratch,
        compiler_params=PIPELINE_PARAMS,
        name="even_latent" if windowed else "even_context",
    )(*args)
    if windowed:
        return out.reshape(x.shape)
    y, k, v, w_in_bf16, w_out_bf16 = out
    return (y.reshape(x.shape), k.reshape(n_seq, seq_len, KV_WIDTH), v.reshape(n_seq, seq_len, KV_WIDTH),
            w_in_bf16, w_out_bf16)


def _odd_kernel(*refs, nc, n_chunks, lag, ring, cast_weights):
    if cast_weights:
        (xp_ref, xa_ref, modp_ref, moda_ref, g_ref, win_f32, pw_f32, ps_ref, wout_f32,
         y_ref, win_ref, pw_ref, wout_ref, h_s, h_w, u_s, sg_s, u_w, sg_w) = refs
    else:
        (xp_ref, xa_ref, modp_ref, moda_ref, g_ref, win_ref, pw_ref, ps_ref, wout_ref,
         y_ref, h_s, h_w, u_s, sg_s, u_w, sg_w) = refs
    tm = h_s.shape[0]
    s = pl.program_id(0)
    i = s - 1 - lag
    ci = lax.rem(i, jnp.int32(nc))

    if cast_weights:
        @pl.when(s == 0)
        def _cast():
            _cast_columns(win_f32, win_ref)
            _cast_columns(wout_f32, wout_ref)
            for gi in range(pw_ref.shape[0]):
                pw_ref[gi] = pw_f32[gi].astype(BF16)

    def normalize(tokens):
        yield from _normalize_items(h_w, xp_ref, g_ref, modp_ref, tokens)

    def project():
        for n in range(D_MODEL // NT):
            yield
            cols = slice(n * NT, (n + 1) * NT)
            u = jnp.dot(h_s[...], win_ref[n], preferred_element_type=F32)
            u_w[:, cols] = u
        for n in range(D_MODEL // NT):
            yield
            cols = slice(n * NT, (n + 1) * NT)
            sg = _silu(jnp.dot(h_s[...], win_ref[D_MODEL // NT + n], preferred_element_type=F32))
            sg_w[:, cols] = sg

    def finish(slot, tokens):
        if nc > 1:
            prev_slot = lax.rem(slot + (ring - 1), jnp.int32(ring))
            next_slot = lax.rem(slot + 1, jnp.int32(ring))
            has_prev = ci > 0
            has_next = ci < nc - 1
        else:
            prev_slot = next_slot = has_prev = has_next = None
        t = ci * tm + lax.broadcasted_iota(jnp.int32, (tm, 1), 0)
        outs = []
        for gi, w in enumerate(POOL_SIZES):
            yield
            cols = slice(gi * POOL_GROUP, (gi + 1) * POOL_GROUP)
            ext = _with_halo(u_s, slot, cols, prev_slot, next_slot, has_prev, has_next)
            acc = ext
            span = 1
            while span < w:
                acc = acc + _rows_down(acc, span)
                span *= 2
            if w > 2:
                acc = _rows_up(acc, w // 2 - 1)
            cnt = (jnp.minimum(t + w // 2, nc * tm) - jnp.maximum(t - w // 2, 0)).astype(F32)
            while tokens:
                cnt = cnt + tokens.pop(0)[:, 0:1]
            pooled = acc[HALO:HALO + tm] / cnt - ext[HALO:HALO + tm]
            outs.append(jnp.dot(pooled.astype(BF16), pw_ref[gi], preferred_element_type=F32))
        yield
        y = (jnp.concatenate(outs, axis=1) * ps_ref[...] * sg_s[slot]).astype(BF16)
        yield from _out_proj_items(lambda: y, wout_ref, xa_ref, moda_ref, y_ref)

    _pipeline_steps(s, n_chunks, lag, ring, (h_w, h_s), ((u_w, u_s), (sg_w, sg_s)), ODD_ORDER,
                    normalize, project, finish)


def _odd_layer(x, mod, per_seq_mod, norm_g, w_in, pool_w, pool_scale, w_out):
    n_seq, seq_len, _ = x.shape
    tm = TM
    nc = seq_len // tm
    n_chunks = n_seq * nc
    lag, ring = (2, 4) if nc > 1 else (1, 2)
    sp = _chunk_specs(n_chunks, nc, lag, per_seq_mod, tm)
    xc = x.reshape(n_chunks, tm, D_MODEL)
    in_specs = [
        sp["x_norm"], sp["x_fin"], sp["mod_norm"], sp["mod_fin"],
        _const_spec((1, D_MODEL)),
        _const_spec(w_in.shape),
        _const_spec(pool_w.shape),
        _const_spec((1, D_MODEL)),
        _const_spec(w_out.shape),
    ]
    cast_weights = w_in.dtype != BF16
    out_shape = jax.ShapeDtypeStruct(xc.shape, F32)
    out_specs = pl.BlockSpec((1, tm, D_MODEL), lambda c: (sp["fin_chunk"](c), 0, 0))
    if cast_weights:
        shapes = (_tiled_shape(w_in), pool_w.shape, _tiled_shape(w_out))
        out_shape = (out_shape,) + tuple(jax.ShapeDtypeStruct(shape, BF16) for shape in shapes)
        out_specs = (out_specs,) + tuple(_const_spec(shape) for shape in shapes)
    out = pl.pallas_call(
        functools.partial(_odd_kernel, nc=nc, n_chunks=n_chunks, lag=lag, ring=ring, cast_weights=cast_weights),
        out_shape=out_shape,
        grid=(n_chunks + lag + 1,),
        in_specs=in_specs,
        out_specs=out_specs,
        scratch_shapes=[
            pltpu.VMEM((tm, D_MODEL), BF16),
            pltpu.VMEM((tm, D_MODEL), BF16),
            pltpu.VMEM((ring, tm, D_MODEL), F32),
            pltpu.VMEM((ring, tm, D_MODEL), F32),
            pltpu.VMEM((tm, D_MODEL), F32),
            pltpu.VMEM((tm, D_MODEL), F32),
        ],
        compiler_params=PIPELINE_PARAMS,
        name="odd_latent" if per_seq_mod else "odd_context",
    )(xc, xc, mod, mod, norm_g.reshape(1, D_MODEL), w_in, pool_w, pool_scale.reshape(1, D_MODEL), w_out)
    if cast_weights:
        y, *bf16_weights = out
        return y.reshape(x.shape), bf16_weights
    return out.reshape(x.shape)


def _rope_tables(seq_len):
    n_rows = seq_len // GRID_W
    row = np.repeat(np.arange(n_rows), GRID_W).astype(np.float32)
    col = np.tile(np.arange(GRID_W), n_rows).astype(np.float32)
    inv = (np.float32(ROPE_BASE) ** (-np.arange(ROPE_FREQS, dtype=np.float32) / np.float32(ROPE_FREQS))).astype(np.float32)
    ang = np.stack([row[:, None] * inv, col[:, None] * inv], axis=1)
    cos = np.broadcast_to(np.cos(ang)[:, :, None, :], (seq_len, 2, 2, ROPE_FREQS))
    sin = np.sin(ang)[:, :, None, :] * np.array([-1.0, 1.0], np.float32)[None, None, :, None]
    cos = cos.reshape(seq_len, HEAD_DIM).astype(np.float32)
    sin = sin.reshape(seq_len, HEAD_DIM).astype(np.float32)
    return jnp.asarray(np.tile(cos, (1, LANES // HEAD_DIM))), jnp.asarray(np.tile(sin, (1, LANES // HEAD_DIM)))


def kernel(x_prompt, x_sample, cache_k, cache_v, c, c_ctx, ada_w_e, ada_b_e, norm_g_e, w_in_e, conv_w, conv_b,
           q_norm_g, k_norm_g, sink, w_out_e, ada_w_o, ada_b_o, norm_g_o, w_in_o, pool_w, pool_scale, w_out_o):
    n_dec = x_sample.shape[0]
    depth = ada_w_e.shape[0] + ada_w_o.shape[0]
    assert ada_w_e.shape[0] == 1 and ada_w_o.shape[0] == 1 and n_dec + 1 <= 8
    mod_e, mod_o = _adaln(c_ctx, c, ada_w_e[0], ada_b_e[0], ada_w_o[0], ada_b_o[0])
    rope = _rope_tables(x_sample.shape[1])

    yp, ys = x_prompt, x_sample
    new_k, new_v = [], []
    for layer in range(depth):
        i = layer // 2
        if layer % 2 == 0:
            mod = mod_e
            gains = jnp.tile(jnp.stack([q_norm_g[i], k_norm_g[i]]), (1, LANES // HEAD_DIM))
            small = (conv_w[i], conv_b[i], gains, sink[i])
            yp, k, v, w_in, w_out = _even_layer(yp, mod, norm_g_e[i], w_in_e[i], *small, w_out_e[i])
            new_k.append(k.reshape(k.shape[0], k.shape[1], N_KV_HEADS, HEAD_DIM))
            new_v.append(v.reshape(v.shape[0], v.shape[1], N_KV_HEADS, HEAD_DIM))
            ck = cache_k[:, i].reshape(n_dec, cache_k.shape[2], KV_WIDTH)
            cv = cache_v[:, i].reshape(n_dec, cache_v.shape[2], KV_WIDTH)
            ys = _even_layer(ys, mod, norm_g_e[i], w_in, *small, w_out, rope=rope, ctx=(ck, cv))
        else:
            mod = mod_o
            yp, (w_in, w_pool, w_out) = _odd_layer(yp, mod, False, norm_g_o[i], w_in_o[i], pool_w[i],
                                                   pool_scale[i], w_out_o[i])
            ys = _odd_layer(ys, mod, True, norm_g_o[i], w_in, w_pool, pool_scale[i], w_out)
    return yp, ys, jnp.stack(new_k, axis=1), jnp.stack(new_v, axis=1)
```

```python
import functools

import jax
import jax.numpy as jnp
import numpy as np
from jax import lax
from jax.experimental import pallas as pl
from jax.experimental.pallas import tpu as pltpu

F32 = jnp.float32
BF16 = jnp.bfloat16

D_MODEL = 1024
GRID_W = 64
HEAD_DIM = 64
HEAD_SHIFT = 6
ATTN_WIDTH = D_MODEL // 2
N_Q_HEADS = ATTN_WIDTH // HEAD_DIM
N_KV_HEADS = N_Q_HEADS // 4
Q_PER_KV = N_Q_HEADS // N_KV_HEADS
KV_WIDTH = N_KV_HEADS * HEAD_DIM
CONV_WIDTH = D_MODEL - ATTN_WIDTH
WINDOW = 128
BLOCK = 128
ROPE_BASE = 10000.0
ROPE_FREQS = HEAD_DIM // 4
ATTN_SCALE = HEAD_DIM ** -0.5
LOG2E = float(np.log2(np.e))
NEG = -1e30
POOL_SIZES = (2, 4, 8, 16)
POOL_GROUP = D_MODEL // len(POOL_SIZES)
EPS = 1e-6
EVEN_SIZES = (CONV_WIDTH, CONV_WIDTH, CONV_WIDTH, CONV_WIDTH, ATTN_WIDTH, KV_WIDTH, KV_WIDTH, ATTN_WIDTH)
EVEN_IN = sum(EVEN_SIZES)
EVEN_OFFS = tuple(int(s) for s in np.cumsum((0,) + EVEN_SIZES))

TM = 256
ODD_TM_LONG = 512
NT = 256
NORM_ROWS = 64
EVEN_ORDER_LATENT = "PF" + "F" + "PFNFPF" * 3 + "PNFPF" + "FFFF"
EVEN_ORDER_CONTEXT = "PF" + "F" + "FNPFPF" * 3 + "NPFPF" + "FFFF"
LANES = 128
GROUP_W = Q_PER_KV * HEAD_DIM
HALO = 8
ADA_TN = 768
VMEM_LIMIT = 56 * 1024 * 1024
PIPELINE_PARAMS = pltpu.CompilerParams(dimension_semantics=("arbitrary",), vmem_limit_bytes=VMEM_LIMIT)


def _silu(x):
    half = 0.5 * x
    return half + half * jnp.tanh(half)


def _rows_down(x, k):
    return pltpu.roll(x, k, axis=0)


def _rows_up(x, k):
    return pltpu.roll(x, x.shape[0] - k, axis=0)


def _rms_modulate(x, gain, shift):
    ms = jnp.mean(x * x, axis=-1, keepdims=True)
    return x * lax.rsqrt(ms + EPS) * gain + shift


def _cast_columns(src_ref, dst_ref):
    for n in range(dst_ref.shape[0]):
        dst_ref[n] = src_ref[:, n * NT:(n + 1) * NT].astype(dst_ref.dtype)


def _tiled_shape(w):
    return (w.shape[1] // NT, w.shape[0], NT)


def _opaque_zero(x):
    bits = pltpu.bitcast(x, jnp.uint32)
    return pltpu.bitcast((bits >> 16) >> 16, F32)


def _normalize_items(h_s, x_ref, g_ref, mod_ref, tokens):
    shift = mod_ref[0, :, 0:D_MODEL]
    gain = g_ref[...] * (1.0 + mod_ref[0, :, D_MODEL:2 * D_MODEL])
    for r in range(0, h_s.shape[0], NORM_ROWS):
        yield
        h = _rms_modulate(x_ref[0, r:r + NORM_ROWS], gain, shift)
        h_s[r:r + NORM_ROWS] = h.astype(BF16)
        tokens.append(_opaque_zero(h[0:8, 0:LANES])[0:1])


def _out_proj_items(get_a, wout_ref, x_ref, mod_ref, y_ref):
    for n in range(D_MODEL // NT):
        yield
        cols = slice(n * NT, (n + 1) * NT)
        mixed = jnp.dot(get_a(), wout_ref[n], preferred_element_type=F32)
        y_ref[0, :, cols] = x_ref[0, :, cols] + mod_ref[0, :, 2 * D_MODEL + n * NT:2 * D_MODEL + (n + 1) * NT] * mixed


def _weave(order, **stages):
    by_letter = {name[0].upper(): stage for name, stage in stages.items()}
    for stage in by_letter.values():
        next(stage, None)
    for letter in order:
        if letter in by_letter:
            next(by_letter[letter], None)
    for stage in by_letter.values():
        for _ in stage:
            pass


def _with_halo(u_s, slot, cols, prev_slot, next_slot, has_prev, has_next):
    u = u_s[slot, :, cols]
    zero = jnp.zeros((HALO, u.shape[1]), F32)
    if prev_slot is None:
        top = bot = zero
    else:
        rows = u_s.shape[1]
        top = jnp.where(has_prev, u_s[prev_slot, rows - HALO:rows, cols], zero)
        bot = jnp.where(has_next, u_s[next_slot, 0:HALO, cols], zero)
    return jnp.concatenate([top, u, bot], axis=0)


def _pipeline_steps(s, n_chunks, lag, ring, normalized, staged, order, normalize, project, finish):
    fin_slot = lax.rem(s + (2 * ring - 1 - lag), jnp.int32(ring))

    @pl.when((s >= 1) & (s <= n_chunks))
    def _publish_normalized():
        h_w, h_s = normalized
        h_s[...] = h_w[...]

    @pl.when((s >= 2) & (s <= n_chunks + 1))
    def _publish_projected():
        slot = lax.rem(s + (ring - 2), jnp.int32(ring))
        for stage_ref, ring_ref in staged:
            ring_ref[slot] = stage_ref[...]

    @pl.when(s == 0)
    def _first():
        _weave(order, normalize=normalize([]))

    @pl.when((s >= 1) & (s <= lag))
    def _fill():
        _weave(order, project=project(), normalize=normalize([]))

    @pl.when((s > lag) & (s <= n_chunks))
    def _steady():
        tokens = []
        _weave(order, finish=finish(fin_slot, tokens), project=project(), normalize=normalize(tokens))

    @pl.when(s > n_chunks)
    def _drain():
        _weave(order, finish=finish(fin_slot, []))


def _adaln_kernel(cctx_ref, c_ref, we_ref, be_ref, wo_ref, bo_ref, oe_ref, oo_ref, cond_s):
    n_dec = c_ref.shape[0]
    cond_s[...] = jnp.zeros(cond_s.shape, F32)
    cond_s[0:1] = cctx_ref[...]
    cond_s[1:1 + n_dec] = c_ref[...]
    s = _silu(cond_s[...]).astype(BF16)
    oe = jnp.dot(s, we_ref[...].astype(BF16), preferred_element_type=F32) + be_ref[...]
    oo = jnp.dot(s, wo_ref[...].astype(BF16), preferred_element_type=F32) + bo_ref[...]
    for r in range(oe_ref.shape[0]):
        oe_ref[r] = oe[r:r + 1]
        oo_ref[r] = oo[r:r + 1]


def _adaln(c_ctx, c, w_e, b_e, w_o, b_o):
    n = 3 * D_MODEL
    n_dec = c.shape[0]
    wspec = pl.BlockSpec((D_MODEL, ADA_TN), lambda j: (0, j))
    vspec = pl.BlockSpec((1, ADA_TN), lambda j: (0, j))
    ospec = pl.BlockSpec((8, 1, ADA_TN), lambda j: (0, 0, j))
    return pl.pallas_call(
        _adaln_kernel,
        out_shape=(jax.ShapeDtypeStruct((8, 1, n), F32), jax.ShapeDtypeStruct((8, 1, n), F32)),
        grid=(n // ADA_TN,),
        in_specs=[pl.BlockSpec((1, D_MODEL), lambda j: (0, 0)), pl.BlockSpec((n_dec, D_MODEL), lambda j: (0, 0)),
                  wspec, vspec, wspec, vspec],
        out_specs=(ospec, ospec),
        scratch_shapes=[pltpu.VMEM((8, D_MODEL), F32)],
        compiler_params=pltpu.CompilerParams(dimension_semantics=("arbitrary",), vmem_limit_bytes=VMEM_LIMIT),
        name="adaln",
    )(c_ctx.reshape(1, D_MODEL), c, w_e, b_e.reshape(1, n), w_o, b_o.reshape(1, n))


def _head_inv_rms(x, bd):
    ms = jnp.dot((x * x).astype(BF16), bd, preferred_element_type=F32) * (1.0 / HEAD_DIM)
    return lax.rsqrt(ms + EPS)


def _rope(x, cos, sin_signed, first_half):
    partner = jnp.where(first_half, pltpu.roll(x, LANES - ROPE_FREQS, axis=1), pltpu.roll(x, ROPE_FREQS, axis=1))
    return x * cos + partner * sin_signed


def _dup_kv(x):
    lane = lax.broadcasted_iota(jnp.int32, (1, LANES), 1)
    swapped = pltpu.roll(x, HEAD_DIM, axis=1)
    return jnp.where(lane < HEAD_DIM, x, swapped), jnp.where(lane < HEAD_DIM, swapped, x)


def _tiled_keys(k):
    ka, kb = _dup_kv(k)
    return jnp.concatenate([ka, ka, kb, kb], axis=1).astype(BF16)


def _values_t(v):
    va, vb = _dup_kv(v)
    return jnp.concatenate([va.T, vb.T], axis=0).astype(BF16)


def _even_kernel(*refs, nc, n_chunks, lag, ring, windowed):
    if windowed:
        (xp_ref, xa_ref, modp_ref, moda_ref, g_ref, win_ref, cw_ref, cb_ref, qkg_ref, sink_ref, wout_ref,
         cos_ref, sin_ref, ck_ref, cv_ref,
         y_ref,
         h_s, h_w, u_s, gate_s, q_s, k_s, vt_s, gb_s, u_w, gate_w, q_w, k_w, vt_w, gb_w, a_s, ck_s, cvt_s) = refs
        ko_ref = vo_ref = None
    else:
        (xp_ref, xa_ref, modp_ref, moda_ref, g_ref, win_f32, cw_ref, cb_ref, qkg_ref, sink_ref, wout_f32,
         y_ref, ko_ref, vo_ref, win_ref, wout_ref,
         h_s, h_w, u_s, gate_s, q_s, k_s, vt_s, gb_s, u_w, gate_w, q_w, k_w, vt_w, gb_w, a_s) = refs
    s = pl.program_id(0)
    i = s - 1 - lag
    ci = lax.rem(i, jnp.int32(nc))

    @pl.when(s == 0)
    def _init():
        k_s[...] = jnp.zeros(k_s.shape, BF16)
        vt_s[...] = jnp.zeros(vt_s.shape, BF16)
        if not windowed:
            _cast_columns(win_f32, win_ref)
            _cast_columns(wout_f32, wout_ref)

    if windowed:
        @pl.when((i >= 0) & (ci == 0))
        def _load_ctx():
            ck_s[...] = _tiled_keys(ck_ref[0])
            cvt_s[...] = _values_t(cv_ref[0])

    def normalize(tokens):
        yield from _normalize_items(h_w, xp_ref, g_ref, modp_ref, tokens)

    def project():
        o_bg, o_cg, o_xs, o_ga, o_q, o_k, _, o_gb = EVEN_OFFS[:8]

        def proj(start):
            return jnp.dot(h_s[...], win_ref[start // NT], preferred_element_type=F32)

        lane = lax.broadcasted_iota(jnp.int32, (1, LANES), 1)
        first_half = (lane & (2 * ROPE_FREQS - 1)) < ROPE_FREQS
        ri = lax.broadcasted_iota(jnp.int32, (NT, NT), 0) >> HEAD_SHIFT
        cj = lax.broadcasted_iota(jnp.int32, (NT, NT), 1) >> HEAD_SHIFT
        bd = (ri == cj).astype(BF16)

        def qk_norm(x, g):
            w = x.shape[1]
            inv = _head_inv_rms(x, bd[0:w, 0:w])
            pieces = []
            for j in range(w // LANES):
                lanes = slice(j * LANES, (j + 1) * LANES)
                xj = x[:, lanes] * inv[:, lanes] * g
                pieces.append(_rope(xj, cos_ref[...], sin_ref[...], first_half) if windowed else xj)
            return pieces

        def keys_values(_):
            kv = proj(o_k)
            k, = qk_norm(kv[:, 0:KV_WIDTH], qkg_ref[1:2, :])
            v = kv[:, KV_WIDTH:2 * KV_WIDTH]
            if not windowed:
                ko_ref[0] = k
                vo_ref[0] = v
            k_w[...] = _tiled_keys(k)
            vt_w[...] = _values_t(v)

        def queries(n):
            for j, qj in enumerate(qk_norm(proj(o_q + n * NT), qkg_ref[0:1, :])):
                q_w[:, n * NT + j * LANES:n * NT + (j + 1) * LANES] = (qj * (ATTN_SCALE * LOG2E)).astype(BF16)

        def conv_gate(n):
            gate_w[:, n * NT:(n + 1) * NT] = proj(o_bg + n * NT) * _silu(proj(o_ga + n * NT))

        def conv_input(n):
            u_w[:, n * NT:(n + 1) * NT] = proj(o_cg + n * NT) * proj(o_xs + n * NT)

        def attn_gate(n):
            gb_w[:, n * NT:(n + 1) * NT] = _silu(proj(o_gb + n * NT))

        items = [(conv_gate, 0), (keys_values, 0), (conv_input, 0), (queries, 0), (conv_gate, 1), (queries, 1),
                 (conv_input, 1), (attn_gate, 0), (attn_gate, 1)]
        for emit, n in items:
            yield
            emit(n)

    def finish(slot, tokens):
        if windowed:
            prev_slot = lax.rem(slot + (ring - 1), jnp.int32(ring))
            next_slot = lax.rem(slot + 1, jnp.int32(ring))
            has_prev = ci > 0
            has_next = ci < nc - 1
        else:
            prev_slot = next_slot = has_prev = has_next = None

        yield
        ext = _with_halo(u_s, slot, slice(None), prev_slot, next_slot, has_prev, has_next)
        conv = (_rows_down(ext, 1)[HALO:HALO + TM] * cw_ref[0:1, :]
                + ext[HALO:HALO + TM] * cw_ref[1:2, :]
                + _rows_up(ext, 1)[HALO:HALO + TM] * cw_ref[2:3, :]
                + cb_ref[...])
        a_s[:, 0:CONV_WIDTH] = (gate_s[slot] * conv).astype(BF16)

        lane_head = lax.broadcasted_iota(jnp.int32, (1, GROUP_W), 1) >> HEAD_SHIFT
        cols4 = Q_PER_KV * BLOCK
        key_i =lax.broadcasted_iota(jnp.int32, (BLOCK, cols4), 0)
        qry_i = lax.broadcasted_iota(jnp.int32, (BLOCK, cols4), 1) & (BLOCK - 1)
        upper = lax.broadcasted_iota(jnp.int32, (LANES, LANES), 0) < HEAD_DIM
        halves = (slice(0, BLOCK), slice(BLOCK, TM))
        @functools.lru_cache(maxsize=None)
        def block_masks(qb):
            if qb == 0:
                blocks = ((prev_slot, halves[1]), (slot, halves[0]), (slot, halves[1]))
                prev_in, next_in = has_prev, True
            else:
                blocks = ((slot, halves[0]), (slot, halves[1]), (next_slot, halves[0]))
                prev_in, next_in = True, has_next
            prev_ok = (key_i - qry_i) >= jnp.where(prev_in, 0, BLOCK)
            next_ok = (qry_i - key_i) >= jnp.where(next_in, 0, BLOCK)
            return blocks, prev_ok, next_ok

        def scores(qb, kv):
            cols = slice(kv * GROUP_W, (kv + 1) * GROUP_W)
            vrows = slice(kv * LANES, (kv + 1) * LANES)
            qblk = q_s[slot, halves[qb], cols]
            q4 = jnp.concatenate(
                [jnp.where(lane_head == g, qblk, jnp.zeros_like(qblk)) for g in range(Q_PER_KV)], axis=0)

            def scores_t(kk):
                return lax.dot_general(kk, q4, (((1,), (1,)), ((), ())), preferred_element_type=F32)

            if not windowed:
                return [([scores_t(k_s[slot, :, cols])], vt_s[slot, vrows, :])]
            blocks, prev_ok, next_ok = block_masks(qb)
            s_loc = scores_t(jnp.concatenate([k_s[sl, r, cols] for sl, r in blocks], axis=0))
            vt_loc = jnp.concatenate([vt_s[sl, vrows, r] for sl, r in blocks], axis=1)
            s_ctx = scores_t(ck_s[:, cols])
            return [([jnp.where(prev_ok, s_loc[0:BLOCK], NEG), s_loc[BLOCK:2 * BLOCK],
                      jnp.where(next_ok, s_loc[2 * BLOCK:3 * BLOCK], NEG)], vt_loc),
                    ([s_ctx], cvt_s[vrows, :])]

        def softmax(kv, groups):
            snk = jnp.concatenate(
                [jnp.full((1, BLOCK), sink_ref[kv * Q_PER_KV + g] * LOG2E, F32) for g in range(Q_PER_KV)], axis=1)
            if tokens:
                token = tokens.pop(0)
                while tokens:
                    token = token + tokens.pop(0)
                snk = snk + jnp.concatenate([token] * Q_PER_KV, axis=1)
            mx = snk
            for ss, _ in groups:
                for sc in ss:
                    mx = jnp.maximum(mx, jnp.max(sc, axis=0, keepdims=True))
            den = jnp.exp2(snk - mx)
            probs = []
            for ss, vt in groups:
                es = []
                for sc in ss:
                    e = jnp.exp2(sc - mx)
                    den = den + jnp.sum(e, axis=0, keepdims=True)
                    es.append(e.astype(BF16))
                probs.append((es[0] if len(es) == 1 else jnp.concatenate(es, axis=0), vt))
            return probs, den

        def values(qb, kv, probs, den):
            o_t = jnp.zeros((LANES, cols4), F32)
            for e_all, vt in probs:
                o_t = o_t + jnp.dot(vt, e_all, preferred_element_type=F32)
            o_t = o_t * (1.0 / den)
            o01 = jnp.where(upper, o_t[:, 0:BLOCK], o_t[:, BLOCK:2 * BLOCK]).T
            o23 = jnp.where(upper, o_t[:, 2 * BLOCK:3 * BLOCK], o_t[:, 3 * BLOCK:4 * BLOCK]).T
            a_s[halves[qb], CONV_WIDTH + kv * GROUP_W:CONV_WIDTH + (kv + 1) * GROUP_W] = (
                jnp.concatenate([o01, o23], axis=1) * gb_s[slot, halves[qb], kv * GROUP_W:(kv + 1) * GROUP_W]
            ).astype(BF16)

        pairs = [(qb, kv) for qb in range(TM // BLOCK) for kv in range(N_KV_HEADS)]
        yield
        ahead = scores(*pairs[0])
        for t, (qb, kv) in enumerate(pairs):
            groups = ahead
            if t + 1 < len(pairs):
                yield
                ahead = scores(*pairs[t + 1])
            yield
            probs, den = softmax(kv, groups)
            yield
            values(qb, kv, probs, den)

        yield from _out_proj_items(lambda: a_s[...], wout_ref, xa_ref, moda_ref, y_ref)

    staged = ((u_w, u_s), (gate_w, gate_s), (q_w, q_s), (k_w, k_s), (vt_w, vt_s), (gb_w, gb_s))
    order = EVEN_ORDER_LATENT if windowed else EVEN_ORDER_CONTEXT
    _pipeline_steps(s, n_chunks, lag, ring, (h_w, h_s), staged, order, normalize, project, finish)


def _chunk_specs(n_chunks, nc, lag, per_seq_mod, tm=TM):
    last = n_chunks - 1
    norm_chunk = lambda s: jnp.minimum(s, last)
    proj_chunk = lambda s: jnp.clip(s - 1, 0, last)
    fin_chunk = lambda s: jnp.clip(s - 1 - lag, 0, last)
    seq_of = (lambda j: 1 + j // nc) if per_seq_mod else (lambda j: 0)
    return dict(
        x_norm=pl.BlockSpec((1, tm, D_MODEL), lambda s: (norm_chunk(s), 0, 0)),
        x_fin=pl.BlockSpec((1, tm, D_MODEL), lambda s: (fin_chunk(s), 0, 0)),
        mod_norm=pl.BlockSpec((1, 1, 3 * D_MODEL), lambda s: (seq_of(norm_chunk(s)), 0, 0)),
        mod_fin=pl.BlockSpec((1, 1, 3 * D_MODEL), lambda s: (seq_of(fin_chunk(s)), 0, 0)),
        proj_chunk=proj_chunk, fin_chunk=fin_chunk)


def _const_spec(shape):
    return pl.BlockSpec(shape, lambda c: (0,) * len(shape))


def _even_layer(x, mod, norm_g, w_in, conv_w, conv_b, qk_gains, sink, w_out, rope=None, ctx=None):
    n_seq, seq_len, _ = x.shape
    windowed = rope is not None
    nc = seq_len // TM
    n_chunks = n_seq * nc
    lag, ring = (2, 4) if windowed else (1, 2)
    assert windowed or nc == 1
    sp = _chunk_specs(n_chunks, nc, lag, windowed)
    xc = x.reshape(n_chunks, TM, D_MODEL)
    in_specs = [
        sp["x_norm"], sp["x_fin"], sp["mod_norm"], sp["mod_fin"],
        _const_spec((1, D_MODEL)),
        _const_spec(w_in.shape),
        _const_spec((3, CONV_WIDTH)),
        _const_spec((1, CONV_WIDTH)),
        _const_spec((2, LANES)),
        pl.BlockSpec(memory_space=pltpu.SMEM),
        _const_spec(w_out.shape),
    ]
    args = [xc, xc, mod, mod, norm_g.reshape(1, D_MODEL), w_in, conv_w, conv_b.reshape(1, CONV_WIDTH),
            qk_gains, sink, w_out]
    per_chunk = [
        ((TM, CONV_WIDTH), F32),
        ((TM, CONV_WIDTH), F32),
        ((TM, ATTN_WIDTH), BF16),
        ((TM, 2 * GROUP_W), BF16),
        ((2 * LANES, TM), BF16),
        ((TM, ATTN_WIDTH), F32),
    ]
    scratch = (
        [pltpu.VMEM((TM, D_MODEL), BF16)] * 2
        + [pltpu.VMEM((ring,) + shape, dt) for shape, dt in per_chunk]
        + [pltpu.VMEM(shape, dt) for shape, dt in per_chunk]
        + [pltpu.VMEM((TM, D_MODEL), BF16)])
    y_shape = jax.ShapeDtypeStruct(xc.shape, F32)
    y_spec = pl.BlockSpec((1, TM, D_MODEL), lambda c: (sp["fin_chunk"](c), 0, 0))
    if windowed:
        cos, sin = rope
        ck, cv = ctx
        rope_spec = pl.BlockSpec((TM, LANES), lambda c: (sp["proj_chunk"](c) % nc, 0))
        ctx_spec = pl.BlockSpec((1,) + ck.shape[1:], lambda c: (sp["fin_chunk"](c) // nc, 0, 0))
        in_specs += [rope_spec, rope_spec, ctx_spec, ctx_spec]
        args += [cos, sin, ck, cv]
        scratch += [pltpu.VMEM((ck.shape[1], 2 * GROUP_W), BF16), pltpu.VMEM((2 * LANES, cv.shape[1]), BF16)]
        out_shape, out_specs = y_shape, y_spec
    else:
        kv_shape = jax.ShapeDtypeStruct((n_chunks, TM, KV_WIDTH), F32)
        kv_spec = pl.BlockSpec((1, TM, KV_WIDTH), lambda c: (sp["proj_chunk"](c), 0, 0))
        out_shape = (y_shape, kv_shape, kv_shape,
                     jax.ShapeDtypeStruct(_tiled_shape(w_in), BF16), jax.ShapeDtypeStruct(_tiled_shape(w_out), BF16))
        out_specs = (y_spec, kv_spec, kv_spec, _const_spec(_tiled_shape(w_in)), _const_spec(_tiled_shape(w_out)))
    out = pl.pallas_call(
        functools.partial(_even_kernel, nc=nc, n_chunks=n_chunks, lag=lag, ring=ring, windowed=windowed),
        out_shape=out_shape,
        grid=(n_chunks + lag + 1,),
        in_specs=in_specs,
        out_specs=out_specs,
        scratch_shapes=scratch,
        compiler_params=PIPELINE_PARAMS,
        name="even_latent" if windowed else "even_context",
    )(*args)
    if windowed:
        return out.reshape(x.shape)
    y, k, v, w_in_bf16, w_out_bf16 = out
    return (y.reshape(x.shape), k.reshape(n_seq, seq_len, KV_WIDTH), v.reshape(n_seq, seq_len, KV_WIDTH),
            w_in_bf16, w_out_bf16)


def _odd_kernel(*refs, nc, n_chunks, lag, ring, cast_weights):
    if cast_weights:
        (xp_ref, xa_ref, modp_ref, moda_ref, g_ref, win_f32, pw_f32, ps_ref, wout_f32,
         y_ref, win_ref, pw_ref, wout_ref, h_s, h_w, u_s, sg_s, u_w, sg_w) = refs
    else:
        (xp_ref, xa_ref, modp_ref, moda_ref, g_ref, win_ref, pw_ref, ps_ref, wout_ref,
         y_ref, h_s, h_w, u_s, sg_s, u_w, sg_w) = refs
    tm = h_s.shape[0]
    s = pl.program_id(0)
    i = s - 1 - lag
    ci = lax.rem(i, jnp.int32(nc))

    if cast_weights:
        @pl.when(s == 0)
        def _cast():
            _cast_columns(win_f32, win_ref)
            _cast_columns(wout_f32, wout_ref)
            for gi in range(pw_ref.shape[0]):
                pw_ref[gi] = pw_f32[gi].astype(BF16)

    def normalize(tokens):
        yield from _normalize_items(h_w, xp_ref, g_ref, modp_ref, tokens)

    def project():
        for n in range(D_MODEL // NT):
            yield
            cols = slice(n * NT, (n + 1) * NT)
            u = jnp.dot(h_s[...], win_ref[n], preferred_element_type=F32)
            u_w[:, cols] = u
        for n in range(D_MODEL // NT):
            yield
            cols = slice(n * NT, (n + 1) * NT)
            sg = _silu(jnp.dot(h_s[...], win_ref[D_MODEL // NT + n], preferred_element_type=F32))
            sg_w[:, cols] = sg

    def finish(slot, tokens):
        if nc > 1:
            prev_slot = lax.rem(slot + (ring - 1), jnp.int32(ring))
            next_slot = lax.rem(slot + 1, jnp.int32(ring))
            has_prev = ci > 0
            has_next = ci < nc - 1
        else:
            prev_slot = next_slot = has_prev = has_next = None
        t = ci * tm + lax.broadcasted_iota(jnp.int32, (tm, 1), 0)
        outs = []
        for gi, w in enumerate(POOL_SIZES):
            yield
            cols = slice(gi * POOL_GROUP, (gi + 1) * POOL_GROUP)
            ext = _with_halo(u_s, slot, cols, prev_slot, next_slot, has_prev, has_next)
            acc = ext
            span = 1
            while span < w:
                acc = acc + _rows_down(acc, span)
                span *= 2
            if w > 2:
                acc = _rows_up(acc, w // 2 - 1)
            cnt = (jnp.minimum(t + w // 2, nc * tm) - jnp.maximum(t - w // 2, 0)).astype(F32)
            while tokens:
                cnt = cnt + tokens.pop(0)[:, 0:1]
            pooled = acc[HALO:HALO + tm] / cnt - ext[HALO:HALO + tm]
            outs.append(jnp.dot(pooled.astype(BF16), pw_ref[gi], preferred_element_type=F32))
        yield
        y = (jnp.concatenate(outs, axis=1) * ps_ref[...] * sg_s[slot]).astype(BF16)
        yield from _out_proj_items(lambda: y, wout_ref, xa_ref, moda_ref, y_ref)

    order = ("P" + "N" * (tm // TM) + "FP") * len(POOL_SIZES) + "FFFFF"
    _pipeline_steps(s, n_chunks, lag, ring, (h_w, h_s), ((u_w, u_s), (sg_w, sg_s)), order,
                    normalize, project, finish)


def _odd_layer(x, mod, per_seq_mod, norm_g, w_in, pool_w, pool_scale, w_out):
    n_seq, seq_len, _ = x.shape
    tm = ODD_TM_LONG if seq_len % ODD_TM_LONG == 0 else TM
    nc = seq_len // tm
    n_chunks = n_seq * nc
    lag, ring = (2, 4) if nc > 1 else (1, 2)
    sp = _chunk_specs(n_chunks, nc, lag, per_seq_mod, tm)
    xc = x.reshape(n_chunks, tm, D_MODEL)
    in_specs = [
        sp["x_norm"], sp["x_fin"], sp["mod_norm"], sp["mod_fin"],
        _const_spec((1, D_MODEL)),
        _const_spec(w_in.shape),
        _const_spec(pool_w.shape),
        _const_spec((1, D_MODEL)),
        _const_spec(w_out.shape),
    ]
    cast_weights = w_in.dtype != BF16
    out_shape = jax.ShapeDtypeStruct(xc.shape, F32)
    out_specs = pl.BlockSpec((1, tm, D_MODEL), lambda c: (sp["fin_chunk"](c), 0, 0))
    if cast_weights:
        shapes = (_tiled_shape(w_in), pool_w.shape, _tiled_shape(w_out))
        out_shape = (out_shape,) + tuple(jax.ShapeDtypeStruct(shape, BF16) for shape in shapes)
        out_specs = (out_specs,) + tuple(_const_spec(shape) for shape in shapes)
    out = pl.pallas_call(
        functools.partial(_odd_kernel, nc=nc, n_chunks=n_chunks, lag=lag, ring=ring, cast_weights=cast_weights),
        out_shape=out_shape,
        grid=(n_chunks + lag + 1,),
        in_specs=in_specs,
        out_specs=out_specs,
        scratch_shapes=[
            pltpu.VMEM((tm, D_MODEL), BF16),
            pltpu.VMEM((tm, D_MODEL), BF16),
            pltpu.VMEM((ring, tm, D_MODEL), F32),
            pltpu.VMEM((ring, tm, D_MODEL), F32),
            pltpu.VMEM((tm, D_MODEL), F32),
            pltpu.VMEM((tm, D_MODEL), F32),
        ],
        compiler_params=PIPELINE_PARAMS,
        name="odd_latent" if per_seq_mod else "odd_context",
    )(xc, xc, mod, mod, norm_g.reshape(1, D_MODEL), w_in, pool_w, pool_scale.reshape(1, D_MODEL), w_out)
    if cast_weights:
        y, *bf16_weights = out
        return y.reshape(x.shape), bf16_weights
    return out.reshape(x.shape)


def _rope_tables(seq_len):
    n_rows = seq_len // GRID_W
    row = np.repeat(np.arange(n_rows), GRID_W).astype(np.float32)
    col = np.tile(np.arange(GRID_W), n_rows).astype(np.float32)
    inv = (np.float32(ROPE_BASE) ** (-np.arange(ROPE_FREQS, dtype=np.float32) / np.float32(ROPE_FREQS))).astype(np.float32)
    ang = np.stack([row[:, None] * inv, col[:, None] * inv], axis=1)
    cos = np.broadcast_to(np.cos(ang)[:, :, None, :], (seq_len, 2, 2, ROPE_FREQS))
    sin = np.sin(ang)[:, :, None, :] * np.array([-1.0, 1.0], np.float32)[None, None, :, None]
    cos = cos.reshape(seq_len, HEAD_DIM).astype(np.float32)
    sin = sin.reshape(seq_len, HEAD_DIM).astype(np.float32)
    return jnp.asarray(np.tile(cos, (1, LANES // HEAD_DIM))), jnp.asarray(np.tile(sin, (1, LANES // HEAD_DIM)))


def kernel(x_prompt, x_sample, cache_k, cache_v, c, c_ctx, ada_w_e, ada_b_e, norm_g_e, w_in_e, conv_w, conv_b,
           q_norm_g, k_norm_g, sink, w_out_e, ada_w_o, ada_b_o, norm_g_o, w_in_o, pool_w, pool_scale, w_out_o):
    n_dec = x_sample.shape[0]
    depth = ada_w_e.shape[0] + ada_w_o.shape[0]
    assert ada_w_e.shape[0] == 1 and ada_w_o.shape[0] == 1 and n_dec + 1 <= 8
    mod_e, mod_o = _adaln(c_ctx, c, ada_w_e[0], ada_b_e[0], ada_w_o[0], ada_b_o[0])
    rope = _rope_tables(x_sample.shape[1])

    yp, ys = x_prompt, x_sample
    new_k, new_v = [], []
    for layer in range(depth):
        i = layer // 2
        if layer % 2 == 0:
            mod = mod_e
            gains = jnp.tile(jnp.stack([q_norm_g[i], k_norm_g[i]]), (1, LANES // HEAD_DIM))
            small = (conv_w[i], conv_b[i], gains, sink[i])
            yp, k, v, w_in, w_out = _even_layer(yp, mod, norm_g_e[i], w_in_e[i], *small, w_out_e[i])
            new_k.append(k.reshape(k.shape[0], k.shape[1], N_KV_HEADS, HEAD_DIM))
            new_v.append(v.reshape(v.shape[0], v.shape[1], N_KV_HEADS, HEAD_DIM))
            ck = cache_k[:, i].reshape(n_dec, cache_k.shape[2], KV_WIDTH)
            cv = cache_v[:, i].reshape(n_dec, cache_v.shape[2], KV_WIDTH)
            ys = _even_layer(ys, mod, norm_g_e[i], w_in, *small, w_out, rope=rope, ctx=(ck, cv))
        else:
            mod = mod_o
            yp, (w_in, w_pool, w_out) = _odd_layer(yp, mod, False, norm_g_o[i], w_in_o[i], pool_w[i],
                                                   pool_scale[i], w_out_o[i])
            ys = _odd_layer(ys, mod, True, norm_g_o[i], w_in, w_pool, pool_scale[i], w_out)
    return yp, ys, jnp.stack(new_k, axis=1), jnp.stack(new_v, axis=1)
```

```python
import functools

import jax
import jax.numpy as jnp
import numpy as np
from jax import lax
from jax.experimental import pallas as pl
from jax.experimental.pallas import tpu as pltpu

F32 = jnp.float32
BF16 = jnp.bfloat16

D_MODEL = 1024
GRID_W = 64
HEAD_DIM = 64
HEAD_SHIFT = 6
ATTN_WIDTH = D_MODEL // 2
N_Q_HEADS = ATTN_WIDTH // HEAD_DIM
N_KV_HEADS = N_Q_HEADS // 4
Q_PER_KV = N_Q_HEADS // N_KV_HEADS
KV_WIDTH = N_KV_HEADS * HEAD_DIM
CONV_WIDTH = D_MODEL - ATTN_WIDTH
WINDOW = 128
BLOCK = 128
ROPE_BASE = 10000.0
ROPE_FREQS = HEAD_DIM // 4
ATTN_SCALE = HEAD_DIM ** -0.5
LOG2E = float(np.log2(np.e))
NEG = -1e30
POOL_SIZES = (2, 4, 8, 16)
POOL_GROUP = D_MODEL // len(POOL_SIZES)
EPS = 1e-6
EVEN_SIZES = (CONV_WIDTH, CONV_WIDTH, CONV_WIDTH, CONV_WIDTH, ATTN_WIDTH, KV_WIDTH, KV_WIDTH, ATTN_WIDTH)
EVEN_IN = sum(EVEN_SIZES)
EVEN_OFFS = tuple(int(s) for s in np.cumsum((0,) + EVEN_SIZES))

TM = 256
NT = 256
NORM_ROWS = 64
EVEN_ORDER_LATENT = "PF" + "F" + "PFNFPF" * 3 + "PNFPF" + "FFFF"
EVEN_ORDER_CONTEXT = "PF" + "F" + "FNPFPF" * 3 + "NPFPF" + "FFFF"
ODD_ORDER = "PNFP" * 4 + "FFFFF"
LANES = 128
GROUP_W = Q_PER_KV * HEAD_DIM
HALO = 8
ADA_TK = 256
VMEM_LIMIT = 56 * 1024 * 1024
PIPELINE_PARAMS = pltpu.CompilerParams(dimension_semantics=("arbitrary",), vmem_limit_bytes=VMEM_LIMIT)


def _silu(x):
    half = 0.5 * x
    return half + half * jnp.tanh(half)


def _rows_down(x, k):
    return pltpu.roll(x, k, axis=0)


def _rows_up(x, k):
    return pltpu.roll(x, x.shape[0] - k, axis=0)


def _rms_modulate(x, gain, shift):
    ms = jnp.mean(x * x, axis=-1, keepdims=True)
    return x * lax.rsqrt(ms + EPS) * gain + shift


def _cast_columns(src_ref, dst_ref):
    for n in range(dst_ref.shape[0]):
        dst_ref[n] = src_ref[:, n * NT:(n + 1) * NT].astype(dst_ref.dtype)


def _tiled_shape(w):
    return (w.shape[1] // NT, w.shape[0], NT)


def _opaque_zero(x):
    bits = pltpu.bitcast(x, jnp.uint32)
    return pltpu.bitcast((bits >> 16) >> 16, F32)


def _normalize_items(h_s, x_ref, g_ref, mod_ref, tokens):
    shift = mod_ref[0, :, 0:D_MODEL]
    gain = g_ref[...] * (1.0 + mod_ref[0, :, D_MODEL:2 * D_MODEL])
    for r in range(0, h_s.shape[0], NORM_ROWS):
        yield
        h = _rms_modulate(x_ref[0, r:r + NORM_ROWS], gain, shift)
        h_s[r:r + NORM_ROWS] = h.astype(BF16)
        tokens.append(_opaque_zero(h[0:8, 0:LANES])[0:1])


def _out_proj_items(get_a, wout_ref, x_ref, mod_ref, y_ref):
    for n in range(D_MODEL // NT):
        yield
        cols = slice(n * NT, (n + 1) * NT)
        mixed = jnp.dot(get_a(), wout_ref[n], preferred_element_type=F32)
        y_ref[0, :, cols] = x_ref[0, :, cols] + mod_ref[0, :, 2 * D_MODEL + n * NT:2 * D_MODEL + (n + 1) * NT] * mixed


def _weave(order, **stages):
    by_letter = {name[0].upper(): stage for name, stage in stages.items()}
    for stage in by_letter.values():
        next(stage, None)
    for letter in order:
        if letter in by_letter:
            next(by_letter[letter], None)
    for stage in by_letter.values():
        for _ in stage:
            pass


def _with_halo(u_s, slot, cols, prev_slot, next_slot, has_prev, has_next):
    u = u_s[slot, :, cols]
    zero = jnp.zeros((HALO, u.shape[1]), F32)
    if prev_slot is None:
        top = bot = zero
    else:
        rows = u_s.shape[1]
        top = jnp.where(has_prev, u_s[prev_slot, rows - HALO:rows, cols], zero)
        bot = jnp.where(has_next, u_s[next_slot, 0:HALO, cols], zero)
    return jnp.concatenate([top, u, bot], axis=0)


def _pipeline_steps(s, n_chunks, lag, ring, normalized, staged, order, normalize, project, finish):
    fin_slot = lax.rem(s + (2 * ring - 1 - lag), jnp.int32(ring))

    @pl.when((s >= 1) & (s <= n_chunks))
    def _publish_normalized():
        h_w, h_s = normalized
        h_s[...] = h_w[...]

    @pl.when((s >= 2) & (s <= n_chunks + 1))
    def _publish_projected():
        slot = lax.rem(s + (ring - 2), jnp.int32(ring))
        for stage_ref, ring_ref in staged:
            ring_ref[slot] = stage_ref[...]

    @pl.when(s == 0)
    def _first():
        _weave(order, normalize=normalize([]))

    @pl.when((s >= 1) & (s <= lag))
    def _fill():
        _weave(order, project=project(), normalize=normalize([]))

    @pl.when((s > lag) & (s <= n_chunks))
    def _steady():
        tokens = []
        _weave(order, finish=finish(fin_slot, tokens), project=project(), normalize=normalize(tokens))

    @pl.when(s > n_chunks)
    def _drain():
        _weave(order, finish=finish(fin_slot, []))


def _adaln_kernel(cctx_ref, c_ref, we_ref, be_ref, wo_ref, bo_ref, oe_ref, oo_ref, cond_s, acc_e, acc_o):
    k = pl.program_id(0)

    @pl.when(k == 0)
    def _init():
        n_dec = c_ref.shape[0]
        cond_s[...] = jnp.zeros(cond_s.shape, F32)
        cond_s[0:1] = cctx_ref[...]
        cond_s[1:1 + n_dec] = c_ref[...]
        acc_e[...] = jnp.broadcast_to(be_ref[...], acc_e.shape)
        acc_o[...] = jnp.broadcast_to(bo_ref[...], acc_o.shape)

    s = _silu(cond_s[:, pl.ds(pl.multiple_of(k * ADA_TK, ADA_TK), ADA_TK)]).astype(BF16)
    acc_e[...] += jnp.dot(s, we_ref[...].astype(BF16), preferred_element_type=F32)
    acc_o[...] += jnp.dot(s, wo_ref[...].astype(BF16), preferred_element_type=F32)

    @pl.when(k == pl.num_programs(0) - 1)
    def _write():
        for r in range(oe_ref.shape[0]):
            oe_ref[r] = acc_e[r:r + 1, :]
            oo_ref[r] = acc_o[r:r + 1, :]


def _adaln(c_ctx, c, w_e, b_e, w_o, b_o):
    n = 3 * D_MODEL
    n_dec = c.shape[0]
    wspec = pl.BlockSpec((ADA_TK, n), lambda k: (k, 0))
    vspec = pl.BlockSpec((1, n), lambda k: (0, 0))
    ospec = pl.BlockSpec((8, 1, n), lambda k: (0, 0, 0))
    return pl.pallas_call(
        _adaln_kernel,
        out_shape=(jax.ShapeDtypeStruct((8, 1, n), F32), jax.ShapeDtypeStruct((8, 1, n), F32)),
        grid=(D_MODEL // ADA_TK,),
        in_specs=[pl.BlockSpec((1, D_MODEL), lambda k: (0, 0)), pl.BlockSpec((n_dec, D_MODEL), lambda k: (0, 0)),
                  wspec, vspec, wspec, vspec],
        out_specs=(ospec, ospec),
        scratch_shapes=[pltpu.VMEM((8, D_MODEL), F32), pltpu.VMEM((8, n), F32), pltpu.VMEM((8, n), F32)],
        compiler_params=pltpu.CompilerParams(dimension_semantics=("arbitrary",), vmem_limit_bytes=VMEM_LIMIT),
        name="adaln",
    )(c_ctx.reshape(1, D_MODEL), c, w_e, b_e.reshape(1, n), w_o, b_o.reshape(1, n))


def _head_inv_rms(x, bd):
    ms = jnp.dot((x * x).astype(BF16), bd, preferred_element_type=F32) * (1.0 / HEAD_DIM)
    return lax.rsqrt(ms + EPS)


def _rope(x, cos, sin_signed, first_half):
    partner = jnp.where(first_half, pltpu.roll(x, LANES - ROPE_FREQS, axis=1), pltpu.roll(x, ROPE_FREQS, axis=1))
    return x * cos + partner * sin_signed


def _dup_kv(x):
    lane = lax.broadcasted_iota(jnp.int32, (1, LANES), 1)
    swapped = pltpu.roll(x, HEAD_DIM, axis=1)
    return jnp.where(lane < HEAD_DIM, x, swapped), jnp.where(lane < HEAD_DIM, swapped, x)


def _tiled_keys(k):
    ka, kb = _dup_kv(k)
    return jnp.concatenate([ka, ka, kb, kb], axis=1).astype(BF16)


def _values_t(v):
    va, vb = _dup_kv(v)
    return jnp.concatenate([va.T, vb.T], axis=0).astype(BF16)


def _even_kernel(*refs, nc, n_chunks, lag, ring, windowed):
    if windowed:
        (xp_ref, xa_ref, modp_ref, moda_ref, g_ref, win_ref, cw_ref, cb_ref, qkg_ref, sink_ref, wout_ref,
         cos_ref, sin_ref, ck_ref, cv_ref,
         y_ref,
         h_s, h_w, u_s, gate_s, q_s, k_s, vt_s, gb_s, u_w, gate_w, q_w, k_w, vt_w, gb_w, a_s, ck_s, cvt_s) = refs
        ko_ref = vo_ref = None
    else:
        (xp_ref, xa_ref, modp_ref, moda_ref, g_ref, win_f32, cw_ref, cb_ref, qkg_ref, sink_ref, wout_f32,
         y_ref, ko_ref, vo_ref, win_ref, wout_ref,
         h_s, h_w, u_s, gate_s, q_s, k_s, vt_s, gb_s, u_w, gate_w, q_w, k_w, vt_w, gb_w, a_s) = refs
    s = pl.program_id(0)
    i = s - 1 - lag
    ci = lax.rem(i, jnp.int32(nc))

    @pl.when(s == 0)
    def _init():
        k_s[...] = jnp.zeros(k_s.shape, BF16)
        vt_s[...] = jnp.zeros(vt_s.shape, BF16)
        if not windowed:
            _cast_columns(win_f32, win_ref)
            _cast_columns(wout_f32, wout_ref)

    if windowed:
        @pl.when((i >= 0) & (ci == 0))
        def _load_ctx():
            ck_s[...] = _tiled_keys(ck_ref[0])
            cvt_s[...] = _values_t(cv_ref[0])

    def normalize(tokens):
        yield from _normalize_items(h_w, xp_ref, g_ref, modp_ref, tokens)

    def project():
        o_bg, o_cg, o_xs, o_ga, o_q, o_k, _, o_gb = EVEN_OFFS[:8]

        def proj(start):
            return jnp.dot(h_s[...], win_ref[start // NT], preferred_element_type=F32)

        lane = lax.broadcasted_iota(jnp.int32, (1, LANES), 1)
        first_half = (lane & (2 * ROPE_FREQS - 1)) < ROPE_FREQS
        ri = lax.broadcasted_iota(jnp.int32, (NT, NT), 0) >> HEAD_SHIFT
        cj = lax.broadcasted_iota(jnp.int32, (NT, NT), 1) >> HEAD_SHIFT
        bd = (ri == cj).astype(BF16)

        def qk_norm(x, g):
            w = x.shape[1]
            inv = _head_inv_rms(x, bd[0:w, 0:w])
            pieces = []
            for j in range(w // LANES):
                lanes = slice(j * LANES, (j + 1) * LANES)
                xj = x[:, lanes] * inv[:, lanes] * g
                pieces.append(_rope(xj, cos_ref[...], sin_ref[...], first_half) if windowed else xj)
            return pieces

        def keys_values(_):
            kv = proj(o_k)
            k, = qk_norm(kv[:, 0:KV_WIDTH], qkg_ref[1:2, :])
            v = kv[:, KV_WIDTH:2 * KV_WIDTH]
            if not windowed:
                ko_ref[0] = k
                vo_ref[0] = v
            k_w[...] = _tiled_keys(k)
            vt_w[...] = _values_t(v)

        def queries(n):
            for j, qj in enumerate(qk_norm(proj(o_q + n * NT), qkg_ref[0:1, :])):
                q_w[:, n * NT + j * LANES:n * NT + (j + 1) * LANES] = (qj * (ATTN_SCALE * LOG2E)).astype(BF16)

        def conv_gate(n):
            gate_w[:, n * NT:(n + 1) * NT] = proj(o_bg + n * NT) * _silu(proj(o_ga + n * NT))

        def conv_input(n):
            u_w[:, n * NT:(n + 1) * NT] = proj(o_cg + n * NT) * proj(o_xs + n * NT)

        def attn_gate(n):
            gb_w[:, n * NT:(n + 1) * NT] = _silu(proj(o_gb + n * NT))

        items = [(conv_gate, 0), (keys_values, 0), (conv_input, 0), (queries, 0), (conv_gate, 1), (queries, 1),
                 (conv_input, 1), (attn_gate, 0), (attn_gate, 1)]
        for emit, n in items:
            yield
            emit(n)

    def finish(slot, tokens):
        if windowed:
            prev_slot = lax.rem(slot + (ring - 1), jnp.int32(ring))
            next_slot = lax.rem(slot + 1, jnp.int32(ring))
            has_prev = ci > 0
            has_next = ci < nc - 1
        else:
            prev_slot = next_slot = has_prev = has_next = None

        yield
        ext = _with_halo(u_s, slot, slice(None), prev_slot, next_slot, has_prev, has_next)
        conv = (_rows_down(ext, 1)[HALO:HALO + TM] * cw_ref[0:1, :]
                + ext[HALO:HALO + TM] * cw_ref[1:2, :]
                + _rows_up(ext, 1)[HALO:HALO + TM] * cw_ref[2:3, :]
                + cb_ref[...])
        a_s[:, 0:CONV_WIDTH] = (gate_s[slot] * conv).astype(BF16)

        lane_head = lax.broadcasted_iota(jnp.int32, (1, GROUP_W), 1) >> HEAD_SHIFT
        cols4 = Q_PER_KV * BLOCK
        key_i =lax.broadcasted_iota(jnp.int32, (BLOCK, cols4), 0)
        qry_i = lax.broadcasted_iota(jnp.int32, (BLOCK, cols4), 1) & (BLOCK - 1)
        upper = lax.broadcasted_iota(jnp.int32, (LANES, LANES), 0) < HEAD_DIM
        halves = (slice(0, BLOCK), slice(BLOCK, TM))
        @functools.lru_cache(maxsize=None)
        def block_masks(qb):
            if qb == 0:
                blocks = ((prev_slot, halves[1]), (slot, halves[0]), (slot, halves[1]))
                prev_in, next_in = has_prev, True
            else:
                blocks = ((slot, halves[0]), (slot, halves[1]), (next_slot, halves[0]))
                prev_in, next_in = True, has_next
            prev_ok = (key_i - qry_i) >= jnp.where(prev_in, 0, BLOCK)
            next_ok = (qry_i - key_i) >= jnp.where(next_in, 0, BLOCK)
            return blocks, prev_ok, next_ok

        def scores(qb, kv):
            cols = slice(kv * GROUP_W, (kv + 1) * GROUP_W)
            vrows = slice(kv * LANES, (kv + 1) * LANES)
            qblk = q_s[slot, halves[qb], cols]
            q4 = jnp.concatenate(
                [jnp.where(lane_head == g, qblk, jnp.zeros_like(qblk)) for g in range(Q_PER_KV)], axis=0)

            def scores_t(kk):
                return lax.dot_general(kk, q4, (((1,), (1,)), ((), ())), preferred_element_type=F32)

            if not windowed:
                return [([scores_t(k_s[slot, :, cols])], vt_s[slot, vrows, :])]
            blocks, prev_ok, next_ok = block_masks(qb)
            s_loc = scores_t(jnp.concatenate([k_s[sl, r, cols] for sl, r in blocks], axis=0))
            vt_loc = jnp.concatenate([vt_s[sl, vrows, r] for sl, r in blocks], axis=1)
            s_ctx = scores_t(ck_s[:, cols])
            return [([jnp.where(prev_ok, s_loc[0:BLOCK], NEG), s_loc[BLOCK:2 * BLOCK],
                      jnp.where(next_ok, s_loc[2 * BLOCK:3 * BLOCK], NEG)], vt_loc),
                    ([s_ctx], cvt_s[vrows, :])]

        def softmax(kv, groups):
            snk = jnp.concatenate(
                [jnp.full((1, BLOCK), sink_ref[kv * Q_PER_KV + g] * LOG2E, F32) for g in range(Q_PER_KV)], axis=1)
            if tokens:
                token = tokens.pop(0)
                while tokens:
                    token = token + tokens.pop(0)
                snk = snk + jnp.concatenate([token] * Q_PER_KV, axis=1)
            mx = snk
            for ss, _ in groups:
                for sc in ss:
                    mx = jnp.maximum(mx, jnp.max(sc, axis=0, keepdims=True))
            den = jnp.exp2(snk - mx)
            probs = []
            for ss, vt in groups:
                es = []
                for sc in ss:
                    e = jnp.exp2(sc - mx)
                    den = den + jnp.sum(e, axis=0, keepdims=True)
                    es.append(e.astype(BF16))
                probs.append((es[0] if len(es) == 1 else jnp.concatenate(es, axis=0), vt))
            return probs, den

        def values(qb, kv, probs, den):
            o_t = jnp.zeros((LANES, cols4), F32)
            for e_all, vt in probs:
                o_t = o_t + jnp.dot(vt, e_all, preferred_element_type=F32)
            o_t = o_t * (1.0 / den)
            o01 = jnp.where(upper, o_t[:, 0:BLOCK], o_t[:, BLOCK:2 * BLOCK]).T
            o23 = jnp.where(upper, o_t[:, 2 * BLOCK:3 * BLOCK], o_t[:, 3 * BLOCK:4 * BLOCK]).T
            a_s[halves[qb], CONV_WIDTH + kv * GROUP_W:CONV_WIDTH + (kv + 1) * GROUP_W] = (
                jnp.concatenate([o01, o23], axis=1) * gb_s[slot, halves[qb], kv * GROUP_W:(kv + 1) * GROUP_W]
            ).astype(BF16)

        pairs = [(qb, kv) for qb in range(TM // BLOCK) for kv in range(N_KV_HEADS)]
        yield
        ahead = scores(*pairs[0])
        for t, (qb, kv) in enumerate(pairs):
            groups = ahead
            if t + 1 < len(pairs):
                yield
                ahead = scores(*pairs[t + 1])
            yield
            probs, den = softmax(kv, groups)
            yield
            values(qb, kv, probs, den)

        yield from _out_proj_items(lambda: a_s[...], wout_ref, xa_ref, moda_ref, y_ref)

    staged = ((u_w, u_s), (gate_w, gate_s), (q_w, q_s), (k_w, k_s), (vt_w, vt_s), (gb_w, gb_s))
    order = EVEN_ORDER_LATENT if windowed else EVEN_ORDER_CONTEXT
    _pipeline_steps(s, n_chunks, lag, ring, (h_w, h_s), staged, order, normalize, project, finish)


def _chunk_specs(n_chunks, nc, lag, per_seq_mod, tm=TM):
    last = n_chunks - 1
    norm_chunk = lambda s: jnp.minimum(s, last)
    proj_chunk = lambda s: jnp.clip(s - 1, 0, last)
    fin_chunk = lambda s: jnp.clip(s - 1 - lag, 0, last)
    seq_of = (lambda j: 1 + j // nc) if per_seq_mod else (lambda j: 0)
    return dict(
        x_norm=pl.BlockSpec((1, tm, D_MODEL), lambda s: (norm_chunk(s), 0, 0)),
        x_fin=pl.BlockSpec((1, tm, D_MODEL), lambda s: (fin_chunk(s), 0, 0)),
        mod_norm=pl.BlockSpec((1, 1, 3 * D_MODEL), lambda s: (seq_of(norm_chunk(s)), 0, 0)),
        mod_fin=pl.BlockSpec((1, 1, 3 * D_MODEL), lambda s: (seq_of(fin_chunk(s)), 0, 0)),
        proj_chunk=proj_chunk, fin_chunk=fin_chunk)


def _const_spec(shape):
    return pl.BlockSpec(shape, lambda c: (0,) * len(shape))


def _even_layer(x, mod, norm_g, w_in, conv_w, conv_b, qk_gains, sink, w_out, rope=None, ctx=None):
    n_seq, seq_len, _ = x.shape
    windowed = rope is not None
    nc = seq_len // TM
    n_chunks = n_seq * nc
    lag, ring = (2, 4) if windowed else (1, 2)
    assert windowed or nc == 1
    sp = _chunk_specs(n_chunks, nc, lag, windowed)
    xc = x.reshape(n_chunks, TM, D_MODEL)
    in_specs = [
        sp["x_norm"], sp["x_fin"], sp["mod_norm"], sp["mod_fin"],
        _const_spec((1, D_MODEL)),
        _const_spec(w_in.shape),
        _const_spec((3, CONV_WIDTH)),
        _const_spec((1, CONV_WIDTH)),
        _const_spec((2, LANES)),
        pl.BlockSpec(memory_space=pltpu.SMEM),
        _const_spec(w_out.shape),
    ]
    args = [xc, xc, mod, mod, norm_g.reshape(1, D_MODEL), w_in, conv_w, conv_b.reshape(1, CONV_WIDTH),
            qk_gains, sink, w_out]
    per_chunk = [
        ((TM, CONV_WIDTH), F32),
        ((TM, CONV_WIDTH), F32),
        ((TM, ATTN_WIDTH), BF16),
        ((TM, 2 * GROUP_W), BF16),
        ((2 * LANES, TM), BF16),
        ((TM, ATTN_WIDTH), F32),
    ]
    scratch = (
        [pltpu.VMEM((TM, D_MODEL), BF16)] * 2
        + [pltpu.VMEM((ring,) + shape, dt) for shape, dt in per_chunk]
        + [pltpu.VMEM(shape, dt) for shape, dt in per_chunk]
        + [pltpu.VMEM((TM, D_MODEL), BF16)])
    y_shape = jax.ShapeDtypeStruct(xc.shape, F32)
    y_spec = pl.BlockSpec((1, TM, D_MODEL), lambda c: (sp["fin_chunk"](c), 0, 0))
    if windowed:
        cos, sin = rope
        ck, cv = ctx
        rope_spec = pl.BlockSpec((TM, LANES), lambda c: (sp["proj_chunk"](c) % nc, 0))
        ctx_spec = pl.BlockSpec((1,) + ck.shape[1:], lambda c: (sp["fin_chunk"](c) // nc, 0, 0))
        in_specs += [rope_spec, rope_spec, ctx_spec, ctx_spec]
        args += [cos, sin, ck, cv]
        scratch += [pltpu.VMEM((ck.shape[1], 2 * GROUP_W), BF16), pltpu.VMEM((2 * LANES, cv.shape[1]), BF16)]
        out_shape, out_specs = y_shape, y_spec
    else:
        kv_shape = jax.ShapeDtypeStruct((n_chunks, TM, KV_WIDTH), F32)
        kv_spec = pl.BlockSpec((1, TM, KV_WIDTH), lambda c: (sp["proj_chunk"](c), 0, 0))
        out_shape = (y_shape, kv_shape, kv_shape,
                     jax.ShapeDtypeStruct(_tiled_shape(w_in), BF16), jax.ShapeDtypeStruct(_tiled_shape(w_out), BF16))
        out_specs = (y_spec, kv_spec, kv_spec, _const_spec(_tiled_shape(w_in)), _const_spec(_tiled_shape(w_out)))
    out = pl.pallas_call(
        functools.partial(_even_kernel, nc=nc, n_chunks=n_chunks, lag=lag, ring=ring, windowed=windowed),
        out_shape=out_shape,
        grid=(n_chunks + lag + 1,),
        in_specs=in_specs,
        out_specs=out_specs,
        scratch_shapes=scratch,
        compiler_params=PIPELINE_PARAMS,
        name="even_latent" if windowed else "even_context",
    )(*args)
    if windowed:
        return out.reshape(x.shape)
    y, k, v, w_in_bf16, w_out_bf16 = out
    return (y.reshape(x.shape), k.reshape(n_seq, seq_len, KV_WIDTH), v.reshape(n_seq, seq_len, KV_WIDTH),
            w_in_bf16, w_out_bf16)


def _odd_kernel(*refs, nc, n_chunks, lag, ring, cast_weights):
    if cast_weights:
        (xp_ref, xa_ref, modp_ref, moda_ref, g_ref, win_f32, pw_f32, ps_ref, wout_f32,
         y_ref, win_ref, pw_ref, wout_ref, h_s, h_w, u_s, sg_s, u_w, sg_w) = refs
    else:
        (xp_ref, xa_ref, modp_ref, moda_ref, g_ref, win_ref, pw_ref, ps_ref, wout_ref,
         y_ref, h_s, h_w, u_s, sg_s, u_w, sg_w) = refs
    tm = h_s.shape[0]
    s = pl.program_id(0)
    i = s - 1 - lag
    ci = lax.rem(i, jnp.int32(nc))

    if cast_weights:
        @pl.when(s == 0)
        def _cast():
            _cast_columns(win_f32, win_ref)
            _cast_columns(wout_f32, wout_ref)
            for gi in range(pw_ref.shape[0]):
                pw_ref[gi] = pw_f32[gi].astype(BF16)

    def normalize(tokens):
        yield from _normalize_items(h_w, xp_ref, g_ref, modp_ref, tokens)

    def project():
        for n in range(D_MODEL // NT):
            yield
            cols = slice(n * NT, (n + 1) * NT)
            u = jnp.dot(h_s[...], win_ref[n], preferred_element_type=F32)
            u_w[:, cols] = u
        for n in range(D_MODEL // NT):
            yield
            cols = slice(n * NT, (n + 1) * NT)
            sg = _silu(jnp.dot(h_s[...], win_ref[D_MODEL // NT + n], preferred_element_type=F32))
            sg_w[:, cols] = sg

    def finish(slot, tokens):
        if nc > 1:
            prev_slot = lax.rem(slot + (ring - 1), jnp.int32(ring))
            next_slot = lax.rem(slot + 1, jnp.int32(ring))
            has_prev = ci > 0
            has_next = ci < nc - 1
        else:
            prev_slot = next_slot = has_prev = has_next = None
        t = ci * tm + lax.broadcasted_iota(jnp.int32, (tm, 1), 0)
        outs = []
        for gi, w in enumerate(POOL_SIZES):
            yield
            cols = slice(gi * POOL_GROUP, (gi + 1) * POOL_GROUP)
            ext = _with_halo(u_s, slot, cols, prev_slot, next_slot, has_prev, has_next)
            acc = ext
            span = 1
            while span < w:
                acc = acc + _rows_down(acc, span)
                span *= 2
            if w > 2:
                acc = _rows_up(acc, w // 2 - 1)
            cnt = (jnp.minimum(t + w // 2, nc * tm) - jnp.maximum(t - w // 2, 0)).astype(F32)
            while tokens:
                cnt = cnt + tokens.pop(0)[:, 0:1]
            pooled = acc[HALO:HALO + tm] / cnt - ext[HALO:HALO + tm]
            outs.append(jnp.dot(pooled.astype(BF16), pw_ref[gi], preferred_element_type=F32))
        yield
        y = (jnp.concatenate(outs, axis=1) * ps_ref[...] * sg_s[slot]).astype(BF16)
        yield from _out_proj_items(lambda: y, wout_ref, xa_ref, moda_ref, y_ref)

    _pipeline_steps(s, n_chunks, lag, ring, (h_w, h_s), ((u_w, u_s), (sg_w, sg_s)), ODD_ORDER,
                    normalize, project, finish)


def _odd_layer(x, mod, per_seq_mod, norm_g, w_in, pool_w, pool_scale, w_out):
    n_seq, seq_len, _ = x.shape
    tm = TM
    nc = seq_len // tm
    n_chunks = n_seq * nc
    lag, ring = (2, 4) if nc > 1 else (1, 2)
    sp = _chunk_specs(n_chunks, nc, lag, per_seq_mod, tm)
    xc = x.reshape(n_chunks, tm, D_MODEL)
    in_specs = [
        sp["x_norm"], sp["x_fin"], sp["mod_norm"], sp["mod_fin"],
        _const_spec((1, D_MODEL)),
        _const_spec(w_in.shape),
        _const_spec(pool_w.shape),
        _const_spec((1, D_MODEL)),
        _const_spec(w_out.shape),
    ]
    cast_weights = w_in.dtype != BF16
    out_shape = jax.ShapeDtypeStruct(xc.shape, F32)
    out_specs = pl.BlockSpec((1, tm, D_MODEL), lambda c: (sp["fin_chunk"](c), 0, 0))
    if cast_weights:
        shapes = (_tiled_shape(w_in), pool_w.shape, _tiled_shape(w_out))
        out_shape = (out_shape,) + tuple(jax.ShapeDtypeStruct(shape, BF16) for shape in shapes)
        out_specs = (out_specs,) + tuple(_const_spec(shape) for shape in shapes)
    out = pl.pallas_call(
        functools.partial(_odd_kernel, nc=nc, n_chunks=n_chunks, lag=lag, ring=ring, cast_weights=cast_weights),
        out_shape=out_shape,
        grid=(n_chunks + lag + 1,),
        in_specs=in_specs,
        out_specs=out_specs,
        scratch_shapes=[
            pltpu.VMEM((tm, D_MODEL), BF16),
            pltpu.VMEM((tm, D_MODEL), BF16),
            pltpu.VMEM((ring, tm, D_MODEL), F32),
            pltpu.VMEM((ring, tm, D_MODEL), F32),
            pltpu.VMEM((tm, D_MODEL), F32),
            pltpu.VMEM((tm, D_MODEL), F32),
        ],
        compiler_params=PIPELINE_PARAMS,
        name="odd_latent" if per_seq_mod else "odd_context",
    )(xc, xc, mod, mod, norm_g.reshape(1, D_MODEL), w_in, pool_w, pool_scale.reshape(1, D_MODEL), w_out)
    if cast_weights:
        y, *bf16_weights = out
        return y.reshape(x.shape), bf16_weights
    return out.reshape(x.shape)


def _rope_tables(seq_len):
    n_rows = seq_len // GRID_W
    row = np.repeat(np.arange(n_rows), GRID_W).astype(np.float32)
    col = np.tile(np.arange(GRID_W), n_rows).astype(np.float32)
    inv = (np.float32(ROPE_BASE) ** (-np.arange(ROPE_FREQS, dtype=np.float32) / np.float32(ROPE_FREQS))).astype(np.float32)
    ang = np.stack([row[:, None] * inv, col[:, None] * inv], axis=1)
    cos = np.broadcast_to(np.cos(ang)[:, :, None, :], (seq_len, 2, 2, ROPE_FREQS))
    sin = np.sin(ang)[:, :, None, :] * np.array([-1.0, 1.0], np.float32)[None, None, :, None]
    cos = cos.reshape(seq_len, HEAD_DIM).astype(np.float32)
    sin = sin.reshape(seq_len, HEAD_DIM).astype(np.float32)
    return jnp.asarray(np.tile(cos, (1, LANES // HEAD_DIM))), jnp.asarray(np.tile(sin, (1, LANES // HEAD_DIM)))


def kernel(x_prompt, x_sample, cache_k, cache_v, c, c_ctx, ada_w_e, ada_b_e, norm_g_e, w_in_e, conv_w, conv_b,
           q_norm_g, k_norm_g, sink, w_out_e, ada_w_o, ada_b_o, norm_g_o, w_in_o, pool_w, pool_scale, w_out_o):
    n_dec = x_sample.shape[0]
    depth = ada_w_e.shape[0] + ada_w_o.shape[0]
    assert ada_w_e.shape[0] == 1 and ada_w_o.shape[0] == 1 and n_dec + 1 <= 8
    mod_e, mod_o = _adaln(c_ctx, c, ada_w_e[0], ada_b_e[0], ada_w_o[0], ada_b_o[0])
    rope = _rope_tables(x_sample.shape[1])

    yp, ys = x_prompt, x_sample
    new_k, new_v = [], []
    for layer in range(depth):
        i = layer // 2
        if layer % 2 == 0:
            mod = mod_e
            gains = jnp.tile(jnp.stack([q_norm_g[i], k_norm_g[i]]), (1, LANES // HEAD_DIM))
            small = (conv_w[i], conv_b[i], gains, sink[i])
            yp, k, v, w_in, w_out = _even_layer(yp, mod, norm_g_e[i], w_in_e[i], *small, w_out_e[i])
            new_k.append(k.reshape(k.shape[0], k.shape[1], N_KV_HEADS, HEAD_DIM))
            new_v.append(v.reshape(v.shape[0], v.shape[1], N_KV_HEADS, HEAD_DIM))
            ck = cache_k[:, i].reshape(n_dec, cache_k.shape[2], KV_WIDTH)
            cv = cache_v[:, i].reshape(n_dec, cache_v.shape[2], KV_WIDTH)
            ys = _even_layer(ys, mod, norm_g_e[i], w_in, *small, w_out, rope=rope, ctx=(ck, cv))
        else:
            mod = mod_o
            yp, (w_in, w_pool, w_out) = _odd_layer(yp, mod, False, norm_g_o[i], w_in_o[i], pool_w[i],
                                                   pool_scale[i], w_out_o[i])
            ys = _odd_layer(ys, mod, True, norm_g_o[i], w_in, w_pool, pool_scale[i], w_out)
    return yp, ys, jnp.stack(new_k, axis=1), jnp.stack(new_v, axis=1)
```

```python
import functools

import jax
import jax.numpy as jnp
import numpy as np
from jax import lax
from jax.experimental import pallas as pl
from jax.experimental.pallas import tpu as pltpu

F32 = jnp.float32
BF16 = jnp.bfloat16

D_MODEL = 1024
GRID_W = 64
HEAD_DIM = 64
HEAD_SHIFT = 6
ATTN_WIDTH = D_MODEL // 2
N_Q_HEADS = ATTN_WIDTH // HEAD_DIM
N_KV_HEADS = N_Q_HEADS // 4
Q_PER_KV = N_Q_HEADS // N_KV_HEADS
KV_WIDTH = N_KV_HEADS * HEAD_DIM
CONV_WIDTH = D_MODEL - ATTN_WIDTH
WINDOW = 128
BLOCK = 128
ROPE_BASE = 10000.0
ROPE_FREQS = HEAD_DIM // 4
ATTN_SCALE = HEAD_DIM ** -0.5
LOG2E = float(np.log2(np.e))
NEG = -1e30
POOL_SIZES = (2, 4, 8, 16)
POOL_GROUP = D_MODEL // len(POOL_SIZES)
EPS = 1e-6
EVEN_SIZES = (CONV_WIDTH, CONV_WIDTH, CONV_WIDTH, CONV_WIDTH, ATTN_WIDTH, KV_WIDTH, KV_WIDTH, ATTN_WIDTH)
EVEN_IN = sum(EVEN_SIZES)
EVEN_OFFS = tuple(int(s) for s in np.cumsum((0,) + EVEN_SIZES))

TM = 256
NT = 256
NORM_ROWS = 64
EVEN_ORDER_LATENT = "PF" + "F" + "PFNFPF" * 3 + "PNFPF" + "FFFF"
EVEN_ORDER_CONTEXT = "PF" + "F" + "FNPFPF" * 3 + "NPFPF" + "FFFF"
ODD_ORDER = "PNFP" * 4 + "FFFFF"
LANES = 128
GROUP_W = Q_PER_KV * HEAD_DIM
HALO = 8
ADA_TK = 256
VMEM_LIMIT = 56 * 1024 * 1024
PIPELINE_PARAMS = pltpu.CompilerParams(dimension_semantics=("arbitrary",), vmem_limit_bytes=VMEM_LIMIT)


def _silu(x):
    half = 0.5 * x
    return half + half * jnp.tanh(half)


def _rows_down(x, k):
    return pltpu.roll(x, k, axis=0)


def _rows_up(x, k):
    return pltpu.roll(x, x.shape[0] - k, axis=0)


def _rms_modulate(x, gain, shift):
    ms = jnp.mean(x * x, axis=-1, keepdims=True)
    return x * lax.rsqrt(ms + EPS) * gain + shift


def _cast_columns(src_ref, dst_ref):
    for n in range(dst_ref.shape[0]):
        dst_ref[n] = src_ref[:, n * NT:(n + 1) * NT].astype(dst_ref.dtype)


def _tiled_shape(w):
    return (w.shape[1] // NT, w.shape[0], NT)


def _opaque_zero(x):
    bits = pltpu.bitcast(x, jnp.uint32)
    return pltpu.bitcast((bits >> 16) >> 16, F32)


def _normalize_items(h_s, x_ref, g_ref, mod_ref, tokens):
    shift = mod_ref[0, :, 0:D_MODEL]
    gain = g_ref[...] * (1.0 + mod_ref[0, :, D_MODEL:2 * D_MODEL])
    for r in range(0, h_s.shape[0], NORM_ROWS):
        yield
        h = _rms_modulate(x_ref[0, r:r + NORM_ROWS], gain, shift)
        h_s[r:r + NORM_ROWS] = h.astype(BF16)
        tokens.append(_opaque_zero(h[0:8, 0:LANES])[0:1])


def _out_proj_items(get_a, wout_ref, x_ref, mod_ref, y_ref):
    for n in range(D_MODEL // NT):
        yield
        cols = slice(n * NT, (n + 1) * NT)
        mixed = jnp.dot(get_a(), wout_ref[n], preferred_element_type=F32)
        y_ref[0, :, cols] = x_ref[0, :, cols] + mod_ref[0, :, 2 * D_MODEL + n * NT:2 * D_MODEL + (n + 1) * NT] * mixed


def _weave(order, **stages):
    by_letter = {name[0].upper(): stage for name, stage in stages.items()}
    for stage in by_letter.values():
        next(stage, None)
    for letter in order:
        if letter in by_letter:
            next(by_letter[letter], None)
    for stage in by_letter.values():
        for _ in stage:
            pass


def _with_halo(u_s, slot, cols, prev_slot, next_slot, has_prev, has_next):
    u = u_s[slot, :, cols]
    zero = jnp.zeros((HALO, u.shape[1]), F32)
    if prev_slot is None:
        top = bot = zero
    else:
        rows = u_s.shape[1]
        top = jnp.where(has_prev, u_s[prev_slot, rows - HALO:rows, cols], zero)
        bot = jnp.where(has_next, u_s[next_slot, 0:HALO, cols], zero)
    return jnp.concatenate([top, u, bot], axis=0)


def _pipeline_steps(s, n_chunks, lag, ring, normalized, staged, order, normalize, project, finish):
    fin_slot = lax.rem(s + (2 * ring - 1 - lag), jnp.int32(ring))

    @pl.when((s >= 1) & (s <= n_chunks))
    def _publish_normalized():
        h_w, h_s = normalized
        h_s[...] = h_w[...]

    @pl.when((s >= 2) & (s <= n_chunks + 1))
    def _publish_projected():
        slot = lax.rem(s + (ring - 2), jnp.int32(ring))
        for stage_ref, ring_ref in staged:
            ring_ref[slot] = stage_ref[...]

    @pl.when(s == 0)
    def _first():
        _weave(order, normalize=normalize([]))

    @pl.when((s >= 1) & (s <= lag))
    def _fill():
        _weave(order, project=project(), normalize=normalize([]))

    @pl.when((s > lag) & (s <= n_chunks))
    def _steady():
        tokens = []
        _weave(order, finish=finish(fin_slot, tokens), project=project(), normalize=normalize(tokens))

    @pl.when(s > n_chunks)
    def _drain():
        _weave(order, finish=finish(fin_slot, []))


def _adaln_kernel(cctx_ref, c_ref, we_ref, be_ref, wo_ref, bo_ref, oe_ref, oo_ref, cond_s, acc_e, acc_o):
    k = pl.program_id(0)

    @pl.when(k == 0)
    def _init():
        n_dec = c_ref.shape[0]
        cond_s[...] = jnp.zeros(cond_s.shape, F32)
        cond_s[0:1] = cctx_ref[...]
        cond_s[1:1 + n_dec] = c_ref[...]
        acc_e[...] = jnp.broadcast_to(be_ref[...], acc_e.shape)
        acc_o[...] = jnp.broadcast_to(bo_ref[...], acc_o.shape)

    s = _silu(cond_s[:, pl.ds(pl.multiple_of(k * ADA_TK, ADA_TK), ADA_TK)]).astype(BF16)
    acc_e[...] += jnp.dot(s, we_ref[...].astype(BF16), preferred_element_type=F32)
    acc_o[...] += jnp.dot(s, wo_ref[...].astype(BF16), preferred_element_type=F32)

    @pl.when(k == pl.num_programs(0) - 1)
    def _write():
        for r in range(oe_ref.shape[0]):
            oe_ref[r] = acc_e[r:r + 1, :]
            oo_ref[r] = acc_o[r:r + 1, :]


def _adaln(c_ctx, c, w_e, b_e, w_o, b_o):
    n = 3 * D_MODEL
    n_dec = c.shape[0]
    wspec = pl.BlockSpec((ADA_TK, n), lambda k: (k, 0))
    vspec = pl.BlockSpec((1, n), lambda k: (0, 0))
    ospec = pl.BlockSpec((8, 1, n), lambda k: (0, 0, 0))
    return pl.pallas_call(
        _adaln_kernel,
        out_shape=(jax.ShapeDtypeStruct((8, 1, n), F32), jax.ShapeDtypeStruct((8, 1, n), F32)),
        grid=(D_MODEL // ADA_TK,),
        in_specs=[pl.BlockSpec((1, D_MODEL), lambda k: (0, 0)), pl.BlockSpec((n_dec, D_MODEL), lambda k: (0, 0)),
                  wspec, vspec, wspec, vspec],
        out_specs=(ospec, ospec),
        scratch_shapes=[pltpu.VMEM((8, D_MODEL), F32), pltpu.VMEM((8, n), F32), pltpu.VMEM((8, n), F32)],
        compiler_params=pltpu.CompilerParams(dimension_semantics=("arbitrary",), vmem_limit_bytes=VMEM_LIMIT),
        name="adaln",
    )(c_ctx.reshape(1, D_MODEL), c, w_e, b_e.reshape(1, n), w_o, b_o.reshape(1, n))


def _head_inv_rms(x, bd):
    ms = jnp.dot((x * x).astype(BF16), bd, preferred_element_type=F32) * (1.0 / HEAD_DIM)
    return lax.rsqrt(ms + EPS)


def _rope(x, cos, sin_signed, first_half):
    partner = jnp.where(first_half, pltpu.roll(x, LANES - ROPE_FREQS, axis=1), pltpu.roll(x, ROPE_FREQS, axis=1))
    return x * cos + partner * sin_signed


def _dup_kv(x):
    lane = lax.broadcasted_iota(jnp.int32, (1, LANES), 1)
    swapped = pltpu.roll(x, HEAD_DIM, axis=1)
    return jnp.where(lane < HEAD_DIM, x, swapped), jnp.where(lane < HEAD_DIM, swapped, x)


def _tiled_keys(k):
    ka, kb = _dup_kv(k)
    return jnp.concatenate([ka, ka, kb, kb], axis=1).astype(BF16)


def _values_t(v):
    va, vb = _dup_kv(v)
    return jnp.concatenate([va.T, vb.T], axis=0).astype(BF16)


def _even_kernel(*refs, nc, n_chunks, lag, ring, windowed):
    if windowed:
        (xp_ref, xa_ref, modp_ref, moda_ref, g_ref, win_ref, cw_ref, cb_ref, qkg_ref, sink_ref, wout_ref,
         cos_ref, sin_ref, ck_ref, cv_ref,
         y_ref,
         h_s, h_w, u_s, gate_s, q_s, k_s, vt_s, gb_s, u_w, gate_w, q_w, k_w, vt_w, gb_w, a_s, ck_s, cvt_s) = refs
        ko_ref = vo_ref = None
    else:
        (xp_ref, xa_ref, modp_ref, moda_ref, g_ref, win_f32, cw_ref, cb_ref, qkg_ref, sink_ref, wout_f32,
         y_ref, ko_ref, vo_ref, win_ref, wout_ref,
         h_s, h_w, u_s, gate_s, q_s, k_s, vt_s, gb_s, u_w, gate_w, q_w, k_w, vt_w, gb_w, a_s) = refs
    s = pl.program_id(0)
    i = s - 1 - lag
    ci = lax.rem(i, jnp.int32(nc))

    @pl.when(s == 0)
    def _init():
        k_s[...] = jnp.zeros(k_s.shape, BF16)
        vt_s[...] = jnp.zeros(vt_s.shape, BF16)
        if not windowed:
            _cast_columns(win_f32, win_ref)
            _cast_columns(wout_f32, wout_ref)

    if windowed:
        @pl.when((i >= 0) & (ci == 0))
        def _load_ctx():
            ck_s[...] = _tiled_keys(ck_ref[0])
            cvt_s[...] = _values_t(cv_ref[0])

    def normalize(tokens):
        yield from _normalize_items(h_w, xp_ref, g_ref, modp_ref, tokens)

    def project():
        o_bg, o_cg, o_xs, o_ga, o_q, o_k, _, o_gb = EVEN_OFFS[:8]

        def proj(start):
            return jnp.dot(h_s[...], win_ref[start // NT], preferred_element_type=F32)

        lane = lax.broadcasted_iota(jnp.int32, (1, LANES), 1)
        first_half = (lane & (2 * ROPE_FREQS - 1)) < ROPE_FREQS
        ri = lax.broadcasted_iota(jnp.int32, (NT, NT), 0) >> HEAD_SHIFT
        cj = lax.broadcasted_iota(jnp.int32, (NT, NT), 1) >> HEAD_SHIFT
        bd = (ri == cj).astype(BF16)

        def qk_norm(x, g):
            w = x.shape[1]
            inv = _head_inv_rms(x, bd[0:w, 0:w])
            pieces = []
            for j in range(w // LANES):
                lanes = slice(j * LANES, (j + 1) * LANES)
                xj = x[:, lanes] * inv[:, lanes] * g
                pieces.append(_rope(xj, cos_ref[...], sin_ref[...], first_half) if windowed else xj)
            return pieces

        def keys_values(_):
            kv = proj(o_k)
            k, = qk_norm(kv[:, 0:KV_WIDTH], qkg_ref[1:2, :])
            v = kv[:, KV_WIDTH:2 * KV_WIDTH]
            if not windowed:
                ko_ref[0] = k.reshape(TM, N_KV_HEADS, HEAD_DIM)
                vo_ref[0] = v.reshape(TM, N_KV_HEADS, HEAD_DIM)
            k_w[...] = _tiled_keys(k)
            vt_w[...] = _values_t(v)

        def queries(n):
            for j, qj in enumerate(qk_norm(proj(o_q + n * NT), qkg_ref[0:1, :])):
                q_w[:, n * NT + j * LANES:n * NT + (j + 1) * LANES] = (qj * (ATTN_SCALE * LOG2E)).astype(BF16)

        def conv_gate(n):
            gate_w[:, n * NT:(n + 1) * NT] = proj(o_bg + n * NT) * _silu(proj(o_ga + n * NT))

        def conv_input(n):
            u_w[:, n * NT:(n + 1) * NT] = proj(o_cg + n * NT) * proj(o_xs + n * NT)

        def attn_gate(n):
            gb_w[:, n * NT:(n + 1) * NT] = _silu(proj(o_gb + n * NT))

        items = [(conv_gate, 0), (keys_values, 0), (conv_input, 0), (queries, 0), (conv_gate, 1), (queries, 1),
                 (conv_input, 1), (attn_gate, 0), (attn_gate, 1)]
        for emit, n in items:
            yield
            emit(n)

    def finish(slot, tokens):
        if windowed:
            prev_slot = lax.rem(slot + (ring - 1), jnp.int32(ring))
            next_slot = lax.rem(slot + 1, jnp.int32(ring))
            has_prev = ci > 0
            has_next = ci < nc - 1
        else:
            prev_slot = next_slot = has_prev = has_next = None

        yield
        ext = _with_halo(u_s, slot, slice(None), prev_slot, next_slot, has_prev, has_next)
        conv = (_rows_down(ext, 1)[HALO:HALO + TM] * cw_ref[0:1, :]
                + ext[HALO:HALO + TM] * cw_ref[1:2, :]
                + _rows_up(ext, 1)[HALO:HALO + TM] * cw_ref[2:3, :]
                + cb_ref[...])
        a_s[:, 0:CONV_WIDTH] = (gate_s[slot] * conv).astype(BF16)

        lane_head = lax.broadcasted_iota(jnp.int32, (1, GROUP_W), 1) >> HEAD_SHIFT
        cols4 = Q_PER_KV * BLOCK
        key_i =lax.broadcasted_iota(jnp.int32, (BLOCK, cols4), 0)
        qry_i = lax.broadcasted_iota(jnp.int32, (BLOCK, cols4), 1) & (BLOCK - 1)
        upper = lax.broadcasted_iota(jnp.int32, (LANES, LANES), 0) < HEAD_DIM
        halves = (slice(0, BLOCK), slice(BLOCK, TM))
        @functools.lru_cache(maxsize=None)
        def block_masks(qb):
            if qb == 0:
                blocks = ((prev_slot, halves[1]), (slot, halves[0]), (slot, halves[1]))
                prev_in, next_in = has_prev, True
            else:
                blocks = ((slot, halves[0]), (slot, halves[1]), (next_slot, halves[0]))
                prev_in, next_in = True, has_next
            prev_ok = (key_i - qry_i) >= jnp.where(prev_in, 0, BLOCK)
            next_ok = (qry_i - key_i) >= jnp.where(next_in, 0, BLOCK)
            return blocks, prev_ok, next_ok

        def scores(qb, kv):
            cols = slice(kv * GROUP_W, (kv + 1) * GROUP_W)
            vrows = slice(kv * LANES, (kv + 1) * LANES)
            qblk = q_s[slot, halves[qb], cols]
            q4 = jnp.concatenate(
                [jnp.where(lane_head == g, qblk, jnp.zeros_like(qblk)) for g in range(Q_PER_KV)], axis=0)

            def scores_t(kk):
                return lax.dot_general(kk, q4, (((1,), (1,)), ((), ())), preferred_element_type=F32)

            if not windowed:
                return [([scores_t(k_s[slot, :, cols])], vt_s[slot, vrows, :])]
            blocks, prev_ok, next_ok = block_masks(qb)
            s_loc = scores_t(jnp.concatenate([k_s[sl, r, cols] for sl, r in blocks], axis=0))
            vt_loc = jnp.concatenate([vt_s[sl, vrows, r] for sl, r in blocks], axis=1)
            s_ctx = scores_t(ck_s[:, cols])
            return [([jnp.where(prev_ok, s_loc[0:BLOCK], NEG), s_loc[BLOCK:2 * BLOCK],
                      jnp.where(next_ok, s_loc[2 * BLOCK:3 * BLOCK], NEG)], vt_loc),
                    ([s_ctx], cvt_s[vrows, :])]

        def softmax(kv, groups):
            snk = jnp.concatenate(
                [jnp.full((1, BLOCK), sink_ref[kv * Q_PER_KV + g] * LOG2E, F32) for g in range(Q_PER_KV)], axis=1)
            if tokens:
                token = tokens.pop(0)
                while tokens:
                    token = token + tokens.pop(0)
                snk = snk + jnp.concatenate([token] * Q_PER_KV, axis=1)
            mx = snk
            for ss, _ in groups:
                for sc in ss:
                    mx = jnp.maximum(mx, jnp.max(sc, axis=0, keepdims=True))
            den = jnp.exp2(snk - mx)
            probs = []
            for ss, vt in groups:
                es = []
                for sc in ss:
                    e = jnp.exp2(sc - mx)
                    den = den + jnp.sum(e, axis=0, keepdims=True)
                    es.append(e.astype(BF16))
                probs.append((es[0] if len(es) == 1 else jnp.concatenate(es, axis=0), vt))
            return probs, den

        def values(qb, kv, probs, den):
            o_t = jnp.zeros((LANES, cols4), F32)
            for e_all, vt in probs:
                o_t = o_t + jnp.dot(vt, e_all, preferred_element_type=F32)
            o_t = o_t * (1.0 / den)
            o01 = jnp.where(upper, o_t[:, 0:BLOCK], o_t[:, BLOCK:2 * BLOCK]).T
            o23 = jnp.where(upper, o_t[:, 2 * BLOCK:3 * BLOCK], o_t[:, 3 * BLOCK:4 * BLOCK]).T
            a_s[halves[qb], CONV_WIDTH + kv * GROUP_W:CONV_WIDTH + (kv + 1) * GROUP_W] = (
                jnp.concatenate([o01, o23], axis=1) * gb_s[slot, halves[qb], kv * GROUP_W:(kv + 1) * GROUP_W]
            ).astype(BF16)

        pairs = [(qb, kv) for qb in range(TM // BLOCK) for kv in range(N_KV_HEADS)]
        yield
        ahead = scores(*pairs[0])
        for t, (qb, kv) in enumerate(pairs):
            groups = ahead
            if t + 1 < len(pairs):
                yield
                ahead = scores(*pairs[t + 1])
            yield
            probs, den = softmax(kv, groups)
            yield
            values(qb, kv, probs, den)

        yield from _out_proj_items(lambda: a_s[...], wout_ref, xa_ref, moda_ref, y_ref)

    staged = ((u_w, u_s), (gate_w, gate_s), (q_w, q_s), (k_w, k_s), (vt_w, vt_s), (gb_w, gb_s))
    order = EVEN_ORDER_LATENT if windowed else EVEN_ORDER_CONTEXT
    _pipeline_steps(s, n_chunks, lag, ring, (h_w, h_s), staged, order, normalize, project, finish)


def _chunk_specs(n_chunks, nc, lag, per_seq_mod, tm=TM):
    last = n_chunks - 1
    norm_chunk = lambda s: jnp.minimum(s, last)
    proj_chunk = lambda s: jnp.clip(s - 1, 0, last)
    fin_chunk = lambda s: jnp.clip(s - 1 - lag, 0, last)
    seq_of = (lambda j: 1 + j // nc) if per_seq_mod else (lambda j: 0)
    return dict(
        x_norm=pl.BlockSpec((1, tm, D_MODEL), lambda s: (norm_chunk(s), 0, 0)),
        x_fin=pl.BlockSpec((1, tm, D_MODEL), lambda s: (fin_chunk(s), 0, 0)),
        mod_norm=pl.BlockSpec((1, 1, 3 * D_MODEL), lambda s: (seq_of(norm_chunk(s)), 0, 0)),
        mod_fin=pl.BlockSpec((1, 1, 3 * D_MODEL), lambda s: (seq_of(fin_chunk(s)), 0, 0)),
        proj_chunk=proj_chunk, fin_chunk=fin_chunk)


def _const_spec(shape):
    return pl.BlockSpec(shape, lambda c: (0,) * len(shape))


def _even_layer(x, mod, norm_g, w_in, conv_w, conv_b, qk_gains, sink, w_out, rope=None, ctx=None):
    n_seq, seq_len, _ = x.shape
    windowed = rope is not None
    nc = seq_len // TM
    n_chunks = n_seq * nc
    lag, ring = (2, 4) if windowed else (1, 2)
    assert windowed or nc == 1
    sp = _chunk_specs(n_chunks, nc, lag, windowed)
    xc = x.reshape(n_chunks, TM, D_MODEL)
    in_specs = [
        sp["x_norm"], sp["x_fin"], sp["mod_norm"], sp["mod_fin"],
        _const_spec((1, D_MODEL)),
        _const_spec(w_in.shape),
        _const_spec((3, CONV_WIDTH)),
        _const_spec((1, CONV_WIDTH)),
        _const_spec((2, LANES)),
        pl.BlockSpec(memory_space=pltpu.SMEM),
        _const_spec(w_out.shape),
    ]
    args = [xc, xc, mod, mod, norm_g.reshape(1, D_MODEL), w_in, conv_w, conv_b.reshape(1, CONV_WIDTH),
            qk_gains, sink, w_out]
    per_chunk = [
        ((TM, CONV_WIDTH), F32),
        ((TM, CONV_WIDTH), F32),
        ((TM, ATTN_WIDTH), BF16),
        ((TM, 2 * GROUP_W), BF16),
        ((2 * LANES, TM), BF16),
        ((TM, ATTN_WIDTH), F32),
    ]
    scratch = (
        [pltpu.VMEM((TM, D_MODEL), BF16)] * 2
        + [pltpu.VMEM((ring,) + shape, dt) for shape, dt in per_chunk]
        + [pltpu.VMEM(shape, dt) for shape, dt in per_chunk]
        + [pltpu.VMEM((TM, D_MODEL), BF16)])
    y_shape = jax.ShapeDtypeStruct(xc.shape, F32)
    y_spec = pl.BlockSpec((1, TM, D_MODEL), lambda c: (sp["fin_chunk"](c), 0, 0))
    if windowed:
        cos, sin = rope
        ck, cv = ctx
        rope_spec = pl.BlockSpec((TM, LANES), lambda c: (sp["proj_chunk"](c) % nc, 0))
        ctx_spec = pl.BlockSpec((1,) + ck.shape[1:], lambda c: (sp["fin_chunk"](c) // nc, 0, 0))
        in_specs += [rope_spec, rope_spec, ctx_spec, ctx_spec]
        args += [cos, sin, ck, cv]
        scratch += [pltpu.VMEM((ck.shape[1], 2 * GROUP_W), BF16), pltpu.VMEM((2 * LANES, cv.shape[1]), BF16)]
        out_shape, out_specs = y_shape, y_spec
    else:
        kv_shape = jax.ShapeDtypeStruct((n_chunks, TM, N_KV_HEADS, HEAD_DIM), F32)
        kv_spec = pl.BlockSpec((1, TM, N_KV_HEADS, HEAD_DIM), lambda c: (sp["proj_chunk"](c), 0, 0, 0))
        out_shape = (y_shape, kv_shape, kv_shape,
                     jax.ShapeDtypeStruct(_tiled_shape(w_in), BF16), jax.ShapeDtypeStruct(_tiled_shape(w_out), BF16))
        out_specs = (y_spec, kv_spec, kv_spec, _const_spec(_tiled_shape(w_in)), _const_spec(_tiled_shape(w_out)))
    out = pl.pallas_call(
        functools.partial(_even_kernel, nc=nc, n_chunks=n_chunks, lag=lag, ring=ring, windowed=windowed),
        out_shape=out_shape,
        grid=(n_chunks + lag + 1,),
        in_specs=in_specs,
        out_specs=out_specs,
        scratch_shapes=scratch,
        compiler_params=PIPELINE_PARAMS,
        name="even_latent" if windowed else "even_context",
    )(*args)
    if windowed:
        return out.reshape(x.shape)
    y, k, v, w_in_bf16, w_out_bf16 = out
    kv_out = (n_seq, seq_len, N_KV_HEADS, HEAD_DIM)
    return y.reshape(x.shape), k.reshape(kv_out), v.reshape(kv_out), w_in_bf16, w_out_bf16


def _odd_kernel(*refs, nc, n_chunks, lag, ring, cast_weights):
    if cast_weights:
        (xp_ref, xa_ref, modp_ref, moda_ref, g_ref, win_f32, pw_f32, ps_ref, wout_f32,
         y_ref, win_ref, pw_ref, wout_ref, h_s, h_w, u_s, sg_s, u_w, sg_w) = refs
    else:
        (xp_ref, xa_ref, modp_ref, moda_ref, g_ref, win_ref, pw_ref, ps_ref, wout_ref,
         y_ref, h_s, h_w, u_s, sg_s, u_w, sg_w) = refs
    tm = h_s.shape[0]
    s = pl.program_id(0)
    i = s - 1 - lag
    ci = lax.rem(i, jnp.int32(nc))

    if cast_weights:
        @pl.when(s == 0)
        def _cast():
            _cast_columns(win_f32, win_ref)
            _cast_columns(wout_f32, wout_ref)
            for gi in range(pw_ref.shape[0]):
                pw_ref[gi] = pw_f32[gi].astype(BF16)

    def normalize(tokens):
        yield from _normalize_items(h_w, xp_ref, g_ref, modp_ref, tokens)

    def project():
        for n in range(D_MODEL // NT):
            yield
            cols = slice(n * NT, (n + 1) * NT)
            u = jnp.dot(h_s[...], win_ref[n], preferred_element_type=F32)
            u_w[:, cols] = u
        for n in range(D_MODEL // NT):
            yield
            cols = slice(n * NT, (n + 1) * NT)
            sg = _silu(jnp.dot(h_s[...], win_ref[D_MODEL // NT + n], preferred_element_type=F32))
            sg_w[:, cols] = sg

    def finish(slot, tokens):
        if nc > 1:
            prev_slot = lax.rem(slot + (ring - 1), jnp.int32(ring))
            next_slot = lax.rem(slot + 1, jnp.int32(ring))
            has_prev = ci > 0
            has_next = ci < nc - 1
        else:
            prev_slot = next_slot = has_prev = has_next = None
        t = ci * tm + lax.broadcasted_iota(jnp.int32, (tm, 1), 0)
        outs = []
        for gi, w in enumerate(POOL_SIZES):
            yield
            cols = slice(gi * POOL_GROUP, (gi + 1) * POOL_GROUP)
            ext = _with_halo(u_s, slot, cols, prev_slot, next_slot, has_prev, has_next)
            acc = ext
            span = 1
            while span < w:
                acc = acc + _rows_down(acc, span)
                span *= 2
            if w > 2:
                acc = _rows_up(acc, w // 2 - 1)
            cnt = (jnp.minimum(t + w // 2, nc * tm) - jnp.maximum(t - w // 2, 0)).astype(F32)
            while tokens:
                cnt = cnt + tokens.pop(0)[:, 0:1]
            pooled = acc[HALO:HALO + tm] / cnt - ext[HALO:HALO + tm]
            outs.append(jnp.dot(pooled.astype(BF16), pw_ref[gi], preferred_element_type=F32))
        yield
        y = (jnp.concatenate(outs, axis=1) * ps_ref[...] * sg_s[slot]).astype(BF16)
        yield from _out_proj_items(lambda: y, wout_ref, xa_ref, moda_ref, y_ref)

    _pipeline_steps(s, n_chunks, lag, ring, (h_w, h_s), ((u_w, u_s), (sg_w, sg_s)), ODD_ORDER,
                    normalize, project, finish)


def _odd_layer(x, mod, per_seq_mod, norm_g, w_in, pool_w, pool_scale, w_out):
    n_seq, seq_len, _ = x.shape
    tm = TM
    nc = seq_len // tm
    n_chunks = n_seq * nc
    lag, ring = (2, 4) if nc > 1 else (1, 2)
    sp = _chunk_specs(n_chunks, nc, lag, per_seq_mod, tm)
    xc = x.reshape(n_chunks, tm, D_MODEL)
    in_specs = [
        sp["x_norm"], sp["x_fin"], sp["mod_norm"], sp["mod_fin"],
        _const_spec((1, D_MODEL)),
        _const_spec(w_in.shape),
        _const_spec(pool_w.shape),
        _const_spec((1, D_MODEL)),
        _const_spec(w_out.shape),
    ]
    cast_weights = w_in.dtype != BF16
    out_shape = jax.ShapeDtypeStruct(xc.shape, F32)
    out_specs = pl.BlockSpec((1, tm, D_MODEL), lambda c: (sp["fin_chunk"](c), 0, 0))
    if cast_weights:
        shapes = (_tiled_shape(w_in), pool_w.shape, _tiled_shape(w_out))
        out_shape = (out_shape,) + tuple(jax.ShapeDtypeStruct(shape, BF16) for shape in shapes)
        out_specs = (out_specs,) + tuple(_const_spec(shape) for shape in shapes)
    out = pl.pallas_call(
        functools.partial(_odd_kernel, nc=nc, n_chunks=n_chunks, lag=lag, ring=ring, cast_weights=cast_weights),
        out_shape=out_shape,
        grid=(n_chunks + lag + 1,),
        in_specs=in_specs,
        out_specs=out_specs,
        scratch_shapes=[
            pltpu.VMEM((tm, D_MODEL), BF16),
            pltpu.VMEM((tm, D_MODEL), BF16),
            pltpu.VMEM((ring, tm, D_MODEL), F32),
            pltpu.VMEM((ring, tm, D_MODEL), F32),
            pltpu.VMEM((tm, D_MODEL), F32),
            pltpu.VMEM((tm, D_MODEL), F32),
        ],
        compiler_params=PIPELINE_PARAMS,
        name="odd_latent" if per_seq_mod else "odd_context",
    )(xc, xc, mod, mod, norm_g.reshape(1, D_MODEL), w_in, pool_w, pool_scale.reshape(1, D_MODEL), w_out)
    if cast_weights:
        y, *bf16_weights = out
        return y.reshape(x.shape), bf16_weights
    return out.reshape(x.shape)


def _rope_tables(seq_len):
    n_rows = seq_len // GRID_W
    row = np.repeat(np.arange(n_rows), GRID_W).astype(np.float32)
    col = np.tile(np.arange(GRID_W), n_rows).astype(np.float32)
    inv = (np.float32(ROPE_BASE) ** (-np.arange(ROPE_FREQS, dtype=np.float32) / np.float32(ROPE_FREQS))).astype(np.float32)
    ang = np.stack([row[:, None] * inv, col[:, None] * inv], axis=1)
    cos = np.broadcast_to(np.cos(ang)[:, :, None, :], (seq_len, 2, 2, ROPE_FREQS))
    sin = np.sin(ang)[:, :, None, :] * np.array([-1.0, 1.0], np.float32)[None, None, :, None]
    cos = cos.reshape(seq_len, HEAD_DIM).astype(np.float32)
    sin = sin.reshape(seq_len, HEAD_DIM).astype(np.float32)
    return jnp.asarray(np.tile(cos, (1, LANES // HEAD_DIM))), jnp.asarray(np.tile(sin, (1, LANES // HEAD_DIM)))


def kernel(x_prompt, x_sample, cache_k, cache_v, c, c_ctx, ada_w_e, ada_b_e, norm_g_e, w_in_e, conv_w, conv_b,
           q_norm_g, k_norm_g, sink, w_out_e, ada_w_o, ada_b_o, norm_g_o, w_in_o, pool_w, pool_scale, w_out_o):
    n_dec = x_sample.shape[0]
    depth = ada_w_e.shape[0] + ada_w_o.shape[0]
    assert ada_w_e.shape[0] == 1 and ada_w_o.shape[0] == 1 and n_dec + 1 <= 8
    mod_e, mod_o = _adaln(c_ctx, c, ada_w_e[0], ada_b_e[0], ada_w_o[0], ada_b_o[0])
    rope = _rope_tables(x_sample.shape[1])

    yp, ys = x_prompt, x_sample
    new_k, new_v = [], []
    for layer in range(depth):
        i = layer // 2
        if layer % 2 == 0:
            mod = mod_e
            gains = jnp.tile(jnp.stack([q_norm_g[i], k_norm_g[i]]), (1, LANES // HEAD_DIM))
            small = (conv_w[i], conv_b[i], gains, sink[i])
            yp, k, v, w_in, w_out = _even_layer(yp, mod, norm_g_e[i], w_in_e[i], *small, w_out_e[i])
            new_k.append(k)
            new_v.append(v)
            ck = cache_k[:, i].reshape(n_dec, cache_k.shape[2], KV_WIDTH)
            cv = cache_v[:, i].reshape(n_dec, cache_v.shape[2], KV_WIDTH)
            ys = _even_layer(ys, mod, norm_g_e[i], w_in, *small, w_out, rope=rope, ctx=(ck, cv))
        else:
            mod = mod_o
            yp, (w_in, w_pool, w_out) = _odd_layer(yp, mod, False, norm_g_o[i], w_in_o[i], pool_w[i],
                                                   pool_scale[i], w_out_o[i])
            ys = _odd_layer(ys, mod, True, norm_g_o[i], w_in, w_pool, pool_scale[i], w_out)
    return yp, ys, jnp.stack(new_k, axis=1), jnp.stack(new_v, axis=1)
```

```python
import functools

import jax
import jax.numpy as jnp
import numpy as np
from jax import lax
from jax.experimental import pallas as pl
from jax.experimental.pallas import tpu as pltpu

F32 = jnp.float32
BF16 = jnp.bfloat16

D_MODEL = 1024
GRID_W = 64
HEAD_DIM = 64
HEAD_SHIFT = 6
ATTN_WIDTH = D_MODEL // 2
N_Q_HEADS = ATTN_WIDTH // HEAD_DIM
N_KV_HEADS = N_Q_HEADS // 4
Q_PER_KV = N_Q_HEADS // N_KV_HEADS
KV_WIDTH = N_KV_HEADS * HEAD_DIM
CONV_WIDTH = D_MODEL - ATTN_WIDTH
WINDOW = 128
BLOCK = 128
ROPE_BASE = 10000.0
ROPE_FREQS = HEAD_DIM // 4
ATTN_SCALE = HEAD_DIM ** -0.5
LOG2E = float(np.log2(np.e))
NEG = -1e30
POOL_SIZES = (2, 4, 8, 16)
POOL_GROUP = D_MODEL // len(POOL_SIZES)
EPS = 1e-6
EVEN_SIZES = (CONV_WIDTH, CONV_WIDTH, CONV_WIDTH, CONV_WIDTH, ATTN_WIDTH, KV_WIDTH, KV_WIDTH, ATTN_WIDTH)
EVEN_IN = sum(EVEN_SIZES)
EVEN_OFFS = tuple(int(s) for s in np.cumsum((0,) + EVEN_SIZES))

TM = 256
NT = 256
NORM_ROWS = 64
EVEN_ORDER_LATENT = "PF" + "F" + "PFNFPF" * 3 + "PNFPF" + "FFFF"
EVEN_ORDER_CONTEXT = "PF" + "F" + "FNPFPF" * 3 + "NPFPF" + "FFFF"
ODD_ORDER = "PNFP" * 4 + "FFFFF"
LANES = 128
GROUP_W = Q_PER_KV * HEAD_DIM
HALO = 8
ADA_TK = 512
VMEM_LIMIT = 56 * 1024 * 1024
PIPELINE_PARAMS = pltpu.CompilerParams(dimension_semantics=("arbitrary",), vmem_limit_bytes=VMEM_LIMIT)


def _silu(x):
    half = 0.5 * x
    return half + half * jnp.tanh(half)


def _rows_down(x, k):
    return pltpu.roll(x, k, axis=0)


def _rows_up(x, k):
    return pltpu.roll(x, x.shape[0] - k, axis=0)


def _rms_modulate(x, gain, shift):
    ms = jnp.mean(x * x, axis=-1, keepdims=True)
    return x * lax.rsqrt(ms + EPS) * gain + shift


def _cast_columns(src_ref, dst_ref):
    for n in range(dst_ref.shape[0]):
        dst_ref[n] = src_ref[:, n * NT:(n + 1) * NT].astype(dst_ref.dtype)


def _tiled_shape(w):
    return (w.shape[1] // NT, w.shape[0], NT)


def _opaque_zero(x):
    bits = pltpu.bitcast(x, jnp.uint32)
    return pltpu.bitcast((bits >> 16) >> 16, F32)


def _normalize_items(h_s, x_ref, g_ref, mod_ref, tokens):
    shift = mod_ref[0, :, 0:D_MODEL]
    gain = g_ref[...] * (1.0 + mod_ref[0, :, D_MODEL:2 * D_MODEL])
    for r in range(0, h_s.shape[0], NORM_ROWS):
        yield
        h = _rms_modulate(x_ref[0, r:r + NORM_ROWS], gain, shift)
        h_s[r:r + NORM_ROWS] = h.astype(BF16)
        tokens.append(_opaque_zero(h[0:8, 0:LANES])[0:1])


def _out_proj_items(get_a, wout_ref, x_ref, mod_ref, y_ref):
    for n in range(D_MODEL // NT):
        yield
        cols = slice(n * NT, (n + 1) * NT)
        mixed = jnp.dot(get_a(), wout_ref[n], preferred_element_type=F32)
        y_ref[0, :, cols] = x_ref[0, :, cols] + mod_ref[0, :, 2 * D_MODEL + n * NT:2 * D_MODEL + (n + 1) * NT] * mixed


def _weave(order, **stages):
    by_letter = {name[0].upper(): stage for name, stage in stages.items()}
    for stage in by_letter.values():
        next(stage, None)
    for letter in order:
        if letter in by_letter:
            next(by_letter[letter], None)
    for stage in by_letter.values():
        for _ in stage:
            pass


def _with_halo(u_s, slot, cols, prev_slot, next_slot, has_prev, has_next):
    u = u_s[slot, :, cols]
    zero = jnp.zeros((HALO, u.shape[1]), F32)
    if prev_slot is None:
        top = bot = zero
    else:
        rows = u_s.shape[1]
        top = jnp.where(has_prev, u_s[prev_slot, rows - HALO:rows, cols], zero)
        bot = jnp.where(has_next, u_s[next_slot, 0:HALO, cols], zero)
    return jnp.concatenate([top, u, bot], axis=0)


def _pipeline_steps(s, n_chunks, lag, ring, normalized, staged, order, normalize, project, finish):
    fin_slot = lax.rem(s + (2 * ring - 1 - lag), jnp.int32(ring))

    @pl.when((s >= 1) & (s <= n_chunks))
    def _publish_normalized():
        h_w, h_s = normalized
        h_s[...] = h_w[...]

    @pl.when((s >= 2) & (s <= n_chunks + 1))
    def _publish_projected():
        slot = lax.rem(s + (ring - 2), jnp.int32(ring))
        for stage_ref, ring_ref in staged:
            ring_ref[slot] = stage_ref[...]

    @pl.when(s == 0)
    def _first():
        _weave(order, normalize=normalize([]))

    @pl.when((s >= 1) & (s <= lag))
    def _fill():
        _weave(order, project=project(), normalize=normalize([]))

    @pl.when((s > lag) & (s <= n_chunks))
    def _steady():
        tokens = []
        _weave(order, finish=finish(fin_slot, tokens), project=project(), normalize=normalize(tokens))

    @pl.when(s > n_chunks)
    def _drain():
        _weave(order, finish=finish(fin_slot, []))


def _adaln_kernel(cctx_ref, c_ref, we_ref, be_ref, wo_ref, bo_ref, oe_ref, oo_ref, cond_s, acc_e, acc_o):
    k = pl.program_id(0)

    @pl.when(k == 0)
    def _init():
        n_dec = c_ref.shape[0]
        cond_s[...] = jnp.zeros(cond_s.shape, F32)
        cond_s[0:1] = cctx_ref[...]
        cond_s[1:1 + n_dec] = c_ref[...]
        acc_e[...] = jnp.broadcast_to(be_ref[...], acc_e.shape)
        acc_o[...] = jnp.broadcast_to(bo_ref[...], acc_o.shape)

    s = _silu(cond_s[:, pl.ds(pl.multiple_of(k * ADA_TK, ADA_TK), ADA_TK)]).astype(BF16)
    acc_e[...] += jnp.dot(s, we_ref[...].astype(BF16), preferred_element_type=F32)
    acc_o[...] += jnp.dot(s, wo_ref[...].astype(BF16), preferred_element_type=F32)

    @pl.when(k == pl.num_programs(0) - 1)
    def _write():
        for r in range(oe_ref.shape[0]):
            oe_ref[r] = acc_e[r:r + 1, :]
            oo_ref[r] = acc_o[r:r + 1, :]


def _adaln(c_ctx, c, w_e, b_e, w_o, b_o):
    n = 3 * D_MODEL
    n_dec = c.shape[0]
    wspec = pl.BlockSpec((ADA_TK, n), lambda k: (k, 0))
    vspec = pl.BlockSpec((1, n), lambda k: (0, 0))
    ospec = pl.BlockSpec((8, 1, n), lambda k: (0, 0, 0))
    return pl.pallas_call(
        _adaln_kernel,
        out_shape=(jax.ShapeDtypeStruct((8, 1, n), F32), jax.ShapeDtypeStruct((8, 1, n), F32)),
        grid=(D_MODEL // ADA_TK,),
        in_specs=[pl.BlockSpec((1, D_MODEL), lambda k: (0, 0)), pl.BlockSpec((n_dec, D_MODEL), lambda k: (0, 0)),
                  wspec, vspec, wspec, vspec],
        out_specs=(ospec, ospec),
        scratch_shapes=[pltpu.VMEM((8, D_MODEL), F32), pltpu.VMEM((8, n), F32), pltpu.VMEM((8, n), F32)],
        compiler_params=pltpu.CompilerParams(dimension_semantics=("arbitrary",), vmem_limit_bytes=VMEM_LIMIT),
        name="adaln",
    )(c_ctx.reshape(1, D_MODEL), c, w_e, b_e.reshape(1, n), w_o, b_o.reshape(1, n))


def _head_inv_rms(x, bd):
    ms = jnp.dot((x * x).astype(BF16), bd, preferred_element_type=F32) * (1.0 / HEAD_DIM)
    return lax.rsqrt(ms + EPS)


def _rope(x, cos, sin_signed, first_half):
    partner = jnp.where(first_half, pltpu.roll(x, LANES - ROPE_FREQS, axis=1), pltpu.roll(x, ROPE_FREQS, axis=1))
    return x * cos + partner * sin_signed


def _dup_kv(x):
    lane = lax.broadcasted_iota(jnp.int32, (1, LANES), 1)
    swapped = pltpu.roll(x, HEAD_DIM, axis=1)
    return jnp.where(lane < HEAD_DIM, x, swapped), jnp.where(lane < HEAD_DIM, swapped, x)


def _tiled_keys(k):
    ka, kb = _dup_kv(k)
    return jnp.concatenate([ka, ka, kb, kb], axis=1).astype(BF16)


def _values_t(v):
    va, vb = _dup_kv(v)
    return jnp.concatenate([va.T, vb.T], axis=0).astype(BF16)


def _even_kernel(*refs, nc, n_chunks, lag, ring, windowed):
    if windowed:
        (xp_ref, xa_ref, modp_ref, moda_ref, g_ref, win_ref, cw_ref, cb_ref, qkg_ref, sink_ref, wout_ref,
         cos_ref, sin_ref, ck_ref, cv_ref,
         y_ref,
         h_s, h_w, u_s, gate_s, q_s, k_s, vt_s, gb_s, u_w, gate_w, q_w, k_w, vt_w, gb_w, a_s, ck_s, cvt_s) = refs
        ko_ref = vo_ref = None
    else:
        (xp_ref, xa_ref, modp_ref, moda_ref, g_ref, win_f32, cw_ref, cb_ref, qkg_ref, sink_ref, wout_f32,
         y_ref, ko_ref, vo_ref, win_ref, wout_ref,
         h_s, h_w, u_s, gate_s, q_s, k_s, vt_s, gb_s, u_w, gate_w, q_w, k_w, vt_w, gb_w, a_s) = refs
    s = pl.program_id(0)
    i = s - 1 - lag
    ci = lax.rem(i, jnp.int32(nc))

    @pl.when(s == 0)
    def _init():
        k_s[...] = jnp.zeros(k_s.shape, BF16)
        vt_s[...] = jnp.zeros(vt_s.shape, BF16)
        if not windowed:
            _cast_columns(win_f32, win_ref)
            _cast_columns(wout_f32, wout_ref)

    if windowed:
        @pl.when((i >= 0) & (ci == 0))
        def _load_ctx():
            ck_s[...] = _tiled_keys(ck_ref[0])
            cvt_s[...] = _values_t(cv_ref[0])

    def normalize(tokens):
        yield from _normalize_items(h_w, xp_ref, g_ref, modp_ref, tokens)

    def project():
        o_bg, o_cg, o_xs, o_ga, o_q, o_k, _, o_gb = EVEN_OFFS[:8]

        def proj(start):
            return jnp.dot(h_s[...], win_ref[start // NT], preferred_element_type=F32)

        lane = lax.broadcasted_iota(jnp.int32, (1, LANES), 1)
        first_half = (lane & (2 * ROPE_FREQS - 1)) < ROPE_FREQS
        ri = lax.broadcasted_iota(jnp.int32, (NT, NT), 0) >> HEAD_SHIFT
        cj = lax.broadcasted_iota(jnp.int32, (NT, NT), 1) >> HEAD_SHIFT
        bd = (ri == cj).astype(BF16)

        def qk_norm(x, g):
            w = x.shape[1]
            inv = _head_inv_rms(x, bd[0:w, 0:w])
            pieces = []
            for j in range(w // LANES):
                lanes = slice(j * LANES, (j + 1) * LANES)
                xj = x[:, lanes] * inv[:, lanes] * g
                pieces.append(_rope(xj, cos_ref[...], sin_ref[...], first_half) if windowed else xj)
            return pieces

        def keys_values(_):
            kv = proj(o_k)
            k, = qk_norm(kv[:, 0:KV_WIDTH], qkg_ref[1:2, :])
            v = kv[:, KV_WIDTH:2 * KV_WIDTH]
            if not windowed:
                ko_ref[0] = k
                vo_ref[0] = v
            k_w[...] = _tiled_keys(k)
            vt_w[...] = _values_t(v)

        def queries(n):
            for j, qj in enumerate(qk_norm(proj(o_q + n * NT), qkg_ref[0:1, :])):
                q_w[:, n * NT + j * LANES:n * NT + (j + 1) * LANES] = (qj * (ATTN_SCALE * LOG2E)).astype(BF16)

        def conv_gate(n):
            gate_w[:, n * NT:(n + 1) * NT] = proj(o_bg + n * NT) * _silu(proj(o_ga + n * NT))

        def conv_input(n):
            u_w[:, n * NT:(n + 1) * NT] = proj(o_cg + n * NT) * proj(o_xs + n * NT)

        def attn_gate(n):
            gb_w[:, n * NT:(n + 1) * NT] = _silu(proj(o_gb + n * NT))

        items = [(conv_gate, 0), (keys_values, 0), (conv_input, 0), (queries, 0), (conv_gate, 1), (queries, 1),
                 (conv_input, 1), (attn_gate, 0), (attn_gate, 1)]
        for emit, n in items:
            yield
            emit(n)

    def finish(slot, tokens):
        if windowed:
            prev_slot = lax.rem(slot + (ring - 1), jnp.int32(ring))
            next_slot = lax.rem(slot + 1, jnp.int32(ring))
            has_prev = ci > 0
            has_next = ci < nc - 1
        else:
            prev_slot = next_slot = has_prev = has_next = None

        yield
        ext = _with_halo(u_s, slot, slice(None), prev_slot, next_slot, has_prev, has_next)
        conv = (_rows_down(ext, 1)[HALO:HALO + TM] * cw_ref[0:1, :]
                + ext[HALO:HALO + TM] * cw_ref[1:2, :]
                + _rows_up(ext, 1)[HALO:HALO + TM] * cw_ref[2:3, :]
                + cb_ref[...])
        a_s[:, 0:CONV_WIDTH] = (gate_s[slot] * conv).astype(BF16)

        lane_head = lax.broadcasted_iota(jnp.int32, (1, GROUP_W), 1) >> HEAD_SHIFT
        cols4 = Q_PER_KV * BLOCK
        key_i =lax.broadcasted_iota(jnp.int32, (BLOCK, cols4), 0)
        qry_i = lax.broadcasted_iota(jnp.int32, (BLOCK, cols4), 1) & (BLOCK - 1)
        upper = lax.broadcasted_iota(jnp.int32, (LANES, LANES), 0) < HEAD_DIM
        halves = (slice(0, BLOCK), slice(BLOCK, TM))
        @functools.lru_cache(maxsize=None)
        def block_masks(qb):
            if qb == 0:
                blocks = ((prev_slot, halves[1]), (slot, halves[0]), (slot, halves[1]))
                prev_in, next_in = has_prev, True
            else:
                blocks = ((slot, halves[0]), (slot, halves[1]), (next_slot, halves[0]))
                prev_in, next_in = True, has_next
            prev_ok = (key_i - qry_i) >= jnp.where(prev_in, 0, BLOCK)
            next_ok = (qry_i - key_i) >= jnp.where(next_in, 0, BLOCK)
            return blocks, prev_ok, next_ok

        def scores(qb, kv):
            cols = slice(kv * GROUP_W, (kv + 1) * GROUP_W)
            vrows = slice(kv * LANES, (kv + 1) * LANES)
            qblk = q_s[slot, halves[qb], cols]
            q4 = jnp.concatenate(
                [jnp.where(lane_head == g, qblk, jnp.zeros_like(qblk)) for g in range(Q_PER_KV)], axis=0)

            def scores_t(kk):
                return lax.dot_general(kk, q4, (((1,), (1,)), ((), ())), preferred_element_type=F32)

            if not windowed:
                return [([scores_t(k_s[slot, :, cols])], vt_s[slot, vrows, :])]
            blocks, prev_ok, next_ok = block_masks(qb)
            s_loc = scores_t(jnp.concatenate([k_s[sl, r, cols] for sl, r in blocks], axis=0))
            vt_loc = jnp.concatenate([vt_s[sl, vrows, r] for sl, r in blocks], axis=1)
            s_ctx = scores_t(ck_s[:, cols])
            return [([jnp.where(prev_ok, s_loc[0:BLOCK], NEG), s_loc[BLOCK:2 * BLOCK],
                      jnp.where(next_ok, s_loc[2 * BLOCK:3 * BLOCK], NEG)], vt_loc),
                    ([s_ctx], cvt_s[vrows, :])]

        def softmax(kv, groups):
            snk = jnp.concatenate(
                [jnp.full((1, BLOCK), sink_ref[kv * Q_PER_KV + g] * LOG2E, F32) for g in range(Q_PER_KV)], axis=1)
            if tokens:
                token = tokens.pop(0)
                while tokens:
                    token = token + tokens.pop(0)
                snk = snk + jnp.concatenate([token] * Q_PER_KV, axis=1)
            mx = snk
            for ss, _ in groups:
                for sc in ss:
                    mx = jnp.maximum(mx, jnp.max(sc, axis=0, keepdims=True))
            den = jnp.exp2(snk - mx)
            probs = []
            for ss, vt in groups:
                es = []
                for sc in ss:
                    e = jnp.exp2(sc - mx)
                    den = den + jnp.sum(e, axis=0, keepdims=True)
                    es.append(e.astype(BF16))
                probs.append((es[0] if len(es) == 1 else jnp.concatenate(es, axis=0), vt))
            return probs, den

        def values(qb, kv, probs, den):
            o_t = jnp.zeros((LANES, cols4), F32)
            for e_all, vt in probs:
                o_t = o_t + jnp.dot(vt, e_all, preferred_element_type=F32)
            o_t = o_t * (1.0 / den)
            o01 = jnp.where(upper, o_t[:, 0:BLOCK], o_t[:, BLOCK:2 * BLOCK]).T
            o23 = jnp.where(upper, o_t[:, 2 * BLOCK:3 * BLOCK], o_t[:, 3 * BLOCK:4 * BLOCK]).T
            a_s[halves[qb], CONV_WIDTH + kv * GROUP_W:CONV_WIDTH + (kv + 1) * GROUP_W] = (
                jnp.concatenate([o01, o23], axis=1) * gb_s[slot, halves[qb], kv * GROUP_W:(kv + 1) * GROUP_W]
            ).astype(BF16)

        pairs = [(qb, kv) for qb in range(TM // BLOCK) for kv in range(N_KV_HEADS)]
        yield
        ahead = scores(*pairs[0])
        for t, (qb, kv) in enumerate(pairs):
            groups = ahead
            if t + 1 < len(pairs):
                yield
                ahead = scores(*pairs[t + 1])
            yield
            probs, den = softmax(kv, groups)
            yield
            values(qb, kv, probs, den)

        yield from _out_proj_items(lambda: a_s[...], wout_ref, xa_ref, moda_ref, y_ref)

    staged = ((u_w, u_s), (gate_w, gate_s), (q_w, q_s), (k_w, k_s), (vt_w, vt_s), (gb_w, gb_s))
    order = EVEN_ORDER_LATENT if windowed else EVEN_ORDER_CONTEXT
    _pipeline_steps(s, n_chunks, lag, ring, (h_w, h_s), staged, order, normalize, project, finish)


def _chunk_specs(n_chunks, nc, lag, per_seq_mod, tm=TM):
    last = n_chunks - 1
    norm_chunk = lambda s: jnp.minimum(s, last)
    proj_chunk = lambda s: jnp.clip(s - 1, 0, last)
    fin_chunk = lambda s: jnp.clip(s - 1 - lag, 0, last)
    seq_of = (lambda j: 1 + j // nc) if per_seq_mod else (lambda j: 0)
    return dict(
        x_norm=pl.BlockSpec((1, tm, D_MODEL), lambda s: (norm_chunk(s), 0, 0)),
        x_fin=pl.BlockSpec((1, tm, D_MODEL), lambda s: (fin_chunk(s), 0, 0)),
        mod_norm=pl.BlockSpec((1, 1, 3 * D_MODEL), lambda s: (seq_of(norm_chunk(s)), 0, 0)),
        mod_fin=pl.BlockSpec((1, 1, 3 * D_MODEL), lambda s: (seq_of(fin_chunk(s)), 0, 0)),
        proj_chunk=proj_chunk, fin_chunk=fin_chunk)


def _const_spec(shape):
    return pl.BlockSpec(shape, lambda c: (0,) * len(shape))


def _even_layer(x, mod, norm_g, w_in, conv_w, conv_b, qk_gains, sink, w_out, rope=None, ctx=None):
    n_seq, seq_len, _ = x.shape
    windowed = rope is not None
    nc = seq_len // TM
    n_chunks = n_seq * nc
    lag, ring = (2, 4) if windowed else (1, 2)
    assert windowed or nc == 1
    sp = _chunk_specs(n_chunks, nc, lag, windowed)
    xc = x.reshape(n_chunks, TM, D_MODEL)
    in_specs = [
        sp["x_norm"], sp["x_fin"], sp["mod_norm"], sp["mod_fin"],
        _const_spec((1, D_MODEL)),
        _const_spec(w_in.shape),
        _const_spec((3, CONV_WIDTH)),
        _const_spec((1, CONV_WIDTH)),
        _const_spec((2, LANES)),
        pl.BlockSpec(memory_space=pltpu.SMEM),
        _const_spec(w_out.shape),
    ]
    args = [xc, xc, mod, mod, norm_g.reshape(1, D_MODEL), w_in, conv_w, conv_b.reshape(1, CONV_WIDTH),
            qk_gains, sink, w_out]
    per_chunk = [
        ((TM, CONV_WIDTH), F32),
        ((TM, CONV_WIDTH), F32),
        ((TM, ATTN_WIDTH), BF16),
        ((TM, 2 * GROUP_W), BF16),
        ((2 * LANES, TM), BF16),
        ((TM, ATTN_WIDTH), F32),
    ]
    scratch = (
        [pltpu.VMEM((TM, D_MODEL), BF16)] * 2
        + [pltpu.VMEM((ring,) + shape, dt) for shape, dt in per_chunk]
        + [pltpu.VMEM(shape, dt) for shape, dt in per_chunk]
        + [pltpu.VMEM((TM, D_MODEL), BF16)])
    y_shape = jax.ShapeDtypeStruct(xc.shape, F32)
    y_spec = pl.BlockSpec((1, TM, D_MODEL), lambda c: (sp["fin_chunk"](c), 0, 0))
    if windowed:
        cos, sin = rope
        ck, cv = ctx
        rope_spec = pl.BlockSpec((TM, LANES), lambda c: (sp["proj_chunk"](c) % nc, 0))
        ctx_spec = pl.BlockSpec((1,) + ck.shape[1:], lambda c: (sp["fin_chunk"](c) // nc, 0, 0))
        in_specs += [rope_spec, rope_spec, ctx_spec, ctx_spec]
        args += [cos, sin, ck, cv]
        scratch += [pltpu.VMEM((ck.shape[1], 2 * GROUP_W), BF16), pltpu.VMEM((2 * LANES, cv.shape[1]), BF16)]
        out_shape, out_specs = y_shape, y_spec
    else:
        kv_shape = jax.ShapeDtypeStruct((n_chunks, TM, KV_WIDTH), F32)
        kv_spec = pl.BlockSpec((1, TM, KV_WIDTH), lambda c: (sp["proj_chunk"](c), 0, 0))
        out_shape = (y_shape, kv_shape, kv_shape,
                     jax.ShapeDtypeStruct(_tiled_shape(w_in), BF16), jax.ShapeDtypeStruct(_tiled_shape(w_out), BF16))
        out_specs = (y_spec, kv_spec, kv_spec, _const_spec(_tiled_shape(w_in)), _const_spec(_tiled_shape(w_out)))
    out = pl.pallas_call(
        functools.partial(_even_kernel, nc=nc, n_chunks=n_chunks, lag=lag, ring=ring, windowed=windowed),
        out_shape=out_shape,
        grid=(n_chunks + lag + 1,),
        in_specs=in_specs,
        out_specs=out_specs,
        scratch_shapes=scratch,
        compiler_params=PIPELINE_PARAMS,
        name="even_latent" if windowed else "even_context",
    )(*args)
    if windowed:
        return out.reshape(x.shape)
    y, k, v, w_in_bf16, w_out_bf16 = out
    return (y.reshape(x.shape), k.reshape(n_seq, seq_len, KV_WIDTH), v.reshape(n_seq, seq_len, KV_WIDTH),
            w_in_bf16, w_out_bf16)


def _odd_kernel(*refs, nc, n_chunks, lag, ring, cast_weights):
    if cast_weights:
        (xp_ref, xa_ref, modp_ref, moda_ref, g_ref, win_f32, pw_f32, ps_ref, wout_f32,
         y_ref, win_ref, pw_ref, wout_ref, h_s, h_w, u_s, sg_s, u_w, sg_w) = refs
    else:
        (xp_ref, xa_ref, modp_ref, moda_ref, g_ref, win_ref, pw_ref, ps_ref, wout_ref,
         y_ref, h_s, h_w, u_s, sg_s, u_w, sg_w) = refs
    tm = h_s.shape[0]
    s = pl.program_id(0)
    i = s - 1 - lag
    ci = lax.rem(i, jnp.int32(nc))

    if cast_weights:
        @pl.when(s == 0)
        def _cast():
            _cast_columns(win_f32, win_ref)
            _cast_columns(wout_f32, wout_ref)
            for gi in range(pw_ref.shape[0]):
                pw_ref[gi] = pw_f32[gi].astype(BF16)

    def normalize(tokens):
        yield from _normalize_items(h_w, xp_ref, g_ref, modp_ref, tokens)

    def project():
        for n in range(D_MODEL // NT):
            yield
            cols = slice(n * NT, (n + 1) * NT)
            u = jnp.dot(h_s[...], win_ref[n], preferred_element_type=F32)
            u_w[:, cols] = u
        for n in range(D_MODEL // NT):
            yield
            cols = slice(n * NT, (n + 1) * NT)
            sg = _silu(jnp.dot(h_s[...], win_ref[D_MODEL // NT + n], preferred_element_type=F32))
            sg_w[:, cols] = sg

    def finish(slot, tokens):
        if nc > 1:
            prev_slot = lax.rem(slot + (ring - 1), jnp.int32(ring))
            next_slot = lax.rem(slot + 1, jnp.int32(ring))
            has_prev = ci > 0
            has_next = ci < nc - 1
        else:
            prev_slot = next_slot = has_prev = has_next = None
        t = ci * tm + lax.broadcasted_iota(jnp.int32, (tm, 1), 0)
        outs = []
        for gi, w in enumerate(POOL_SIZES):
            yield
            cols = slice(gi * POOL_GROUP, (gi + 1) * POOL_GROUP)
            ext = _with_halo(u_s, slot, cols, prev_slot, next_slot, has_prev, has_next)
            acc = ext
            span = 1
            while span < w:
                acc = acc + _rows_down(acc, span)
                span *= 2
            if w > 2:
                acc = _rows_up(acc, w // 2 - 1)
            cnt = (jnp.minimum(t + w // 2, nc * tm) - jnp.maximum(t - w // 2, 0)).astype(F32)
            while tokens:
                cnt = cnt + tokens.pop(0)[:, 0:1]
            pooled = acc[HALO:HALO + tm] / cnt - ext[HALO:HALO + tm]
            outs.append(jnp.dot(pooled.astype(BF16), pw_ref[gi], preferred_element_type=F32))
        yield
        y = (jnp.concatenate(outs, axis=1) * ps_ref[...] * sg_s[slot]).astype(BF16)
        yield from _out_proj_items(lambda: y, wout_ref, xa_ref, moda_ref, y_ref)

    _pipeline_steps(s, n_chunks, lag, ring, (h_w, h_s), ((u_w, u_s), (sg_w, sg_s)), ODD_ORDER,
                    normalize, project, finish)


def _odd_layer(x, mod, per_seq_mod, norm_g, w_in, pool_w, pool_scale, w_out):
    n_seq, seq_len, _ = x.shape
    tm = TM
    nc = seq_len // tm
    n_chunks = n_seq * nc
    lag, ring = (2, 4) if nc > 1 else (1, 2)
    sp = _chunk_specs(n_chunks, nc, lag, per_seq_mod, tm)
    xc = x.reshape(n_chunks, tm, D_MODEL)
    in_specs = [
        sp["x_norm"], sp["x_fin"], sp["mod_norm"], sp["mod_fin"],
        _const_spec((1, D_MODEL)),
        _const_spec(w_in.shape),
        _const_spec(pool_w.shape),
        _const_spec((1, D_MODEL)),
        _const_spec(w_out.shape),
    ]
    cast_weights = w_in.dtype != BF16
    out_shape = jax.ShapeDtypeStruct(xc.shape, F32)
    out_specs = pl.BlockSpec((1, tm, D_MODEL), lambda c: (sp["fin_chunk"](c), 0, 0))
    if cast_weights:
        shapes = (_tiled_shape(w_in), pool_w.shape, _tiled_shape(w_out))
        out_shape = (out_shape,) + tuple(jax.ShapeDtypeStruct(shape, BF16) for shape in shapes)
        out_specs = (out_specs,) + tuple(_const_spec(shape) for shape in shapes)
    out = pl.pallas_call(
        functools.partial(_odd_kernel, nc=nc, n_chunks=n_chunks, lag=lag, ring=ring, cast_weights=cast_weights),
        out_shape=out_shape,
        grid=(n_chunks + lag + 1,),
        in_specs=in_specs,
        out_specs=out_specs,
        scratch_shapes=[
            pltpu.VMEM((tm, D_MODEL), BF16),
            pltpu.VMEM((tm, D_MODEL), BF16),
            pltpu.VMEM((ring, tm, D_MODEL), F32),
            pltpu.VMEM((ring, tm, D_MODEL), F32),
            pltpu.VMEM((tm, D_MODEL), F32),
            pltpu.VMEM((tm, D_MODEL), F32),
        ],
        compiler_params=PIPELINE_PARAMS,
        name="odd_latent" if per_seq_mod else "odd_context",
    )(xc, xc, mod, mod, norm_g.reshape(1, D_MODEL), w_in, pool_w, pool_scale.reshape(1, D_MODEL), w_out)
    if cast_weights:
        y, *bf16_weights = out
        return y.reshape(x.shape), bf16_weights
    return out.reshape(x.shape)


def _rope_tables(seq_len):
    n_rows = seq_len // GRID_W
    row = np.repeat(np.arange(n_rows), GRID_W).astype(np.float32)
    col = np.tile(np.arange(GRID_W), n_rows).astype(np.float32)
    inv = (np.float32(ROPE_BASE) ** (-np.arange(ROPE_FREQS, dtype=np.float32) / np.float32(ROPE_FREQS))).astype(np.float32)
    ang = np.stack([row[:, None] * inv, col[:, None] * inv], axis=1)
    cos = np.broadcast_to(np.cos(ang)[:, :, None, :], (seq_len, 2, 2, ROPE_FREQS))
    sin = np.sin(ang)[:, :, None, :] * np.array([-1.0, 1.0], np.float32)[None, None, :, None]
    cos = cos.reshape(seq_len, HEAD_DIM).astype(np.float32)
    sin = sin.reshape(seq_len, HEAD_DIM).astype(np.float32)
    return jnp.asarray(np.tile(cos, (1, LANES // HEAD_DIM))), jnp.asarray(np.tile(sin, (1, LANES // HEAD_DIM)))


def kernel(x_prompt, x_sample, cache_k, cache_v, c, c_ctx, ada_w_e, ada_b_e, norm_g_e, w_in_e, conv_w, conv_b,
           q_norm_g, k_norm_g, sink, w_out_e, ada_w_o, ada_b_o, norm_g_o, w_in_o, pool_w, pool_scale, w_out_o):
    n_dec = x_sample.shape[0]
    depth = ada_w_e.shape[0] + ada_w_o.shape[0]
    assert ada_w_e.shape[0] == 1 and ada_w_o.shape[0] == 1 and n_dec + 1 <= 8
    mod_e, mod_o = _adaln(c_ctx, c, ada_w_e[0], ada_b_e[0], ada_w_o[0], ada_b_o[0])
    rope = _rope_tables(x_sample.shape[1])

    yp, ys = x_prompt, x_sample
    new_k, new_v = [], []
    for layer in range(depth):
        i = layer // 2
        if layer % 2 == 0:
            mod = mod_e
            gains = jnp.tile(jnp.stack([q_norm_g[i], k_norm_g[i]]), (1, LANES // HEAD_DIM))
            small = (conv_w[i], conv_b[i], gains, sink[i])
            yp, k, v, w_in, w_out = _even_layer(yp, mod, norm_g_e[i], w_in_e[i], *small, w_out_e[i])
            new_k.append(k.reshape(k.shape[0], k.shape[1], N_KV_HEADS, HEAD_DIM))
            new_v.append(v.reshape(v.shape[0], v.shape[1], N_KV_HEADS, HEAD_DIM))
            ck = cache_k[:, i].reshape(n_dec, cache_k.shape[2], KV_WIDTH)
            cv = cache_v[:, i].reshape(n_dec, cache_v.shape[2], KV_WIDTH)
            ys = _even_layer(ys, mod, norm_g_e[i], w_in, *small, w_out, rope=rope, ctx=(ck, cv))
        else:
            mod = mod_o
            yp, (w_in, w_pool, w_out) = _odd_layer(yp, mod, False, norm_g_o[i], w_in_o[i], pool_w[i],
                                                   pool_scale[i], w_out_o[i])
            ys = _odd_layer(ys, mod, True, norm_g_o[i], w_in, w_pool, pool_scale[i], w_out)
    return yp, ys, jnp.stack(new_k, axis=1), jnp.stack(new_v, axis=1)
```

```python
import functools

import jax
import jax.numpy as jnp
import numpy as np
from jax import lax
from jax.experimental import pallas as pl
from jax.experimental.pallas import tpu as pltpu

F32 = jnp.float32
BF16 = jnp.bfloat16

D_MODEL = 1024
GRID_W = 64
HEAD_DIM = 64
HEAD_SHIFT = 6
ATTN_WIDTH = D_MODEL // 2
N_Q_HEADS = ATTN_WIDTH // HEAD_DIM
N_KV_HEADS = N_Q_HEADS // 4
Q_PER_KV = N_Q_HEADS // N_KV_HEADS
KV_WIDTH = N_KV_HEADS * HEAD_DIM
CONV_WIDTH = D_MODEL - ATTN_WIDTH
WINDOW = 128
BLOCK = 128
ROPE_BASE = 10000.0
ROPE_FREQS = HEAD_DIM // 4
ATTN_SCALE = HEAD_DIM ** -0.5
LOG2E = float(np.log2(np.e))
NEG = -1e30
POOL_SIZES = (2, 4, 8, 16)
POOL_GROUP = D_MODEL // len(POOL_SIZES)
EPS = 1e-6
EVEN_SIZES = (CONV_WIDTH, CONV_WIDTH, CONV_WIDTH, CONV_WIDTH, ATTN_WIDTH, KV_WIDTH, KV_WIDTH, ATTN_WIDTH)
EVEN_IN = sum(EVEN_SIZES)
EVEN_OFFS = tuple(int(s) for s in np.cumsum((0,) + EVEN_SIZES))

TM = 256
NT = 256
NORM_ROWS = 64
EVEN_ORDER_LATENT = "PF" + "F" + "PFNFPF" * 3 + "PNFPF" + "FFFF"
EVEN_ORDER_CONTEXT = "PF" + "F" + "FNPFPF" * 3 + "NPFPF" + "FFFF"
ODD_ORDER = "PNFP" * 4 + "FFFFF"
LANES = 128
GROUP_W = Q_PER_KV * HEAD_DIM
HALO = 8
ADA_TK = 256
STREAM_STEPS = 32
VMEM_LIMIT = 56 * 1024 * 1024
PIPELINE_PARAMS = pltpu.CompilerParams(dimension_semantics=("arbitrary",), vmem_limit_bytes=VMEM_LIMIT)


def _silu(x):
    half = 0.5 * x
    return half + half * jnp.tanh(half)


def _rows_down(x, k):
    return pltpu.roll(x, k, axis=0)


def _rows_up(x, k):
    return pltpu.roll(x, x.shape[0] - k, axis=0)


def _rms_modulate(x, gain, shift):
    ms = jnp.mean(x * x, axis=-1, keepdims=True)
    return x * lax.rsqrt(ms + EPS) * gain + shift


def _cast_columns(src_ref, dst_ref):
    for n in range(dst_ref.shape[0]):
        dst_ref[n] = src_ref[:, n * NT:(n + 1) * NT].astype(dst_ref.dtype)


def _tiled_shape(w):
    return (w.shape[1] // NT, w.shape[0], NT)


def _opaque_zero(x):
    bits = pltpu.bitcast(x, jnp.uint32)
    return pltpu.bitcast((bits >> 16) >> 16, F32)


def _normalize_items(h_s, x_ref, g_ref, mod_ref, tokens):
    shift = mod_ref[0, :, 0:D_MODEL]
    gain = g_ref[...] * (1.0 + mod_ref[0, :, D_MODEL:2 * D_MODEL])
    for r in range(0, h_s.shape[0], NORM_ROWS):
        yield
        h = _rms_modulate(x_ref[0, r:r + NORM_ROWS], gain, shift)
        h_s[r:r + NORM_ROWS] = h.astype(BF16)
        tokens.append(_opaque_zero(h[0:8, 0:LANES])[0:1])


def _out_proj_items(get_a, wout_ref, x_ref, mod_ref, y_ref):
    for n in range(D_MODEL // NT):
        yield
        cols = slice(n * NT, (n + 1) * NT)
        mixed = jnp.dot(get_a(), wout_ref[n], preferred_element_type=F32)
        y_ref[0, :, cols] = x_ref[0, :, cols] + mod_ref[0, :, 2 * D_MODEL + n * NT:2 * D_MODEL + (n + 1) * NT] * mixed


def _weave(order, **stages):
    by_letter = {name[0].upper(): stage for name, stage in stages.items()}
    for stage in by_letter.values():
        next(stage, None)
    for letter in order:
        if letter in by_letter:
            next(by_letter[letter], None)
    for stage in by_letter.values():
        for _ in stage:
            pass


def _with_halo(u_s, slot, cols, prev_slot, next_slot, has_prev, has_next):
    u = u_s[slot, :, cols]
    zero = jnp.zeros((HALO, u.shape[1]), F32)
    if prev_slot is None:
        top = bot = zero
    else:
        rows = u_s.shape[1]
        top = jnp.where(has_prev, u_s[prev_slot, rows - HALO:rows, cols], zero)
        bot = jnp.where(has_next, u_s[next_slot, 0:HALO, cols], zero)
    return jnp.concatenate([top, u, bot], axis=0)


def _pipeline_steps(s, n_chunks, lag, ring, normalized, staged, order, normalize, project, finish):
    fin_slot = lax.rem(s + (2 * ring - 1 - lag), jnp.int32(ring))

    @pl.when((s >= 1) & (s <= n_chunks))
    def _publish_normalized():
        h_w, h_s = normalized
        h_s[...] = h_w[...]

    @pl.when((s >= 2) & (s <= n_chunks + 1))
    def _publish_projected():
        slot = lax.rem(s + (ring - 2), jnp.int32(ring))
        for stage_ref, ring_ref in staged:
            ring_ref[slot] = stage_ref[...]

    @pl.when(s == 0)
    def _first():
        _weave(order, normalize=normalize([]))

    @pl.when((s >= 1) & (s <= lag))
    def _fill():
        _weave(order, project=project(), normalize=normalize([]))

    @pl.when((s > lag) & (s <= n_chunks))
    def _steady():
        tokens = []
        _weave(order, finish=finish(fin_slot, tokens), project=project(), normalize=normalize(tokens))

    @pl.when(s > n_chunks)
    def _drain():
        _weave(order, finish=finish(fin_slot, []))


def _adaln_kernel(cctx_ref, c_ref, we_ref, be_ref, wo_ref, bo_ref, oe_ref, oo_ref, cond_s, acc_e, acc_o):
    k = pl.program_id(0)

    @pl.when(k == 0)
    def _init():
        n_dec = c_ref.shape[0]
        cond_s[...] = jnp.zeros(cond_s.shape, F32)
        cond_s[0:1] = cctx_ref[...]
        cond_s[1:1 + n_dec] = c_ref[...]
        acc_e[...] = jnp.broadcast_to(be_ref[...], acc_e.shape)
        acc_o[...] = jnp.broadcast_to(bo_ref[...], acc_o.shape)

    s = _silu(cond_s[:, pl.ds(pl.multiple_of(k * ADA_TK, ADA_TK), ADA_TK)]).astype(BF16)
    acc_e[...] += jnp.dot(s, we_ref[...].astype(BF16), preferred_element_type=F32)
    acc_o[...] += jnp.dot(s, wo_ref[...].astype(BF16), preferred_element_type=F32)

    @pl.when(k == pl.num_programs(0) - 1)
    def _write():
        for r in range(oe_ref.shape[0]):
            oe_ref[r] = acc_e[r:r + 1, :]
            oo_ref[r] = acc_o[r:r + 1, :]


def _adaln(c_ctx, c, w_e, b_e, w_o, b_o):
    n = 3 * D_MODEL
    n_dec = c.shape[0]
    wspec = pl.BlockSpec((ADA_TK, n), lambda k: (k, 0))
    vspec = pl.BlockSpec((1, n), lambda k: (0, 0))
    ospec = pl.BlockSpec((8, 1, n), lambda k: (0, 0, 0))
    return pl.pallas_call(
        _adaln_kernel,
        out_shape=(jax.ShapeDtypeStruct((8, 1, n), F32), jax.ShapeDtypeStruct((8, 1, n), F32)),
        grid=(D_MODEL // ADA_TK,),
        in_specs=[pl.BlockSpec((1, D_MODEL), lambda k: (0, 0)), pl.BlockSpec((n_dec, D_MODEL), lambda k: (0, 0)),
                  wspec, vspec, wspec, vspec],
        out_specs=(ospec, ospec),
        scratch_shapes=[pltpu.VMEM((8, D_MODEL), F32), pltpu.VMEM((8, n), F32), pltpu.VMEM((8, n), F32)],
        compiler_params=pltpu.CompilerParams(dimension_semantics=("arbitrary",), vmem_limit_bytes=VMEM_LIMIT),
        name="adaln",
    )(c_ctx.reshape(1, D_MODEL), c, w_e, b_e.reshape(1, n), w_o, b_o.reshape(1, n))


def _head_inv_rms(x, bd):
    ms = jnp.dot((x * x).astype(BF16), bd, preferred_element_type=F32) * (1.0 / HEAD_DIM)
    return lax.rsqrt(ms + EPS)


def _rope(x, cos, sin_signed, first_half):
    partner = jnp.where(first_half, pltpu.roll(x, LANES - ROPE_FREQS, axis=1), pltpu.roll(x, ROPE_FREQS, axis=1))
    return x * cos + partner * sin_signed


def _dup_kv(x):
    lane = lax.broadcasted_iota(jnp.int32, (1, LANES), 1)
    swapped = pltpu.roll(x, HEAD_DIM, axis=1)
    return jnp.where(lane < HEAD_DIM, x, swapped), jnp.where(lane < HEAD_DIM, swapped, x)


def _tiled_keys(k):
    ka, kb = _dup_kv(k)
    return jnp.concatenate([ka, ka, kb, kb], axis=1).astype(BF16)


def _values_t(v):
    va, vb = _dup_kv(v)
    return jnp.concatenate([va.T, vb.T], axis=0).astype(BF16)


def _even_kernel(*refs, nc, n_chunks, lag, ring, windowed, n_stream=0):
    if windowed:
        n_in = 15
        stream_src = refs[n_in:n_in + n_stream]
        stream_dst = refs[n_in + n_stream + 1:n_in + 2 * n_stream + 1]
        refs = refs[:n_in] + refs[n_in + n_stream:n_in + n_stream + 1] + refs[n_in + 2 * n_stream + 1:]
        (xp_ref, xa_ref, modp_ref, moda_ref, g_ref, win_ref, cw_ref, cb_ref, qkg_ref, sink_ref, wout_ref,
         cos_ref, sin_ref, ck_ref, cv_ref,
         y_ref,
         h_s, h_w, u_s, gate_s, q_s, k_s, vt_s, gb_s, u_w, gate_w, q_w, k_w, vt_w, gb_w, a_s, ck_s, cvt_s) = refs
        ko_ref = vo_ref = None
    else:
        (xp_ref, xa_ref, modp_ref, moda_ref, g_ref, win_f32, cw_ref, cb_ref, qkg_ref, sink_ref, wout_f32,
         y_ref, ko_ref, vo_ref, win_ref, wout_ref,
         h_s, h_w, u_s, gate_s, q_s, k_s, vt_s, gb_s, u_w, gate_w, q_w, k_w, vt_w, gb_w, a_s) = refs
    s = pl.program_id(0)
    i = s - 1 - lag
    ci = lax.rem(i, jnp.int32(nc))

    @pl.when(s == 0)
    def _init():
        k_s[...] = jnp.zeros(k_s.shape, BF16)
        vt_s[...] = jnp.zeros(vt_s.shape, BF16)
        if not windowed:
            _cast_columns(win_f32, win_ref)
            _cast_columns(wout_f32, wout_ref)

    if windowed and n_stream:
        @pl.when(s < STREAM_STEPS)
        def _cast_next():
            for src, dst in zip(stream_src, stream_dst):
                if len(src.shape) == 2:
                    _cast_columns(src, dst)
                else:
                    dst[...] = src[...].astype(BF16)

    if windowed:
        @pl.when((i >= 0) & (ci == 0))
        def _load_ctx():
            ck_s[...] = _tiled_keys(ck_ref[0])
            cvt_s[...] = _values_t(cv_ref[0])

    def normalize(tokens):
        yield from _normalize_items(h_w, xp_ref, g_ref, modp_ref, tokens)

    def project():
        o_bg, o_cg, o_xs, o_ga, o_q, o_k, _, o_gb = EVEN_OFFS[:8]

        def proj(start):
            return jnp.dot(h_s[...], win_ref[start // NT], preferred_element_type=F32)

        lane = lax.broadcasted_iota(jnp.int32, (1, LANES), 1)
        first_half = (lane & (2 * ROPE_FREQS - 1)) < ROPE_FREQS
        ri = lax.broadcasted_iota(jnp.int32, (NT, NT), 0) >> HEAD_SHIFT
        cj = lax.broadcasted_iota(jnp.int32, (NT, NT), 1) >> HEAD_SHIFT
        bd = (ri == cj).astype(BF16)

        def qk_norm(x, g):
            w = x.shape[1]
            inv = _head_inv_rms(x, bd[0:w, 0:w])
            pieces = []
            for j in range(w // LANES):
                lanes = slice(j * LANES, (j + 1) * LANES)
                xj = x[:, lanes] * inv[:, lanes] * g
                pieces.append(_rope(xj, cos_ref[...], sin_ref[...], first_half) if windowed else xj)
            return pieces

        def keys_values(_):
            kv = proj(o_k)
            k, = qk_norm(kv[:, 0:KV_WIDTH], qkg_ref[1:2, :])
            v = kv[:, KV_WIDTH:2 * KV_WIDTH]
            if not windowed:
                ko_ref[0] = k
                vo_ref[0] = v
            k_w[...] = _tiled_keys(k)
            vt_w[...] = _values_t(v)

        def queries(n):
            for j, qj in enumerate(qk_norm(proj(o_q + n * NT), qkg_ref[0:1, :])):
                q_w[:, n * NT + j * LANES:n * NT + (j + 1) * LANES] = (qj * (ATTN_SCALE * LOG2E)).astype(BF16)

        def conv_gate(n):
            gate_w[:, n * NT:(n + 1) * NT] = proj(o_bg + n * NT) * _silu(proj(o_ga + n * NT))

        def conv_input(n):
            u_w[:, n * NT:(n + 1) * NT] = proj(o_cg + n * NT) * proj(o_xs + n * NT)

        def attn_gate(n):
            gb_w[:, n * NT:(n + 1) * NT] = _silu(proj(o_gb + n * NT))

        items = [(conv_gate, 0), (keys_values, 0), (conv_input, 0), (queries, 0), (conv_gate, 1), (queries, 1),
                 (conv_input, 1), (attn_gate, 0), (attn_gate, 1)]
        for emit, n in items:
            yield
            emit(n)

    def finish(slot, tokens):
        if windowed:
            prev_slot = lax.rem(slot + (ring - 1), jnp.int32(ring))
            next_slot = lax.rem(slot + 1, jnp.int32(ring))
            has_prev = ci > 0
            has_next = ci < nc - 1
        else:
            prev_slot = next_slot = has_prev = has_next = None

        yield
        ext = _with_halo(u_s, slot, slice(None), prev_slot, next_slot, has_prev, has_next)
        conv = (_rows_down(ext, 1)[HALO:HALO + TM] * cw_ref[0:1, :]
                + ext[HALO:HALO + TM] * cw_ref[1:2, :]
                + _rows_up(ext, 1)[HALO:HALO + TM] * cw_ref[2:3, :]
                + cb_ref[...])
        a_s[:, 0:CONV_WIDTH] = (gate_s[slot] * conv).astype(BF16)

        lane_head = lax.broadcasted_iota(jnp.int32, (1, GROUP_W), 1) >> HEAD_SHIFT
        cols4 = Q_PER_KV * BLOCK
        key_i =lax.broadcasted_iota(jnp.int32, (BLOCK, cols4), 0)
        qry_i = lax.broadcasted_iota(jnp.int32, (BLOCK, cols4), 1) & (BLOCK - 1)
        upper = lax.broadcasted_iota(jnp.int32, (LANES, LANES), 0) < HEAD_DIM
        halves = (slice(0, BLOCK), slice(BLOCK, TM))
        @functools.lru_cache(maxsize=None)
        def block_masks(qb):
            if qb == 0:
                blocks = ((prev_slot, halves[1]), (slot, halves[0]), (slot, halves[1]))
                prev_in, next_in = has_prev, True
            else:
                blocks = ((slot, halves[0]), (slot, halves[1]), (next_slot, halves[0]))
                prev_in, next_in = True, has_next
            prev_ok = (key_i - qry_i) >= jnp.where(prev_in, 0, BLOCK)
            next_ok = (qry_i - key_i) >= jnp.where(next_in, 0, BLOCK)
            return blocks, prev_ok, next_ok

        def scores(qb, kv):
            cols = slice(kv * GROUP_W, (kv + 1) * GROUP_W)
            vrows = slice(kv * LANES, (kv + 1) * LANES)
            qblk = q_s[slot, halves[qb], cols]
            q4 = jnp.concatenate(
                [jnp.where(lane_head == g, qblk, jnp.zeros_like(qblk)) for g in range(Q_PER_KV)], axis=0)

            def scores_t(kk):
                return lax.dot_general(kk, q4, (((1,), (1,)), ((), ())), preferred_element_type=F32)

            if not windowed:
                return [([scores_t(k_s[slot, :, cols])], vt_s[slot, vrows, :])]
            blocks, prev_ok, next_ok = block_masks(qb)
            s_loc = scores_t(jnp.concatenate([k_s[sl, r, cols] for sl, r in blocks], axis=0))
            vt_loc = jnp.concatenate([vt_s[sl, vrows, r] for sl, r in blocks], axis=1)
            s_ctx = scores_t(ck_s[:, cols])
            return [([jnp.where(prev_ok, s_loc[0:BLOCK], NEG), s_loc[BLOCK:2 * BLOCK],
                      jnp.where(next_ok, s_loc[2 * BLOCK:3 * BLOCK], NEG)], vt_loc),
                    ([s_ctx], cvt_s[vrows, :])]

        def softmax(kv, groups):
            snk = jnp.concatenate(
                [jnp.full((1, BLOCK), sink_ref[kv * Q_PER_KV + g] * LOG2E, F32) for g in range(Q_PER_KV)], axis=1)
            if tokens:
                token = tokens.pop(0)
                while tokens:
                    token = token + tokens.pop(0)
                snk = snk + jnp.concatenate([token] * Q_PER_KV, axis=1)
            mx = snk
            for ss, _ in groups:
                for sc in ss:
                    mx = jnp.maximum(mx, jnp.max(sc, axis=0, keepdims=True))
            den = jnp.exp2(snk - mx)
            probs = []
            for ss, vt in groups:
                es = []
                for sc in ss:
                    e = jnp.exp2(sc - mx)
                    den = den + jnp.sum(e, axis=0, keepdims=True)
                    es.append(e.astype(BF16))
                probs.append((es[0] if len(es) == 1 else jnp.concatenate(es, axis=0), vt))
            return probs, den

        def values(qb, kv, probs, den):
            o_t = jnp.zeros((LANES, cols4), F32)
            for e_all, vt in probs:
                o_t = o_t + jnp.dot(vt, e_all, preferred_element_type=F32)
            o_t = o_t * (1.0 / den)
            o01 = jnp.where(upper, o_t[:, 0:BLOCK], o_t[:, BLOCK:2 * BLOCK]).T
            o23 = jnp.where(upper, o_t[:, 2 * BLOCK:3 * BLOCK], o_t[:, 3 * BLOCK:4 * BLOCK]).T
            a_s[halves[qb], CONV_WIDTH + kv * GROUP_W:CONV_WIDTH + (kv + 1) * GROUP_W] = (
                jnp.concatenate([o01, o23], axis=1) * gb_s[slot, halves[qb], kv * GROUP_W:(kv + 1) * GROUP_W]
            ).astype(BF16)

        pairs = [(qb, kv) for qb in range(TM // BLOCK) for kv in range(N_KV_HEADS)]
        yield
        ahead = scores(*pairs[0])
        for t, (qb, kv) in enumerate(pairs):
            groups = ahead
            if t + 1 < len(pairs):
                yield
                ahead = scores(*pairs[t + 1])
            yield
            probs, den = softmax(kv, groups)
            yield
            values(qb, kv, probs, den)

        yield from _out_proj_items(lambda: a_s[...], wout_ref, xa_ref, moda_ref, y_ref)

    staged = ((u_w, u_s), (gate_w, gate_s), (q_w, q_s), (k_w, k_s), (vt_w, vt_s), (gb_w, gb_s))
    order = EVEN_ORDER_LATENT if windowed else EVEN_ORDER_CONTEXT
    _pipeline_steps(s, n_chunks, lag, ring, (h_w, h_s), staged, order, normalize, project, finish)


def _chunk_specs(n_chunks, nc, lag, per_seq_mod, tm=TM):
    last = n_chunks - 1
    norm_chunk = lambda s: jnp.minimum(s, last)
    proj_chunk = lambda s: jnp.clip(s - 1, 0, last)
    fin_chunk = lambda s: jnp.clip(s - 1 - lag, 0, last)
    seq_of = (lambda j: 1 + j // nc) if per_seq_mod else (lambda j: 0)
    return dict(
        x_norm=pl.BlockSpec((1, tm, D_MODEL), lambda s: (norm_chunk(s), 0, 0)),
        x_fin=pl.BlockSpec((1, tm, D_MODEL), lambda s: (fin_chunk(s), 0, 0)),
        mod_norm=pl.BlockSpec((1, 1, 3 * D_MODEL), lambda s: (seq_of(norm_chunk(s)), 0, 0)),
        mod_fin=pl.BlockSpec((1, 1, 3 * D_MODEL), lambda s: (seq_of(fin_chunk(s)), 0, 0)),
        proj_chunk=proj_chunk, fin_chunk=fin_chunk)


def _const_spec(shape):
    return pl.BlockSpec(shape, lambda c: (0,) * len(shape))


def _stream_specs(w):
    if w.ndim == 3:
        last = w.shape[0] - 1
        spec = pl.BlockSpec((1,) + w.shape[1:], lambda c: (jnp.minimum(c, last), 0, 0))
        return spec, spec, jax.ShapeDtypeStruct(w.shape, BF16)
    rows = w.shape[0] // STREAM_STEPS
    assert rows % 16 == 0 and rows * STREAM_STEPS == w.shape[0]
    tiled = _tiled_shape(w)
    return (pl.BlockSpec((rows, w.shape[1]), lambda c: (jnp.minimum(c, STREAM_STEPS - 1), 0)),
            pl.BlockSpec((tiled[0], rows, NT), lambda c: (0, jnp.minimum(c, STREAM_STEPS - 1), 0)),
            jax.ShapeDtypeStruct(tiled, BF16))


def _even_layer(x, mod, norm_g, w_in, conv_w, conv_b, qk_gains, sink, w_out, rope=None, ctx=None, next_weights=()):
    n_seq, seq_len, _ = x.shape
    windowed = rope is not None
    nc = seq_len // TM
    n_chunks = n_seq * nc
    lag, ring = (2, 4) if windowed else (1, 2)
    assert windowed or nc == 1
    sp = _chunk_specs(n_chunks, nc, lag, windowed)
    xc = x.reshape(n_chunks, TM, D_MODEL)
    in_specs = [
        sp["x_norm"], sp["x_fin"], sp["mod_norm"], sp["mod_fin"],
        _const_spec((1, D_MODEL)),
        _const_spec(w_in.shape),
        _const_spec((3, CONV_WIDTH)),
        _const_spec((1, CONV_WIDTH)),
        _const_spec((2, LANES)),
        pl.BlockSpec(memory_space=pltpu.SMEM),
        _const_spec(w_out.shape),
    ]
    args = [xc, xc, mod, mod, norm_g.reshape(1, D_MODEL), w_in, conv_w, conv_b.reshape(1, CONV_WIDTH),
            qk_gains, sink, w_out]
    per_chunk = [
        ((TM, CONV_WIDTH), F32),
        ((TM, CONV_WIDTH), F32),
        ((TM, ATTN_WIDTH), BF16),
        ((TM, 2 * GROUP_W), BF16),
        ((2 * LANES, TM), BF16),
        ((TM, ATTN_WIDTH), F32),
    ]
    scratch = (
        [pltpu.VMEM((TM, D_MODEL), BF16)] * 2
        + [pltpu.VMEM((ring,) + shape, dt) for shape, dt in per_chunk]
        + [pltpu.VMEM(shape, dt) for shape, dt in per_chunk]
        + [pltpu.VMEM((TM, D_MODEL), BF16)])
    y_shape = jax.ShapeDtypeStruct(xc.shape, F32)
    y_spec = pl.BlockSpec((1, TM, D_MODEL), lambda c: (sp["fin_chunk"](c), 0, 0))
    if windowed:
        cos, sin = rope
        ck, cv = ctx
        rope_spec = pl.BlockSpec((TM, LANES), lambda c: (sp["proj_chunk"](c) % nc, 0))
        ctx_spec = pl.BlockSpec((1,) + ck.shape[1:], lambda c: (sp["fin_chunk"](c) // nc, 0, 0))
        in_specs += [rope_spec, rope_spec, ctx_spec, ctx_spec]
        args += [cos, sin, ck, cv]
        scratch += [pltpu.VMEM((ck.shape[1], 2 * GROUP_W), BF16), pltpu.VMEM((2 * LANES, cv.shape[1]), BF16)]
        out_shape, out_specs = y_shape, y_spec
        if next_weights:
            assert n_chunks + lag + 1 >= STREAM_STEPS
            streams = [_stream_specs(w) for w in next_weights]
            in_specs += [st[0] for st in streams]
            args += list(next_weights)
            out_shape = (y_shape,) + tuple(st[2] for st in streams)
            out_specs = (y_spec,) + tuple(st[1] for st in streams)
    else:
        kv_shape = jax.ShapeDtypeStruct((n_chunks, TM, KV_WIDTH), F32)
        kv_spec = pl.BlockSpec((1, TM, KV_WIDTH), lambda c: (sp["proj_chunk"](c), 0, 0))
        out_shape = (y_shape, kv_shape, kv_shape,
                     jax.ShapeDtypeStruct(_tiled_shape(w_in), BF16), jax.ShapeDtypeStruct(_tiled_shape(w_out), BF16))
        out_specs = (y_spec, kv_spec, kv_spec, _const_spec(_tiled_shape(w_in)), _const_spec(_tiled_shape(w_out)))
    out = pl.pallas_call(
        functools.partial(_even_kernel, nc=nc, n_chunks=n_chunks, lag=lag, ring=ring, windowed=windowed,
                          n_stream=len(next_weights)),
        out_shape=out_shape,
        grid=(n_chunks + lag + 1,),
        in_specs=in_specs,
        out_specs=out_specs,
        scratch_shapes=scratch,
        compiler_params=PIPELINE_PARAMS,
        name="even_latent" if windowed else "even_context",
    )(*args)
    if windowed and next_weights:
        y, *next_bf16 = out
        return y.reshape(x.shape), next_bf16
    if windowed:
        return out.reshape(x.shape)
    y, k, v, w_in_bf16, w_out_bf16 = out
    return (y.reshape(x.shape), k.reshape(n_seq, seq_len, KV_WIDTH), v.reshape(n_seq, seq_len, KV_WIDTH),
            w_in_bf16, w_out_bf16)


def _odd_kernel(*refs, nc, n_chunks, lag, ring, cast_weights):
    if cast_weights:
        (xp_ref, xa_ref, modp_ref, moda_ref, g_ref, win_f32, pw_f32, ps_ref, wout_f32,
         y_ref, win_ref, pw_ref, wout_ref, h_s, h_w, u_s, sg_s, u_w, sg_w) = refs
    else:
        (xp_ref, xa_ref, modp_ref, moda_ref, g_ref, win_ref, pw_ref, ps_ref, wout_ref,
         y_ref, h_s, h_w, u_s, sg_s, u_w, sg_w) = refs
    tm = h_s.shape[0]
    s = pl.program_id(0)
    i = s - 1 - lag
    ci = lax.rem(i, jnp.int32(nc))

    if cast_weights:
        @pl.when(s == 0)
        def _cast():
            _cast_columns(win_f32, win_ref)
            _cast_columns(wout_f32, wout_ref)
            for gi in range(pw_ref.shape[0]):
                pw_ref[gi] = pw_f32[gi].astype(BF16)

    def normalize(tokens):
        yield from _normalize_items(h_w, xp_ref, g_ref, modp_ref, tokens)

    def project():
        for n in range(D_MODEL // NT):
            yield
            cols = slice(n * NT, (n + 1) * NT)
            u = jnp.dot(h_s[...], win_ref[n], preferred_element_type=F32)
            u_w[:, cols] = u
        for n in range(D_MODEL // NT):
            yield
            cols = slice(n * NT, (n + 1) * NT)
            sg = _silu(jnp.dot(h_s[...], win_ref[D_MODEL // NT + n], preferred_element_type=F32))
            sg_w[:, cols] = sg

    def finish(slot, tokens):
        if nc > 1:
            prev_slot = lax.rem(slot + (ring - 1), jnp.int32(ring))
            next_slot = lax.rem(slot + 1, jnp.int32(ring))
            has_prev = ci > 0
            has_next = ci < nc - 1
        else:
            prev_slot = next_slot = has_prev = has_next = None
        t = ci * tm + lax.broadcasted_iota(jnp.int32, (tm, 1), 0)
        outs = []
        for gi, w in enumerate(POOL_SIZES):
            yield
            cols = slice(gi * POOL_GROUP, (gi + 1) * POOL_GROUP)
            ext = _with_halo(u_s, slot, cols, prev_slot, next_slot, has_prev, has_next)
            acc = ext
            span = 1
            while span < w:
                acc = acc + _rows_down(acc, span)
                span *= 2
            if w > 2:
                acc = _rows_up(acc, w // 2 - 1)
            cnt = (jnp.minimum(t + w // 2, nc * tm) - jnp.maximum(t - w // 2, 0)).astype(F32)
            while tokens:
                cnt = cnt + tokens.pop(0)[:, 0:1]
            pooled = acc[HALO:HALO + tm] / cnt - ext[HALO:HALO + tm]
            outs.append(jnp.dot(pooled.astype(BF16), pw_ref[gi], preferred_element_type=F32))
        yield
        y = (jnp.concatenate(outs, axis=1) * ps_ref[...] * sg_s[slot]).astype(BF16)
        yield from _out_proj_items(lambda: y, wout_ref, xa_ref, moda_ref, y_ref)

    _pipeline_steps(s, n_chunks, lag, ring, (h_w, h_s), ((u_w, u_s), (sg_w, sg_s)), ODD_ORDER,
                    normalize, project, finish)


def _odd_layer(x, mod, per_seq_mod, norm_g, w_in, pool_w, pool_scale, w_out):
    n_seq, seq_len, _ = x.shape
    tm = TM
    nc = seq_len // tm
    n_chunks = n_seq * nc
    lag, ring = (2, 4) if nc > 1 else (1, 2)
    sp = _chunk_specs(n_chunks, nc, lag, per_seq_mod, tm)
    xc = x.reshape(n_chunks, tm, D_MODEL)
    in_specs = [
        sp["x_norm"], sp["x_fin"], sp["mod_norm"], sp["mod_fin"],
        _const_spec((1, D_MODEL)),
        _const_spec(w_in.shape),
        _const_spec(pool_w.shape),
        _const_spec((1, D_MODEL)),
        _const_spec(w_out.shape),
    ]
    cast_weights = w_in.dtype != BF16
    out_shape = jax.ShapeDtypeStruct(xc.shape, F32)
    out_specs = pl.BlockSpec((1, tm, D_MODEL), lambda c: (sp["fin_chunk"](c), 0, 0))
    if cast_weights:
        shapes = (_tiled_shape(w_in), pool_w.shape, _tiled_shape(w_out))
        out_shape = (out_shape,) + tuple(jax.ShapeDtypeStruct(shape, BF16) for shape in shapes)
        out_specs = (out_specs,) + tuple(_const_spec(shape) for shape in shapes)
    out = pl.pallas_call(
        functools.partial(_odd_kernel, nc=nc, n_chunks=n_chunks, lag=lag, ring=ring, cast_weights=cast_weights),
        out_shape=out_shape,
        grid=(n_chunks + lag + 1,),
        in_specs=in_specs,
        out_specs=out_specs,
        scratch_shapes=[
            pltpu.VMEM((tm, D_MODEL), BF16),
            pltpu.VMEM((tm, D_MODEL), BF16),
            pltpu.VMEM((ring, tm, D_MODEL), F32),
            pltpu.VMEM((ring, tm, D_MODEL), F32),
            pltpu.VMEM((tm, D_MODEL), F32),
            pltpu.VMEM((tm, D_MODEL), F32),
        ],
        compiler_params=PIPELINE_PARAMS,
        name="odd_latent" if per_seq_mod else "odd_context",
    )(xc, xc, mod, mod, norm_g.reshape(1, D_MODEL), w_in, pool_w, pool_scale.reshape(1, D_MODEL), w_out)
    if cast_weights:
        y, *bf16_weights = out
        return y.reshape(x.shape), bf16_weights
    return out.reshape(x.shape)


def _rope_tables(seq_len):
    n_rows = seq_len // GRID_W
    row = np.repeat(np.arange(n_rows), GRID_W).astype(np.float32)
    col = np.tile(np.arange(GRID_W), n_rows).astype(np.float32)
    inv = (np.float32(ROPE_BASE) ** (-np.arange(ROPE_FREQS, dtype=np.float32) / np.float32(ROPE_FREQS))).astype(np.float32)
    ang = np.stack([row[:, None] * inv, col[:, None] * inv], axis=1)
    cos = np.broadcast_to(np.cos(ang)[:, :, None, :], (seq_len, 2, 2, ROPE_FREQS))
    sin = np.sin(ang)[:, :, None, :] * np.array([-1.0, 1.0], np.float32)[None, None, :, None]
    cos = cos.reshape(seq_len, HEAD_DIM).astype(np.float32)
    sin = sin.reshape(seq_len, HEAD_DIM).astype(np.float32)
    return jnp.asarray(np.tile(cos, (1, LANES // HEAD_DIM))), jnp.asarray(np.tile(sin, (1, LANES // HEAD_DIM)))


def kernel(x_prompt, x_sample, cache_k, cache_v, c, c_ctx, ada_w_e, ada_b_e, norm_g_e, w_in_e, conv_w, conv_b,
           q_norm_g, k_norm_g, sink, w_out_e, ada_w_o, ada_b_o, norm_g_o, w_in_o, pool_w, pool_scale, w_out_o):
    n_dec = x_sample.shape[0]
    depth = ada_w_e.shape[0] + ada_w_o.shape[0]
    assert ada_w_e.shape[0] == 1 and ada_w_o.shape[0] == 1 and n_dec + 1 <= 8
    mod_e, mod_o = _adaln(c_ctx, c, ada_w_e[0], ada_b_e[0], ada_w_o[0], ada_b_o[0])
    rope = _rope_tables(x_sample.shape[1])

    yp, ys = x_prompt, x_sample
    new_k, new_v = [], []
    for layer in range(depth):
        i = layer // 2
        if layer % 2 == 0:
            mod = mod_e
            gains = jnp.tile(jnp.stack([q_norm_g[i], k_norm_g[i]]), (1, LANES // HEAD_DIM))
            small = (conv_w[i], conv_b[i], gains, sink[i])
            yp, k, v, w_in, w_out = _even_layer(yp, mod, norm_g_e[i], w_in_e[i], *small, w_out_e[i])
            new_k.append(k.reshape(k.shape[0], k.shape[1], N_KV_HEADS, HEAD_DIM))
            new_v.append(v.reshape(v.shape[0], v.shape[1], N_KV_HEADS, HEAD_DIM))
            ck = cache_k[:, i].reshape(n_dec, cache_k.shape[2], KV_WIDTH)
            cv = cache_v[:, i].reshape(n_dec, cache_v.shape[2], KV_WIDTH)
            following = (w_in_o[i], pool_w[i], w_out_o[i]) if layer + 1 < depth else ()
            ys = _even_layer(ys, mod, norm_g_e[i], w_in, *small, w_out, rope=rope, ctx=(ck, cv), next_weights=following)
            if following:
                ys, odd_bf16 = ys
        else:
            mod = mod_o
            if layer > 0:
                w_in, w_pool, w_out = odd_bf16
                yp = _odd_layer(yp, mod, False, norm_g_o[i], w_in, w_pool, pool_scale[i], w_out)
            else:
                yp, (w_in, w_pool, w_out) = _odd_layer(yp, mod, False, norm_g_o[i], w_in_o[i], pool_w[i],
                                                       pool_scale[i], w_out_o[i])
            ys = _odd_layer(ys, mod, True, norm_g_o[i], w_in, w_pool, pool_scale[i], w_out)
    return yp, ys, jnp.stack(new_k, axis=1), jnp.stack(new_v, axis=1)
```

```python
import functools

import jax
import jax.numpy as jnp
import numpy as np
from jax import lax
from jax.experimental import pallas as pl
from jax.experimental.pallas import tpu as pltpu

F32 = jnp.float32
BF16 = jnp.bfloat16

D_MODEL = 1024
GRID_W = 64
HEAD_DIM = 64
HEAD_SHIFT = 6
ATTN_WIDTH = D_MODEL // 2
N_Q_HEADS = ATTN_WIDTH // HEAD_DIM
N_KV_HEADS = N_Q_HEADS // 4
Q_PER_KV = N_Q_HEADS // N_KV_HEADS
KV_WIDTH = N_KV_HEADS * HEAD_DIM
CONV_WIDTH = D_MODEL - ATTN_WIDTH
WINDOW = 128
BLOCK = 128
ROPE_BASE = 10000.0
ROPE_FREQS = HEAD_DIM // 4
ATTN_SCALE = HEAD_DIM ** -0.5
LOG2E = float(np.log2(np.e))
NEG = -1e30
POOL_SIZES = (2, 4, 8, 16)
POOL_GROUP = D_MODEL // len(POOL_SIZES)
EPS = 1e-6
EVEN_SIZES = (CONV_WIDTH, CONV_WIDTH, CONV_WIDTH, CONV_WIDTH, ATTN_WIDTH, KV_WIDTH, KV_WIDTH, ATTN_WIDTH)
EVEN_IN = sum(EVEN_SIZES)
EVEN_OFFS = tuple(int(s) for s in np.cumsum((0,) + EVEN_SIZES))

TM = 256
NT = 256
NORM_ROWS = 64
EVEN_ORDER_LATENT = "PF" + "F" + "PFNFPF" * 3 + "PNFPF" + "FFFF"
EVEN_ORDER_CONTEXT = "PF" + "F" + "FNPFPF" * 3 + "NPFPF" + "FFFF"
ODD_ORDER = "PNFP" * 4 + "FFFFF"
LANES = 128
GROUP_W = Q_PER_KV * HEAD_DIM
HALO = 8
ADA_TK = 256
STREAM_STEPS = 16
VMEM_LIMIT = 56 * 1024 * 1024
PIPELINE_PARAMS = pltpu.CompilerParams(dimension_semantics=("arbitrary",), vmem_limit_bytes=VMEM_LIMIT)


def _silu(x):
    half = 0.5 * x
    return half + half * jnp.tanh(half)


def _rows_down(x, k):
    return pltpu.roll(x, k, axis=0)


def _rows_up(x, k):
    return pltpu.roll(x, x.shape[0] - k, axis=0)


def _rms_modulate(x, gain, shift):
    ms = jnp.mean(x * x, axis=-1, keepdims=True)
    return x * lax.rsqrt(ms + EPS) * gain + shift


def _cast_columns(src_ref, dst_ref):
    for n in range(dst_ref.shape[0]):
        dst_ref[n] = src_ref[:, n * NT:(n + 1) * NT].astype(dst_ref.dtype)


def _tiled_shape(w):
    return (w.shape[1] // NT, w.shape[0], NT)


def _opaque_zero(x):
    bits = pltpu.bitcast(x, jnp.uint32)
    return pltpu.bitcast((bits >> 16) >> 16, F32)


def _normalize_items(h_s, x_ref, g_ref, mod_ref, tokens):
    shift = mod_ref[0, :, 0:D_MODEL]
    gain = g_ref[...] * (1.0 + mod_ref[0, :, D_MODEL:2 * D_MODEL])
    for r in range(0, h_s.shape[0], NORM_ROWS):
        yield
        h = _rms_modulate(x_ref[0, r:r + NORM_ROWS], gain, shift)
        h_s[r:r + NORM_ROWS] = h.astype(BF16)
        tokens.append(_opaque_zero(h[0:8, 0:LANES])[0:1])


def _out_proj_items(get_a, wout_ref, x_ref, mod_ref, y_ref):
    for n in range(D_MODEL // NT):
        yield
        cols = slice(n * NT, (n + 1) * NT)
        mixed = jnp.dot(get_a(), wout_ref[n], preferred_element_type=F32)
        y_ref[0, :, cols] = x_ref[0, :, cols] + mod_ref[0, :, 2 * D_MODEL + n * NT:2 * D_MODEL + (n + 1) * NT] * mixed


def _weave(order, **stages):
    by_letter = {name[0].upper(): stage for name, stage in stages.items()}
    for stage in by_letter.values():
        next(stage, None)
    for letter in order:
        if letter in by_letter:
            next(by_letter[letter], None)
    for stage in by_letter.values():
        for _ in stage:
            pass


def _with_halo(u_s, slot, cols, prev_slot, next_slot, has_prev, has_next):
    u = u_s[slot, :, cols]
    zero = jnp.zeros((HALO, u.shape[1]), F32)
    if prev_slot is None:
        top = bot = zero
    else:
        rows = u_s.shape[1]
        top = jnp.where(has_prev, u_s[prev_slot, rows - HALO:rows, cols], zero)
        bot = jnp.where(has_next, u_s[next_slot, 0:HALO, cols], zero)
    return jnp.concatenate([top, u, bot], axis=0)


def _pipeline_steps(s, n_chunks, lag, ring, normalized, staged, order, normalize, project, finish):
    fin_slot = lax.rem(s + (2 * ring - 1 - lag), jnp.int32(ring))

    @pl.when((s >= 1) & (s <= n_chunks))
    def _publish_normalized():
        h_w, h_s = normalized
        h_s[...] = h_w[...]

    @pl.when((s >= 2) & (s <= n_chunks + 1))
    def _publish_projected():
        slot = lax.rem(s + (ring - 2), jnp.int32(ring))
        for stage_ref, ring_ref in staged:
            ring_ref[slot] = stage_ref[...]

    @pl.when(s == 0)
    def _first():
        _weave(order, normalize=normalize([]))

    @pl.when((s >= 1) & (s <= lag))
    def _fill():
        _weave(order, project=project(), normalize=normalize([]))

    @pl.when((s > lag) & (s <= n_chunks))
    def _steady():
        tokens = []
        _weave(order, finish=finish(fin_slot, tokens), project=project(), normalize=normalize(tokens))

    @pl.when(s > n_chunks)
    def _drain():
        _weave(order, finish=finish(fin_slot, []))


def _adaln_kernel(cctx_ref, c_ref, we_ref, be_ref, wo_ref, bo_ref, oe_ref, oo_ref, cond_s, acc_e, acc_o):
    k = pl.program_id(0)

    @pl.when(k == 0)
    def _init():
        n_dec = c_ref.shape[0]
        cond_s[...] = jnp.zeros(cond_s.shape, F32)
        cond_s[0:1] = cctx_ref[...]
        cond_s[1:1 + n_dec] = c_ref[...]
        acc_e[...] = jnp.broadcast_to(be_ref[...], acc_e.shape)
        acc_o[...] = jnp.broadcast_to(bo_ref[...], acc_o.shape)

    s = _silu(cond_s[:, pl.ds(pl.multiple_of(k * ADA_TK, ADA_TK), ADA_TK)]).astype(BF16)
    acc_e[...] += jnp.dot(s, we_ref[...].astype(BF16), preferred_element_type=F32)
    acc_o[...] += jnp.dot(s, wo_ref[...].astype(BF16), preferred_element_type=F32)

    @pl.when(k == pl.num_programs(0) - 1)
    def _write():
        for r in range(oe_ref.shape[0]):
            oe_ref[r] = acc_e[r:r + 1, :]
            oo_ref[r] = acc_o[r:r + 1, :]


def _adaln(c_ctx, c, w_e, b_e, w_o, b_o):
    n = 3 * D_MODEL
    n_dec = c.shape[0]
    wspec = pl.BlockSpec((ADA_TK, n), lambda k: (k, 0))
    vspec = pl.BlockSpec((1, n), lambda k: (0, 0))
    ospec = pl.BlockSpec((8, 1, n), lambda k: (0, 0, 0))
    return pl.pallas_call(
        _adaln_kernel,
        out_shape=(jax.ShapeDtypeStruct((8, 1, n), F32), jax.ShapeDtypeStruct((8, 1, n), F32)),
        grid=(D_MODEL // ADA_TK,),
        in_specs=[pl.BlockSpec((1, D_MODEL), lambda k: (0, 0)), pl.BlockSpec((n_dec, D_MODEL), lambda k: (0, 0)),
                  wspec, vspec, wspec, vspec],
        out_specs=(ospec, ospec),
        scratch_shapes=[pltpu.VMEM((8, D_MODEL), F32), pltpu.VMEM((8, n), F32), pltpu.VMEM((8, n), F32)],
        compiler_params=pltpu.CompilerParams(dimension_semantics=("arbitrary",), vmem_limit_bytes=VMEM_LIMIT),
        name="adaln",
    )(c_ctx.reshape(1, D_MODEL), c, w_e, b_e.reshape(1, n), w_o, b_o.reshape(1, n))


def _head_inv_rms(x, bd):
    ms = jnp.dot((x * x).astype(BF16), bd, preferred_element_type=F32) * (1.0 / HEAD_DIM)
    return lax.rsqrt(ms + EPS)


def _rope(x, cos, sin_signed, first_half):
    partner = jnp.where(first_half, pltpu.roll(x, LANES - ROPE_FREQS, axis=1), pltpu.roll(x, ROPE_FREQS, axis=1))
    return x * cos + partner * sin_signed


def _dup_kv(x):
    lane = lax.broadcasted_iota(jnp.int32, (1, LANES), 1)
    swapped = pltpu.roll(x, HEAD_DIM, axis=1)
    return jnp.where(lane < HEAD_DIM, x, swapped), jnp.where(lane < HEAD_DIM, swapped, x)


def _tiled_keys(k):
    ka, kb = _dup_kv(k)
    return jnp.concatenate([ka, ka, kb, kb], axis=1).astype(BF16)


def _values_t(v):
    va, vb = _dup_kv(v)
    return jnp.concatenate([va.T, vb.T], axis=0).astype(BF16)


def _even_kernel(*refs, nc, n_chunks, lag, ring, windowed, n_stream=0):
    if windowed:
        n_in = 15
        stream_src = refs[n_in:n_in + n_stream]
        stream_dst = refs[n_in + n_stream + 1:n_in + 2 * n_stream + 1]
        refs = refs[:n_in] + refs[n_in + n_stream:n_in + n_stream + 1] + refs[n_in + 2 * n_stream + 1:]
        (xp_ref, xa_ref, modp_ref, moda_ref, g_ref, win_ref, cw_ref, cb_ref, qkg_ref, sink_ref, wout_ref,
         cos_ref, sin_ref, ck_ref, cv_ref,
         y_ref,
         h_s, h_w, u_s, gate_s, q_s, k_s, vt_s, gb_s, u_w, gate_w, q_w, k_w, vt_w, gb_w, a_s, ck_s, cvt_s) = refs
        ko_ref = vo_ref = None
    else:
        (xp_ref, xa_ref, modp_ref, moda_ref, g_ref, win_f32, cw_ref, cb_ref, qkg_ref, sink_ref, wout_f32,
         y_ref, ko_ref, vo_ref, win_ref, wout_ref,
         h_s, h_w, u_s, gate_s, q_s, k_s, vt_s, gb_s, u_w, gate_w, q_w, k_w, vt_w, gb_w, a_s) = refs
    s = pl.program_id(0)
    i = s - 1 - lag
    ci = lax.rem(i, jnp.int32(nc))

    @pl.when(s == 0)
    def _init():
        k_s[...] = jnp.zeros(k_s.shape, BF16)
        vt_s[...] = jnp.zeros(vt_s.shape, BF16)
        if not windowed:
            _cast_columns(win_f32, win_ref)
            _cast_columns(wout_f32, wout_ref)

    if windowed and n_stream:
        @pl.when(s < STREAM_STEPS)
        def _cast_next():
            for src, dst in zip(stream_src, stream_dst):
                if len(src.shape) == 2:
                    _cast_columns(src, dst)
                else:
                    dst[...] = src[...].astype(BF16)

    if windowed:
        @pl.when((i >= 0) & (ci == 0))
        def _load_ctx():
            ck_s[...] = _tiled_keys(ck_ref[0])
            cvt_s[...] = _values_t(cv_ref[0])

    def normalize(tokens):
        yield from _normalize_items(h_w, xp_ref, g_ref, modp_ref, tokens)

    def project():
        o_bg, o_cg, o_xs, o_ga, o_q, o_k, _, o_gb = EVEN_OFFS[:8]

        def proj(start):
            return jnp.dot(h_s[...], win_ref[start // NT], preferred_element_type=F32)

        lane = lax.broadcasted_iota(jnp.int32, (1, LANES), 1)
        first_half = (lane & (2 * ROPE_FREQS - 1)) < ROPE_FREQS
        ri = lax.broadcasted_iota(jnp.int32, (NT, NT), 0) >> HEAD_SHIFT
        cj = lax.broadcasted_iota(jnp.int32, (NT, NT), 1) >> HEAD_SHIFT
        bd = (ri == cj).astype(BF16)

        def qk_norm(x, g):
            w = x.shape[1]
            inv = _head_inv_rms(x, bd[0:w, 0:w])
            pieces = []
            for j in range(w // LANES):
                lanes = slice(j * LANES, (j + 1) * LANES)
                xj = x[:, lanes] * inv[:, lanes] * g
                pieces.append(_rope(xj, cos_ref[...], sin_ref[...], first_half) if windowed else xj)
            return pieces

        def keys_values(_):
            kv = proj(o_k)
            k, = qk_norm(kv[:, 0:KV_WIDTH], qkg_ref[1:2, :])
            v = kv[:, KV_WIDTH:2 * KV_WIDTH]
            if not windowed:
                ko_ref[0] = k
                vo_ref[0] = v
            k_w[...] = _tiled_keys(k)
            vt_w[...] = _values_t(v)

        def queries(n):
            for j, qj in enumerate(qk_norm(proj(o_q + n * NT), qkg_ref[0:1, :])):
                q_w[:, n * NT + j * LANES:n * NT + (j + 1) * LANES] = (qj * (ATTN_SCALE * LOG2E)).astype(BF16)

        def conv_gate(n):
            gate_w[:, n * NT:(n + 1) * NT] = proj(o_bg + n * NT) * _silu(proj(o_ga + n * NT))

        def conv_input(n):
            u_w[:, n * NT:(n + 1) * NT] = proj(o_cg + n * NT) * proj(o_xs + n * NT)

        def attn_gate(n):
            gb_w[:, n * NT:(n + 1) * NT] = _silu(proj(o_gb + n * NT))

        items = [(conv_gate, 0), (keys_values, 0), (conv_input, 0), (queries, 0), (conv_gate, 1), (queries, 1),
                 (conv_input, 1), (attn_gate, 0), (attn_gate, 1)]
        for emit, n in items:
            yield
            emit(n)

    def finish(slot, tokens):
        if windowed:
            prev_slot = lax.rem(slot + (ring - 1), jnp.int32(ring))
            next_slot = lax.rem(slot + 1, jnp.int32(ring))
            has_prev = ci > 0
            has_next = ci < nc - 1
        else:
            prev_slot = next_slot = has_prev = has_next = None

        yield
        ext = _with_halo(u_s, slot, slice(None), prev_slot, next_slot, has_prev, has_next)
        conv = (_rows_down(ext, 1)[HALO:HALO + TM] * cw_ref[0:1, :]
                + ext[HALO:HALO + TM] * cw_ref[1:2, :]
                + _rows_up(ext, 1)[HALO:HALO + TM] * cw_ref[2:3, :]
                + cb_ref[...])
        a_s[:, 0:CONV_WIDTH] = (gate_s[slot] * conv).astype(BF16)

        lane_head = lax.broadcasted_iota(jnp.int32, (1, GROUP_W), 1) >> HEAD_SHIFT
        cols4 = Q_PER_KV * BLOCK
        key_i =lax.broadcasted_iota(jnp.int32, (BLOCK, cols4), 0)
        qry_i = lax.broadcasted_iota(jnp.int32, (BLOCK, cols4), 1) & (BLOCK - 1)
        upper = lax.broadcasted_iota(jnp.int32, (LANES, LANES), 0) < HEAD_DIM
        halves = (slice(0, BLOCK), slice(BLOCK, TM))
        @functools.lru_cache(maxsize=None)
        def block_masks(qb):
            if qb == 0:
                blocks = ((prev_slot, halves[1]), (slot, halves[0]), (slot, halves[1]))
                prev_in, next_in = has_prev, True
            else:
                blocks = ((slot, halves[0]), (slot, halves[1]), (next_slot, halves[0]))
                prev_in, next_in = True, has_next
            prev_ok = (key_i - qry_i) >= jnp.where(prev_in, 0, BLOCK)
            next_ok = (qry_i - key_i) >= jnp.where(next_in, 0, BLOCK)
            return blocks, prev_ok, next_ok

        def scores(qb, kv):
            cols = slice(kv * GROUP_W, (kv + 1) * GROUP_W)
            vrows = slice(kv * LANES, (kv + 1) * LANES)
            qblk = q_s[slot, halves[qb], cols]
            q4 = jnp.concatenate(
                [jnp.where(lane_head == g, qblk, jnp.zeros_like(qblk)) for g in range(Q_PER_KV)], axis=0)

            def scores_t(kk):
                return lax.dot_general(kk, q4, (((1,), (1,)), ((), ())), preferred_element_type=F32)

            if not windowed:
                return [([scores_t(k_s[slot, :, cols])], vt_s[slot, vrows, :])]
            blocks, prev_ok, next_ok = block_masks(qb)
            s_loc = scores_t(jnp.concatenate([k_s[sl, r, cols] for sl, r in blocks], axis=0))
            vt_loc = jnp.concatenate([vt_s[sl, vrows, r] for sl, r in blocks], axis=1)
            s_ctx = scores_t(ck_s[:, cols])
            return [([jnp.where(prev_ok, s_loc[0:BLOCK], NEG), s_loc[BLOCK:2 * BLOCK],
                      jnp.where(next_ok, s_loc[2 * BLOCK:3 * BLOCK], NEG)], vt_loc),
                    ([s_ctx], cvt_s[vrows, :])]

        def softmax(kv, groups):
            snk = jnp.concatenate(
                [jnp.full((1, BLOCK), sink_ref[kv * Q_PER_KV + g] * LOG2E, F32) for g in range(Q_PER_KV)], axis=1)
            if tokens:
                token = tokens.pop(0)
                while tokens:
                    token = token + tokens.pop(0)
                snk = snk + jnp.concatenate([token] * Q_PER_KV, axis=1)
            mx = snk
            for ss, _ in groups:
                for sc in ss:
                    mx = jnp.maximum(mx, jnp.max(sc, axis=0, keepdims=True))
            den = jnp.exp2(snk - mx)
            probs = []
            for ss, vt in groups:
                es = []
                for sc in ss:
                    e = jnp.exp2(sc - mx)
                    den = den + jnp.sum(e, axis=0, keepdims=True)
                    es.append(e.astype(BF16))
                probs.append((es[0] if len(es) == 1 else jnp.concatenate(es, axis=0), vt))
            return probs, den

        def values(qb, kv, probs, den):
            o_t = jnp.zeros((LANES, cols4), F32)
            for e_all, vt in probs:
                o_t = o_t + jnp.dot(vt, e_all, preferred_element_type=F32)
            o_t = o_t * (1.0 / den)
            o01 = jnp.where(upper, o_t[:, 0:BLOCK], o_t[:, BLOCK:2 * BLOCK]).T
            o23 = jnp.where(upper, o_t[:, 2 * BLOCK:3 * BLOCK], o_t[:, 3 * BLOCK:4 * BLOCK]).T
            a_s[halves[qb], CONV_WIDTH + kv * GROUP_W:CONV_WIDTH + (kv + 1) * GROUP_W] = (
                jnp.concatenate([o01, o23], axis=1) * gb_s[slot, halves[qb], kv * GROUP_W:(kv + 1) * GROUP_W]
            ).astype(BF16)

        pairs = [(qb, kv) for qb in range(TM // BLOCK) for kv in range(N_KV_HEADS)]
        yield
        ahead = scores(*pairs[0])
        for t, (qb, kv) in enumerate(pairs):
            groups = ahead
            if t + 1 < len(pairs):
                yield
                ahead = scores(*pairs[t + 1])
            yield
            probs, den = softmax(kv, groups)
            yield
            values(qb, kv, probs, den)

        yield from _out_proj_items(lambda: a_s[...], wout_ref, xa_ref, moda_ref, y_ref)

    staged = ((u_w, u_s), (gate_w, gate_s), (q_w, q_s), (k_w, k_s), (vt_w, vt_s), (gb_w, gb_s))
    order = EVEN_ORDER_LATENT if windowed else EVEN_ORDER_CONTEXT
    _pipeline_steps(s, n_chunks, lag, ring, (h_w, h_s), staged, order, normalize, project, finish)


def _chunk_specs(n_chunks, nc, lag, per_seq_mod, tm=TM):
    last = n_chunks - 1
    norm_chunk = lambda s: jnp.minimum(s, last)
    proj_chunk = lambda s: jnp.clip(s - 1, 0, last)
    fin_chunk = lambda s: jnp.clip(s - 1 - lag, 0, last)
    seq_of = (lambda j: 1 + j // nc) if per_seq_mod else (lambda j: 0)
    return dict(
        x_norm=pl.BlockSpec((1, tm, D_MODEL), lambda s: (norm_chunk(s), 0, 0)),
        x_fin=pl.BlockSpec((1, tm, D_MODEL), lambda s: (fin_chunk(s), 0, 0)),
        mod_norm=pl.BlockSpec((1, 1, 3 * D_MODEL), lambda s: (seq_of(norm_chunk(s)), 0, 0)),
        mod_fin=pl.BlockSpec((1, 1, 3 * D_MODEL), lambda s: (seq_of(fin_chunk(s)), 0, 0)),
        proj_chunk=proj_chunk, fin_chunk=fin_chunk)


def _const_spec(shape):
    return pl.BlockSpec(shape, lambda c: (0,) * len(shape))


def _stream_specs(w):
    if w.ndim == 3:
        last = w.shape[0] - 1
        spec = pl.BlockSpec((1,) + w.shape[1:], lambda c: (jnp.minimum(c, last), 0, 0))
        return spec, spec, jax.ShapeDtypeStruct(w.shape, BF16)
    rows = w.shape[0] // STREAM_STEPS
    assert rows % 16 == 0 and rows * STREAM_STEPS == w.shape[0]
    tiled = _tiled_shape(w)
    return (pl.BlockSpec((rows, w.shape[1]), lambda c: (jnp.minimum(c, STREAM_STEPS - 1), 0)),
            pl.BlockSpec((tiled[0], rows, NT), lambda c: (0, jnp.minimum(c, STREAM_STEPS - 1), 0)),
            jax.ShapeDtypeStruct(tiled, BF16))


def _even_layer(x, mod, norm_g, w_in, conv_w, conv_b, qk_gains, sink, w_out, rope=None, ctx=None, next_weights=()):
    n_seq, seq_len, _ = x.shape
    windowed = rope is not None
    nc = seq_len // TM
    n_chunks = n_seq * nc
    lag, ring = (2, 4) if windowed else (1, 2)
    assert windowed or nc == 1
    sp = _chunk_specs(n_chunks, nc, lag, windowed)
    xc = x.reshape(n_chunks, TM, D_MODEL)
    in_specs = [
        sp["x_norm"], sp["x_fin"], sp["mod_norm"], sp["mod_fin"],
        _const_spec((1, D_MODEL)),
        _const_spec(w_in.shape),
        _const_spec((3, CONV_WIDTH)),
        _const_spec((1, CONV_WIDTH)),
        _const_spec((2, LANES)),
        pl.BlockSpec(memory_space=pltpu.SMEM),
        _const_spec(w_out.shape),
    ]
    args = [xc, xc, mod, mod, norm_g.reshape(1, D_MODEL), w_in, conv_w, conv_b.reshape(1, CONV_WIDTH),
            qk_gains, sink, w_out]
    per_chunk = [
        ((TM, CONV_WIDTH), F32),
        ((TM, CONV_WIDTH), F32),
        ((TM, ATTN_WIDTH), BF16),
        ((TM, 2 * GROUP_W), BF16),
        ((2 * LANES, TM), BF16),
        ((TM, ATTN_WIDTH), F32),
    ]
    scratch = (
        [pltpu.VMEM((TM, D_MODEL), BF16)] * 2
        + [pltpu.VMEM((ring,) + shape, dt) for shape, dt in per_chunk]
        + [pltpu.VMEM(shape, dt) for shape, dt in per_chunk]
        + [pltpu.VMEM((TM, D_MODEL), BF16)])
    y_shape = jax.ShapeDtypeStruct(xc.shape, F32)
    y_spec = pl.BlockSpec((1, TM, D_MODEL), lambda c: (sp["fin_chunk"](c), 0, 0))
    if windowed:
        cos, sin = rope
        ck, cv = ctx
        rope_spec = pl.BlockSpec((TM, LANES), lambda c: (sp["proj_chunk"](c) % nc, 0))
        ctx_spec = pl.BlockSpec((1,) + ck.shape[1:], lambda c: (sp["fin_chunk"](c) // nc, 0, 0))
        in_specs += [rope_spec, rope_spec, ctx_spec, ctx_spec]
        args += [cos, sin, ck, cv]
        scratch += [pltpu.VMEM((ck.shape[1], 2 * GROUP_W), BF16), pltpu.VMEM((2 * LANES, cv.shape[1]), BF16)]
        out_shape, out_specs = y_shape, y_spec
        if next_weights:
            assert n_chunks + lag + 1 >= STREAM_STEPS
            streams = [_stream_specs(w) for w in next_weights]
            in_specs += [st[0] for st in streams]
            args += list(next_weights)
            out_shape = (y_shape,) + tuple(st[2] for st in streams)
            out_specs = (y_spec,) + tuple(st[1] for st in streams)
    else:
        kv_shape = jax.ShapeDtypeStruct((n_chunks, TM, KV_WIDTH), F32)
        kv_spec = pl.BlockSpec((1, TM, KV_WIDTH), lambda c: (sp["proj_chunk"](c), 0, 0))
        out_shape = (y_shape, kv_shape, kv_shape,
                     jax.ShapeDtypeStruct(_tiled_shape(w_in), BF16), jax.ShapeDtypeStruct(_tiled_shape(w_out), BF16))
        out_specs = (y_spec, kv_spec, kv_spec, _const_spec(_tiled_shape(w_in)), _const_spec(_tiled_shape(w_out)))
    out = pl.pallas_call(
        functools.partial(_even_kernel, nc=nc, n_chunks=n_chunks, lag=lag, ring=ring, windowed=windowed,
                          n_stream=len(next_weights)),
        out_shape=out_shape,
        grid=(n_chunks + lag + 1,),
        in_specs=in_specs,
        out_specs=out_specs,
        scratch_shapes=scratch,
        compiler_params=PIPELINE_PARAMS,
        name="even_latent" if windowed else "even_context",
    )(*args)
    if windowed and next_weights:
        y, *next_bf16 = out
        return y.reshape(x.shape), next_bf16
    if windowed:
        return out.reshape(x.shape)
    y, k, v, w_in_bf16, w_out_bf16 = out
    return (y.reshape(x.shape), k.reshape(n_seq, seq_len, KV_WIDTH), v.reshape(n_seq, seq_len, KV_WIDTH),
            w_in_bf16, w_out_bf16)


def _odd_kernel(*refs, nc, n_chunks, lag, ring, cast_weights):
    if cast_weights:
        (xp_ref, xa_ref, modp_ref, moda_ref, g_ref, win_f32, pw_f32, ps_ref, wout_f32,
         y_ref, win_ref, pw_ref, wout_ref, h_s, h_w, u_s, sg_s, u_w, sg_w) = refs
    else:
        (xp_ref, xa_ref, modp_ref, moda_ref, g_ref, win_ref, pw_ref, ps_ref, wout_ref,
         y_ref, h_s, h_w, u_s, sg_s, u_w, sg_w) = refs
    tm = h_s.shape[0]
    s = pl.program_id(0)
    i = s - 1 - lag
    ci = lax.rem(i, jnp.int32(nc))

    if cast_weights:
        @pl.when(s == 0)
        def _cast():
            _cast_columns(win_f32, win_ref)
            _cast_columns(wout_f32, wout_ref)
            for gi in range(pw_ref.shape[0]):
                pw_ref[gi] = pw_f32[gi].astype(BF16)

    def normalize(tokens):
        yield from _normalize_items(h_w, xp_ref, g_ref, modp_ref, tokens)

    def project():
        for n in range(D_MODEL // NT):
            yield
            cols = slice(n * NT, (n + 1) * NT)
            u = jnp.dot(h_s[...], win_ref[n], preferred_element_type=F32)
            u_w[:, cols] = u
        for n in range(D_MODEL // NT):
            yield
            cols = slice(n * NT, (n + 1) * NT)
            sg = _silu(jnp.dot(h_s[...], win_ref[D_MODEL // NT + n], preferred_element_type=F32))
            sg_w[:, cols] = sg

    def finish(slot, tokens):
        if nc > 1:
            prev_slot = lax.rem(slot + (ring - 1), jnp.int32(ring))
            next_slot = lax.rem(slot + 1, jnp.int32(ring))
            has_prev = ci > 0
            has_next = ci < nc - 1
        else:
            prev_slot = next_slot = has_prev = has_next = None
        t = ci * tm + lax.broadcasted_iota(jnp.int32, (tm, 1), 0)
        outs = []
        for gi, w in enumerate(POOL_SIZES):
            yield
            cols = slice(gi * POOL_GROUP, (gi + 1) * POOL_GROUP)
            ext = _with_halo(u_s, slot, cols, prev_slot, next_slot, has_prev, has_next)
            acc = ext
            span = 1
            while span < w:
                acc = acc + _rows_down(acc, span)
                span *= 2
            if w > 2:
                acc = _rows_up(acc, w // 2 - 1)
            cnt = (jnp.minimum(t + w // 2, nc * tm) - jnp.maximum(t - w // 2, 0)).astype(F32)
            while tokens:
                cnt = cnt + tokens.pop(0)[:, 0:1]
            pooled = acc[HALO:HALO + tm] / cnt - ext[HALO:HALO + tm]
            outs.append(jnp.dot(pooled.astype(BF16), pw_ref[gi], preferred_element_type=F32))
        yield
        y = (jnp.concatenate(outs, axis=1) * ps_ref[...] * sg_s[slot]).astype(BF16)
        yield from _out_proj_items(lambda: y, wout_ref, xa_ref, moda_ref, y_ref)

    _pipeline_steps(s, n_chunks, lag, ring, (h_w, h_s), ((u_w, u_s), (sg_w, sg_s)), ODD_ORDER,
                    normalize, project, finish)


def _odd_layer(x, mod, per_seq_mod, norm_g, w_in, pool_w, pool_scale, w_out):
    n_seq, seq_len, _ = x.shape
    tm = TM
    nc = seq_len // tm
    n_chunks = n_seq * nc
    lag, ring = (2, 4) if nc > 1 else (1, 2)
    sp = _chunk_specs(n_chunks, nc, lag, per_seq_mod, tm)
    xc = x.reshape(n_chunks, tm, D_MODEL)
    in_specs = [
        sp["x_norm"], sp["x_fin"], sp["mod_norm"], sp["mod_fin"],
        _const_spec((1, D_MODEL)),
        _const_spec(w_in.shape),
        _const_spec(pool_w.shape),
        _const_spec((1, D_MODEL)),
        _const_spec(w_out.shape),
    ]
    cast_weights = w_in.dtype != BF16
    out_shape = jax.ShapeDtypeStruct(xc.shape, F32)
    out_specs = pl.BlockSpec((1, tm, D_MODEL), lambda c: (sp["fin_chunk"](c), 0, 0))
    if cast_weights:
        shapes = (_tiled_shape(w_in), pool_w.shape, _tiled_shape(w_out))
        out_shape = (out_shape,) + tuple(jax.ShapeDtypeStruct(shape, BF16) for shape in shapes)
        out_specs = (out_specs,) + tuple(_const_spec(shape) for shape in shapes)
    out = pl.pallas_call(
        functools.partial(_odd_kernel, nc=nc, n_chunks=n_chunks, lag=lag, ring=ring, cast_weights=cast_weights),
        out_shape=out_shape,
        grid=(n_chunks + lag + 1,),
        in_specs=in_specs,
        out_specs=out_specs,
        scratch_shapes=[
            pltpu.VMEM((tm, D_MODEL), BF16),
            pltpu.VMEM((tm, D_MODEL), BF16),
            pltpu.VMEM((ring, tm, D_MODEL), F32),
            pltpu.VMEM((ring, tm, D_MODEL), F32),
            pltpu.VMEM((tm, D_MODEL), F32),
            pltpu.VMEM((tm, D_MODEL), F32),
        ],
        compiler_params=PIPELINE_PARAMS,
        name="odd_latent" if per_seq_mod else "odd_context",
    )(xc, xc, mod, mod, norm_g.reshape(1, D_MODEL), w_in, pool_w, pool_scale.reshape(1, D_MODEL), w_out)
    if cast_weights:
        y, *bf16_weights = out
        return y.reshape(x.shape), bf16_weights
    return out.reshape(x.shape)


def _rope_tables(seq_len):
    n_rows = seq_len // GRID_W
    row = np.repeat(np.arange(n_rows), GRID_W).astype(np.float32)
    col = np.tile(np.arange(GRID_W), n_rows).astype(np.float32)
    inv = (np.float32(ROPE_BASE) ** (-np.arange(ROPE_FREQS, dtype=np.float32) / np.float32(ROPE_FREQS))).astype(np.float32)
    ang = np.stack([row[:, None] * inv, col[:, None] * inv], axis=1)
    cos = np.broadcast_to(np.cos(ang)[:, :, None, :], (seq_len, 2, 2, ROPE_FREQS))
    sin = np.sin(ang)[:, :, None, :] * np.array([-1.0, 1.0], np.float32)[None, None, :, None]
    cos = cos.reshape(seq_len, HEAD_DIM).astype(np.float32)
    sin = sin.reshape(seq_len, HEAD_DIM).astype(np.float32)
    return jnp.asarray(np.tile(cos, (1, LANES // HEAD_DIM))), jnp.asarray(np.tile(sin, (1, LANES // HEAD_DIM)))


def kernel(x_prompt, x_sample, cache_k, cache_v, c, c_ctx, ada_w_e, ada_b_e, norm_g_e, w_in_e, conv_w, conv_b,
           q_norm_g, k_norm_g, sink, w_out_e, ada_w_o, ada_b_o, norm_g_o, w_in_o, pool_w, pool_scale, w_out_o):
    n_dec = x_sample.shape[0]
    depth = ada_w_e.shape[0] + ada_w_o.shape[0]
    assert ada_w_e.shape[0] == 1 and ada_w_o.shape[0] == 1 and n_dec + 1 <= 8
    mod_e, mod_o = _adaln(c_ctx, c, ada_w_e[0], ada_b_e[0], ada_w_o[0], ada_b_o[0])
    rope = _rope_tables(x_sample.shape[1])

    yp, ys = x_prompt, x_sample
    new_k, new_v = [], []
    for layer in range(depth):
        i = layer // 2
        if layer % 2 == 0:
            mod = mod_e
            gains = jnp.tile(jnp.stack([q_norm_g[i], k_norm_g[i]]), (1, LANES // HEAD_DIM))
            small = (conv_w[i], conv_b[i], gains, sink[i])
            yp, k, v, w_in, w_out = _even_layer(yp, mod, norm_g_e[i], w_in_e[i], *small, w_out_e[i])
            new_k.append(k.reshape(k.shape[0], k.shape[1], N_KV_HEADS, HEAD_DIM))
            new_v.append(v.reshape(v.shape[0], v.shape[1], N_KV_HEADS, HEAD_DIM))
            ck = cache_k[:, i].reshape(n_dec, cache_k.shape[2], KV_WIDTH)
            cv = cache_v[:, i].reshape(n_dec, cache_v.shape[2], KV_WIDTH)
            following = (w_in_o[i], pool_w[i], w_out_o[i]) if layer + 1 < depth else ()
            ys = _even_layer(ys, mod, norm_g_e[i], w_in, *small, w_out, rope=rope, ctx=(ck, cv), next_weights=following)
            if following:
                ys, odd_bf16 = ys
        else:
            mod = mod_o
            if layer > 0:
                w_in, w_pool, w_out = odd_bf16
                yp = _odd_layer(yp, mod, False, norm_g_o[i], w_in, w_pool, pool_scale[i], w_out)
            else:
                yp, (w_in, w_pool, w_out) = _odd_layer(yp, mod, False, norm_g_o[i], w_in_o[i], pool_w[i],
                                                       pool_scale[i], w_out_o[i])
            ys = _odd_layer(ys, mod, True, norm_g_o[i], w_in, w_pool, pool_scale[i], w_out)
    return yp, ys, jnp.stack(new_k, axis=1), jnp.stack(new_v, axis=1)
```
